```python
import math
import jax, jax.numpy as jnp
from jax import lax
import numpy as np


D_MODEL = 1024
BATCH = 8
SEQ = 4096
DEPTH = 1

ATTN_Q_HEADS = 8
ATTN_KV_HEADS = 2
ATTN_HEAD_DIM = 64
WINDOW = 128
ATTN_BLOCK = 128
ROPE_THETA = 10000.0
GLA_HEADS = 4
GLA_KEY_DIM = 64
GLA_VAL_DIM = 128
GLA_GATE_RANK = 16
GLA_GATE_NORM = 16.0
GLA_CHUNK = 64
ATTN_WIDTH = ATTN_Q_HEADS * ATTN_HEAD_DIM
GLA_WIDTH = GLA_HEADS * GLA_VAL_DIM
MIX_WIDTH = ATTN_WIDTH + GLA_WIDTH
IN_SIZES = (ATTN_Q_HEADS * ATTN_HEAD_DIM, ATTN_KV_HEADS * ATTN_HEAD_DIM, ATTN_KV_HEADS * ATTN_HEAD_DIM,
            GLA_HEADS * GLA_KEY_DIM, GLA_HEADS * GLA_KEY_DIM, GLA_WIDTH, GLA_WIDTH, GLA_GATE_RANK)
IN_WIDTH = sum(IN_SIZES)
N_EXPERTS = 256
TOP_K = 8
N_GROUPS = 8
TOPK_GROUPS = 4
EXPERT_HIDDEN = 256
SHARED_HIDDEN = 256
ROUTED_SCALE = 2.5
EXPERT_BLOCK = 128
DEEPNORM_ALPHA = (2 * DEPTH) ** 0.25
DEEPNORM_BETA = (8 * DEPTH) ** -0.25
LN_EPS = 1e-5
NEG_INF = -1e30

kernel_name = 'hybrid_swa_sink_gla_moe_deepnorm_adaln'


def layer_norm_plain(x):
    x32 = x.astype(jnp.float32)
    mu = jnp.mean(x32, axis=-1, keepdims=True)
    var = jnp.mean(jnp.square(x32 - mu), axis=-1, keepdims=True)
    return (x32 - mu) * lax.rsqrt(var + LN_EPS)


def layer_norm(x, g, b):
    return (layer_norm_plain(x) * g.astype(jnp.float32) + b.astype(jnp.float32)).astype(x.dtype)


def modulate(x, shift, scale):
    y = layer_norm_plain(x) * (1.0 + scale.astype(jnp.float32)) + shift.astype(jnp.float32)
    return y.astype(x.dtype)


def rope(t, positions):
    half = t.shape[-1] // 2
    inv_freq = ROPE_THETA ** (-jnp.arange(half, dtype=jnp.float32) / half)
    ang = positions.astype(jnp.float32)[..., None] * inv_freq
    cos = jnp.cos(ang)[:, :, None, :]
    sin = jnp.sin(ang)[:, :, None, :]
    t32 = t.astype(jnp.float32)
    t1, t2 = t32[..., :half], t32[..., half:]
    return jnp.concatenate([t1 * cos - t2 * sin, t2 * cos + t1 * sin], axis=-1)


def sliding_window_sink_attention(q, k, v, sinks):
    B, S, HQ, Dh = q.shape
    HKV = k.shape[2]
    G = HQ // HKV
    nb = S // ATTN_BLOCK
    qb = q.reshape(B, nb, ATTN_BLOCK, HKV, G, Dh)

    def with_prev(t):
        prev = jnp.pad(t[:, :-1], ((0, 0), (1, 0), (0, 0), (0, 0), (0, 0)))
        return jnp.concatenate([prev, t], axis=2)

    kw = with_prev(k.astype(jnp.float32).reshape(B, nb, ATTN_BLOCK, HKV, Dh))
    vw = with_prev(v.astype(jnp.float32).reshape(B, nb, ATTN_BLOCK, HKV, Dh))
    s = jnp.einsum('bnqhgd,bnkhd->bnhgqk', qb, kw) * (Dh ** -0.5)
    qi = jnp.arange(ATTN_BLOCK)[:, None]
    kj = jnp.arange(2 * ATTN_BLOCK)[None, :]
    dist = qi + ATTN_BLOCK - kj
    band = (dist >= 0) & (dist < WINDOW)
    key_abs = (jnp.arange(nb)[:, None] - 1) * ATTN_BLOCK + jnp.arange(2 * ATTN_BLOCK)[None, :]
    valid = band[None] & (key_abs >= 0)[:, None, :]
    s = jnp.where(valid[None, :, None, None], s, NEG_INF)
    sink = jnp.broadcast_to(sinks.astype(jnp.float32).reshape(1, 1, HKV, G, 1, 1), s.shape[:-1] + (1,))
    p = jax.nn.softmax(jnp.concatenate([s, sink], axis=-1), axis=-1)[..., :-1]
    o = jnp.einsum('bnhgqk,bnkhd->bnqhgd', p, vw)
    return o.reshape(B, S, HQ * Dh)


def gla_chunked(q, k, v, log_g):
    B, S, H, dk = q.shape
    dv = v.shape[-1]
    n = S // GLA_CHUNK
    C = GLA_CHUNK
    qc = q.astype(jnp.float32).reshape(B, n, C, H, dk) * (dk ** -0.5)
    kc = k.astype(jnp.float32).reshape(B, n, C, H, dk)
    vc = v.astype(jnp.float32).reshape(B, n, C, H, dv)
    b = jnp.cumsum(log_g.reshape(B, n, C, H, dk), axis=2)
    b_last = b[:, :, -1]
    q_in = qc * jnp.exp(b)
    k_in = kc * jnp.exp(-b)
    k_out = kc * jnp.exp(b_last[:, :, None] - b)
    causal = jnp.tril(jnp.ones((C, C), dtype=bool))
    a = jnp.einsum('bnchd,bnshd->bnhcs', q_in, k_in)
    a = jnp.where(causal, a, 0.0)
    o_intra = jnp.einsum('bnhcs,bnshv->bnchv', a, vc)
    u = jnp.einsum('bnshd,bnshv->bnhdv', k_out, vc)
    decay = jnp.exp(b_last)

    def step(state, inp):
        dec, inc = inp
        return dec[..., None] * state + inc, state

    _, s_prev = lax.scan(step, jnp.zeros((B, H, dk, dv), jnp.float32),
                         (jnp.moveaxis(decay, 1, 0), jnp.moveaxis(u, 1, 0)))
    s_prev = jnp.moveaxis(s_prev, 0, 1)
    o_inter = jnp.einsum('bnchd,bnhdv->bnchv', q_in, s_prev)
    return (o_intra + o_inter).reshape(B, S, H, dv)


def token_mixer(h, positions, w_in, b_in, attn_sinks, w_gk2, b_gk2, gla_norm_g, w_o, b_o):
    B, S, _ = h.shape
    proj = h @ w_in + b_in
    q_a, k_a, v_a, q_l, k_l, v_l, g_l, gk_lo = jnp.split(proj, np.cumsum(IN_SIZES)[:-1].tolist(), axis=-1)
    q_a = rope(q_a.reshape(B, S, ATTN_Q_HEADS, ATTN_HEAD_DIM), positions)
    k_a = rope(k_a.reshape(B, S, ATTN_KV_HEADS, ATTN_HEAD_DIM), positions)
    v_a = v_a.reshape(B, S, ATTN_KV_HEADS, ATTN_HEAD_DIM)
    o_attn = sliding_window_sink_attention(q_a, k_a, v_a, attn_sinks)
    log_g = jax.nn.log_sigmoid((gk_lo @ w_gk2 + b_gk2).astype(jnp.float32)) / GLA_GATE_NORM
    o_gla = gla_chunked(q_l.reshape(B, S, GLA_HEADS, GLA_KEY_DIM),
                        k_l.reshape(B, S, GLA_HEADS, GLA_KEY_DIM),
                        v_l.reshape(B, S, GLA_HEADS, GLA_VAL_DIM),
                        log_g.reshape(B, S, GLA_HEADS, GLA_KEY_DIM))
    o_gla = o_gla * lax.rsqrt(jnp.mean(jnp.square(o_gla), axis=-1, keepdims=True) + LN_EPS) * gla_norm_g.astype(jnp.float32)
    o_gla = o_gla.reshape(B, S, GLA_WIDTH) * jax.nn.silu(g_l.astype(jnp.float32))
    mixed = jnp.concatenate([o_attn, o_gla], axis=-1).astype(h.dtype)
    return mixed @ w_o + b_o


def swiglu(x, w_gate, w_up, w_down):
    return (jax.nn.silu(x @ w_gate) * (x @ w_up)) @ w_down


def moe_ffn(h, w_router, router_bias, w_exp_gate, w_exp_up, w_exp_down, w_sh_gate, w_sh_up, w_sh_down):
    B, S, D = h.shape
    N = B * S
    xt = h.reshape(N, D)
    scores = jax.nn.sigmoid((xt @ w_router).astype(jnp.float32))
    biased = scores + router_bias.astype(jnp.float32)
    grp = biased.reshape(N, N_GROUPS, N_EXPERTS // N_GROUPS)
    grp_score = jnp.sum(lax.top_k(grp, 2)[0], axis=-1)
    _, top_g = lax.top_k(grp_score, TOPK_GROUPS)
    gmask = jnp.any(top_g[..., None] == jnp.arange(N_GROUPS), axis=1)
    emask = jnp.repeat(gmask, N_EXPERTS // N_GROUPS, axis=-1)
    _, top_idx = lax.top_k(jnp.where(emask, biased, NEG_INF), TOP_K)
    top_w = jnp.take_along_axis(scores, top_idx, axis=-1)
    top_w = top_w / jnp.sum(top_w, axis=-1, keepdims=True) * ROUTED_SCALE
    A = N * TOP_K
    flat_e = top_idx.reshape(A)
    flat_w = top_w.reshape(A)
    flat_t = jnp.arange(A, dtype=jnp.int32) // TOP_K
    order = jnp.argsort(flat_e, stable=True)
    e_s, t_s, w_s = flat_e[order], flat_t[order], flat_w[order]
    counts = jnp.bincount(flat_e, length=N_EXPERTS)
    starts = jnp.cumsum(counts) - counts
    padded = (counts + EXPERT_BLOCK - 1) // EXPERT_BLOCK * EXPERT_BLOCK
    pends = jnp.cumsum(padded)
    pstarts = pends - padded
    dest = pstarts[e_s] + jnp.arange(A, dtype=jnp.int32) - starts[e_s]
    P = A + N_EXPERTS * EXPERT_BLOCK
    n_blk = P // EXPERT_BLOCK
    slot_tok = jnp.full((P,), N, jnp.int32).at[dest].set(t_s)
    slot_w = jnp.zeros((P,), jnp.float32).at[dest].set(w_s)
    blk_exp = jnp.minimum(jnp.searchsorted(pends, jnp.arange(n_blk) * EXPERT_BLOCK, side='right'), N_EXPERTS - 1)
    x_pad = jnp.concatenate([xt, jnp.zeros((1, D), xt.dtype)], axis=0)

    def expert_block(args):
        tok, e, w = args
        yb = swiglu(x_pad[tok], w_exp_gate[e], w_exp_up[e], w_exp_down[e])
        return yb.astype(jnp.float32) * w[:, None]

    y = lax.map(expert_block, (slot_tok.reshape(n_blk, EXPERT_BLOCK), blk_exp, slot_w.reshape(n_blk, EXPERT_BLOCK)))
    routed = jax.ops.segment_sum(y.reshape(P, D), slot_tok, num_segments=N + 1)[:N]
    shared = swiglu(xt, w_sh_gate, w_sh_up, w_sh_down).astype(jnp.float32)
    return (shared + routed).astype(h.dtype).reshape(B, S, D)


def setup_inputs(seed: int = 0) -> dict:
    key = jax.random.key(seed)
    ks = jax.random.split(key, 28)
    f32 = jnp.float32
    L, D, E, H = DEPTH, D_MODEL, N_EXPERTS, EXPERT_HIDDEN

    def nrm(k, shape, scale):
        return jax.random.normal(k, shape, f32) * scale

    offs = np.cumsum((0,) + IN_SIZES)
    col_scale = np.ones((IN_WIDTH,), np.float32)
    col_scale[offs[2]:offs[3]] = DEEPNORM_BETA
    col_scale[offs[5]:offs[6]] = DEEPNORM_BETA
    positions = (jax.random.randint(ks[2], (BATCH, 1), 0, 1024, jnp.int32)
                 + jnp.arange(SEQ, dtype=jnp.int32)[None, :])
    return {
        'x': nrm(ks[0], (BATCH, SEQ, D), 1.0),
        'c': nrm(ks[1], (BATCH, D), 1.0),
        'positions': positions,
        'w_ada': nrm(ks[3], (L, D, 6 * D), 0.2 * D ** -0.5),
        'b_ada': nrm(ks[4], (L, 6 * D), 0.02),
        'w_in': nrm(ks[5], (L, D, IN_WIDTH), D ** -0.5) * jnp.asarray(col_scale),
        'b_in': nrm(ks[6], (L, IN_WIDTH), 0.02),
        'attn_sinks': nrm(ks[7], (L, ATTN_Q_HEADS), 0.5),
        'w_gk2': nrm(ks[8], (L, GLA_GATE_RANK, GLA_HEADS * GLA_KEY_DIM), GLA_GATE_RANK ** -0.5),
        'b_gk2': nrm(ks[9], (L, GLA_HEADS * GLA_KEY_DIM), 0.1),
        'gla_norm_g': 1.0 + nrm(ks[10], (L, GLA_VAL_DIM), 0.02),
        'w_o': nrm(ks[11], (L, MIX_WIDTH, D), MIX_WIDTH ** -0.5 * DEEPNORM_BETA),
        'b_o': nrm(ks[12], (L, D), 0.02),
        'ln1_g': 1.0 + nrm(ks[13], (L, D), 0.02),
        'ln1_b': nrm(ks[14], (L, D), 0.02),
        'w_router': nrm(ks[15], (L, D, E), D ** -0.5),
        'router_bias': nrm(ks[16], (L, E), 0.01),
        'w_exp_gate': nrm(ks[17], (L, E, D, H), D ** -0.5),
        'w_exp_up': nrm(ks[18], (L, E, D, H), D ** -0.5 * DEEPNORM_BETA),
        'w_exp_down': nrm(ks[19], (L, E, H, D), H ** -0.5 * DEEPNORM_BETA),
        'w_sh_gate': nrm(ks[20], (L, D, SHARED_HIDDEN), D ** -0.5),
        'w_sh_up': nrm(ks[21], (L, D, SHARED_HIDDEN), D ** -0.5 * DEEPNORM_BETA),
        'w_sh_down': nrm(ks[22], (L, SHARED_HIDDEN, D), SHARED_HIDDEN ** -0.5 * DEEPNORM_BETA),
        'ln2_g': 1.0 + nrm(ks[23], (L, D), 0.02),
        'ln2_b': nrm(ks[24], (L, D), 0.02),
    }


def reference(x, c, positions, w_ada, b_ada, w_in, b_in, attn_sinks, w_gk2, b_gk2, gla_norm_g, w_o, b_o,
              ln1_g, ln1_b, w_router, router_bias, w_exp_gate, w_exp_up, w_exp_down,
              w_sh_gate, w_sh_up, w_sh_down, ln2_g, ln2_b):
    c_act = jax.nn.silu(c)
    for l in range(DEPTH):
        mod = (c_act @ w_ada[l] + b_ada[l])[:, None, :]
        sh1, sc1, g1, sh2, sc2, g2 = jnp.split(mod, 6, axis=-1)
        h = modulate(x, sh1, sc1)
        y = token_mixer(h, positions, w_in[l], b_in[l], attn_sinks[l], w_gk2[l], b_gk2[l],
                        gla_norm_g[l], w_o[l], b_o[l])
        x = layer_norm(DEEPNORM_ALPHA * x + (1.0 + g1) * y, ln1_g[l], ln1_b[l])
        h = modulate(x, sh2, sc2)
        y = moe_ffn(h, w_router[l], router_bias[l], w_exp_gate[l], w_exp_up[l], w_exp_down[l],
                    w_sh_gate[l], w_sh_up[l], w_sh_down[l])
        x = layer_norm(DEEPNORM_ALPHA * x + (1.0 + g2) * y, ln2_g[l], ln2_b[l])
    return x
```

```python
import functools

import jax
import jax.numpy as jnp
import numpy as np
from jax import lax
from jax.experimental import pallas as pl
from jax.experimental.pallas import tpu as pltpu

F32 = jnp.float32
BF16 = jnp.bfloat16
I32 = jnp.int32

ATTN_Q_HEADS = 8
ATTN_KV_HEADS = 2
ATTN_HEAD_DIM = 64
ATTN_BLOCK = 128
ROPE_THETA = 10000.0
GLA_HEADS = 4
GLA_KEY_DIM = 64
GLA_VAL_DIM = 128
GLA_GATE_RANK = 16
GLA_GATE_NORM = 16.0
GLA_CHUNK = 64
N_GROUPS = 8
TOPK_GROUPS = 4
TOP_K = 8
ROUTED_SCALE = 2.5
LN_EPS = 1e-5
NEG_INF = -1e30
REMOVED = -3e38

ATTN_WIDTH = ATTN_Q_HEADS * ATTN_HEAD_DIM
KV_WIDTH = ATTN_KV_HEADS * ATTN_HEAD_DIM
GLA_KWIDTH = GLA_HEADS * GLA_KEY_DIM
GLA_WIDTH = GLA_HEADS * GLA_VAL_DIM

LANES = 128
VMEM_LIMIT = 56 * 1024 * 1024
EXPERT_ROWS = 256


def _params(sem):
    return pltpu.CompilerParams(dimension_semantics=sem, vmem_limit_bytes=VMEM_LIMIT)


def _full(shape):
    return pl.BlockSpec(shape, lambda *_: (0,) * len(shape))


def _split_bf16(a):
    hi = a.astype(BF16)
    lo = (a - hi.astype(F32)).astype(BF16)
    return hi, lo


def _dot(a, b, dims=(((1,), (0,)), ((), ()))):
    return lax.dot_general(a, b, dims, preferred_element_type=F32)


NT = (((1,), (1,)), ((), ()))
TN = (((0,), (0,)), ((), ()))


def _dot3(a, b, dims=(((1,), (0,)), ((), ()))):
    ah, al = _split_bf16(a)
    bh, bl = _split_bf16(b)
    return _dot(ah, bh, dims) + (_dot(ah, bl, dims) + _dot(al, bh, dims))


def _ln_plain(x):
    mu = jnp.mean(x, axis=-1, keepdims=True)
    xc = x - mu
    var = jnp.mean(xc * xc, axis=-1, keepdims=True)
    return xc * lax.rsqrt(var + LN_EPS)


def _silu(x):
    return x * (1.0 / (1.0 + jnp.exp(-x)))


def _sigmoid(x):
    return 1.0 / (1.0 + jnp.exp(-x))


def _mod_kernel(c_ref, w_ref, b_ref, o_ref):
    c = c_ref[...]
    o_ref[...] = _dot3(_silu(c), w_ref[...]) + b_ref[...]


def _mod(c, w_ada, b_ada):
    B, D = c.shape
    n = w_ada.shape[1] // D
    return pl.pallas_call(
        _mod_kernel,
        grid=(n,),
        in_specs=[_full((B, D)),
                  pl.BlockSpec((D, D), lambda j: (0, j)),
                  pl.BlockSpec((1, D), lambda j: (0, j))],
        out_specs=pl.BlockSpec((B, D), lambda j: (0, j)),
        out_shape=jax.ShapeDtypeStruct((B, n * D), F32),
        compiler_params=_params(("arbitrary",)),
        name="mod",
    )(c, w_ada, b_ada.reshape(1, -1))


_SEG = np.cumsum([0, ATTN_WIDTH, KV_WIDTH, KV_WIDTH, GLA_KWIDTH, GLA_KWIDTH, GLA_WIDTH, GLA_WIDTH])


def _rope_chunk(t, cos, sin_signed, first_half):
    up = pltpu.roll(t, LANES - 32, axis=1)
    dn = pltpu.roll(t, 32, axis=1)
    return t * cos + jnp.where(first_half, up, dn) * sin_signed


def _inproj_kernel(x_ref, mod_ref, pos_ref, invf_ref, w_ref, b_ref, wlo_ref, blo_ref, wgk_ref, bgk_ref,
                   qa_ref, ka_ref, va_ref, ql_ref, kl_ref, vl_ref, gl_ref, lg_ref):
    x = x_ref[...]
    h = _ln_plain(x) * (1.0 + mod_ref[0, 1:2, :]) + mod_ref[0, 0:1, :]
    hb = h.astype(BF16)

    def seg(i):
        lo, hi = int(_SEG[i]), int(_SEG[i + 1])
        return _dot(hb, w_ref[:, lo:hi]) + b_ref[:, lo:hi]

    ang = pos_ref[...] * invf_ref[...]
    cos = jnp.cos(ang)
    sin = jnp.sin(ang)
    lane = lax.broadcasted_iota(I32, ang.shape, 1)
    first_half = (lane % ATTN_HEAD_DIM) < (ATTN_HEAD_DIM // 2)
    sin_signed = jnp.where(first_half, -sin, sin)

    q = seg(0)
    scale = ATTN_HEAD_DIM ** -0.5
    for c in range(ATTN_WIDTH // LANES):
        t = q[:, c * LANES:(c + 1) * LANES]
        qa_ref[:, c * LANES:(c + 1) * LANES] = (_rope_chunk(t, cos, sin_signed, first_half) * scale).astype(BF16)
    ka_ref[...] = _rope_chunk(seg(1), cos, sin_signed, first_half).astype(BF16)
    va_ref[...] = seg(2).astype(BF16)
    ql_ref[...] = seg(3).astype(BF16)
    kl_ref[...] = seg(4).astype(BF16)
    vl_ref[...] = seg(5).astype(BF16)
    gl_ref[...] = seg(6).astype(BF16)
    gk_lo = _dot(hb, wlo_ref[...]) + blo_ref[...]
    gk = _dot3(gk_lo, wgk_ref[...]) + bgk_ref[...]
    log_sig = jnp.minimum(gk, 0.0) - jnp.log(1.0 + jnp.exp(-jnp.abs(gk)))
    lg_ref[...] = log_sig * (1.0 / GLA_GATE_NORM)


def _inproj(x2, mod3, posb, invf, w_main, b_main, w_lo, b_lo, w_gk, b_gk, *, seq, tm):
    N, D = x2.shape
    per_b = seq // tm
    widths = [ATTN_WIDTH, KV_WIDTH, KV_WIDTH, GLA_KWIDTH, GLA_KWIDTH, GLA_WIDTH, GLA_WIDTH, GLA_KWIDTH]
    dtypes = [BF16] * 7 + [F32]
    row = lambda i: (i, 0)
    return pl.pallas_call(
        _inproj_kernel,
        grid=(N // tm,),
        in_specs=[pl.BlockSpec((tm, D), row),
                  pl.BlockSpec((1, 6, D), lambda i: (i // per_b, 0, 0)),
                  pl.BlockSpec((tm, LANES), row),
                  _full(invf.shape), _full(w_main.shape), _full(b_main.shape),
                  _full(w_lo.shape), _full(b_lo.shape), _full(w_gk.shape), _full(b_gk.shape)],
        out_specs=[pl.BlockSpec((tm, w), row) for w in widths],
        out_shape=[jax.ShapeDtypeStruct((N, w), dt) for w, dt in zip(widths, dtypes)],
        compiler_params=_params(("parallel",)),
        name="inproj",
    )(x2, mod3, posb, invf, w_main, b_main, w_lo, b_lo, w_gk, b_gk)


def _swa_kernel(sink_ref, q_ref, kc_ref, kp_ref, vc_ref, vp_ref, o_ref):
    j = pl.program_id(1)
    blk = ATTN_BLOCK
    k2 = jnp.concatenate([kp_ref[...], kc_ref[...]], axis=0)
    v2 = jnp.concatenate([vp_ref[...], vc_ref[...]], axis=0)
    row = lax.broadcasted_iota(I32, (blk, 2 * blk), 0)
    col = lax.broadcasted_iota(I32, (blk, 2 * blk), 1)
    dist = row + blk - col
    valid = (dist >= 0) & (dist < blk) & ((col >= blk) | (j > 0))
    group = ATTN_Q_HEADS // ATTN_KV_HEADS
    for h in range(ATTN_Q_HEADS):
        kv = h // group
        qh = q_ref[:, h * ATTN_HEAD_DIM:(h + 1) * ATTN_HEAD_DIM]
        kh = k2[:, kv * ATTN_HEAD_DIM:(kv + 1) * ATTN_HEAD_DIM]
        vh = v2[:, kv * ATTN_HEAD_DIM:(kv + 1) * ATTN_HEAD_DIM]
        s = jnp.where(valid, _dot(qh, kh, NT), NEG_INF)
        sink = sink_ref[h]
        m = jnp.maximum(jnp.max(s, axis=-1, keepdims=True), sink)
        p = jnp.exp(s - m)
        denom = jnp.sum(p, axis=-1, keepdims=True) + jnp.exp(sink - m)
        o = _dot(p.astype(BF16), vh) / denom
        o_ref[:, h * ATTN_HEAD_DIM:(h + 1) * ATTN_HEAD_DIM] = o.astype(BF16)


def _swa(q, k, v, sinks, *, seq):
    N = q.shape[0]
    nb = seq // ATTN_BLOCK
    B = N // seq
    cur = lambda b, j: (b * nb + j, 0)
    prev = lambda b, j: (b * nb + jnp.maximum(j - 1, 0), 0)
    return pl.pallas_call(
        _swa_kernel,
        grid=(B, nb),
        in_specs=[pl.BlockSpec(memory_space=pltpu.SMEM),
                  pl.BlockSpec((ATTN_BLOCK, ATTN_WIDTH), cur),
                  pl.BlockSpec((ATTN_BLOCK, KV_WIDTH), cur),
                  pl.BlockSpec((ATTN_BLOCK, KV_WIDTH), prev),
                  pl.BlockSpec((ATTN_BLOCK, KV_WIDTH), cur),
                  pl.BlockSpec((ATTN_BLOCK, KV_WIDTH), prev)],
        out_specs=pl.BlockSpec((ATTN_BLOCK, ATTN_WIDTH), cur),
        out_shape=jax.ShapeDtypeStruct((N, ATTN_WIDTH), BF16),
        compiler_params=_params(("parallel", "parallel")),
        name="swa",
    )(sinks, q, k, k, v, v)


def _gla_kernel(q_ref, k_ref, v_ref, g_ref, lg_ref, ng_ref, o_ref, st_ref, *, chunks):
    @pl.when(pl.program_id(1) == 0)
    def _():
        st_ref[...] = jnp.zeros_like(st_ref)

    C = GLA_CHUNK
    dk, dv = GLA_KEY_DIM, GLA_VAL_DIM
    r = lax.broadcasted_iota(I32, (C, C), 0)
    c = lax.broadcasted_iota(I32, (C, C), 1)
    causal = c <= r
    tri = jnp.where(causal, 1.0, 0.0).astype(BF16)
    st = st_ref[...]
    for n in range(chunks):
        rows = slice(n * C, (n + 1) * C)
        lg_hi, lg_lo = _split_bf16(lg_ref[rows, :])
        b = _dot(tri, lg_hi) + _dot(tri, lg_lo)
        b_last = b[C - 1:C, :]
        q_in = (q_ref[rows, :].astype(F32) * (dk ** -0.5) * jnp.exp(b)).astype(BF16)
        kf = k_ref[rows, :].astype(F32)
        k_in = (kf * jnp.exp(-b)).astype(BF16)
        k_out = (kf * jnp.exp(b_last - b)).astype(BF16)
        decay = jnp.exp(b_last)
        stb = st.astype(BF16)
        ut = []
        for h in range(GLA_HEADS):
            ks = slice(h * dk, (h + 1) * dk)
            vs = slice(h * dv, (h + 1) * dv)
            vh = v_ref[rows, vs]
            a = jnp.where(causal, _dot(q_in[:, ks], k_in[:, ks], NT), 0.0).astype(BF16)
            o = _dot(a, vh) + _dot(q_in[:, ks], stb[:, ks], NT)
            ut.append(_dot(vh, k_out[:, ks], TN))
            o = o * lax.rsqrt(jnp.mean(o * o, axis=-1, keepdims=True) + LN_EPS) * ng_ref[...]
            o_ref[rows, vs] = (o * _silu(g_ref[rows, vs].astype(F32))).astype(BF16)
        st = st * decay + jnp.concatenate(ut, axis=1)
    st_ref[...] = st


def _gla(q, k, v, g, lg, norm_g, *, seq, tc):
    N = q.shape[0]
    B = N // seq
    per_b = seq // tc
    row = lambda b, j: (b * per_b + j, 0)
    return pl.pallas_call(
        functools.partial(_gla_kernel, chunks=tc // GLA_CHUNK),
        grid=(B, per_b),
        in_specs=[pl.BlockSpec((tc, GLA_KWIDTH), row), pl.BlockSpec((tc, GLA_KWIDTH), row),
                  pl.BlockSpec((tc, GLA_WIDTH), row), pl.BlockSpec((tc, GLA_WIDTH), row),
                  pl.BlockSpec((tc, GLA_KWIDTH), row), _full(norm_g.shape)],
        out_specs=pl.BlockSpec((tc, GLA_WIDTH), row),
        out_shape=jax.ShapeDtypeStruct((N, GLA_WIDTH), BF16),
        scratch_shapes=[pltpu.VMEM((GLA_VAL_DIM, GLA_KWIDTH), F32)],
        compiler_params=_params(("parallel", "arbitrary")),
        name="gla",
    )(q, k, v, g, lg, norm_g)


def _outproj_kernel(oa_ref, og_ref, x_ref, mod_ref, woa_ref, wog_ref, bo_ref, g_ref, b_ref, wrh_ref, wrl_ref,
                    x1_ref, h2_ref, lt_ref, *, alpha):
    y = _dot(oa_ref[...], woa_ref[...]) + _dot(og_ref[...], wog_ref[...]) + bo_ref[...]
    z = alpha * x_ref[...] + (1.0 + mod_ref[0, 2:3, :]) * y
    x1 = _ln_plain(z) * g_ref[...] + b_ref[...]
    x1_ref[...] = x1
    h2 = _ln_plain(x1) * (1.0 + mod_ref[0, 4:5, :]) + mod_ref[0, 3:4, :]
    h2_ref[...] = h2
    hh, hl = _split_bf16(h2)
    wh = wrh_ref[...]
    lt_ref[...] = _dot(wh, hh, NT) + (_dot(wh, hl, NT) + _dot(wrl_ref[...], hh, NT))


def _outproj(oa, og, x2, mod3, w_oa, w_og, b_o, ln_g, ln_b, wr_hi, wr_lo, *, seq, tm, alpha):
    N, D = x2.shape
    E = wr_hi.shape[0]
    per_b = seq // tm
    row = lambda i: (i, 0)
    return pl.pallas_call(
        functools.partial(_outproj_kernel, alpha=alpha),
        grid=(N // tm,),
        in_specs=[pl.BlockSpec((tm, ATTN_WIDTH), row), pl.BlockSpec((tm, GLA_WIDTH), row),
                  pl.BlockSpec((tm, D), row),
                  pl.BlockSpec((1, 6, D), lambda i: (i // per_b, 0, 0)),
                  _full(w_oa.shape), _full(w_og.shape), _full(b_o.shape), _full(ln_g.shape), _full(ln_b.shape),
                  _full(wr_hi.shape), _full(wr_lo.shape)],
        out_specs=[pl.BlockSpec((tm, D), row), pl.BlockSpec((tm, D), row),
                   pl.BlockSpec((E, tm), lambda i: (0, i))],
        out_shape=[jax.ShapeDtypeStruct((N, D), F32), jax.ShapeDtypeStruct((N, D), F32),
                   jax.ShapeDtypeStruct((E, N), F32)],
        compiler_params=_params(("parallel",)),
        name="outproj",
    )(oa, og, x2, mod3, w_oa, w_og, b_o, ln_g, ln_b, wr_hi, wr_lo)


def _first_index(eq, idx, size):
    return jnp.min(jnp.where(eq, idx, float(size)), axis=0, keepdims=True)


def _route_kernel(lt_ref, bias_ref, upper_ref, idx_ref, w_ref, wt_ref, rank_ref, cnt_ref, base_ref):
    @pl.when(pl.program_id(0) == 0)
    def _():
        base_ref[...] = jnp.zeros_like(base_ref)

    E, t = lt_ref.shape
    gsz = E // N_GROUPS
    scores = _sigmoid(lt_ref[...])
    biased = scores + bias_ref[:, 0:1]
    gi = lax.broadcasted_iota(I32, (gsz, t), 0).astype(F32)
    gs_rows = []
    for g in range(N_GROUPS):
        grp = biased[g * gsz:(g + 1) * gsz, :]
        m1 = jnp.max(grp, axis=0, keepdims=True)
        first = _first_index(grp == m1, gi, gsz)
        m2 = jnp.max(jnp.where(gi == first, REMOVED, grp), axis=0, keepdims=True)
        gs_rows.append(m1 + m2)
    gs = jnp.concatenate(gs_rows, axis=0)
    ri = lax.broadcasted_iota(I32, (N_GROUPS, t), 0).astype(F32)
    gsel = jnp.zeros((N_GROUPS, t), F32)
    for _ in range(TOPK_GROUPS):
        m = jnp.max(gs, axis=0, keepdims=True)
        hit = ri == _first_index(gs == m, ri, N_GROUPS)
        gsel = jnp.where(hit, 1.0, gsel)
        gs = jnp.where(hit, REMOVED, gs)
    cand = jnp.concatenate(
        [jnp.where(gsel[g:g + 1, :] > 0.5, biased[g * gsz:(g + 1) * gsz, :], NEG_INF) for g in range(N_GROUPS)],
        axis=0)
    ei = lax.broadcasted_iota(I32, (E, t), 0).astype(F32)
    idx_rows, w_rows, hits = [], [], []
    chosen = jnp.zeros((E, t), F32)
    for _ in range(TOP_K):
        m = jnp.max(cand, axis=0, keepdims=True)
        fi = _first_index(cand == m, ei, E)
        hit = ei == fi
        idx_rows.append(fi)
        w_rows.append(jnp.sum(jnp.where(hit, scores, 0.0), axis=0, keepdims=True))
        hits.append(hit)
        chosen = jnp.where(hit, 1.0, chosen)
        cand = jnp.where(hit, REMOVED, cand)
    w = jnp.concatenate(w_rows, axis=0)
    w = w / jnp.sum(w, axis=0, keepdims=True) * ROUTED_SCALE
    idx_ref[...] = jnp.concatenate(idx_rows, axis=0).astype(I32)
    w_ref[...] = w
    wpad = jnp.concatenate([w, jnp.zeros((LANES - TOP_K, t), F32)], axis=0)
    wt_ref[...] = wpad.T
    prefix = _dot(chosen.astype(BF16), upper_ref[...])
    pos = base_ref[:, 0:1] + prefix
    rank_ref[...] = jnp.concatenate(
        [jnp.sum(jnp.where(hit, pos, 0.0), axis=0, keepdims=True) for hit in hits], axis=0).astype(I32)
    base_ref[...] = base_ref[...] + jnp.sum(chosen, axis=1, keepdims=True)
    cnt_ref[...] = base_ref[...]


def _route(logits_t, bias_col, upper, *, tr):
    E, N = logits_t.shape
    col = lambda i: (0, i)
    return pl.pallas_call(
        _route_kernel,
        grid=(N // tr,),
        in_specs=[pl.BlockSpec((E, tr), col), _full(bias_col.shape), _full(upper.shape)],
        out_specs=[pl.BlockSpec((TOP_K, tr), col), pl.BlockSpec((TOP_K, tr), col),
                   pl.BlockSpec((tr, LANES), lambda i: (i, 0)),
                   pl.BlockSpec((TOP_K, tr), col), _full((E, LANES))],
        out_shape=[jax.ShapeDtypeStruct((TOP_K, N), I32), jax.ShapeDtypeStruct((TOP_K, N), F32),
                   jax.ShapeDtypeStruct((N, LANES), F32),
                   jax.ShapeDtypeStruct((TOP_K, N), I32), jax.ShapeDtypeStruct((E, LANES), F32)],
        scratch_shapes=[pltpu.VMEM((E, LANES), F32)],
        compiler_params=_params(("arbitrary",)),
        name="route",
    )(logits_t, bias_col, upper)


def _plan_kernel(cnt_ref, lower_ref, idx_ref, rank_ref, dest_ref, bexp_ref, nused_ref, pstart_ref, *, n_blk):
    E = cnt_ref.shape[0]
    bm = float(EXPERT_ROWS)

    @pl.when(pl.program_id(0) == 0)
    def _():
        cnt = cnt_ref[...]
        padded = jnp.floor((cnt + (bm - 1.0)) * (1.0 / bm)) * bm
        hi = jnp.floor(padded * (1.0 / 256.0))
        lo = padded - hi * 256.0
        low = lower_ref[...]
        pend = 256.0 * _dot(low, hi.astype(BF16)) + _dot(low, lo.astype(BF16))
        pstart_ref[...] = pend - padded
        blk0 = lax.broadcasted_iota(I32, (E, n_blk), 1).astype(F32) * bm
        n_le = jnp.sum(jnp.where(pend[:, 0:1] <= blk0, 1.0, 0.0), axis=0, keepdims=True)
        bexp_ref[...] = jnp.minimum(n_le, float(E - 1)).astype(I32)
        nused_ref[...] = (pend[E - 1:E, 0:1] * (1.0 / bm)).astype(I32)

    t = idx_ref.shape[1]
    ei = lax.broadcasted_iota(I32, (E, t), 0)
    ps = pstart_ref[:, 0:1]
    rows = []
    for k in range(TOP_K):
        hit = ei == idx_ref[k:k + 1, :]
        rows.append(jnp.sum(jnp.where(hit, ps, 0.0), axis=0, keepdims=True))
    dest_ref[...] = rank_ref[...] + jnp.concatenate(rows, axis=0).astype(I32)


def _plan(cnt, lower, idx, rank, *, tr, n_blk):
    E = cnt.shape[0]
    N = idx.shape[1]
    col = lambda i: (0, i)
    return pl.pallas_call(
        functools.partial(_plan_kernel, n_blk=n_blk),
        grid=(N // tr,),
        in_specs=[_full(cnt.shape), _full(lower.shape), pl.BlockSpec((TOP_K, tr), col),
                  pl.BlockSpec((TOP_K, tr), col)],
        out_specs=[pl.BlockSpec((TOP_K, tr), col), _full((1, n_blk)), _full((1, 1))],
        out_shape=[jax.ShapeDtypeStruct((TOP_K, N), I32), jax.ShapeDtypeStruct((1, n_blk), I32),
                   jax.ShapeDtypeStruct((1, 1), I32)],
        scratch_shapes=[pltpu.VMEM((E, LANES), F32)],
        compiler_params=_params(("arbitrary",)),
        name="plan",
    )(cnt, lower, idx, rank)


def _scatter_kernel(bexp_ref, nused_ref, dest_ref, h_ref, xs_ref, zero_ref, sem, zsem):
    ts = h_ref.shape[0]
    n_blk = bexp_ref.shape[0]

    @pl.when(pl.program_id(0) == 0)
    def _():
        zero_ref[...] = jnp.zeros_like(zero_ref)
        nused = nused_ref[0]

        def zstart(b, n):
            pad = (b >= nused - 1) | (bexp_ref[jnp.minimum(b + 1, n_blk - 1)] != bexp_ref[b])

            @pl.when(pad)
            def _():
                start = pl.multiple_of(b * EXPERT_ROWS, EXPERT_ROWS)
                pltpu.make_async_copy(zero_ref, xs_ref.at[pl.ds(start, EXPERT_ROWS)], zsem).start()

            return n + pad.astype(I32)

        n_zero = lax.fori_loop(0, n_blk, zstart, 0)

        def zwait(_, carry):
            pltpu.make_async_copy(zero_ref, xs_ref.at[pl.ds(0, EXPERT_ROWS)], zsem).wait()
            return carry

        lax.fori_loop(0, n_zero, zwait, 0)

    def body(t, carry):
        for k in range(TOP_K):
            pltpu.make_async_copy(h_ref.at[pl.ds(t, 1)], xs_ref.at[pl.ds(dest_ref[k, t], 1)], sem).start()
        return carry

    lax.fori_loop(0, ts, body, 0)
    for _ in range(TOP_K):
        pltpu.make_async_copy(h_ref, xs_ref.at[pl.ds(0, ts)], sem).wait()


def _scatter(bexp, nused, dest, h2, *, ts):
    N, W = h2.shape
    n_blk = bexp.shape[0]
    grid_spec = pltpu.PrefetchScalarGridSpec(
        num_scalar_prefetch=2,
        grid=(N // ts,),
        in_specs=[pl.BlockSpec((TOP_K, ts), lambda i, be, nu: (0, i), memory_space=pltpu.SMEM),
                  pl.BlockSpec((ts, W), lambda i, be, nu: (i, 0))],
        out_specs=pl.BlockSpec(memory_space=pl.ANY),
        scratch_shapes=[pltpu.VMEM((EXPERT_ROWS, W), h2.dtype), pltpu.SemaphoreType.DMA(()),
                        pltpu.SemaphoreType.DMA(())],
    )
    return pl.pallas_call(
        _scatter_kernel,
        grid_spec=grid_spec,
        out_shape=jax.ShapeDtypeStruct((n_blk * EXPERT_ROWS, W), h2.dtype),
        compiler_params=_params(("arbitrary",)),
        name="scatter",
    )(bexp, nused, dest, h2)


def _experts_kernel(bexp_ref, nused_ref, xs_ref, wg_ref, wu_ref, wd_ref, ys_ref, wgb_ref, wub_ref, wdb_ref):
    i = pl.program_id(0)
    used = i < nused_ref[0]
    fresh = (i == 0) | (bexp_ref[i] != bexp_ref[jnp.maximum(i - 1, 0)])

    @pl.when(used & fresh)
    def _():
        wgb_ref[...] = wg_ref[0].astype(BF16)
        wub_ref[...] = wu_ref[0].astype(BF16)
        wdb_ref[...] = wd_ref[0].astype(BF16)

    @pl.when(used)
    def _():
        x = xs_ref[...].astype(BF16)
        g = _dot(x, wgb_ref[...])
        u = _dot(x, wub_ref[...])
        ys_ref[...] = _dot((_silu(g) * u).astype(BF16), wdb_ref[...])

    @pl.when(jnp.logical_not(used))
    def _():
        ys_ref[...] = jnp.zeros_like(ys_ref)


def _experts(bexp, nused, xs, wg, wu, wd):
    P, W = xs.shape
    E, D, H = wg.shape
    n_blk = P // EXPERT_ROWS
    grid_spec = pltpu.PrefetchScalarGridSpec(
        num_scalar_prefetch=2,
        grid=(n_blk,),
        in_specs=[pl.BlockSpec((EXPERT_ROWS, W), lambda i, be, nu: (i, 0)),
                  pl.BlockSpec((1, D, H), lambda i, be, nu: (be[i], 0, 0)),
                  pl.BlockSpec((1, D, H), lambda i, be, nu: (be[i], 0, 0)),
                  pl.BlockSpec((1, H, D), lambda i, be, nu: (be[i], 0, 0))],
        out_specs=pl.BlockSpec((EXPERT_ROWS, D), lambda i, be, nu: (i, 0)),
        scratch_shapes=[pltpu.VMEM((D, H), BF16), pltpu.VMEM((D, H), BF16), pltpu.VMEM((H, D), BF16)],
    )
    return pl.pallas_call(
        _experts_kernel,
        grid_spec=grid_spec,
        out_shape=jax.ShapeDtypeStruct((P, D), F32),
        compiler_params=_params(("arbitrary",)),
        name="experts",
    )(bexp, nused, xs, wg, wu, wd)


def _combine_kernel(dest_ref, wt_ref, h2_ref, x1_ref, mod_ref, wsg_ref, wsu_ref, wsd_ref, g_ref, b_ref, ys_ref,
                    o_ref, buf_ref, sem, *, alpha):
    tc = h2_ref.shape[0]

    def body(t, carry):
        for k in range(TOP_K):
            pltpu.make_async_copy(ys_ref.at[pl.ds(dest_ref[k, t], 1)], buf_ref.at[k, pl.ds(t, 1)], sem).start()
        return carry

    lax.fori_loop(0, tc, body, 0)
    h = h2_ref[...].astype(BF16)
    mid = (_silu(_dot(h, wsg_ref[...])) * _dot(h, wsu_ref[...])).astype(BF16)
    y = _dot(mid, wsd_ref[...])
    for k in range(TOP_K):
        pltpu.make_async_copy(ys_ref.at[pl.ds(0, tc)], buf_ref.at[k], sem).wait()
    for k in range(TOP_K):
        y = y + wt_ref[:, k:k + 1] * buf_ref[k]
    z = alpha * x1_ref[...] + (1.0 + mod_ref[0, 5:6, :]) * y
    o_ref[...] = _ln_plain(z) * g_ref[...] + b_ref[...]


def _combine(dest, wt, h2, x1, mod3, wsg, wsu, wsd, ln_g, ln_b, ys, *, seq, tc, alpha):
    N, D = x1.shape
    per_b = seq // tc
    row = lambda i: (i, 0)
    return pl.pallas_call(
        functools.partial(_combine_kernel, alpha=alpha),
        grid=(N // tc,),
        in_specs=[pl.BlockSpec((TOP_K, tc), lambda i: (0, i), memory_space=pltpu.SMEM),
                  pl.BlockSpec((tc, LANES), row), pl.BlockSpec((tc, D), row), pl.BlockSpec((tc, D), row),
                  pl.BlockSpec((1, 6, D), lambda i: (i // per_b, 0, 0)),
                  _full(wsg.shape), _full(wsu.shape), _full(wsd.shape), _full(ln_g.shape), _full(ln_b.shape),
                  pl.BlockSpec(memory_space=pl.ANY)],
        out_specs=pl.BlockSpec((tc, D), row),
        out_shape=jax.ShapeDtypeStruct((N, D), F32),
        scratch_shapes=[pltpu.VMEM((TOP_K, tc, D), F32), pltpu.SemaphoreType.DMA(())],
        compiler_params=_params(("arbitrary",)),
        name="combine",
    )(dest, wt, h2, x1, mod3, wsg, wsu, wsd, ln_g, ln_b, ys)


def _tiles(seq):
    t = lambda want: min(want, seq)
    return dict(proj=t(512), gla=t(256), route=t(256), move=t(256))


def _layer(x2, mod3, posb, invf, p, *, seq, alpha):
    N, D = x2.shape
    tl = _tiles(seq)
    qa, ka, va, ql, kl, vl, gl, lg = _inproj(x2, mod3, posb, invf, p["w_main"], p["b_main"], p["w_lo"], p["b_lo"],
                                             p["w_gk"], p["b_gk"], seq=seq, tm=tl["proj"])
    oa = _swa(qa, ka, va, p["sinks"], seq=seq)
    og = _gla(ql, kl, vl, gl, lg, p["norm_g"], seq=seq, tc=tl["gla"])
    x1, h2, logits_t = _outproj(oa, og, x2, mod3, p["w_oa"], p["w_og"], p["b_o"], p["ln1_g"], p["ln1_b"],
                                p["wr_hi"], p["wr_lo"], seq=seq, tm=tl["proj"], alpha=alpha)
    idx, _, wt, rank, cnt = _route(logits_t, p["bias_col"], p["upper"], tr=tl["route"])
    E = cnt.shape[0]
    n_blk = (N * TOP_K) // EXPERT_ROWS + E
    dest, bexp, nused = _plan(cnt, p["lower"], idx, rank, tr=tl["route"], n_blk=n_blk)
    bexp, nused = bexp.reshape(n_blk), nused.reshape(1)
    xs = _scatter(bexp, nused, dest, h2, ts=tl["move"])
    ys = _experts(bexp, nused, xs, p["wg"], p["wu"], p["wd"])
    return _combine(dest, wt, h2, x1, mod3, p["wsg"], p["wsu"], p["wsd"], p["ln2_g"], p["ln2_b"], ys,
                    seq=seq, tc=tl["move"], alpha=alpha)


def kernel(x, c, positions, w_ada, b_ada, w_in, b_in, attn_sinks, w_gk2, b_gk2, gla_norm_g, w_o, b_o, ln1_g, ln1_b, w_router, router_bias, w_exp_gate, w_exp_up, w_exp_down, w_sh_gate, w_sh_up, w_sh_down, ln2_g, ln2_b):
    B, S, D = x.shape
    depth = w_ada.shape[0]
    E = w_router.shape[2]
    alpha = float((2 * depth) ** 0.25)
    tl = _tiles(S)
    n_main = int(_SEG[-1])

    posb = jnp.broadcast_to(positions.astype(F32).reshape(B * S, 1), (B * S, LANES))
    half = ATTN_HEAD_DIM // 2
    invf = (ROPE_THETA ** (-(jnp.arange(LANES) % half).astype(F32) / half)).reshape(1, LANES)
    tr = tl["route"]
    upper = (jnp.arange(tr)[:, None] < jnp.arange(tr)[None, :]).astype(BF16)
    lower = (jnp.arange(E)[:, None] >= jnp.arange(E)[None, :]).astype(BF16)
    row = lambda v: v.reshape(1, -1)

    x2 = x.reshape(B * S, D)
    for l in range(depth):
        wr_t = w_router[l].T
        wr_hi = wr_t.astype(BF16)
        p = dict(
            w_main=w_in[l][:, :n_main].astype(BF16), b_main=row(b_in[l][:n_main]),
            w_lo=jnp.pad(w_in[l][:, n_main:], ((0, 0), (0, LANES - GLA_GATE_RANK))).astype(BF16),
            b_lo=row(jnp.pad(b_in[l][n_main:], (0, LANES - GLA_GATE_RANK))),
            w_gk=jnp.pad(w_gk2[l], ((0, LANES - GLA_GATE_RANK), (0, 0))), b_gk=row(b_gk2[l]),
            sinks=attn_sinks[l], norm_g=row(gla_norm_g[l]),
            w_oa=w_o[l][:ATTN_WIDTH].astype(BF16), w_og=w_o[l][ATTN_WIDTH:].astype(BF16), b_o=row(b_o[l]),
            ln1_g=row(ln1_g[l]), ln1_b=row(ln1_b[l]),
            wr_hi=wr_hi, wr_lo=(wr_t - wr_hi.astype(F32)).astype(BF16),
            bias_col=jnp.broadcast_to(router_bias[l].reshape(E, 1), (E, LANES)),
            upper=upper, lower=lower,
            wg=w_exp_gate[l], wu=w_exp_up[l], wd=w_exp_down[l],
            wsg=w_sh_gate[l].astype(BF16), wsu=w_sh_up[l].astype(BF16), wsd=w_sh_down[l].astype(BF16),
            ln2_g=row(ln2_g[l]), ln2_b=row(ln2_b[l]),
        )
        mod = _mod(c, w_ada[l], b_ada[l])
        mod3 = mod.reshape(B, 6, D)
        x2 = _layer(x2, mod3, posb, invf, p, seq=S, alpha=alpha)
    return x2.reshape(B, S, D)
```

```python
import functools

import jax
import jax.numpy as jnp
import numpy as np
from jax import lax
from jax.experimental import pallas as pl
from jax.experimental.pallas import tpu as pltpu

F32 = jnp.float32
BF16 = jnp.bfloat16
I32 = jnp.int32

ATTN_Q_HEADS = 8
ATTN_KV_HEADS = 2
ATTN_HEAD_DIM = 64
ATTN_BLOCK = 128
ROPE_THETA = 10000.0
GLA_HEADS = 4
GLA_KEY_DIM = 64
GLA_VAL_DIM = 128
GLA_GATE_RANK = 16
GLA_GATE_NORM = 16.0
GLA_CHUNK = 64
N_GROUPS = 8
TOPK_GROUPS = 4
TOP_K = 8
ROUTED_SCALE = 2.5
LN_EPS = 1e-5
NEG_INF = -1e30
REMOVED = -3e38

ATTN_WIDTH = ATTN_Q_HEADS * ATTN_HEAD_DIM
KV_WIDTH = ATTN_KV_HEADS * ATTN_HEAD_DIM
GLA_KWIDTH = GLA_HEADS * GLA_KEY_DIM
GLA_WIDTH = GLA_HEADS * GLA_VAL_DIM

LANES = 128
VMEM_LIMIT = 56 * 1024 * 1024
EXPERT_ROWS = 256


def _params(sem):
    return pltpu.CompilerParams(dimension_semantics=sem, vmem_limit_bytes=VMEM_LIMIT)


def _full(shape):
    return pl.BlockSpec(shape, lambda *_: (0,) * len(shape))


def _split_bf16(a):
    hi = a.astype(BF16)
    lo = (a - hi.astype(F32)).astype(BF16)
    return hi, lo


def _dot(a, b, dims=(((1,), (0,)), ((), ()))):
    return lax.dot_general(a, b, dims, preferred_element_type=F32)


NT = (((1,), (1,)), ((), ()))
TN = (((0,), (0,)), ((), ()))


def _dot3(a, b, dims=(((1,), (0,)), ((), ()))):
    ah, al = _split_bf16(a)
    bh, bl = _split_bf16(b)
    return _dot(ah, bh, dims) + (_dot(ah, bl, dims) + _dot(al, bh, dims))


def _ln_plain(x):
    mu = jnp.mean(x, axis=-1, keepdims=True)
    xc = x - mu
    var = jnp.mean(xc * xc, axis=-1, keepdims=True)
    return xc * lax.rsqrt(var + LN_EPS)


def _silu(x):
    return x * (1.0 / (1.0 + jnp.exp(-x)))


def _sigmoid(x):
    return 1.0 / (1.0 + jnp.exp(-x))


def _mod_kernel(c_ref, w_ref, b_ref, o_ref):
    c = c_ref[...]
    o_ref[...] = _dot3(_silu(c), w_ref[...]) + b_ref[...]


def _mod(c, w_ada, b_ada):
    B, D = c.shape
    n = w_ada.shape[1] // D
    return pl.pallas_call(
        _mod_kernel,
        grid=(n,),
        in_specs=[_full((B, D)),
                  pl.BlockSpec((D, D), lambda j: (0, j)),
                  pl.BlockSpec((1, D), lambda j: (0, j))],
        out_specs=pl.BlockSpec((B, D), lambda j: (0, j)),
        out_shape=jax.ShapeDtypeStruct((B, n * D), F32),
        compiler_params=_params(("arbitrary",)),
        name="mod",
    )(c, w_ada, b_ada.reshape(1, -1))


_SEG = np.cumsum([0, ATTN_WIDTH, KV_WIDTH, KV_WIDTH, GLA_KWIDTH, GLA_KWIDTH, GLA_WIDTH, GLA_WIDTH])


def _rope_chunk(t, cos, sin_signed, first_half):
    up = pltpu.roll(t, LANES - 32, axis=1)
    dn = pltpu.roll(t, 32, axis=1)
    return t * cos + jnp.where(first_half, up, dn) * sin_signed


def _inproj_kernel(x_ref, mod_ref, pos_ref, invf_ref, w_ref, b_ref, wlo_ref, blo_ref, wgk_ref, bgk_ref,
                   qa_ref, ka_ref, va_ref, ql_ref, kl_ref, vl_ref, gl_ref, lg_ref):
    x = x_ref[...]
    h = _ln_plain(x) * (1.0 + mod_ref[0, 1:2, :]) + mod_ref[0, 0:1, :]
    hb = h.astype(BF16)

    def seg(i):
        lo, hi = int(_SEG[i]), int(_SEG[i + 1])
        return _dot(hb, w_ref[:, lo:hi]) + b_ref[:, lo:hi]

    ang = pos_ref[...] * invf_ref[...]
    cos = jnp.cos(ang)
    sin = jnp.sin(ang)
    lane = lax.broadcasted_iota(I32, ang.shape, 1)
    first_half = (lane % ATTN_HEAD_DIM) < (ATTN_HEAD_DIM // 2)
    sin_signed = jnp.where(first_half, -sin, sin)

    q = seg(0)
    scale = ATTN_HEAD_DIM ** -0.5
    for c in range(ATTN_WIDTH // LANES):
        t = q[:, c * LANES:(c + 1) * LANES]
        qa_ref[:, c * LANES:(c + 1) * LANES] = (_rope_chunk(t, cos, sin_signed, first_half) * scale).astype(BF16)
    ka_ref[...] = _rope_chunk(seg(1), cos, sin_signed, first_half).astype(BF16)
    va_ref[...] = seg(2).astype(BF16)
    ql_ref[...] = seg(3).astype(BF16)
    kl_ref[...] = seg(4).astype(BF16)
    vl_ref[...] = seg(5).astype(BF16)
    gl_ref[...] = seg(6).astype(BF16)
    gk_lo = _dot(hb, wlo_ref[...]) + blo_ref[...]
    gk = _dot3(gk_lo, wgk_ref[...]) + bgk_ref[...]
    log_sig = jnp.minimum(gk, 0.0) - jnp.log(1.0 + jnp.exp(-jnp.abs(gk)))
    lg_ref[...] = log_sig * (1.0 / GLA_GATE_NORM)


def _inproj(x2, mod3, posb, invf, w_main, b_main, w_lo, b_lo, w_gk, b_gk, *, seq, tm):
    N, D = x2.shape
    per_b = seq // tm
    widths = [ATTN_WIDTH, KV_WIDTH, KV_WIDTH, GLA_KWIDTH, GLA_KWIDTH, GLA_WIDTH, GLA_WIDTH, GLA_KWIDTH]
    dtypes = [BF16] * 7 + [F32]
    row = lambda i: (i, 0)
    return pl.pallas_call(
        _inproj_kernel,
        grid=(N // tm,),
        in_specs=[pl.BlockSpec((tm, D), row),
                  pl.BlockSpec((1, 6, D), lambda i: (i // per_b, 0, 0)),
                  pl.BlockSpec((tm, LANES), row),
                  _full(invf.shape), _full(w_main.shape), _full(b_main.shape),
                  _full(w_lo.shape), _full(b_lo.shape), _full(w_gk.shape), _full(b_gk.shape)],
        out_specs=[pl.BlockSpec((tm, w), row) for w in widths],
        out_shape=[jax.ShapeDtypeStruct((N, w), dt) for w, dt in zip(widths, dtypes)],
        compiler_params=_params(("parallel",)),
        name="inproj",
    )(x2, mod3, posb, invf, w_main, b_main, w_lo, b_lo, w_gk, b_gk)


def _swa_kernel(sink_ref, q_ref, kc_ref, kp_ref, vc_ref, vp_ref, o_ref):
    j = pl.program_id(1)
    blk = ATTN_BLOCK
    k2 = jnp.concatenate([kp_ref[...], kc_ref[...]], axis=0)
    v2 = jnp.concatenate([vp_ref[...], vc_ref[...]], axis=0)
    row = lax.broadcasted_iota(I32, (blk, 2 * blk), 0)
    col = lax.broadcasted_iota(I32, (blk, 2 * blk), 1)
    dist = row + blk - col
    valid = (dist >= 0) & (dist < blk) & ((col >= blk) | (j > 0))
    group = ATTN_Q_HEADS // ATTN_KV_HEADS
    for h in range(ATTN_Q_HEADS):
        kv = h // group
        qh = q_ref[:, h * ATTN_HEAD_DIM:(h + 1) * ATTN_HEAD_DIM]
        kh = k2[:, kv * ATTN_HEAD_DIM:(kv + 1) * ATTN_HEAD_DIM]
        vh = v2[:, kv * ATTN_HEAD_DIM:(kv + 1) * ATTN_HEAD_DIM]
        s = jnp.where(valid, _dot(qh, kh, NT), NEG_INF)
        sink = sink_ref[h]
        m = jnp.maximum(jnp.max(s, axis=-1, keepdims=True), sink)
        p = jnp.exp(s - m)
        denom = jnp.sum(p, axis=-1, keepdims=True) + jnp.exp(sink - m)
        o = _dot(p.astype(BF16), vh) / denom
        o_ref[:, h * ATTN_HEAD_DIM:(h + 1) * ATTN_HEAD_DIM] = o.astype(BF16)


def _swa(q, k, v, sinks, *, seq):
    N = q.shape[0]
    nb = seq // ATTN_BLOCK
    B = N // seq
    cur = lambda b, j: (b * nb + j, 0)
    prev = lambda b, j: (b * nb + jnp.maximum(j - 1, 0), 0)
    return pl.pallas_call(
        _swa_kernel,
        grid=(B, nb),
        in_specs=[pl.BlockSpec(memory_space=pltpu.SMEM),
                  pl.BlockSpec((ATTN_BLOCK, ATTN_WIDTH), cur),
                  pl.BlockSpec((ATTN_BLOCK, KV_WIDTH), cur),
                  pl.BlockSpec((ATTN_BLOCK, KV_WIDTH), prev),
                  pl.BlockSpec((ATTN_BLOCK, KV_WIDTH), cur),
                  pl.BlockSpec((ATTN_BLOCK, KV_WIDTH), prev)],
        out_specs=pl.BlockSpec((ATTN_BLOCK, ATTN_WIDTH), cur),
        out_shape=jax.ShapeDtypeStruct((N, ATTN_WIDTH), BF16),
        compiler_params=_params(("parallel", "parallel")),
        name="swa",
    )(sinks, q, k, k, v, v)


def _gla_kernel(q_ref, k_ref, v_ref, g_ref, lg_ref, ng_ref, o_ref, st_ref, *, chunks):
    @pl.when(pl.program_id(1) == 0)
    def _():
        st_ref[...] = jnp.zeros_like(st_ref)

    C = GLA_CHUNK
    dk, dv = GLA_KEY_DIM, GLA_VAL_DIM
    r = lax.broadcasted_iota(I32, (C, C), 0)
    c = lax.broadcasted_iota(I32, (C, C), 1)
    causal = c <= r
    tri = jnp.where(causal, 1.0, 0.0).astype(BF16)
    st = st_ref[...]
    for n in range(chunks):
        rows = slice(n * C, (n + 1) * C)
        lg_hi, lg_lo = _split_bf16(lg_ref[rows, :])
        b = _dot(tri, lg_hi) + _dot(tri, lg_lo)
        b_last = b[C - 1:C, :]
        q_in = (q_ref[rows, :].astype(F32) * (dk ** -0.5) * jnp.exp(b)).astype(BF16)
        kf = k_ref[rows, :].astype(F32)
        k_in = (kf * jnp.exp(-b)).astype(BF16)
        k_out = (kf * jnp.exp(b_last - b)).astype(BF16)
        decay = jnp.exp(b_last)
        stb = st.astype(BF16)
        ut = []
        for h in range(GLA_HEADS):
            ks = slice(h * dk, (h + 1) * dk)
            vs = slice(h * dv, (h + 1) * dv)
            vh = v_ref[rows, vs]
            a = jnp.where(causal, _dot(q_in[:, ks], k_in[:, ks], NT), 0.0).astype(BF16)
            o = _dot(a, vh) + _dot(q_in[:, ks], stb[:, ks], NT)
            ut.append(_dot(vh, k_out[:, ks], TN))
            o = o * lax.rsqrt(jnp.mean(o * o, axis=-1, keepdims=True) + LN_EPS) * ng_ref[...]
            o_ref[rows, vs] = (o * _silu(g_ref[rows, vs].astype(F32))).astype(BF16)
        st = st * decay + jnp.concatenate(ut, axis=1)
    st_ref[...] = st


def _gla(q, k, v, g, lg, norm_g, *, seq, tc):
    N = q.shape[0]
    B = N // seq
    per_b = seq // tc
    row = lambda b, j: (b * per_b + j, 0)
    return pl.pallas_call(
        functools.partial(_gla_kernel, chunks=tc // GLA_CHUNK),
        grid=(B, per_b),
        in_specs=[pl.BlockSpec((tc, GLA_KWIDTH), row), pl.BlockSpec((tc, GLA_KWIDTH), row),
                  pl.BlockSpec((tc, GLA_WIDTH), row), pl.BlockSpec((tc, GLA_WIDTH), row),
                  pl.BlockSpec((tc, GLA_KWIDTH), row), _full(norm_g.shape)],
        out_specs=pl.BlockSpec((tc, GLA_WIDTH), row),
        out_shape=jax.ShapeDtypeStruct((N, GLA_WIDTH), BF16),
        scratch_shapes=[pltpu.VMEM((GLA_VAL_DIM, GLA_KWIDTH), F32)],
        compiler_params=_params(("parallel", "arbitrary")),
        name="gla",
    )(q, k, v, g, lg, norm_g)


def _pairs_scratch(rows, d):
    return pltpu.VMEM((d // 2 // LANES, 2 * rows, LANES), F32)


def _store_pairs(pairs_ref, stage_ref, v):
    rows, d = v.shape
    w = d // 2
    for c in range(w // LANES):
        stage_ref[c, pl.ds(0, rows, stride=2), :] = v[:, c * LANES:(c + 1) * LANES]
        stage_ref[c, pl.ds(1, rows, stride=2), :] = v[:, w + c * LANES:w + (c + 1) * LANES]
        pairs_ref[:, c * LANES:(c + 1) * LANES] = stage_ref[c].astype(pairs_ref.dtype)


def _load_pairs(stage_ref, pairs):
    chunks, rows2, _ = stage_ref.shape
    rows = rows2 // 2
    lo, hi = [], []
    for c in range(chunks):
        stage_ref[c] = pairs[:, c * LANES:(c + 1) * LANES].astype(F32)
        lo.append(stage_ref[c, pl.ds(0, rows, stride=2), :])
        hi.append(stage_ref[c, pl.ds(1, rows, stride=2), :])
    return jnp.concatenate(lo + hi, axis=1)


def _outproj_kernel(oa_ref, og_ref, x_ref, mod_ref, woa_ref, wog_ref, bo_ref, g_ref, b_ref, wrh_ref, wrl_ref,
                    x1_ref, h2_ref, h2p_ref, lt_ref, stage_ref, *, alpha):
    y = _dot(oa_ref[...], woa_ref[...]) + _dot(og_ref[...], wog_ref[...]) + bo_ref[...]
    z = alpha * x_ref[...] + (1.0 + mod_ref[0, 2:3, :]) * y
    x1 = _ln_plain(z) * g_ref[...] + b_ref[...]
    x1_ref[...] = x1
    h2 = _ln_plain(x1) * (1.0 + mod_ref[0, 4:5, :]) + mod_ref[0, 3:4, :]
    h2_ref[...] = h2.astype(BF16)
    _store_pairs(h2p_ref, stage_ref, h2)
    hh, hl = _split_bf16(h2)
    wh = wrh_ref[...]
    lt_ref[...] = _dot(wh, hh, NT) + (_dot(wh, hl, NT) + _dot(wrl_ref[...], hh, NT))


def _outproj(oa, og, x2, mod3, w_oa, w_og, b_o, ln_g, ln_b, wr_hi, wr_lo, *, seq, tm, alpha):
    N, D = x2.shape
    E = wr_hi.shape[0]
    per_b = seq // tm
    row = lambda i: (i, 0)
    return pl.pallas_call(
        functools.partial(_outproj_kernel, alpha=alpha),
        grid=(N // tm,),
        in_specs=[pl.BlockSpec((tm, ATTN_WIDTH), row), pl.BlockSpec((tm, GLA_WIDTH), row),
                  pl.BlockSpec((tm, D), row),
                  pl.BlockSpec((1, 6, D), lambda i: (i // per_b, 0, 0)),
                  _full(w_oa.shape), _full(w_og.shape), _full(b_o.shape), _full(ln_g.shape), _full(ln_b.shape),
                  _full(wr_hi.shape), _full(wr_lo.shape)],
        out_specs=[pl.BlockSpec((tm, D), row), pl.BlockSpec((tm, D), row), pl.BlockSpec((2 * tm, D // 2), row),
                   pl.BlockSpec((E, tm), lambda i: (0, i))],
        out_shape=[jax.ShapeDtypeStruct((N, D), F32), jax.ShapeDtypeStruct((N, D), BF16),
                   jax.ShapeDtypeStruct((2 * N, D // 2), BF16), jax.ShapeDtypeStruct((E, N), F32)],
        scratch_shapes=[_pairs_scratch(tm, D)],
        compiler_params=_params(("parallel",)),
        name="outproj",
    )(oa, og, x2, mod3, w_oa, w_og, b_o, ln_g, ln_b, wr_hi, wr_lo)


def _first_index(eq, idx, size):
    return jnp.min(jnp.where(eq, idx, float(size)), axis=0, keepdims=True)


def _route_kernel(lt_ref, bias_ref, upper_ref, idx_ref, w_ref, wt_ref, rank_ref, cnt_ref, base_ref):
    @pl.when(pl.program_id(0) == 0)
    def _():
        base_ref[...] = jnp.zeros_like(base_ref)

    E, t = lt_ref.shape
    gsz = E // N_GROUPS
    scores = _sigmoid(lt_ref[...])
    biased = scores + bias_ref[:, 0:1]
    gi = lax.broadcasted_iota(I32, (gsz, t), 0).astype(F32)
    gs_rows = []
    for g in range(N_GROUPS):
        grp = biased[g * gsz:(g + 1) * gsz, :]
        m1 = jnp.max(grp, axis=0, keepdims=True)
        first = _first_index(grp == m1, gi, gsz)
        m2 = jnp.max(jnp.where(gi == first, REMOVED, grp), axis=0, keepdims=True)
        gs_rows.append(m1 + m2)
    gs = jnp.concatenate(gs_rows, axis=0)
    ri = lax.broadcasted_iota(I32, (N_GROUPS, t), 0).astype(F32)
    gsel = jnp.zeros((N_GROUPS, t), F32)
    for _ in range(TOPK_GROUPS):
        m = jnp.max(gs, axis=0, keepdims=True)
        hit = ri == _first_index(gs == m, ri, N_GROUPS)
        gsel = jnp.where(hit, 1.0, gsel)
        gs = jnp.where(hit, REMOVED, gs)
    cand = jnp.concatenate(
        [jnp.where(gsel[g:g + 1, :] > 0.5, biased[g * gsz:(g + 1) * gsz, :], NEG_INF) for g in range(N_GROUPS)],
        axis=0)
    ei = lax.broadcasted_iota(I32, (E, t), 0).astype(F32)
    idx_rows, w_rows, hits = [], [], []
    chosen = jnp.zeros((E, t), F32)
    for _ in range(TOP_K):
        m = jnp.max(cand, axis=0, keepdims=True)
        fi = _first_index(cand == m, ei, E)
        hit = ei == fi
        idx_rows.append(fi)
        w_rows.append(jnp.sum(jnp.where(hit, scores, 0.0), axis=0, keepdims=True))
        hits.append(hit)
        chosen = jnp.where(hit, 1.0, chosen)
        cand = jnp.where(hit, REMOVED, cand)
    w = jnp.concatenate(w_rows, axis=0)
    w = w / jnp.sum(w, axis=0, keepdims=True) * ROUTED_SCALE
    idx_ref[...] = jnp.concatenate(idx_rows, axis=0).astype(I32)
    w_ref[...] = w
    wpad = jnp.concatenate([w, jnp.zeros((LANES - TOP_K, t), F32)], axis=0)
    wt = wpad.T
    wt_ref[pl.ds(0, t, stride=2), :] = wt
    wt_ref[pl.ds(1, t, stride=2), :] = wt
    prefix = _dot(chosen.astype(BF16), upper_ref[...])
    pos = base_ref[:, 0:1] + prefix
    rank_ref[...] = jnp.concatenate(
        [jnp.sum(jnp.where(hit, pos, 0.0), axis=0, keepdims=True) for hit in hits], axis=0).astype(I32)
    base_ref[...] = base_ref[...] + jnp.sum(chosen, axis=1, keepdims=True)
    cnt_ref[...] = base_ref[...]


def _route(logits_t, bias_col, upper, *, tr):
    E, N = logits_t.shape
    col = lambda i: (0, i)
    return pl.pallas_call(
        _route_kernel,
        grid=(N // tr,),
        in_specs=[pl.BlockSpec((E, tr), col), _full(bias_col.shape), _full(upper.shape)],
        out_specs=[pl.BlockSpec((TOP_K, tr), col), pl.BlockSpec((TOP_K, tr), col),
                   pl.BlockSpec((2 * tr, LANES), lambda i: (i, 0)),
                   pl.BlockSpec((TOP_K, tr), col), _full((E, LANES))],
        out_shape=[jax.ShapeDtypeStruct((TOP_K, N), I32), jax.ShapeDtypeStruct((TOP_K, N), F32),
                   jax.ShapeDtypeStruct((2 * N, LANES), F32),
                   jax.ShapeDtypeStruct((TOP_K, N), I32), jax.ShapeDtypeStruct((E, LANES), F32)],
        scratch_shapes=[pltpu.VMEM((E, LANES), F32)],
        compiler_params=_params(("arbitrary",)),
        name="route",
    )(logits_t, bias_col, upper)


def _plan_kernel(cnt_ref, lower_ref, idx_ref, rank_ref, dest_ref, bexp_ref, nused_ref, pstart_ref, *, n_blk):
    E = cnt_ref.shape[0]
    bm = float(EXPERT_ROWS)

    @pl.when(pl.program_id(0) == 0)
    def _():
        cnt = cnt_ref[...]
        padded = jnp.floor((cnt + (bm - 1.0)) * (1.0 / bm)) * bm
        hi = jnp.floor(padded * (1.0 / 256.0))
        lo = padded - hi * 256.0
        low = lower_ref[...]
        pend = 256.0 * _dot(low, hi.astype(BF16)) + _dot(low, lo.astype(BF16))
        pstart_ref[...] = pend - padded
        blk0 = lax.broadcasted_iota(I32, (E, n_blk), 1).astype(F32) * bm
        n_le = jnp.sum(jnp.where(pend[:, 0:1] <= blk0, 1.0, 0.0), axis=0, keepdims=True)
        bexp_ref[...] = jnp.minimum(n_le, float(E - 1)).astype(I32)
        nused_ref[...] = (pend[E - 1:E, 0:1] * (1.0 / bm)).astype(I32)

    t = idx_ref.shape[1]
    ei = lax.broadcasted_iota(I32, (E, t), 0)
    ps = pstart_ref[:, 0:1]
    rows = []
    for k in range(TOP_K):
        hit = ei == idx_ref[k:k + 1, :]
        rows.append(jnp.sum(jnp.where(hit, ps, 0.0), axis=0, keepdims=True))
    dest_ref[...] = rank_ref[...] + jnp.concatenate(rows, axis=0).astype(I32)


def _plan(cnt, lower, idx, rank, *, tr, n_blk):
    E = cnt.shape[0]
    N = idx.shape[1]
    col = lambda i: (0, i)
    return pl.pallas_call(
        functools.partial(_plan_kernel, n_blk=n_blk),
        grid=(N // tr,),
        in_specs=[_full(cnt.shape), _full(lower.shape), pl.BlockSpec((TOP_K, tr), col),
                  pl.BlockSpec((TOP_K, tr), col)],
        out_specs=[pl.BlockSpec((TOP_K, tr), col), _full((1, n_blk)), _full((1, 1))],
        out_shape=[jax.ShapeDtypeStruct((TOP_K, N), I32), jax.ShapeDtypeStruct((1, n_blk), I32),
                   jax.ShapeDtypeStruct((1, 1), I32)],
        scratch_shapes=[pltpu.VMEM((E, LANES), F32)],
        compiler_params=_params(("arbitrary",)),
        name="plan",
    )(cnt, lower, idx, rank)


def _scatter_kernel(bexp_ref, nused_ref, dest_ref, h_ref, xs_ref, zero_ref, sem, zsem):
    ts = h_ref.shape[0] // 2
    n_blk = bexp_ref.shape[0]
    blk = zero_ref.shape[0]

    @pl.when(pl.program_id(0) == 0)
    def _():
        zero_ref[...] = jnp.zeros_like(zero_ref)
        nused = nused_ref[0]

        def zstart(b, n):
            pad = (b >= nused - 1) | (bexp_ref[jnp.minimum(b + 1, n_blk - 1)] != bexp_ref[b])

            @pl.when(pad)
            def _():
                start = pl.multiple_of(b * blk, blk)
                pltpu.make_async_copy(zero_ref, xs_ref.at[pl.ds(start, blk)], zsem).start()

            return n + pad.astype(I32)

        n_zero = lax.fori_loop(0, n_blk, zstart, 0)

        def zwait(_, carry):
            pltpu.make_async_copy(zero_ref, xs_ref.at[pl.ds(0, blk)], zsem).wait()
            return carry

        lax.fori_loop(0, n_zero, zwait, 0)

    def body(t, carry):
        src = h_ref.at[pl.ds(pl.multiple_of(2 * t, 2), 2)]
        for k in range(TOP_K):
            dst = xs_ref.at[pl.ds(pl.multiple_of(2 * dest_ref[k, t], 2), 2)]
            pltpu.make_async_copy(src, dst, sem).start(priority=k % 2)
        return carry

    lax.fori_loop(0, ts, body, 0)
    for _ in range(TOP_K):
        pltpu.make_async_copy(h_ref, xs_ref.at[pl.ds(0, 2 * ts)], sem).wait()


def _scatter(bexp, nused, dest, h2p, *, ts):
    rows, W = h2p.shape
    n_blk = bexp.shape[0]
    grid_spec = pltpu.PrefetchScalarGridSpec(
        num_scalar_prefetch=2,
        grid=(rows // (2 * ts),),
        in_specs=[pl.BlockSpec((TOP_K, ts), lambda i, be, nu: (0, i), memory_space=pltpu.SMEM),
                  pl.BlockSpec((2 * ts, W), lambda i, be, nu: (i, 0))],
        out_specs=pl.BlockSpec(memory_space=pl.ANY),
        scratch_shapes=[pltpu.VMEM((2 * EXPERT_ROWS, W), h2p.dtype), pltpu.SemaphoreType.DMA(()),
                        pltpu.SemaphoreType.DMA(())],
    )
    return pl.pallas_call(
        _scatter_kernel,
        grid_spec=grid_spec,
        out_shape=jax.ShapeDtypeStruct((n_blk * 2 * EXPERT_ROWS, W), h2p.dtype),
        compiler_params=_params(("arbitrary",)),
        name="scatter",
    )(bexp, nused, dest, h2p)


def _experts_kernel(bexp_ref, nused_ref, xs_ref, wg_ref, wu_ref, wd_ref, ys_ref, wgb_ref, wub_ref, wdb_ref,
                    xstage_ref, ystage_ref):
    i = pl.program_id(0)
    used = i < nused_ref[0]
    fresh = (i == 0) | (bexp_ref[i] != bexp_ref[jnp.maximum(i - 1, 0)])

    @pl.when(used & fresh)
    def _():
        wgb_ref[...] = wg_ref[0].astype(BF16)
        wub_ref[...] = wu_ref[0].astype(BF16)
        wdb_ref[...] = wd_ref[0].astype(BF16)

    @pl.when(used)
    def _():
        x = _load_pairs(xstage_ref, xs_ref[...]).astype(BF16)
        g = _dot(x, wgb_ref[...])
        u = _dot(x, wub_ref[...])
        _store_pairs(ys_ref, ystage_ref, _dot((_silu(g) * u).astype(BF16), wdb_ref[...]))

    @pl.when(jnp.logical_not(used))
    def _():
        ys_ref[...] = jnp.zeros_like(ys_ref)


def _experts(bexp, nused, xs, wg, wu, wd):
    rows, W = xs.shape
    E, D, H = wg.shape
    blk = 2 * EXPERT_ROWS
    grid_spec = pltpu.PrefetchScalarGridSpec(
        num_scalar_prefetch=2,
        grid=(rows // blk,),
        in_specs=[pl.BlockSpec((blk, W), lambda i, be, nu: (i, 0)),
                  pl.BlockSpec((1, D, H), lambda i, be, nu: (be[i], 0, 0)),
                  pl.BlockSpec((1, D, H), lambda i, be, nu: (be[i], 0, 0)),
                  pl.BlockSpec((1, H, D), lambda i, be, nu: (be[i], 0, 0))],
        out_specs=pl.BlockSpec((blk, W), lambda i, be, nu: (i, 0)),
        scratch_shapes=[pltpu.VMEM((D, H), BF16), pltpu.VMEM((D, H), BF16), pltpu.VMEM((H, D), BF16),
                        _pairs_scratch(EXPERT_ROWS, D), _pairs_scratch(EXPERT_ROWS, D)],
    )
    return pl.pallas_call(
        _experts_kernel,
        grid_spec=grid_spec,
        out_shape=jax.ShapeDtypeStruct((rows, W), BF16),
        compiler_params=_params(("arbitrary",)),
        name="experts",
    )(bexp, nused, xs, wg, wu, wd)


def _combine_kernel(dest_ref, wt_ref, h2_ref, x1_ref, mod_ref, wsg_ref, wsu_ref, wsd_ref, g_ref, b_ref, ys_ref,
                    o_ref, buf_ref, stage_ref, sem, *, alpha):
    tc = h2_ref.shape[0]

    def body(t, carry):
        for k in range(TOP_K):
            src = ys_ref.at[pl.ds(pl.multiple_of(2 * dest_ref[k, t], 2), 2)]
            dst = buf_ref.at[k, pl.ds(pl.multiple_of(2 * t, 2), 2)]
            pltpu.make_async_copy(src, dst, sem).start(priority=k % 2)
        return carry

    lax.fori_loop(0, tc, body, 0)
    h = h2_ref[...]
    mid = (_silu(_dot(h, wsg_ref[...])) * _dot(h, wsu_ref[...])).astype(BF16)
    y = _dot(mid, wsd_ref[...])
    for k in range(TOP_K):
        pltpu.make_async_copy(ys_ref.at[pl.ds(0, 2 * tc)], buf_ref.at[k], sem).wait()
    routed = wt_ref[:, 0:1] * buf_ref[0].astype(F32)
    for k in range(1, TOP_K):
        routed = routed + wt_ref[:, k:k + 1] * buf_ref[k].astype(F32)
    y = y + _load_pairs(stage_ref, routed)
    z = alpha * x1_ref[...] + (1.0 + mod_ref[0, 5:6, :]) * y
    o_ref[...] = _ln_plain(z) * g_ref[...] + b_ref[...]


def _combine(dest, wt, h2, x1, mod3, wsg, wsu, wsd, ln_g, ln_b, ys, *, seq, tc, alpha):
    N, D = x1.shape
    per_b = seq // tc
    row = lambda i: (i, 0)
    return pl.pallas_call(
        functools.partial(_combine_kernel, alpha=alpha),
        grid=(N // tc,),
        in_specs=[pl.BlockSpec((TOP_K, tc), lambda i: (0, i), memory_space=pltpu.SMEM),
                  pl.BlockSpec((2 * tc, LANES), row), pl.BlockSpec((tc, D), row), pl.BlockSpec((tc, D), row),
                  pl.BlockSpec((1, 6, D), lambda i: (i // per_b, 0, 0)),
                  _full(wsg.shape), _full(wsu.shape), _full(wsd.shape), _full(ln_g.shape), _full(ln_b.shape),
                  pl.BlockSpec(memory_space=pl.ANY)],
        out_specs=pl.BlockSpec((tc, D), row),
        out_shape=jax.ShapeDtypeStruct((N, D), F32),
        scratch_shapes=[pltpu.VMEM((TOP_K, 2 * tc, D // 2), ys.dtype), _pairs_scratch(tc, D),
                        pltpu.SemaphoreType.DMA(())],
        compiler_params=_params(("arbitrary",)),
        name="combine",
    )(dest, wt, h2, x1, mod3, wsg, wsu, wsd, ln_g, ln_b, ys)


def _tiles(seq):
    t = lambda want: min(want, seq)
    return dict(proj=t(512), gla=t(256), route=t(256), move=t(256))


def _layer(x2, mod3, posb, invf, p, *, seq, alpha):
    N, D = x2.shape
    tl = _tiles(seq)
    qa, ka, va, ql, kl, vl, gl, lg = _inproj(x2, mod3, posb, invf, p["w_main"], p["b_main"], p["w_lo"], p["b_lo"],
                                             p["w_gk"], p["b_gk"], seq=seq, tm=tl["proj"])
    oa = _swa(qa, ka, va, p["sinks"], seq=seq)
    og = _gla(ql, kl, vl, gl, lg, p["norm_g"], seq=seq, tc=tl["gla"])
    x1, h2, h2p, logits_t = _outproj(oa, og, x2, mod3, p["w_oa"], p["w_og"], p["b_o"], p["ln1_g"], p["ln1_b"],
                                     p["wr_hi"], p["wr_lo"], seq=seq, tm=tl["proj"], alpha=alpha)
    idx, _, wt, rank, cnt = _route(logits_t, p["bias_col"], p["upper"], tr=tl["route"])
    E = cnt.shape[0]
    n_blk = (N * TOP_K) // EXPERT_ROWS + E
    dest, bexp, nused = _plan(cnt, p["lower"], idx, rank, tr=tl["route"], n_blk=n_blk)
    bexp, nused = bexp.reshape(n_blk), nused.reshape(1)
    xs = _scatter(bexp, nused, dest, h2p, ts=tl["move"])
    ys = _experts(bexp, nused, xs, p["wg"], p["wu"], p["wd"])
    return _combine(dest, wt, h2, x1, mod3, p["wsg"], p["wsu"], p["wsd"], p["ln2_g"], p["ln2_b"], ys,
                    seq=seq, tc=tl["move"], alpha=alpha)


def kernel(x, c, positions, w_ada, b_ada, w_in, b_in, attn_sinks, w_gk2, b_gk2, gla_norm_g, w_o, b_o, ln1_g, ln1_b, w_router, router_bias, w_exp_gate, w_exp_up, w_exp_down, w_sh_gate, w_sh_up, w_sh_down, ln2_g, ln2_b):
    B, S, D = x.shape
    depth = w_ada.shape[0]
    E = w_router.shape[2]
    alpha = float((2 * depth) ** 0.25)
    tl = _tiles(S)
    n_main = int(_SEG[-1])

    posb = jnp.broadcast_to(positions.astype(F32).reshape(B * S, 1), (B * S, LANES))
    half = ATTN_HEAD_DIM // 2
    invf = (ROPE_THETA ** (-(jnp.arange(LANES) % half).astype(F32) / half)).reshape(1, LANES)
    tr = tl["route"]
    upper = (jnp.arange(tr)[:, None] < jnp.arange(tr)[None, :]).astype(BF16)
    lower = (jnp.arange(E)[:, None] >= jnp.arange(E)[None, :]).astype(BF16)
    row = lambda v: v.reshape(1, -1)

    x2 = x.reshape(B * S, D)
    for l in range(depth):
        wr_t = w_router[l].T
        wr_hi = wr_t.astype(BF16)
        p = dict(
            w_main=w_in[l][:, :n_main].astype(BF16), b_main=row(b_in[l][:n_main]),
            w_lo=jnp.pad(w_in[l][:, n_main:], ((0, 0), (0, LANES - GLA_GATE_RANK))).astype(BF16),
            b_lo=row(jnp.pad(b_in[l][n_main:], (0, LANES - GLA_GATE_RANK))),
            w_gk=jnp.pad(w_gk2[l], ((0, LANES - GLA_GATE_RANK), (0, 0))), b_gk=row(b_gk2[l]),
            sinks=attn_sinks[l], norm_g=row(gla_norm_g[l]),
            w_oa=w_o[l][:ATTN_WIDTH].astype(BF16), w_og=w_o[l][ATTN_WIDTH:].astype(BF16), b_o=row(b_o[l]),
            ln1_g=row(ln1_g[l]), ln1_b=row(ln1_b[l]),
            wr_hi=wr_hi, wr_lo=(wr_t - wr_hi.astype(F32)).astype(BF16),
            bias_col=jnp.broadcast_to(router_bias[l].reshape(E, 1), (E, LANES)),
            upper=upper, lower=lower,
            wg=w_exp_gate[l], wu=w_exp_up[l], wd=w_exp_down[l],
            wsg=w_sh_gate[l].astype(BF16), wsu=w_sh_up[l].astype(BF16), wsd=w_sh_down[l].astype(BF16),
            ln2_g=row(ln2_g[l]), ln2_b=row(ln2_b[l]),
        )
        mod = _mod(c, w_ada[l], b_ada[l])
        mod3 = mod.reshape(B, 6, D)
        x2 = _layer(x2, mod3, posb, invf, p, seq=S, alpha=alpha)
    return x2.reshape(B, S, D)
```

```python
import functools

import jax
import jax.numpy as jnp
import numpy as np
from jax import lax
from jax.experimental import pallas as pl
from jax.experimental.pallas import tpu as pltpu

F32 = jnp.float32
BF16 = jnp.bfloat16
I32 = jnp.int32

ATTN_Q_HEADS = 8
ATTN_KV_HEADS = 2
ATTN_HEAD_DIM = 64
ATTN_BLOCK = 128
ROPE_THETA = 10000.0
GLA_HEADS = 4
GLA_KEY_DIM = 64
GLA_VAL_DIM = 128
GLA_GATE_RANK = 16
GLA_GATE_NORM = 16.0
GLA_CHUNK = 64
N_GROUPS = 8
TOPK_GROUPS = 4
TOP_K = 8
ROUTED_SCALE = 2.5
LN_EPS = 1e-5
NEG_INF = -1e30
REMOVED = -3e38

ATTN_WIDTH = ATTN_Q_HEADS * ATTN_HEAD_DIM
KV_WIDTH = ATTN_KV_HEADS * ATTN_HEAD_DIM
GLA_KWIDTH = GLA_HEADS * GLA_KEY_DIM
GLA_WIDTH = GLA_HEADS * GLA_VAL_DIM

LANES = 128
VMEM_LIMIT = 56 * 1024 * 1024
EXPERT_ROWS = 256


def _params(sem):
    return pltpu.CompilerParams(dimension_semantics=sem, vmem_limit_bytes=VMEM_LIMIT)


def _full(shape):
    return pl.BlockSpec(shape, lambda *_: (0,) * len(shape))


def _split_bf16(a):
    hi = a.astype(BF16)
    lo = (a - hi.astype(F32)).astype(BF16)
    return hi, lo


def _dot(a, b, dims=(((1,), (0,)), ((), ()))):
    return lax.dot_general(a, b, dims, preferred_element_type=F32)


NT = (((1,), (1,)), ((), ()))
TN = (((0,), (0,)), ((), ()))


def _dot3(a, b, dims=(((1,), (0,)), ((), ()))):
    ah, al = _split_bf16(a)
    bh, bl = _split_bf16(b)
    return _dot(ah, bh, dims) + (_dot(ah, bl, dims) + _dot(al, bh, dims))


def _ln_plain(x):
    mu = jnp.mean(x, axis=-1, keepdims=True)
    xc = x - mu
    var = jnp.mean(xc * xc, axis=-1, keepdims=True)
    return xc * lax.rsqrt(var + LN_EPS)


def _silu(x):
    return x * (1.0 / (1.0 + jnp.exp(-x)))


def _sigmoid(x):
    return 1.0 / (1.0 + jnp.exp(-x))


def _mod_kernel(c_ref, w_ref, b_ref, o_ref):
    c = c_ref[...]
    o_ref[...] = _dot3(_silu(c), w_ref[...]) + b_ref[...]


def _mod(c, w_ada, b_ada):
    B, D = c.shape
    n = w_ada.shape[1] // D
    return pl.pallas_call(
        _mod_kernel,
        grid=(n,),
        in_specs=[_full((B, D)),
                  pl.BlockSpec((D, D), lambda j: (0, j)),
                  pl.BlockSpec((1, D), lambda j: (0, j))],
        out_specs=pl.BlockSpec((B, D), lambda j: (0, j)),
        out_shape=jax.ShapeDtypeStruct((B, n * D), F32),
        compiler_params=_params(("arbitrary",)),
        name="mod",
    )(c, w_ada, b_ada.reshape(1, -1))


_SEG = np.cumsum([0, ATTN_WIDTH, KV_WIDTH, KV_WIDTH, GLA_KWIDTH, GLA_KWIDTH, GLA_WIDTH, GLA_WIDTH])


def _rope_chunk(t, cos, sin_signed, first_half):
    up = pltpu.roll(t, LANES - 32, axis=1)
    dn = pltpu.roll(t, 32, axis=1)
    return t * cos + jnp.where(first_half, up, dn) * sin_signed


def _inproj_kernel(x_ref, mod_ref, pos_ref, invf_ref, w_ref, b_ref, wlo_ref, blo_ref, wgk_ref, bgk_ref,
                   qa_ref, ka_ref, va_ref, ql_ref, kl_ref, vl_ref, gl_ref, lg_ref):
    x = x_ref[...]
    h = _ln_plain(x) * (1.0 + mod_ref[0, 1:2, :]) + mod_ref[0, 0:1, :]
    hb = h.astype(BF16)

    def seg(i):
        lo, hi = int(_SEG[i]), int(_SEG[i + 1])
        return _dot(hb, w_ref[:, lo:hi]) + b_ref[:, lo:hi]

    ang = pos_ref[...] * invf_ref[...]
    cos = jnp.cos(ang)
    sin = jnp.sin(ang)
    lane = lax.broadcasted_iota(I32, ang.shape, 1)
    first_half = (lane % ATTN_HEAD_DIM) < (ATTN_HEAD_DIM // 2)
    sin_signed = jnp.where(first_half, -sin, sin)

    q = seg(0)
    scale = ATTN_HEAD_DIM ** -0.5
    for c in range(ATTN_WIDTH // LANES):
        t = q[:, c * LANES:(c + 1) * LANES]
        qa_ref[:, c * LANES:(c + 1) * LANES] = (_rope_chunk(t, cos, sin_signed, first_half) * scale).astype(BF16)
    ka_ref[...] = _rope_chunk(seg(1), cos, sin_signed, first_half).astype(BF16)
    va_ref[...] = seg(2).astype(BF16)
    ql_ref[...] = seg(3).astype(BF16)
    kl_ref[...] = seg(4).astype(BF16)
    vl_ref[...] = seg(5).astype(BF16)
    gl_ref[...] = seg(6).astype(BF16)
    gk_lo = _dot(hb, wlo_ref[...]) + blo_ref[...]
    gk = _dot3(gk_lo, wgk_ref[...]) + bgk_ref[...]
    log_sig = jnp.minimum(gk, 0.0) - jnp.log(1.0 + jnp.exp(-jnp.abs(gk)))
    lg_ref[...] = log_sig * (1.0 / GLA_GATE_NORM)


def _inproj(x2, mod3, posb, invf, w_main, b_main, w_lo, b_lo, w_gk, b_gk, *, seq, tm):
    N, D = x2.shape
    per_b = seq // tm
    widths = [ATTN_WIDTH, KV_WIDTH, KV_WIDTH, GLA_KWIDTH, GLA_KWIDTH, GLA_WIDTH, GLA_WIDTH, GLA_KWIDTH]
    dtypes = [BF16] * 7 + [F32]
    row = lambda i: (i, 0)
    return pl.pallas_call(
        _inproj_kernel,
        grid=(N // tm,),
        in_specs=[pl.BlockSpec((tm, D), row),
                  pl.BlockSpec((1, 6, D), lambda i: (i // per_b, 0, 0)),
                  pl.BlockSpec((tm, LANES), row),
                  _full(invf.shape), _full(w_main.shape), _full(b_main.shape),
                  _full(w_lo.shape), _full(b_lo.shape), _full(w_gk.shape), _full(b_gk.shape)],
        out_specs=[pl.BlockSpec((tm, w), row) for w in widths],
        out_shape=[jax.ShapeDtypeStruct((N, w), dt) for w, dt in zip(widths, dtypes)],
        compiler_params=_params(("parallel",)),
        name="inproj",
    )(x2, mod3, posb, invf, w_main, b_main, w_lo, b_lo, w_gk, b_gk)


def _swa_kernel(sink_ref, q_ref, kc_ref, kp_ref, vc_ref, vp_ref, o_ref):
    j = pl.program_id(1)
    blk = ATTN_BLOCK
    k2 = jnp.concatenate([kp_ref[...], kc_ref[...]], axis=0)
    v2 = jnp.concatenate([vp_ref[...], vc_ref[...]], axis=0)
    row = lax.broadcasted_iota(I32, (blk, 2 * blk), 0)
    col = lax.broadcasted_iota(I32, (blk, 2 * blk), 1)
    dist = row + blk - col
    valid = (dist >= 0) & (dist < blk) & ((col >= blk) | (j > 0))
    group = ATTN_Q_HEADS // ATTN_KV_HEADS
    for h in range(ATTN_Q_HEADS):
        kv = h // group
        qh = q_ref[:, h * ATTN_HEAD_DIM:(h + 1) * ATTN_HEAD_DIM]
        kh = k2[:, kv * ATTN_HEAD_DIM:(kv + 1) * ATTN_HEAD_DIM]
        vh = v2[:, kv * ATTN_HEAD_DIM:(kv + 1) * ATTN_HEAD_DIM]
        s = jnp.where(valid, _dot(qh, kh, NT), NEG_INF)
        sink = sink_ref[h]
        m = jnp.maximum(jnp.max(s, axis=-1, keepdims=True), sink)
        p = jnp.exp(s - m)
        denom = jnp.sum(p, axis=-1, keepdims=True) + jnp.exp(sink - m)
        o = _dot(p.astype(BF16), vh) / denom
        o_ref[:, h * ATTN_HEAD_DIM:(h + 1) * ATTN_HEAD_DIM] = o.astype(BF16)


def _swa(q, k, v, sinks, *, seq):
    N = q.shape[0]
    nb = seq // ATTN_BLOCK
    B = N // seq
    cur = lambda b, j: (b * nb + j, 0)
    prev = lambda b, j: (b * nb + jnp.maximum(j - 1, 0), 0)
    return pl.pallas_call(
        _swa_kernel,
        grid=(B, nb),
        in_specs=[pl.BlockSpec(memory_space=pltpu.SMEM),
                  pl.BlockSpec((ATTN_BLOCK, ATTN_WIDTH), cur),
                  pl.BlockSpec((ATTN_BLOCK, KV_WIDTH), cur),
                  pl.BlockSpec((ATTN_BLOCK, KV_WIDTH), prev),
                  pl.BlockSpec((ATTN_BLOCK, KV_WIDTH), cur),
                  pl.BlockSpec((ATTN_BLOCK, KV_WIDTH), prev)],
        out_specs=pl.BlockSpec((ATTN_BLOCK, ATTN_WIDTH), cur),
        out_shape=jax.ShapeDtypeStruct((N, ATTN_WIDTH), BF16),
        compiler_params=_params(("parallel", "parallel")),
        name="swa",
    )(sinks, q, k, k, v, v)


def _gla_kernel(q_ref, k_ref, v_ref, g_ref, lg_ref, ng_ref, o_ref, st_ref, *, chunks):
    @pl.when(pl.program_id(1) == 0)
    def _():
        st_ref[...] = jnp.zeros_like(st_ref)

    C = GLA_CHUNK
    dk, dv = GLA_KEY_DIM, GLA_VAL_DIM
    r = lax.broadcasted_iota(I32, (C, C), 0)
    c = lax.broadcasted_iota(I32, (C, C), 1)
    causal = c <= r
    tri = jnp.where(causal, 1.0, 0.0).astype(BF16)
    st = st_ref[...]
    for n in range(chunks):
        rows = slice(n * C, (n + 1) * C)
        lg_hi, lg_lo = _split_bf16(lg_ref[rows, :])
        b = _dot(tri, lg_hi) + _dot(tri, lg_lo)
        b_last = b[C - 1:C, :]
        q_in = (q_ref[rows, :].astype(F32) * (dk ** -0.5) * jnp.exp(b)).astype(BF16)
        kf = k_ref[rows, :].astype(F32)
        k_in = (kf * jnp.exp(-b)).astype(BF16)
        k_out = (kf * jnp.exp(b_last - b)).astype(BF16)
        decay = jnp.exp(b_last)
        stb = st.astype(BF16)
        ut = []
        for h in range(GLA_HEADS):
            ks = slice(h * dk, (h + 1) * dk)
            vs = slice(h * dv, (h + 1) * dv)
            vh = v_ref[rows, vs]
            a = jnp.where(causal, _dot(q_in[:, ks], k_in[:, ks], NT), 0.0).astype(BF16)
            o = _dot(a, vh) + _dot(q_in[:, ks], stb[:, ks], NT)
            ut.append(_dot(vh, k_out[:, ks], TN))
            o = o * lax.rsqrt(jnp.mean(o * o, axis=-1, keepdims=True) + LN_EPS) * ng_ref[...]
            o_ref[rows, vs] = (o * _silu(g_ref[rows, vs].astype(F32))).astype(BF16)
        st = st * decay + jnp.concatenate(ut, axis=1)
    st_ref[...] = st


def _gla(q, k, v, g, lg, norm_g, *, seq, tc):
    N = q.shape[0]
    B = N // seq
    per_b = seq // tc
    row = lambda b, j: (b * per_b + j, 0)
    return pl.pallas_call(
        functools.partial(_gla_kernel, chunks=tc // GLA_CHUNK),
        grid=(B, per_b),
        in_specs=[pl.BlockSpec((tc, GLA_KWIDTH), row), pl.BlockSpec((tc, GLA_KWIDTH), row),
                  pl.BlockSpec((tc, GLA_WIDTH), row), pl.BlockSpec((tc, GLA_WIDTH), row),
                  pl.BlockSpec((tc, GLA_KWIDTH), row), _full(norm_g.shape)],
        out_specs=pl.BlockSpec((tc, GLA_WIDTH), row),
        out_shape=jax.ShapeDtypeStruct((N, GLA_WIDTH), BF16),
        scratch_shapes=[pltpu.VMEM((GLA_VAL_DIM, GLA_KWIDTH), F32)],
        compiler_params=_params(("parallel", "arbitrary")),
        name="gla",
    )(q, k, v, g, lg, norm_g)


def _slab_scratch(rows, d):
    return pltpu.VMEM((rows * (d // LANES), LANES), F32)


def _store_slabs(slabs_ref, stage_ref, v):
    rows, d = v.shape
    n = d // LANES
    for c in range(n):
        stage_ref[pl.ds(c, rows, stride=n), :] = v[:, c * LANES:(c + 1) * LANES]
    slabs_ref[...] = stage_ref[...].astype(slabs_ref.dtype)


def _load_slabs(stage_ref, slabs, n):
    stage_ref[...] = slabs.astype(F32)
    rows = stage_ref.shape[0] // n
    return jnp.concatenate([stage_ref[pl.ds(c, rows, stride=n), :] for c in range(n)], axis=1)


def _outproj_kernel(oa_ref, og_ref, x_ref, mod_ref, woa_ref, wog_ref, bo_ref, g_ref, b_ref, wrh_ref, wrl_ref,
                    x1_ref, h2_ref, h2s_ref, lt_ref, stage_ref, *, alpha):
    y = _dot(oa_ref[...], woa_ref[...]) + _dot(og_ref[...], wog_ref[...]) + bo_ref[...]
    z = alpha * x_ref[...] + (1.0 + mod_ref[0, 2:3, :]) * y
    x1 = _ln_plain(z) * g_ref[...] + b_ref[...]
    x1_ref[...] = x1
    h2 = _ln_plain(x1) * (1.0 + mod_ref[0, 4:5, :]) + mod_ref[0, 3:4, :]
    h2_ref[...] = h2.astype(BF16)
    _store_slabs(h2s_ref, stage_ref, h2)
    hh, hl = _split_bf16(h2)
    wh = wrh_ref[...]
    lt_ref[...] = _dot(wh, hh, NT) + (_dot(wh, hl, NT) + _dot(wrl_ref[...], hh, NT))


def _outproj(oa, og, x2, mod3, w_oa, w_og, b_o, ln_g, ln_b, wr_hi, wr_lo, *, seq, tm, alpha):
    N, D = x2.shape
    E = wr_hi.shape[0]
    per_b = seq // tm
    row = lambda i: (i, 0)
    return pl.pallas_call(
        functools.partial(_outproj_kernel, alpha=alpha),
        grid=(N // tm,),
        in_specs=[pl.BlockSpec((tm, ATTN_WIDTH), row), pl.BlockSpec((tm, GLA_WIDTH), row),
                  pl.BlockSpec((tm, D), row),
                  pl.BlockSpec((1, 6, D), lambda i: (i // per_b, 0, 0)),
                  _full(w_oa.shape), _full(w_og.shape), _full(b_o.shape), _full(ln_g.shape), _full(ln_b.shape),
                  _full(wr_hi.shape), _full(wr_lo.shape)],
        out_specs=[pl.BlockSpec((tm, D), row), pl.BlockSpec((tm, D), row),
                   pl.BlockSpec((tm * (D // LANES), LANES), row), pl.BlockSpec((E, tm), lambda i: (0, i))],
        out_shape=[jax.ShapeDtypeStruct((N, D), F32), jax.ShapeDtypeStruct((N, D), BF16),
                   jax.ShapeDtypeStruct((N * (D // LANES), LANES), BF16), jax.ShapeDtypeStruct((E, N), F32)],
        scratch_shapes=[_slab_scratch(tm, D)],
        compiler_params=_params(("parallel",)),
        name="outproj",
    )(oa, og, x2, mod3, w_oa, w_og, b_o, ln_g, ln_b, wr_hi, wr_lo)


def _first_index(eq, idx, size):
    return jnp.min(jnp.where(eq, idx, float(size)), axis=0, keepdims=True)


def _route_kernel(lt_ref, bias_ref, upper_ref, idx_ref, w_ref, wt_ref, rank_ref, cnt_ref, base_ref):
    @pl.when(pl.program_id(0) == 0)
    def _():
        base_ref[...] = jnp.zeros_like(base_ref)

    E, t = lt_ref.shape
    gsz = E // N_GROUPS
    scores = _sigmoid(lt_ref[...])
    biased = scores + bias_ref[:, 0:1]
    gi = lax.broadcasted_iota(I32, (gsz, t), 0).astype(F32)
    gs_rows = []
    for g in range(N_GROUPS):
        grp = biased[g * gsz:(g + 1) * gsz, :]
        m1 = jnp.max(grp, axis=0, keepdims=True)
        first = _first_index(grp == m1, gi, gsz)
        m2 = jnp.max(jnp.where(gi == first, REMOVED, grp), axis=0, keepdims=True)
        gs_rows.append(m1 + m2)
    gs = jnp.concatenate(gs_rows, axis=0)
    ri = lax.broadcasted_iota(I32, (N_GROUPS, t), 0).astype(F32)
    gsel = jnp.zeros((N_GROUPS, t), F32)
    for _ in range(TOPK_GROUPS):
        m = jnp.max(gs, axis=0, keepdims=True)
        hit = ri == _first_index(gs == m, ri, N_GROUPS)
        gsel = jnp.where(hit, 1.0, gsel)
        gs = jnp.where(hit, REMOVED, gs)
    cand = jnp.concatenate(
        [jnp.where(gsel[g:g + 1, :] > 0.5, biased[g * gsz:(g + 1) * gsz, :], NEG_INF) for g in range(N_GROUPS)],
        axis=0)
    ei = lax.broadcasted_iota(I32, (E, t), 0).astype(F32)
    idx_rows, w_rows, hits = [], [], []
    chosen = jnp.zeros((E, t), F32)
    for _ in range(TOP_K):
        m = jnp.max(cand, axis=0, keepdims=True)
        fi = _first_index(cand == m, ei, E)
        hit = ei == fi
        idx_rows.append(fi)
        w_rows.append(jnp.sum(jnp.where(hit, scores, 0.0), axis=0, keepdims=True))
        hits.append(hit)
        chosen = jnp.where(hit, 1.0, chosen)
        cand = jnp.where(hit, REMOVED, cand)
    w = jnp.concatenate(w_rows, axis=0)
    w = w / jnp.sum(w, axis=0, keepdims=True) * ROUTED_SCALE
    idx_ref[...] = jnp.concatenate(idx_rows, axis=0).astype(I32)
    w_ref[...] = w
    wpad = jnp.concatenate([w, jnp.zeros((LANES - TOP_K, t), F32)], axis=0)
    wt_ref[...] = wpad.T
    prefix = _dot(chosen.astype(BF16), upper_ref[...])
    pos = base_ref[:, 0:1] + prefix
    rank_ref[...] = jnp.concatenate(
        [jnp.sum(jnp.where(hit, pos, 0.0), axis=0, keepdims=True) for hit in hits], axis=0).astype(I32)
    base_ref[...] = base_ref[...] + jnp.sum(chosen, axis=1, keepdims=True)
    cnt_ref[...] = base_ref[...]


def _route(logits_t, bias_col, upper, *, tr):
    E, N = logits_t.shape
    col = lambda i: (0, i)
    return pl.pallas_call(
        _route_kernel,
        grid=(N // tr,),
        in_specs=[pl.BlockSpec((E, tr), col), _full(bias_col.shape), _full(upper.shape)],
        out_specs=[pl.BlockSpec((TOP_K, tr), col), pl.BlockSpec((TOP_K, tr), col),
                   pl.BlockSpec((tr, LANES), lambda i: (i, 0)),
                   pl.BlockSpec((TOP_K, tr), col), _full((E, LANES))],
        out_shape=[jax.ShapeDtypeStruct((TOP_K, N), I32), jax.ShapeDtypeStruct((TOP_K, N), F32),
                   jax.ShapeDtypeStruct((N, LANES), F32),
                   jax.ShapeDtypeStruct((TOP_K, N), I32), jax.ShapeDtypeStruct((E, LANES), F32)],
        scratch_shapes=[pltpu.VMEM((E, LANES), F32)],
        compiler_params=_params(("arbitrary",)),
        name="route",
    )(logits_t, bias_col, upper)


def _plan_kernel(cnt_ref, lower_ref, idx_ref, rank_ref, dest_ref, bexp_ref, nused_ref, pstart_ref, *, n_blk):
    E = cnt_ref.shape[0]
    bm = float(EXPERT_ROWS)

    @pl.when(pl.program_id(0) == 0)
    def _():
        cnt = cnt_ref[...]
        padded = jnp.floor((cnt + (bm - 1.0)) * (1.0 / bm)) * bm
        hi = jnp.floor(padded * (1.0 / 256.0))
        lo = padded - hi * 256.0
        low = lower_ref[...]
        pend = 256.0 * _dot(low, hi.astype(BF16)) + _dot(low, lo.astype(BF16))
        pstart_ref[...] = pend - padded
        blk0 = lax.broadcasted_iota(I32, (E, n_blk), 1).astype(F32) * bm
        n_le = jnp.sum(jnp.where(pend[:, 0:1] <= blk0, 1.0, 0.0), axis=0, keepdims=True)
        bexp_ref[...] = jnp.minimum(n_le, float(E - 1)).astype(I32)
        nused_ref[...] = (pend[E - 1:E, 0:1] * (1.0 / bm)).astype(I32)

    t = idx_ref.shape[1]
    ei = lax.broadcasted_iota(I32, (E, t), 0)
    ps = pstart_ref[:, 0:1]
    rows = []
    for k in range(TOP_K):
        hit = ei == idx_ref[k:k + 1, :]
        rows.append(jnp.sum(jnp.where(hit, ps, 0.0), axis=0, keepdims=True))
    dest_ref[...] = rank_ref[...] + jnp.concatenate(rows, axis=0).astype(I32)


def _plan(cnt, lower, idx, rank, *, tr, n_blk):
    E = cnt.shape[0]
    N = idx.shape[1]
    col = lambda i: (0, i)
    return pl.pallas_call(
        functools.partial(_plan_kernel, n_blk=n_blk),
        grid=(N // tr,),
        in_specs=[_full(cnt.shape), _full(lower.shape), pl.BlockSpec((TOP_K, tr), col),
                  pl.BlockSpec((TOP_K, tr), col)],
        out_specs=[pl.BlockSpec((TOP_K, tr), col), _full((1, n_blk)), _full((1, 1))],
        out_shape=[jax.ShapeDtypeStruct((TOP_K, N), I32), jax.ShapeDtypeStruct((1, n_blk), I32),
                   jax.ShapeDtypeStruct((1, 1), I32)],
        scratch_shapes=[pltpu.VMEM((E, LANES), F32)],
        compiler_params=_params(("arbitrary",)),
        name="plan",
    )(cnt, lower, idx, rank)


def _slab_rows(ref, row, slab):
    return ref.at[pl.ds(pl.multiple_of(row * slab, slab), slab)]


def _scatter_kernel(bexp_ref, nused_ref, *refs):
    dest_refs = refs[:TOP_K]
    h_ref, xs_ref, zero_ref, sem, zsem = refs[TOP_K:]
    ts = dest_refs[0].shape[0]
    slab = h_ref.shape[0] // ts
    n_blk = bexp_ref.shape[0]
    blk = zero_ref.shape[0]

    @pl.when(pl.program_id(0) == 0)
    def _():
        zero_ref[...] = jnp.zeros_like(zero_ref)
        nused = nused_ref[0]

        def zstart(b, n):
            pad = (b >= nused - 1) | (bexp_ref[jnp.minimum(b + 1, n_blk - 1)] != bexp_ref[b])

            @pl.when(pad)
            def _():
                start = pl.multiple_of(b * blk, blk)
                pltpu.make_async_copy(zero_ref, xs_ref.at[pl.ds(start, blk)], zsem).start()

            return n + pad.astype(I32)

        n_zero = lax.fori_loop(0, n_blk, zstart, 0)

        def zwait(_, carry):
            pltpu.make_async_copy(zero_ref, xs_ref.at[pl.ds(0, blk)], zsem).wait()
            return carry

        lax.fori_loop(0, n_zero, zwait, 0)

    def body(t, carry):
        src = _slab_rows(h_ref, t, slab)
        for k in range(TOP_K):
            pltpu.make_async_copy(src, _slab_rows(xs_ref, dest_refs[k][t], slab), sem).start(priority=k % 2)
        return carry

    lax.fori_loop(0, ts, body, 0)
    for _ in range(TOP_K):
        pltpu.make_async_copy(h_ref, xs_ref.at[pl.ds(0, slab * ts)], sem).wait()


def _scatter(bexp, nused, dests, h2s, *, ts, slab):
    rows, W = h2s.shape
    n_blk = bexp.shape[0]
    grid_spec = pltpu.PrefetchScalarGridSpec(
        num_scalar_prefetch=2,
        grid=(rows // (slab * ts),),
        in_specs=[pl.BlockSpec((ts,), lambda i, be, nu: (i,), memory_space=pltpu.SMEM)] * TOP_K
        + [pl.BlockSpec((slab * ts, W), lambda i, be, nu: (i, 0))],
        out_specs=pl.BlockSpec(memory_space=pl.ANY),
        scratch_shapes=[pltpu.VMEM((slab * EXPERT_ROWS, W), h2s.dtype), pltpu.SemaphoreType.DMA(()),
                        pltpu.SemaphoreType.DMA(())],
    )
    return pl.pallas_call(
        _scatter_kernel,
        grid_spec=grid_spec,
        out_shape=jax.ShapeDtypeStruct((n_blk * slab * EXPERT_ROWS, W), h2s.dtype),
        compiler_params=_params(("arbitrary",)),
        name="scatter",
    )(bexp, nused, *dests, h2s)


def _experts_kernel(bexp_ref, nused_ref, xs_ref, wg_ref, wu_ref, wd_ref, ys_ref, wgb_ref, wub_ref, wdb_ref,
                    xstage_ref, ystage_ref):
    i = pl.program_id(0)
    used = i < nused_ref[0]
    fresh = (i == 0) | (bexp_ref[i] != bexp_ref[jnp.maximum(i - 1, 0)])

    @pl.when(used & fresh)
    def _():
        wgb_ref[...] = wg_ref[0].astype(BF16)
        wub_ref[...] = wu_ref[0].astype(BF16)
        wdb_ref[...] = wd_ref[0].astype(BF16)

    @pl.when(used)
    def _():
        x = _load_slabs(xstage_ref, xs_ref[...], wgb_ref.shape[0] // LANES).astype(BF16)
        g = _dot(x, wgb_ref[...])
        u = _dot(x, wub_ref[...])
        _store_slabs(ys_ref, ystage_ref, _dot((_silu(g) * u).astype(BF16), wdb_ref[...]))

    @pl.when(jnp.logical_not(used))
    def _():
        ys_ref[...] = jnp.zeros_like(ys_ref)


def _experts(bexp, nused, xs, wg, wu, wd):
    rows, W = xs.shape
    E, D, H = wg.shape
    blk = (D // LANES) * EXPERT_ROWS
    grid_spec = pltpu.PrefetchScalarGridSpec(
        num_scalar_prefetch=2,
        grid=(rows // blk,),
        in_specs=[pl.BlockSpec((blk, W), lambda i, be, nu: (i, 0)),
                  pl.BlockSpec((1, D, H), lambda i, be, nu: (be[i], 0, 0)),
                  pl.BlockSpec((1, D, H), lambda i, be, nu: (be[i], 0, 0)),
                  pl.BlockSpec((1, H, D), lambda i, be, nu: (be[i], 0, 0))],
        out_specs=pl.BlockSpec((blk, W), lambda i, be, nu: (i, 0)),
        scratch_shapes=[pltpu.VMEM((D, H), BF16), pltpu.VMEM((D, H), BF16), pltpu.VMEM((H, D), BF16),
                        _slab_scratch(EXPERT_ROWS, D), _slab_scratch(EXPERT_ROWS, D)],
    )
    return pl.pallas_call(
        _experts_kernel,
        grid_spec=grid_spec,
        out_shape=jax.ShapeDtypeStruct((rows, W), BF16),
        compiler_params=_params(("arbitrary",)),
        name="experts",
    )(bexp, nused, xs, wg, wu, wd)


def _combine_kernel(*refs, alpha):
    dest_refs = refs[:TOP_K]
    (wt_ref, h2_ref, x1_ref, mod_ref, wsg_ref, wsu_ref, wsd_ref, g_ref, b_ref, ys_ref,
     o_ref, buf_ref, stage0_ref, stage1_ref, sem) = refs[TOP_K:]
    tc, d = h2_ref.shape
    slab = d // LANES

    def body(t, carry):
        for k in range(TOP_K):
            src = _slab_rows(ys_ref, dest_refs[k][t], slab)
            pltpu.make_async_copy(src, _slab_rows(buf_ref.at[k], t, slab), sem).start(priority=k % 2)
        return carry

    lax.fori_loop(0, tc, body, 0)
    h = h2_ref[...]
    mid = (_silu(_dot(h, wsg_ref[...])) * _dot(h, wsu_ref[...])).astype(BF16)
    y = _dot(mid, wsd_ref[...])
    for k in range(TOP_K):
        pltpu.make_async_copy(ys_ref.at[pl.ds(0, slab * tc)], buf_ref.at[k], sem).wait()
    for k in range(TOP_K):
        stage_ref = stage0_ref if k % 2 == 0 else stage1_ref
        y = y + wt_ref[:, k:k + 1] * _load_slabs(stage_ref, buf_ref[k], slab)
    z = alpha * x1_ref[...] + (1.0 + mod_ref[0, 5:6, :]) * y
    o_ref[...] = _ln_plain(z) * g_ref[...] + b_ref[...]


def _combine(dests, wt, h2, x1, mod3, wsg, wsu, wsd, ln_g, ln_b, ys, *, seq, tc, alpha):
    N, D = x1.shape
    slab = D // LANES
    per_b = seq // tc
    row = lambda i: (i, 0)
    return pl.pallas_call(
        functools.partial(_combine_kernel, alpha=alpha),
        grid=(N // tc,),
        in_specs=[pl.BlockSpec((tc,), lambda i: (i,), memory_space=pltpu.SMEM)] * TOP_K
        + [pl.BlockSpec((tc, LANES), row), pl.BlockSpec((tc, D), row), pl.BlockSpec((tc, D), row),
           pl.BlockSpec((1, 6, D), lambda i: (i // per_b, 0, 0)),
           _full(wsg.shape), _full(wsu.shape), _full(wsd.shape), _full(ln_g.shape), _full(ln_b.shape),
           pl.BlockSpec(memory_space=pl.ANY)],
        out_specs=pl.BlockSpec((tc, D), row),
        out_shape=jax.ShapeDtypeStruct((N, D), F32),
        scratch_shapes=[pltpu.VMEM((TOP_K, slab * tc, LANES), ys.dtype), _slab_scratch(tc, D), _slab_scratch(tc, D),
                        pltpu.SemaphoreType.DMA(())],
        compiler_params=_params(("arbitrary",)),
        name="combine",
    )(*dests, wt, h2, x1, mod3, wsg, wsu, wsd, ln_g, ln_b, ys)


def _tiles(seq):
    t = lambda want: min(want, seq)
    return dict(proj=t(512), gla=t(256), route=t(256), move=t(256))


def _layer(x2, mod3, posb, invf, p, *, seq, alpha):
    N, D = x2.shape
    tl = _tiles(seq)
    qa, ka, va, ql, kl, vl, gl, lg = _inproj(x2, mod3, posb, invf, p["w_main"], p["b_main"], p["w_lo"], p["b_lo"],
                                             p["w_gk"], p["b_gk"], seq=seq, tm=tl["proj"])
    oa = _swa(qa, ka, va, p["sinks"], seq=seq)
    og = _gla(ql, kl, vl, gl, lg, p["norm_g"], seq=seq, tc=tl["gla"])
    x1, h2, h2s, logits_t = _outproj(oa, og, x2, mod3, p["w_oa"], p["w_og"], p["b_o"], p["ln1_g"], p["ln1_b"],
                                     p["wr_hi"], p["wr_lo"], seq=seq, tm=tl["proj"], alpha=alpha)
    idx, _, wt, rank, cnt = _route(logits_t, p["bias_col"], p["upper"], tr=tl["route"])
    E = cnt.shape[0]
    n_blk = (N * TOP_K) // EXPERT_ROWS + E
    dest, bexp, nused = _plan(cnt, p["lower"], idx, rank, tr=tl["route"], n_blk=n_blk)
    bexp, nused = bexp.reshape(n_blk), nused.reshape(1)
    dests = [dest[k] for k in range(TOP_K)]
    xs = _scatter(bexp, nused, dests, h2s, ts=tl["move"], slab=D // LANES)
    ys = _experts(bexp, nused, xs, p["wg"], p["wu"], p["wd"])
    return _combine(dests, wt, h2, x1, mod3, p["wsg"], p["wsu"], p["wsd"], p["ln2_g"], p["ln2_b"], ys,
                    seq=seq, tc=tl["move"], alpha=alpha)


def kernel(x, c, positions, w_ada, b_ada, w_in, b_in, attn_sinks, w_gk2, b_gk2, gla_norm_g, w_o, b_o, ln1_g, ln1_b, w_router, router_bias, w_exp_gate, w_exp_up, w_exp_down, w_sh_gate, w_sh_up, w_sh_down, ln2_g, ln2_b):
    B, S, D = x.shape
    depth = w_ada.shape[0]
    E = w_router.shape[2]
    alpha = float((2 * depth) ** 0.25)
    tl = _tiles(S)
    n_main = int(_SEG[-1])

    posb = jnp.broadcast_to(positions.astype(F32).reshape(B * S, 1), (B * S, LANES))
    half = ATTN_HEAD_DIM // 2
    invf = (ROPE_THETA ** (-(jnp.arange(LANES) % half).astype(F32) / half)).reshape(1, LANES)
    tr = tl["route"]
    upper = (jnp.arange(tr)[:, None] < jnp.arange(tr)[None, :]).astype(BF16)
    lower = (jnp.arange(E)[:, None] >= jnp.arange(E)[None, :]).astype(BF16)
    row = lambda v: v.reshape(1, -1)

    x2 = x.reshape(B * S, D)
    for l in range(depth):
        wr_t = w_router[l].T
        wr_hi = wr_t.astype(BF16)
        p = dict(
            w_main=w_in[l][:, :n_main].astype(BF16), b_main=row(b_in[l][:n_main]),
            w_lo=jnp.pad(w_in[l][:, n_main:], ((0, 0), (0, LANES - GLA_GATE_RANK))).astype(BF16),
            b_lo=row(jnp.pad(b_in[l][n_main:], (0, LANES - GLA_GATE_RANK))),
            w_gk=jnp.pad(w_gk2[l], ((0, LANES - GLA_GATE_RANK), (0, 0))), b_gk=row(b_gk2[l]),
            sinks=attn_sinks[l], norm_g=row(gla_norm_g[l]),
            w_oa=w_o[l][:ATTN_WIDTH].astype(BF16), w_og=w_o[l][ATTN_WIDTH:].astype(BF16), b_o=row(b_o[l]),
            ln1_g=row(ln1_g[l]), ln1_b=row(ln1_b[l]),
            wr_hi=wr_hi, wr_lo=(wr_t - wr_hi.astype(F32)).astype(BF16),
            bias_col=jnp.broadcast_to(router_bias[l].reshape(E, 1), (E, LANES)),
            upper=upper, lower=lower,
            wg=w_exp_gate[l], wu=w_exp_up[l], wd=w_exp_down[l],
            wsg=w_sh_gate[l].astype(BF16), wsu=w_sh_up[l].astype(BF16), wsd=w_sh_down[l].astype(BF16),
            ln2_g=row(ln2_g[l]), ln2_b=row(ln2_b[l]),
        )
        mod = _mod(c, w_ada[l], b_ada[l])
        mod3 = mod.reshape(B, 6, D)
        x2 = _layer(x2, mod3, posb, invf, p, seq=S, alpha=alpha)
    return x2.reshape(B, S, D)
```

```python
import functools

import jax
import jax.numpy as jnp
import numpy as np
from jax import lax
from jax.experimental import pallas as pl
from jax.experimental.pallas import tpu as pltpu

F32 = jnp.float32
BF16 = jnp.bfloat16
I32 = jnp.int32

ATTN_Q_HEADS = 8
ATTN_KV_HEADS = 2
ATTN_HEAD_DIM = 64
ATTN_BLOCK = 128
ROPE_THETA = 10000.0
GLA_HEADS = 4
GLA_KEY_DIM = 64
GLA_VAL_DIM = 128
GLA_GATE_RANK = 16
GLA_GATE_NORM = 16.0
GLA_CHUNK = 64
N_GROUPS = 8
TOPK_GROUPS = 4
TOP_K = 8
ROUTED_SCALE = 2.5
LN_EPS = 1e-5
NEG_INF = -1e30
REMOVED = -3e38

ATTN_WIDTH = ATTN_Q_HEADS * ATTN_HEAD_DIM
KV_WIDTH = ATTN_KV_HEADS * ATTN_HEAD_DIM
GLA_KWIDTH = GLA_HEADS * GLA_KEY_DIM
GLA_WIDTH = GLA_HEADS * GLA_VAL_DIM

LANES = 128
VMEM_LIMIT = 56 * 1024 * 1024
EXPERT_ROWS = 512


def _params(sem):
    return pltpu.CompilerParams(dimension_semantics=sem, vmem_limit_bytes=VMEM_LIMIT)


def _full(shape):
    return pl.BlockSpec(shape, lambda *_: (0,) * len(shape))


def _split_bf16(a):
    hi = a.astype(BF16)
    lo = (a - hi.astype(F32)).astype(BF16)
    return hi, lo


def _dot(a, b, dims=(((1,), (0,)), ((), ()))):
    return lax.dot_general(a, b, dims, preferred_element_type=F32)


NT = (((1,), (1,)), ((), ()))
TN = (((0,), (0,)), ((), ()))


def _dot3(a, b, dims=(((1,), (0,)), ((), ()))):
    ah, al = _split_bf16(a)
    bh, bl = _split_bf16(b)
    return _dot(ah, bh, dims) + (_dot(ah, bl, dims) + _dot(al, bh, dims))


def _ln_plain(x):
    mu = jnp.mean(x, axis=-1, keepdims=True)
    xc = x - mu
    var = jnp.mean(xc * xc, axis=-1, keepdims=True)
    return xc * lax.rsqrt(var + LN_EPS)


def _silu(x):
    return x * (1.0 / (1.0 + jnp.exp(-x)))


def _sigmoid(x):
    return 1.0 / (1.0 + jnp.exp(-x))


def _mod_kernel(c_ref, w_ref, b_ref, o_ref):
    c = c_ref[...]
    o_ref[...] = _dot3(_silu(c), w_ref[...]) + b_ref[...]


def _mod(c, w_ada, b_ada):
    B, D = c.shape
    n = w_ada.shape[1] // D
    return pl.pallas_call(
        _mod_kernel,
        grid=(n,),
        in_specs=[_full((B, D)),
                  pl.BlockSpec((D, D), lambda j: (0, j)),
                  pl.BlockSpec((1, D), lambda j: (0, j))],
        out_specs=pl.BlockSpec((B, D), lambda j: (0, j)),
        out_shape=jax.ShapeDtypeStruct((B, n * D), F32),
        compiler_params=_params(("arbitrary",)),
        name="mod",
    )(c, w_ada, b_ada.reshape(1, -1))


_SEG = np.cumsum([0, ATTN_WIDTH, KV_WIDTH, KV_WIDTH, GLA_KWIDTH, GLA_KWIDTH, GLA_WIDTH, GLA_WIDTH])


def _rope_chunk(t, cos, sin_signed, first_half):
    up = pltpu.roll(t, LANES - 32, axis=1)
    dn = pltpu.roll(t, 32, axis=1)
    return t * cos + jnp.where(first_half, up, dn) * sin_signed


def _inproj_kernel(x_ref, mod_ref, pos_ref, invf_ref, w_ref, b_ref, wlo_ref, blo_ref, wgk_ref, bgk_ref,
                   qa_ref, ka_ref, va_ref, ql_ref, kl_ref, vl_ref, gl_ref, lg_ref):
    x = x_ref[...]
    h = _ln_plain(x) * (1.0 + mod_ref[0, 1:2, :]) + mod_ref[0, 0:1, :]
    hb = h.astype(BF16)

    def seg(i):
        lo, hi = int(_SEG[i]), int(_SEG[i + 1])
        return _dot(hb, w_ref[:, lo:hi]) + b_ref[:, lo:hi]

    ang = pos_ref[...] * invf_ref[...]
    cos = jnp.cos(ang)
    sin = jnp.sin(ang)
    lane = lax.broadcasted_iota(I32, ang.shape, 1)
    first_half = (lane % ATTN_HEAD_DIM) < (ATTN_HEAD_DIM // 2)
    sin_signed = jnp.where(first_half, -sin, sin)

    q = seg(0)
    scale = ATTN_HEAD_DIM ** -0.5
    for c in range(ATTN_WIDTH // LANES):
        t = q[:, c * LANES:(c + 1) * LANES]
        qa_ref[:, c * LANES:(c + 1) * LANES] = (_rope_chunk(t, cos, sin_signed, first_half) * scale).astype(BF16)
    ka_ref[...] = _rope_chunk(seg(1), cos, sin_signed, first_half).astype(BF16)
    va_ref[...] = seg(2).astype(BF16)
    ql_ref[...] = seg(3).astype(BF16)
    kl_ref[...] = seg(4).astype(BF16)
    vl_ref[...] = seg(5).astype(BF16)
    gl_ref[...] = seg(6).astype(BF16)
    gk_lo = _dot(hb, wlo_ref[...]) + blo_ref[...]
    gk = _dot3(gk_lo, wgk_ref[...]) + bgk_ref[...]
    log_sig = jnp.minimum(gk, 0.0) - jnp.log(1.0 + jnp.exp(-jnp.abs(gk)))
    lg_ref[...] = log_sig * (1.0 / GLA_GATE_NORM)


def _inproj(x2, mod3, posb, invf, w_main, b_main, w_lo, b_lo, w_gk, b_gk, *, seq, tm):
    N, D = x2.shape
    per_b = seq // tm
    widths = [ATTN_WIDTH, KV_WIDTH, KV_WIDTH, GLA_KWIDTH, GLA_KWIDTH, GLA_WIDTH, GLA_WIDTH, GLA_KWIDTH]
    dtypes = [BF16] * 7 + [F32]
    row = lambda i: (i, 0)
    return pl.pallas_call(
        _inproj_kernel,
        grid=(N // tm,),
        in_specs=[pl.BlockSpec((tm, D), row),
                  pl.BlockSpec((1, 6, D), lambda i: (i // per_b, 0, 0)),
                  pl.BlockSpec((tm, LANES), row),
                  _full(invf.shape), _full(w_main.shape), _full(b_main.shape),
                  _full(w_lo.shape), _full(b_lo.shape), _full(w_gk.shape), _full(b_gk.shape)],
        out_specs=[pl.BlockSpec((tm, w), row) for w in widths],
        out_shape=[jax.ShapeDtypeStruct((N, w), dt) for w, dt in zip(widths, dtypes)],
        compiler_params=_params(("parallel",)),
        name="inproj",
    )(x2, mod3, posb, invf, w_main, b_main, w_lo, b_lo, w_gk, b_gk)


def _swa_kernel(sink_ref, q_ref, kc_ref, kp_ref, vc_ref, vp_ref, o_ref):
    j = pl.program_id(1)
    blk = ATTN_BLOCK
    k2 = jnp.concatenate([kp_ref[...], kc_ref[...]], axis=0)
    v2 = jnp.concatenate([vp_ref[...], vc_ref[...]], axis=0)
    row = lax.broadcasted_iota(I32, (blk, 2 * blk), 0)
    col = lax.broadcasted_iota(I32, (blk, 2 * blk), 1)
    dist = row + blk - col
    valid = (dist >= 0) & (dist < blk) & ((col >= blk) | (j > 0))
    group = ATTN_Q_HEADS // ATTN_KV_HEADS
    for h in range(ATTN_Q_HEADS):
        kv = h // group
        qh = q_ref[:, h * ATTN_HEAD_DIM:(h + 1) * ATTN_HEAD_DIM]
        kh = k2[:, kv * ATTN_HEAD_DIM:(kv + 1) * ATTN_HEAD_DIM]
        vh = v2[:, kv * ATTN_HEAD_DIM:(kv + 1) * ATTN_HEAD_DIM]
        s = jnp.where(valid, _dot(qh, kh, NT), NEG_INF)
        sink = sink_ref[h]
        m = jnp.maximum(jnp.max(s, axis=-1, keepdims=True), sink)
        p = jnp.exp(s - m)
        denom = jnp.sum(p, axis=-1, keepdims=True) + jnp.exp(sink - m)
        o = _dot(p.astype(BF16), vh) / denom
        o_ref[:, h * ATTN_HEAD_DIM:(h + 1) * ATTN_HEAD_DIM] = o.astype(BF16)


def _swa(q, k, v, sinks, *, seq):
    N = q.shape[0]
    nb = seq // ATTN_BLOCK
    B = N // seq
    cur = lambda b, j: (b * nb + j, 0)
    prev = lambda b, j: (b * nb + jnp.maximum(j - 1, 0), 0)
    return pl.pallas_call(
        _swa_kernel,
        grid=(B, nb),
        in_specs=[pl.BlockSpec(memory_space=pltpu.SMEM),
                  pl.BlockSpec((ATTN_BLOCK, ATTN_WIDTH), cur),
                  pl.BlockSpec((ATTN_BLOCK, KV_WIDTH), cur),
                  pl.BlockSpec((ATTN_BLOCK, KV_WIDTH), prev),
                  pl.BlockSpec((ATTN_BLOCK, KV_WIDTH), cur),
                  pl.BlockSpec((ATTN_BLOCK, KV_WIDTH), prev)],
        out_specs=pl.BlockSpec((ATTN_BLOCK, ATTN_WIDTH), cur),
        out_shape=jax.ShapeDtypeStruct((N, ATTN_WIDTH), BF16),
        compiler_params=_params(("parallel", "parallel")),
        name="swa",
    )(sinks, q, k, k, v, v)


def _gla_kernel(q_ref, k_ref, v_ref, g_ref, lg_ref, ng_ref, o_ref, st_ref, *, chunks):
    @pl.when(pl.program_id(1) == 0)
    def _():
        st_ref[...] = jnp.zeros_like(st_ref)

    C = GLA_CHUNK
    dk, dv = GLA_KEY_DIM, GLA_VAL_DIM
    r = lax.broadcasted_iota(I32, (C, C), 0)
    c = lax.broadcasted_iota(I32, (C, C), 1)
    causal = c <= r
    tri = jnp.where(causal, 1.0, 0.0).astype(BF16)
    st = st_ref[...]
    for n in range(chunks):
        rows = slice(n * C, (n + 1) * C)
        lg_hi, lg_lo = _split_bf16(lg_ref[rows, :])
        b = _dot(tri, lg_hi) + _dot(tri, lg_lo)
        b_last = b[C - 1:C, :]
        q_in = (q_ref[rows, :].astype(F32) * (dk ** -0.5) * jnp.exp(b)).astype(BF16)
        kf = k_ref[rows, :].astype(F32)
        k_in = (kf * jnp.exp(-b)).astype(BF16)
        k_out = (kf * jnp.exp(b_last - b)).astype(BF16)
        decay = jnp.exp(b_last)
        stb = st.astype(BF16)
        ut = []
        for h in range(GLA_HEADS):
            ks = slice(h * dk, (h + 1) * dk)
            vs = slice(h * dv, (h + 1) * dv)
            vh = v_ref[rows, vs]
            a = jnp.where(causal, _dot(q_in[:, ks], k_in[:, ks], NT), 0.0).astype(BF16)
            o = _dot(a, vh) + _dot(q_in[:, ks], stb[:, ks], NT)
            ut.append(_dot(vh, k_out[:, ks], TN))
            o = o * lax.rsqrt(jnp.mean(o * o, axis=-1, keepdims=True) + LN_EPS) * ng_ref[...]
            o_ref[rows, vs] = (o * _silu(g_ref[rows, vs].astype(F32))).astype(BF16)
        st = st * decay + jnp.concatenate(ut, axis=1)
    st_ref[...] = st


def _gla(q, k, v, g, lg, norm_g, *, seq, tc):
    N = q.shape[0]
    B = N // seq
    per_b = seq // tc
    row = lambda b, j: (b * per_b + j, 0)
    return pl.pallas_call(
        functools.partial(_gla_kernel, chunks=tc // GLA_CHUNK),
        grid=(B, per_b),
        in_specs=[pl.BlockSpec((tc, GLA_KWIDTH), row), pl.BlockSpec((tc, GLA_KWIDTH), row),
                  pl.BlockSpec((tc, GLA_WIDTH), row), pl.BlockSpec((tc, GLA_WIDTH), row),
                  pl.BlockSpec((tc, GLA_KWIDTH), row), _full(norm_g.shape)],
        out_specs=pl.BlockSpec((tc, GLA_WIDTH), row),
        out_shape=jax.ShapeDtypeStruct((N, GLA_WIDTH), BF16),
        scratch_shapes=[pltpu.VMEM((GLA_VAL_DIM, GLA_KWIDTH), F32)],
        compiler_params=_params(("parallel", "arbitrary")),
        name="gla",
    )(q, k, v, g, lg, norm_g)


def _slab_scratch(rows, d):
    return pltpu.VMEM((rows * (d // LANES), LANES), F32)


def _store_slabs(slabs_ref, stage_ref, v):
    rows, d = v.shape
    n = d // LANES
    for c in range(n):
        stage_ref[pl.ds(c, rows, stride=n), :] = v[:, c * LANES:(c + 1) * LANES]
    slabs_ref[...] = stage_ref[...].astype(slabs_ref.dtype)


def _load_slabs(stage_ref, slabs, n):
    stage_ref[...] = slabs.astype(F32)
    rows = stage_ref.shape[0] // n
    return jnp.concatenate([stage_ref[pl.ds(c, rows, stride=n), :] for c in range(n)], axis=1)


def _outproj_kernel(oa_ref, og_ref, x_ref, mod_ref, woa_ref, wog_ref, bo_ref, g_ref, b_ref, wrh_ref, wrl_ref,
                    x1_ref, h2_ref, h2s_ref, lt_ref, stage_ref, *, alpha):
    y = _dot(oa_ref[...], woa_ref[...]) + _dot(og_ref[...], wog_ref[...]) + bo_ref[...]
    z = alpha * x_ref[...] + (1.0 + mod_ref[0, 2:3, :]) * y
    x1 = _ln_plain(z) * g_ref[...] + b_ref[...]
    x1_ref[...] = x1
    h2 = _ln_plain(x1) * (1.0 + mod_ref[0, 4:5, :]) + mod_ref[0, 3:4, :]
    h2_ref[...] = h2.astype(BF16)
    _store_slabs(h2s_ref, stage_ref, h2)
    hh, hl = _split_bf16(h2)
    wh = wrh_ref[...]
    lt_ref[...] = _dot(wh, hh, NT) + (_dot(wh, hl, NT) + _dot(wrl_ref[...], hh, NT))


def _outproj(oa, og, x2, mod3, w_oa, w_og, b_o, ln_g, ln_b, wr_hi, wr_lo, *, seq, tm, alpha):
    N, D = x2.shape
    E = wr_hi.shape[0]
    per_b = seq // tm
    row = lambda i: (i, 0)
    return pl.pallas_call(
        functools.partial(_outproj_kernel, alpha=alpha),
        grid=(N // tm,),
        in_specs=[pl.BlockSpec((tm, ATTN_WIDTH), row), pl.BlockSpec((tm, GLA_WIDTH), row),
                  pl.BlockSpec((tm, D), row),
                  pl.BlockSpec((1, 6, D), lambda i: (i // per_b, 0, 0)),
                  _full(w_oa.shape), _full(w_og.shape), _full(b_o.shape), _full(ln_g.shape), _full(ln_b.shape),
                  _full(wr_hi.shape), _full(wr_lo.shape)],
        out_specs=[pl.BlockSpec((tm, D), row), pl.BlockSpec((tm, D), row),
                   pl.BlockSpec((tm * (D // LANES), LANES), row), pl.BlockSpec((E, tm), lambda i: (0, i))],
        out_shape=[jax.ShapeDtypeStruct((N, D), F32), jax.ShapeDtypeStruct((N, D), BF16),
                   jax.ShapeDtypeStruct((N * (D // LANES), LANES), BF16), jax.ShapeDtypeStruct((E, N), F32)],
        scratch_shapes=[_slab_scratch(tm, D)],
        compiler_params=_params(("parallel",)),
        name="outproj",
    )(oa, og, x2, mod3, w_oa, w_og, b_o, ln_g, ln_b, wr_hi, wr_lo)


def _first_index(eq, idx, size):
    return jnp.min(jnp.where(eq, idx, float(size)), axis=0, keepdims=True)


def _route_kernel(lt_ref, bias_ref, upper_ref, idx_ref, w_ref, wt_ref, rank_ref, cnt_ref, base_ref):
    @pl.when(pl.program_id(0) == 0)
    def _():
        base_ref[...] = jnp.zeros_like(base_ref)

    E, t = lt_ref.shape
    gsz = E // N_GROUPS
    scores = _sigmoid(lt_ref[...])
    biased = scores + bias_ref[:, 0:1]
    gi = lax.broadcasted_iota(I32, (gsz, t), 0).astype(F32)
    gs_rows = []
    for g in range(N_GROUPS):
        grp = biased[g * gsz:(g + 1) * gsz, :]
        m1 = jnp.max(grp, axis=0, keepdims=True)
        first = _first_index(grp == m1, gi, gsz)
        m2 = jnp.max(jnp.where(gi == first, REMOVED, grp), axis=0, keepdims=True)
        gs_rows.append(m1 + m2)
    gs = jnp.concatenate(gs_rows, axis=0)
    ri = lax.broadcasted_iota(I32, (N_GROUPS, t), 0).astype(F32)
    gsel = jnp.zeros((N_GROUPS, t), F32)
    for _ in range(TOPK_GROUPS):
        m = jnp.max(gs, axis=0, keepdims=True)
        hit = ri == _first_index(gs == m, ri, N_GROUPS)
        gsel = jnp.where(hit, 1.0, gsel)
        gs = jnp.where(hit, REMOVED, gs)
    cand = jnp.concatenate(
        [jnp.where(gsel[g:g + 1, :] > 0.5, biased[g * gsz:(g + 1) * gsz, :], NEG_INF) for g in range(N_GROUPS)],
        axis=0)
    ei = lax.broadcasted_iota(I32, (E, t), 0).astype(F32)
    idx_rows, w_rows, hits = [], [], []
    chosen = jnp.zeros((E, t), F32)
    for _ in range(TOP_K):
        m = jnp.max(cand, axis=0, keepdims=True)
        fi = _first_index(cand == m, ei, E)
        hit = ei == fi
        idx_rows.append(fi)
        w_rows.append(jnp.sum(jnp.where(hit, scores, 0.0), axis=0, keepdims=True))
        hits.append(hit)
        chosen = jnp.where(hit, 1.0, chosen)
        cand = jnp.where(hit, REMOVED, cand)
    w = jnp.concatenate(w_rows, axis=0)
    w = w / jnp.sum(w, axis=0, keepdims=True) * ROUTED_SCALE
    idx_ref[...] = jnp.concatenate(idx_rows, axis=0).astype(I32)
    w_ref[...] = w
    wpad = jnp.concatenate([w, jnp.zeros((LANES - TOP_K, t), F32)], axis=0)
    wt_ref[...] = wpad.T
    prefix = _dot(chosen.astype(BF16), upper_ref[...])
    pos = base_ref[:, 0:1] + prefix
    rank_ref[...] = jnp.concatenate(
        [jnp.sum(jnp.where(hit, pos, 0.0), axis=0, keepdims=True) for hit in hits], axis=0).astype(I32)
    base_ref[...] = base_ref[...] + jnp.sum(chosen, axis=1, keepdims=True)
    cnt_ref[...] = base_ref[...]


def _route(logits_t, bias_col, upper, *, tr):
    E, N = logits_t.shape
    col = lambda i: (0, i)
    return pl.pallas_call(
        _route_kernel,
        grid=(N // tr,),
        in_specs=[pl.BlockSpec((E, tr), col), _full(bias_col.shape), _full(upper.shape)],
        out_specs=[pl.BlockSpec((TOP_K, tr), col), pl.BlockSpec((TOP_K, tr), col),
                   pl.BlockSpec((tr, LANES), lambda i: (i, 0)),
                   pl.BlockSpec((TOP_K, tr), col), _full((E, LANES))],
        out_shape=[jax.ShapeDtypeStruct((TOP_K, N), I32), jax.ShapeDtypeStruct((TOP_K, N), F32),
                   jax.ShapeDtypeStruct((N, LANES), F32),
                   jax.ShapeDtypeStruct((TOP_K, N), I32), jax.ShapeDtypeStruct((E, LANES), F32)],
        scratch_shapes=[pltpu.VMEM((E, LANES), F32)],
        compiler_params=_params(("arbitrary",)),
        name="route",
    )(logits_t, bias_col, upper)


def _plan_kernel(cnt_ref, lower_ref, idx_ref, rank_ref, dest_ref, bexp_ref, nused_ref, pstart_ref, *, n_blk):
    E = cnt_ref.shape[0]
    bm = float(EXPERT_ROWS)

    @pl.when(pl.program_id(0) == 0)
    def _():
        cnt = cnt_ref[...]
        padded = jnp.floor((cnt + (bm - 1.0)) * (1.0 / bm)) * bm
        hi = jnp.floor(padded * (1.0 / 256.0))
        lo = padded - hi * 256.0
        low = lower_ref[...]
        pend = 256.0 * _dot(low, hi.astype(BF16)) + _dot(low, lo.astype(BF16))
        pstart_ref[...] = pend - padded
        blk0 = lax.broadcasted_iota(I32, (E, n_blk), 1).astype(F32) * bm
        n_le = jnp.sum(jnp.where(pend[:, 0:1] <= blk0, 1.0, 0.0), axis=0, keepdims=True)
        bexp_ref[...] = jnp.minimum(n_le, float(E - 1)).astype(I32)
        nused_ref[...] = (pend[E - 1:E, 0:1] * (1.0 / bm)).astype(I32)

    t = idx_ref.shape[1]
    ei = lax.broadcasted_iota(I32, (E, t), 0)
    ps = pstart_ref[:, 0:1]
    rows = []
    for k in range(TOP_K):
        hit = ei == idx_ref[k:k + 1, :]
        rows.append(jnp.sum(jnp.where(hit, ps, 0.0), axis=0, keepdims=True))
    dest_ref[...] = rank_ref[...] + jnp.concatenate(rows, axis=0).astype(I32)


def _plan(cnt, lower, idx, rank, *, tr, n_blk):
    E = cnt.shape[0]
    N = idx.shape[1]
    col = lambda i: (0, i)
    return pl.pallas_call(
        functools.partial(_plan_kernel, n_blk=n_blk),
        grid=(N // tr,),
        in_specs=[_full(cnt.shape), _full(lower.shape), pl.BlockSpec((TOP_K, tr), col),
                  pl.BlockSpec((TOP_K, tr), col)],
        out_specs=[pl.BlockSpec((TOP_K, tr), col), _full((1, n_blk)), _full((1, 1))],
        out_shape=[jax.ShapeDtypeStruct((TOP_K, N), I32), jax.ShapeDtypeStruct((1, n_blk), I32),
                   jax.ShapeDtypeStruct((1, 1), I32)],
        scratch_shapes=[pltpu.VMEM((E, LANES), F32)],
        compiler_params=_params(("arbitrary",)),
        name="plan",
    )(cnt, lower, idx, rank)


def _slab_rows(ref, row, slab):
    return ref.at[pl.ds(pl.multiple_of(row * slab, slab), slab)]


def _scatter_kernel(bexp_ref, nused_ref, *refs):
    dest_refs = refs[:TOP_K]
    h_ref, xs_ref, zero_ref, sem, zsem = refs[TOP_K:]
    ts = dest_refs[0].shape[0]
    slab = h_ref.shape[0] // ts
    n_blk = bexp_ref.shape[0]
    blk = zero_ref.shape[0]

    @pl.when(pl.program_id(0) == 0)
    def _():
        zero_ref[...] = jnp.zeros_like(zero_ref)
        nused = nused_ref[0]

        def zstart(b, n):
            pad = (b >= nused - 1) | (bexp_ref[jnp.minimum(b + 1, n_blk - 1)] != bexp_ref[b])

            @pl.when(pad)
            def _():
                start = pl.multiple_of(b * blk, blk)
                pltpu.make_async_copy(zero_ref, xs_ref.at[pl.ds(start, blk)], zsem).start()

            return n + pad.astype(I32)

        n_zero = lax.fori_loop(0, n_blk, zstart, 0)

        def zwait(_, carry):
            pltpu.make_async_copy(zero_ref, xs_ref.at[pl.ds(0, blk)], zsem).wait()
            return carry

        lax.fori_loop(0, n_zero, zwait, 0)

    def body(t, carry):
        src = _slab_rows(h_ref, t, slab)
        for k in range(TOP_K):
            pltpu.make_async_copy(src, _slab_rows(xs_ref, dest_refs[k][t], slab), sem).start(priority=k % 2)
        return carry

    lax.fori_loop(0, ts, body, 0)
    for _ in range(TOP_K):
        pltpu.make_async_copy(h_ref, xs_ref.at[pl.ds(0, slab * ts)], sem).wait()


def _scatter(bexp, nused, dests, h2s, *, ts, slab):
    rows, W = h2s.shape
    n_blk = bexp.shape[0]
    grid_spec = pltpu.PrefetchScalarGridSpec(
        num_scalar_prefetch=2,
        grid=(rows // (slab * ts),),
        in_specs=[pl.BlockSpec((ts,), lambda i, be, nu: (i,), memory_space=pltpu.SMEM)] * TOP_K
        + [pl.BlockSpec((slab * ts, W), lambda i, be, nu: (i, 0))],
        out_specs=pl.BlockSpec(memory_space=pl.ANY),
        scratch_shapes=[pltpu.VMEM((slab * EXPERT_ROWS, W), h2s.dtype), pltpu.SemaphoreType.DMA(()),
                        pltpu.SemaphoreType.DMA(())],
    )
    return pl.pallas_call(
        _scatter_kernel,
        grid_spec=grid_spec,
        out_shape=jax.ShapeDtypeStruct((n_blk * slab * EXPERT_ROWS, W), h2s.dtype),
        compiler_params=_params(("arbitrary",)),
        name="scatter",
    )(bexp, nused, *dests, h2s)


def _experts_kernel(bexp_ref, nused_ref, xs_ref, wg_ref, wu_ref, wd_ref, ys_ref, wgb_ref, wub_ref, wdb_ref,
                    xstage_ref, ystage_ref):
    i = pl.program_id(0)
    used = i < nused_ref[0]
    fresh = (i == 0) | (bexp_ref[i] != bexp_ref[jnp.maximum(i - 1, 0)])

    @pl.when(used & fresh)
    def _():
        wgb_ref[...] = wg_ref[0].astype(BF16)
        wub_ref[...] = wu_ref[0].astype(BF16)
        wdb_ref[...] = wd_ref[0].astype(BF16)

    @pl.when(used)
    def _():
        x = _load_slabs(xstage_ref, xs_ref[...], wgb_ref.shape[0] // LANES).astype(BF16)
        g = _dot(x, wgb_ref[...])
        u = _dot(x, wub_ref[...])
        _store_slabs(ys_ref, ystage_ref, _dot((_silu(g) * u).astype(BF16), wdb_ref[...]))

    @pl.when(jnp.logical_not(used))
    def _():
        ys_ref[...] = jnp.zeros_like(ys_ref)


def _experts(bexp, nused, xs, wg, wu, wd):
    rows, W = xs.shape
    E, D, H = wg.shape
    blk = (D // LANES) * EXPERT_ROWS
    grid_spec = pltpu.PrefetchScalarGridSpec(
        num_scalar_prefetch=2,
        grid=(rows // blk,),
        in_specs=[pl.BlockSpec((blk, W), lambda i, be, nu: (i, 0)),
                  pl.BlockSpec((1, D, H), lambda i, be, nu: (be[i], 0, 0)),
                  pl.BlockSpec((1, D, H), lambda i, be, nu: (be[i], 0, 0)),
                  pl.BlockSpec((1, H, D), lambda i, be, nu: (be[i], 0, 0))],
        out_specs=pl.BlockSpec((blk, W), lambda i, be, nu: (i, 0)),
        scratch_shapes=[pltpu.VMEM((D, H), BF16), pltpu.VMEM((D, H), BF16), pltpu.VMEM((H, D), BF16),
                        _slab_scratch(EXPERT_ROWS, D), _slab_scratch(EXPERT_ROWS, D)],
    )
    return pl.pallas_call(
        _experts_kernel,
        grid_spec=grid_spec,
        out_shape=jax.ShapeDtypeStruct((rows, W), BF16),
        compiler_params=_params(("arbitrary",)),
        name="experts",
    )(bexp, nused, xs, wg, wu, wd)


def _combine_kernel(*refs, alpha):
    dest_refs = refs[:TOP_K]
    (wt_ref, h2_ref, x1_ref, mod_ref, wsg_ref, wsu_ref, wsd_ref, g_ref, b_ref, ys_ref,
     o_ref, buf_ref, stage0_ref, stage1_ref, sem) = refs[TOP_K:]
    tc, d = h2_ref.shape
    slab = d // LANES

    def body(t, carry):
        for k in range(TOP_K):
            src = _slab_rows(ys_ref, dest_refs[k][t], slab)
            pltpu.make_async_copy(src, _slab_rows(buf_ref.at[k], t, slab), sem).start(priority=k % 2)
        return carry

    lax.fori_loop(0, tc, body, 0)
    h = h2_ref[...]
    mid = (_silu(_dot(h, wsg_ref[...])) * _dot(h, wsu_ref[...])).astype(BF16)
    y = _dot(mid, wsd_ref[...])
    for k in range(TOP_K):
        pltpu.make_async_copy(ys_ref.at[pl.ds(0, slab * tc)], buf_ref.at[k], sem).wait()
    for k in range(TOP_K):
        stage_ref = stage0_ref if k % 2 == 0 else stage1_ref
        y = y + wt_ref[:, k:k + 1] * _load_slabs(stage_ref, buf_ref[k], slab)
    z = alpha * x1_ref[...] + (1.0 + mod_ref[0, 5:6, :]) * y
    o_ref[...] = _ln_plain(z) * g_ref[...] + b_ref[...]


def _combine(dests, wt, h2, x1, mod3, wsg, wsu, wsd, ln_g, ln_b, ys, *, seq, tc, alpha):
    N, D = x1.shape
    slab = D // LANES
    per_b = seq // tc
    row = lambda i: (i, 0)
    return pl.pallas_call(
        functools.partial(_combine_kernel, alpha=alpha),
        grid=(N // tc,),
        in_specs=[pl.BlockSpec((tc,), lambda i: (i,), memory_space=pltpu.SMEM)] * TOP_K
        + [pl.BlockSpec((tc, LANES), row), pl.BlockSpec((tc, D), row), pl.BlockSpec((tc, D), row),
           pl.BlockSpec((1, 6, D), lambda i: (i // per_b, 0, 0)),
           _full(wsg.shape), _full(wsu.shape), _full(wsd.shape), _full(ln_g.shape), _full(ln_b.shape),
           pl.BlockSpec(memory_space=pl.ANY)],
        out_specs=pl.BlockSpec((tc, D), row),
        out_shape=jax.ShapeDtypeStruct((N, D), F32),
        scratch_shapes=[pltpu.VMEM((TOP_K, slab * tc, LANES), ys.dtype), _slab_scratch(tc, D), _slab_scratch(tc, D),
                        pltpu.SemaphoreType.DMA(())],
        compiler_params=_params(("arbitrary",)),
        name="combine",
    )(*dests, wt, h2, x1, mod3, wsg, wsu, wsd, ln_g, ln_b, ys)


def _tiles(seq):
    t = lambda want: min(want, seq)
    return dict(proj=t(512), gla=t(256), route=t(256), move=t(256))


def _layer(x2, mod3, posb, invf, p, *, seq, alpha):
    N, D = x2.shape
    tl = _tiles(seq)
    qa, ka, va, ql, kl, vl, gl, lg = _inproj(x2, mod3, posb, invf, p["w_main"], p["b_main"], p["w_lo"], p["b_lo"],
                                             p["w_gk"], p["b_gk"], seq=seq, tm=tl["proj"])
    oa = _swa(qa, ka, va, p["sinks"], seq=seq)
    og = _gla(ql, kl, vl, gl, lg, p["norm_g"], seq=seq, tc=tl["gla"])
    x1, h2, h2s, logits_t = _outproj(oa, og, x2, mod3, p["w_oa"], p["w_og"], p["b_o"], p["ln1_g"], p["ln1_b"],
                                     p["wr_hi"], p["wr_lo"], seq=seq, tm=tl["proj"], alpha=alpha)
    idx, _, wt, rank, cnt = _route(logits_t, p["bias_col"], p["upper"], tr=tl["route"])
    E = cnt.shape[0]
    n_blk = (N * TOP_K) // EXPERT_ROWS + E
    dest, bexp, nused = _plan(cnt, p["lower"], idx, rank, tr=tl["route"], n_blk=n_blk)
    bexp, nused = bexp.reshape(n_blk), nused.reshape(1)
    dests = [dest[k] for k in range(TOP_K)]
    xs = _scatter(bexp, nused, dests, h2s, ts=tl["move"], slab=D // LANES)
    ys = _experts(bexp, nused, xs, p["wg"], p["wu"], p["wd"])
    return _combine(dests, wt, h2, x1, mod3, p["wsg"], p["wsu"], p["wsd"], p["ln2_g"], p["ln2_b"], ys,
                    seq=seq, tc=tl["move"], alpha=alpha)


def kernel(x, c, positions, w_ada, b_ada, w_in, b_in, attn_sinks, w_gk2, b_gk2, gla_norm_g, w_o, b_o, ln1_g, ln1_b, w_router, router_bias, w_exp_gate, w_exp_up, w_exp_down, w_sh_gate, w_sh_up, w_sh_down, ln2_g, ln2_b):
    B, S, D = x.shape
    depth = w_ada.shape[0]
    E = w_router.shape[2]
    alpha = float((2 * depth) ** 0.25)
    tl = _tiles(S)
    n_main = int(_SEG[-1])

    posb = jnp.broadcast_to(positions.astype(F32).reshape(B * S, 1), (B * S, LANES))
    half = ATTN_HEAD_DIM // 2
    invf = (ROPE_THETA ** (-(jnp.arange(LANES) % half).astype(F32) / half)).reshape(1, LANES)
    tr = tl["route"]
    upper = (jnp.arange(tr)[:, None] < jnp.arange(tr)[None, :]).astype(BF16)
    lower = (jnp.arange(E)[:, None] >= jnp.arange(E)[None, :]).astype(BF16)
    row = lambda v: v.reshape(1, -1)

    x2 = x.reshape(B * S, D)
    for l in range(depth):
        wr_t = w_router[l].T
        wr_hi = wr_t.astype(BF16)
        p = dict(
            w_main=w_in[l][:, :n_main].astype(BF16), b_main=row(b_in[l][:n_main]),
            w_lo=jnp.pad(w_in[l][:, n_main:], ((0, 0), (0, LANES - GLA_GATE_RANK))).astype(BF16),
            b_lo=row(jnp.pad(b_in[l][n_main:], (0, LANES - GLA_GATE_RANK))),
            w_gk=jnp.pad(w_gk2[l], ((0, LANES - GLA_GATE_RANK), (0, 0))), b_gk=row(b_gk2[l]),
            sinks=attn_sinks[l], norm_g=row(gla_norm_g[l]),
            w_oa=w_o[l][:ATTN_WIDTH].astype(BF16), w_og=w_o[l][ATTN_WIDTH:].astype(BF16), b_o=row(b_o[l]),
            ln1_g=row(ln1_g[l]), ln1_b=row(ln1_b[l]),
            wr_hi=wr_hi, wr_lo=(wr_t - wr_hi.astype(F32)).astype(BF16),
            bias_col=jnp.broadcast_to(router_bias[l].reshape(E, 1), (E, LANES)),
            upper=upper, lower=lower,
            wg=w_exp_gate[l], wu=w_exp_up[l], wd=w_exp_down[l],
            wsg=w_sh_gate[l].astype(BF16), wsu=w_sh_up[l].astype(BF16), wsd=w_sh_down[l].astype(BF16),
            ln2_g=row(ln2_g[l]), ln2_b=row(ln2_b[l]),
        )
        mod = _mod(c, w_ada[l], b_ada[l])
        mod3 = mod.reshape(B, 6, D)
        x2 = _layer(x2, mod3, posb, invf, p, seq=S, alpha=alpha)
    return x2.reshape(B, S, D)
```

```python
import functools

import jax
import jax.numpy as jnp
import numpy as np
from jax import lax
from jax.experimental import pallas as pl
from jax.experimental.pallas import tpu as pltpu

F32 = jnp.float32
BF16 = jnp.bfloat16
I32 = jnp.int32

ATTN_Q_HEADS = 8
ATTN_KV_HEADS = 2
ATTN_HEAD_DIM = 64
ATTN_BLOCK = 128
ROPE_THETA = 10000.0
GLA_HEADS = 4
GLA_KEY_DIM = 64
GLA_VAL_DIM = 128
GLA_GATE_RANK = 16
GLA_GATE_NORM = 16.0
GLA_CHUNK = 64
N_GROUPS = 8
TOPK_GROUPS = 4
TOP_K = 8
ROUTED_SCALE = 2.5
LN_EPS = 1e-5
NEG_INF = -1e30
REMOVED = -3e38

ATTN_WIDTH = ATTN_Q_HEADS * ATTN_HEAD_DIM
KV_WIDTH = ATTN_KV_HEADS * ATTN_HEAD_DIM
GLA_KWIDTH = GLA_HEADS * GLA_KEY_DIM
GLA_WIDTH = GLA_HEADS * GLA_VAL_DIM

LANES = 128
VMEM_LIMIT = 56 * 1024 * 1024
EXPERT_ROWS = 256


def _params(sem):
    return pltpu.CompilerParams(dimension_semantics=sem, vmem_limit_bytes=VMEM_LIMIT)


def _full(shape):
    return pl.BlockSpec(shape, lambda *_: (0,) * len(shape))


def _split_bf16(a):
    hi = a.astype(BF16)
    lo = (a - hi.astype(F32)).astype(BF16)
    return hi, lo


def _dot(a, b, dims=(((1,), (0,)), ((), ()))):
    return lax.dot_general(a, b, dims, preferred_element_type=F32)


NT = (((1,), (1,)), ((), ()))
TN = (((0,), (0,)), ((), ()))


def _dot3(a, b, dims=(((1,), (0,)), ((), ()))):
    ah, al = _split_bf16(a)
    bh, bl = _split_bf16(b)
    return _dot(ah, bh, dims) + (_dot(ah, bl, dims) + _dot(al, bh, dims))


def _ln_plain(x):
    mu = jnp.mean(x, axis=-1, keepdims=True)
    xc = x - mu
    var = jnp.mean(xc * xc, axis=-1, keepdims=True)
    return xc * lax.rsqrt(var + LN_EPS)


def _silu(x):
    return x * (1.0 / (1.0 + jnp.exp(-x)))


def _sigmoid(x):
    return 1.0 / (1.0 + jnp.exp(-x))


def _mod_kernel(c_ref, w_ref, b_ref, o_ref):
    c = c_ref[...]
    o_ref[...] = _dot3(_silu(c), w_ref[...]) + b_ref[...]


def _mod(c, w_ada, b_ada):
    B, D = c.shape
    n = w_ada.shape[1] // D
    return pl.pallas_call(
        _mod_kernel,
        grid=(n,),
        in_specs=[_full((B, D)),
                  pl.BlockSpec((D, D), lambda j: (0, j)),
                  pl.BlockSpec((1, D), lambda j: (0, j))],
        out_specs=pl.BlockSpec((B, D), lambda j: (0, j)),
        out_shape=jax.ShapeDtypeStruct((B, n * D), F32),
        compiler_params=_params(("arbitrary",)),
        name="mod",
    )(c, w_ada, b_ada.reshape(1, -1))


_SEG = np.cumsum([0, ATTN_WIDTH, KV_WIDTH, KV_WIDTH, GLA_KWIDTH, GLA_KWIDTH, GLA_WIDTH, GLA_WIDTH])


def _rope_chunk(t, cos, sin_signed, first_half):
    up = pltpu.roll(t, LANES - 32, axis=1)
    dn = pltpu.roll(t, 32, axis=1)
    return t * cos + jnp.where(first_half, up, dn) * sin_signed


def _inproj_kernel(x_ref, mod_ref, pos_ref, invf_ref, w_ref, b_ref, wlo_ref, blo_ref, wgk_ref, bgk_ref,
                   qa_ref, ka_ref, va_ref, ql_ref, kl_ref, vl_ref, gl_ref, lg_ref):
    x = x_ref[...]
    h = _ln_plain(x) * (1.0 + mod_ref[0, 1:2, :]) + mod_ref[0, 0:1, :]
    hb = h.astype(BF16)

    def seg(i):
        lo, hi = int(_SEG[i]), int(_SEG[i + 1])
        return _dot(hb, w_ref[:, lo:hi]) + b_ref[:, lo:hi]

    ang = pos_ref[...] * invf_ref[...]
    cos = jnp.cos(ang)
    sin = jnp.sin(ang)
    lane = lax.broadcasted_iota(I32, ang.shape, 1)
    first_half = (lane % ATTN_HEAD_DIM) < (ATTN_HEAD_DIM // 2)
    sin_signed = jnp.where(first_half, -sin, sin)

    q = seg(0)
    scale = ATTN_HEAD_DIM ** -0.5
    for c in range(ATTN_WIDTH // LANES):
        t = q[:, c * LANES:(c + 1) * LANES]
        qa_ref[:, c * LANES:(c + 1) * LANES] = (_rope_chunk(t, cos, sin_signed, first_half) * scale).astype(BF16)
    ka_ref[...] = _rope_chunk(seg(1), cos, sin_signed, first_half).astype(BF16)
    va_ref[...] = seg(2).astype(BF16)
    ql_ref[...] = seg(3).astype(BF16)
    kl_ref[...] = seg(4).astype(BF16)
    vl_ref[...] = seg(5).astype(BF16)
    gl_ref[...] = seg(6).astype(BF16)
    gk_lo = _dot(hb, wlo_ref[...]) + blo_ref[...]
    gk = _dot3(gk_lo, wgk_ref[...]) + bgk_ref[...]
    log_sig = jnp.minimum(gk, 0.0) - jnp.log(1.0 + jnp.exp(-jnp.abs(gk)))
    lg_ref[...] = log_sig * (1.0 / GLA_GATE_NORM)


def _inproj(x2, mod3, posb, invf, w_main, b_main, w_lo, b_lo, w_gk, b_gk, *, seq, tm):
    N, D = x2.shape
    per_b = seq // tm
    widths = [ATTN_WIDTH, KV_WIDTH, KV_WIDTH, GLA_KWIDTH, GLA_KWIDTH, GLA_WIDTH, GLA_WIDTH, GLA_KWIDTH]
    dtypes = [BF16] * 7 + [F32]
    row = lambda i: (i, 0)
    return pl.pallas_call(
        _inproj_kernel,
        grid=(N // tm,),
        in_specs=[pl.BlockSpec((tm, D), row),
                  pl.BlockSpec((1, 6, D), lambda i: (i // per_b, 0, 0)),
                  pl.BlockSpec((tm, LANES), row),
                  _full(invf.shape), _full(w_main.shape), _full(b_main.shape),
                  _full(w_lo.shape), _full(b_lo.shape), _full(w_gk.shape), _full(b_gk.shape)],
        out_specs=[pl.BlockSpec((tm, w), row) for w in widths],
        out_shape=[jax.ShapeDtypeStruct((N, w), dt) for w, dt in zip(widths, dtypes)],
        compiler_params=_params(("parallel",)),
        name="inproj",
    )(x2, mod3, posb, invf, w_main, b_main, w_lo, b_lo, w_gk, b_gk)


def _swa_kernel(sink_ref, q_ref, kc_ref, kp_ref, vc_ref, vp_ref, o_ref):
    j = pl.program_id(1)
    blk = ATTN_BLOCK
    k2 = jnp.concatenate([kp_ref[...], kc_ref[...]], axis=0)
    v2 = jnp.concatenate([vp_ref[...], vc_ref[...]], axis=0)
    row = lax.broadcasted_iota(I32, (blk, 2 * blk), 0)
    col = lax.broadcasted_iota(I32, (blk, 2 * blk), 1)
    dist = row + blk - col
    valid = (dist >= 0) & (dist < blk) & ((col >= blk) | (j > 0))
    group = ATTN_Q_HEADS // ATTN_KV_HEADS
    for h in range(ATTN_Q_HEADS):
        kv = h // group
        qh = q_ref[:, h * ATTN_HEAD_DIM:(h + 1) * ATTN_HEAD_DIM]
        kh = k2[:, kv * ATTN_HEAD_DIM:(kv + 1) * ATTN_HEAD_DIM]
        vh = v2[:, kv * ATTN_HEAD_DIM:(kv + 1) * ATTN_HEAD_DIM]
        s = jnp.where(valid, _dot(qh, kh, NT), NEG_INF)
        sink = sink_ref[h]
        m = jnp.maximum(jnp.max(s, axis=-1, keepdims=True), sink)
        p = jnp.exp(s - m)
        denom = jnp.sum(p, axis=-1, keepdims=True) + jnp.exp(sink - m)
        o = _dot(p.astype(BF16), vh) / denom
        o_ref[:, h * ATTN_HEAD_DIM:(h + 1) * ATTN_HEAD_DIM] = o.astype(BF16)


def _swa(q, k, v, sinks, *, seq):
    N = q.shape[0]
    nb = seq // ATTN_BLOCK
    B = N // seq
    cur = lambda b, j: (b * nb + j, 0)
    prev = lambda b, j: (b * nb + jnp.maximum(j - 1, 0), 0)
    return pl.pallas_call(
        _swa_kernel,
        grid=(B, nb),
        in_specs=[pl.BlockSpec(memory_space=pltpu.SMEM),
                  pl.BlockSpec((ATTN_BLOCK, ATTN_WIDTH), cur),
                  pl.BlockSpec((ATTN_BLOCK, KV_WIDTH), cur),
                  pl.BlockSpec((ATTN_BLOCK, KV_WIDTH), prev),
                  pl.BlockSpec((ATTN_BLOCK, KV_WIDTH), cur),
                  pl.BlockSpec((ATTN_BLOCK, KV_WIDTH), prev)],
        out_specs=pl.BlockSpec((ATTN_BLOCK, ATTN_WIDTH), cur),
        out_shape=jax.ShapeDtypeStruct((N, ATTN_WIDTH), BF16),
        compiler_params=_params(("parallel", "parallel")),
        name="swa",
    )(sinks, q, k, k, v, v)


def _gla_kernel(q_ref, k_ref, v_ref, g_ref, lg_ref, ng_ref, o_ref, st_ref, *, chunks):
    @pl.when(pl.program_id(1) == 0)
    def _():
        st_ref[...] = jnp.zeros_like(st_ref)

    C = GLA_CHUNK
    dk, dv = GLA_KEY_DIM, GLA_VAL_DIM
    r = lax.broadcasted_iota(I32, (C, C), 0)
    c = lax.broadcasted_iota(I32, (C, C), 1)
    causal = c <= r
    tri = jnp.where(causal, 1.0, 0.0).astype(BF16)
    st = st_ref[...]
    for n in range(chunks):
        rows = slice(n * C, (n + 1) * C)
        lg_hi, lg_lo = _split_bf16(lg_ref[rows, :])
        b = _dot(tri, lg_hi) + _dot(tri, lg_lo)
        b_last = b[C - 1:C, :]
        q_in = (q_ref[rows, :].astype(F32) * (dk ** -0.5) * jnp.exp(b)).astype(BF16)
        kf = k_ref[rows, :].astype(F32)
        k_in = (kf * jnp.exp(-b)).astype(BF16)
        k_out = (kf * jnp.exp(b_last - b)).astype(BF16)
        decay = jnp.exp(b_last)
        stb = st.astype(BF16)
        ut = []
        for h in range(GLA_HEADS):
            ks = slice(h * dk, (h + 1) * dk)
            vs = slice(h * dv, (h + 1) * dv)
            vh = v_ref[rows, vs]
            a = jnp.where(causal, _dot(q_in[:, ks], k_in[:, ks], NT), 0.0).astype(BF16)
            o = _dot(a, vh) + _dot(q_in[:, ks], stb[:, ks], NT)
            ut.append(_dot(vh, k_out[:, ks], TN))
            o = o * lax.rsqrt(jnp.mean(o * o, axis=-1, keepdims=True) + LN_EPS) * ng_ref[...]
            o_ref[rows, vs] = (o * _silu(g_ref[rows, vs].astype(F32))).astype(BF16)
        st = st * decay + jnp.concatenate(ut, axis=1)
    st_ref[...] = st


def _gla(q, k, v, g, lg, norm_g, *, seq, tc):
    N = q.shape[0]
    B = N // seq
    per_b = seq // tc
    row = lambda b, j: (b * per_b + j, 0)
    return pl.pallas_call(
        functools.partial(_gla_kernel, chunks=tc // GLA_CHUNK),
        grid=(B, per_b),
        in_specs=[pl.BlockSpec((tc, GLA_KWIDTH), row), pl.BlockSpec((tc, GLA_KWIDTH), row),
                  pl.BlockSpec((tc, GLA_WIDTH), row), pl.BlockSpec((tc, GLA_WIDTH), row),
                  pl.BlockSpec((tc, GLA_KWIDTH), row), _full(norm_g.shape)],
        out_specs=pl.BlockSpec((tc, GLA_WIDTH), row),
        out_shape=jax.ShapeDtypeStruct((N, GLA_WIDTH), BF16),
        scratch_shapes=[pltpu.VMEM((GLA_VAL_DIM, GLA_KWIDTH), F32)],
        compiler_params=_params(("parallel", "arbitrary")),
        name="gla",
    )(q, k, v, g, lg, norm_g)


def _slab_scratch(rows, d):
    return pltpu.VMEM((rows * (d // LANES), LANES), F32)


def _store_slabs(slabs_ref, stage_ref, v):
    rows, d = v.shape
    n = d // LANES
    for c in range(n):
        stage_ref[pl.ds(c, rows, stride=n), :] = v[:, c * LANES:(c + 1) * LANES]
    slabs_ref[...] = stage_ref[...].astype(slabs_ref.dtype)


def _load_slabs(stage_ref, slabs, n):
    stage_ref[...] = slabs.astype(F32)
    rows = stage_ref.shape[0] // n
    return jnp.concatenate([stage_ref[pl.ds(c, rows, stride=n), :] for c in range(n)], axis=1)


def _outproj_kernel(oa_ref, og_ref, x_ref, mod_ref, woa_ref, wog_ref, bo_ref, g_ref, b_ref, wrh_ref, wrl_ref,
                    x1_ref, h2_ref, h2s_ref, lt_ref, stage_ref, *, alpha):
    y = _dot(oa_ref[...], woa_ref[...]) + _dot(og_ref[...], wog_ref[...]) + bo_ref[...]
    z = alpha * x_ref[...] + (1.0 + mod_ref[0, 2:3, :]) * y
    x1 = _ln_plain(z) * g_ref[...] + b_ref[...]
    x1_ref[...] = x1
    h2 = _ln_plain(x1) * (1.0 + mod_ref[0, 4:5, :]) + mod_ref[0, 3:4, :]
    h2_ref[...] = h2.astype(BF16)
    _store_slabs(h2s_ref, stage_ref, h2)
    hh, hl = _split_bf16(h2)
    wh = wrh_ref[...]
    lt_ref[...] = _dot(wh, hh, NT) + (_dot(wh, hl, NT) + _dot(wrl_ref[...], hh, NT))


def _outproj(oa, og, x2, mod3, w_oa, w_og, b_o, ln_g, ln_b, wr_hi, wr_lo, *, seq, tm, alpha):
    N, D = x2.shape
    E = wr_hi.shape[0]
    per_b = seq // tm
    row = lambda i: (i, 0)
    return pl.pallas_call(
        functools.partial(_outproj_kernel, alpha=alpha),
        grid=(N // tm,),
        in_specs=[pl.BlockSpec((tm, ATTN_WIDTH), row), pl.BlockSpec((tm, GLA_WIDTH), row),
                  pl.BlockSpec((tm, D), row),
                  pl.BlockSpec((1, 6, D), lambda i: (i // per_b, 0, 0)),
                  _full(w_oa.shape), _full(w_og.shape), _full(b_o.shape), _full(ln_g.shape), _full(ln_b.shape),
                  _full(wr_hi.shape), _full(wr_lo.shape)],
        out_specs=[pl.BlockSpec((tm, D), row), pl.BlockSpec((tm, D), row),
                   pl.BlockSpec((tm * (D // LANES), LANES), row), pl.BlockSpec((E, tm), lambda i: (0, i))],
        out_shape=[jax.ShapeDtypeStruct((N, D), F32), jax.ShapeDtypeStruct((N, D), BF16),
                   jax.ShapeDtypeStruct((N * (D // LANES), LANES), BF16), jax.ShapeDtypeStruct((E, N), F32)],
        scratch_shapes=[_slab_scratch(tm, D)],
        compiler_params=_params(("parallel",)),
        name="outproj",
    )(oa, og, x2, mod3, w_oa, w_og, b_o, ln_g, ln_b, wr_hi, wr_lo)


def _first_index(eq, idx, size):
    return jnp.min(jnp.where(eq, idx, float(size)), axis=0, keepdims=True)


def _route_kernel(lt_ref, bias_ref, upper_ref, idx_ref, w_ref, wt_ref, rank_ref, cnt_ref, base_ref):
    @pl.when(pl.program_id(0) == 0)
    def _():
        base_ref[...] = jnp.zeros_like(base_ref)

    E, t = lt_ref.shape
    gsz = E // N_GROUPS
    scores = _sigmoid(lt_ref[...])
    biased = scores + bias_ref[:, 0:1]
    gi = lax.broadcasted_iota(I32, (gsz, t), 0).astype(F32)
    gs_rows = []
    for g in range(N_GROUPS):
        grp = biased[g * gsz:(g + 1) * gsz, :]
        m1 = jnp.max(grp, axis=0, keepdims=True)
        first = _first_index(grp == m1, gi, gsz)
        m2 = jnp.max(jnp.where(gi == first, REMOVED, grp), axis=0, keepdims=True)
        gs_rows.append(m1 + m2)
    gs = jnp.concatenate(gs_rows, axis=0)
    ri = lax.broadcasted_iota(I32, (N_GROUPS, t), 0).astype(F32)
    gsel = jnp.zeros((N_GROUPS, t), F32)
    for _ in range(TOPK_GROUPS):
        m = jnp.max(gs, axis=0, keepdims=True)
        hit = ri == _first_index(gs == m, ri, N_GROUPS)
        gsel = jnp.where(hit, 1.0, gsel)
        gs = jnp.where(hit, REMOVED, gs)
    cand = jnp.concatenate(
        [jnp.where(gsel[g:g + 1, :] > 0.5, biased[g * gsz:(g + 1) * gsz, :], NEG_INF) for g in range(N_GROUPS)],
        axis=0)
    ei = lax.broadcasted_iota(I32, (E, t), 0).astype(F32)
    idx_rows, w_rows, hits = [], [], []
    chosen = jnp.zeros((E, t), F32)
    for _ in range(TOP_K):
        m = jnp.max(cand, axis=0, keepdims=True)
        fi = _first_index(cand == m, ei, E)
        hit = ei == fi
        idx_rows.append(fi)
        w_rows.append(jnp.sum(jnp.where(hit, scores, 0.0), axis=0, keepdims=True))
        hits.append(hit)
        chosen = jnp.where(hit, 1.0, chosen)
        cand = jnp.where(hit, REMOVED, cand)
    w = jnp.concatenate(w_rows, axis=0)
    w = w / jnp.sum(w, axis=0, keepdims=True) * ROUTED_SCALE
    idx_ref[...] = jnp.concatenate(idx_rows, axis=0).astype(I32)
    w_ref[...] = w
    wpad = jnp.concatenate([w, jnp.zeros((LANES - TOP_K, t), F32)], axis=0)
    wt_ref[...] = wpad.T
    prefix = _dot(chosen.astype(BF16), upper_ref[...])
    pos = base_ref[:, 0:1] + prefix
    rank_ref[...] = jnp.concatenate(
        [jnp.sum(jnp.where(hit, pos, 0.0), axis=0, keepdims=True) for hit in hits], axis=0).astype(I32)
    base_ref[...] = base_ref[...] + jnp.sum(chosen, axis=1, keepdims=True)
    cnt_ref[...] = base_ref[...]


def _route(logits_t, bias_col, upper, *, tr):
    E, N = logits_t.shape
    col = lambda i: (0, i)
    return pl.pallas_call(
        _route_kernel,
        grid=(N // tr,),
        in_specs=[pl.BlockSpec((E, tr), col), _full(bias_col.shape), _full(upper.shape)],
        out_specs=[pl.BlockSpec((TOP_K, tr), col), pl.BlockSpec((TOP_K, tr), col),
                   pl.BlockSpec((tr, LANES), lambda i: (i, 0)),
                   pl.BlockSpec((TOP_K, tr), col), _full((E, LANES))],
        out_shape=[jax.ShapeDtypeStruct((TOP_K, N), I32), jax.ShapeDtypeStruct((TOP_K, N), F32),
                   jax.ShapeDtypeStruct((N, LANES), F32),
                   jax.ShapeDtypeStruct((TOP_K, N), I32), jax.ShapeDtypeStruct((E, LANES), F32)],
        scratch_shapes=[pltpu.VMEM((E, LANES), F32)],
        compiler_params=_params(("arbitrary",)),
        name="route",
    )(logits_t, bias_col, upper)


def _plan_kernel(cnt_ref, lower_ref, idx_ref, rank_ref, dest_ref, gstart_ref, nblk_ref, pstart_ref):
    E = cnt_ref.shape[0]
    bm = float(EXPERT_ROWS)

    @pl.when(pl.program_id(0) == 0)
    def _():
        cnt = cnt_ref[...]
        nblk = jnp.floor((cnt + (bm - 1.0)) * (1.0 / bm))
        padded = nblk * bm
        hi = jnp.floor(padded * (1.0 / 256.0))
        lo = padded - hi * 256.0
        low = lower_ref[...]
        pend = 256.0 * _dot(low, hi.astype(BF16)) + _dot(low, lo.astype(BF16))
        pstart_ref[...] = pend - padded
        gstart_ref[...] = ((pend - padded) * (1.0 / bm)).astype(I32)
        nblk_ref[...] = nblk.astype(I32)

    t = idx_ref.shape[1]
    ei = lax.broadcasted_iota(I32, (E, t), 0)
    ps = pstart_ref[:, 0:1]
    rows = []
    for k in range(TOP_K):
        hit = ei == idx_ref[k:k + 1, :]
        rows.append(jnp.sum(jnp.where(hit, ps, 0.0), axis=0, keepdims=True))
    dest_ref[...] = rank_ref[...] + jnp.concatenate(rows, axis=0).astype(I32)


def _plan(cnt, lower, idx, rank, *, tr):
    E = cnt.shape[0]
    N = idx.shape[1]
    col = lambda i: (0, i)
    return pl.pallas_call(
        _plan_kernel,
        grid=(N // tr,),
        in_specs=[_full(cnt.shape), _full(lower.shape), pl.BlockSpec((TOP_K, tr), col),
                  pl.BlockSpec((TOP_K, tr), col)],
        out_specs=[pl.BlockSpec((TOP_K, tr), col), _full((E, LANES)), _full((E, LANES))],
        out_shape=[jax.ShapeDtypeStruct((TOP_K, N), I32), jax.ShapeDtypeStruct((E, LANES), I32),
                   jax.ShapeDtypeStruct((E, LANES), I32)],
        scratch_shapes=[pltpu.VMEM((E, LANES), F32)],
        compiler_params=_params(("arbitrary",)),
        name="plan",
    )(cnt, lower, idx, rank)


def _slab_rows(ref, row, slab):
    return ref.at[pl.ds(pl.multiple_of(row * slab, slab), slab)]


def _fill_zero_blocks(zero_ref, out_ref, zsem, gstart_ref, nblk_ref, *, last_of_expert):
    blk = zero_ref.shape[0]
    n_exp = gstart_ref.shape[0]
    n_blk = out_ref.shape[0] // blk
    nused = gstart_ref[n_exp - 1] + nblk_ref[n_exp - 1]
    zero_ref[...] = jnp.zeros_like(zero_ref)

    def blk_copy(b):
        return pltpu.make_async_copy(zero_ref, out_ref.at[pl.ds(pl.multiple_of(b * blk, blk), blk)], zsem)

    def tail(b, carry):
        blk_copy(b).start()
        return carry

    lax.fori_loop(nused, n_blk, tail, 0)
    n_started = n_blk - nused
    if last_of_expert:
        def last(e, n):
            has = nblk_ref[e] > 0

            @pl.when(has)
            def _():
                blk_copy(gstart_ref[e] + nblk_ref[e] - 1).start()

            return n + has.astype(I32)

        n_started = lax.fori_loop(0, n_exp, last, n_started)

    def wait(_, carry):
        blk_copy(0).wait()
        return carry

    lax.fori_loop(0, n_started, wait, 0)


def _scatter_kernel(gstart_ref, nblk_ref, *refs):
    dest_refs = refs[:TOP_K]
    h_ref, xs_ref, zero_ref, sem, zsem = refs[TOP_K:]
    ts = dest_refs[0].shape[0]
    slab = h_ref.shape[0] // ts

    @pl.when(pl.program_id(0) == 0)
    def _():
        _fill_zero_blocks(zero_ref, xs_ref, zsem, gstart_ref, nblk_ref, last_of_expert=True)

    def body(t, carry):
        src = _slab_rows(h_ref, t, slab)
        for k in range(TOP_K):
            pltpu.make_async_copy(src, _slab_rows(xs_ref, dest_refs[k][t], slab), sem).start(priority=k % 2)
        return carry

    lax.fori_loop(0, ts, body, 0)
    for _ in range(TOP_K):
        pltpu.make_async_copy(h_ref, xs_ref.at[pl.ds(0, slab * ts)], sem).wait()


def _scatter(gstart, nblk, dests, h2s, *, ts, slab, n_blk):
    rows, W = h2s.shape
    grid_spec = pltpu.PrefetchScalarGridSpec(
        num_scalar_prefetch=2,
        grid=(rows // (slab * ts),),
        in_specs=[pl.BlockSpec((ts,), lambda i, be, nu: (i,), memory_space=pltpu.SMEM)] * TOP_K
        + [pl.BlockSpec((slab * ts, W), lambda i, be, nu: (i, 0))],
        out_specs=pl.BlockSpec(memory_space=pl.ANY),
        scratch_shapes=[pltpu.VMEM((slab * EXPERT_ROWS, W), h2s.dtype), pltpu.SemaphoreType.DMA(()),
                        pltpu.SemaphoreType.DMA(())],
    )
    return pl.pallas_call(
        _scatter_kernel,
        grid_spec=grid_spec,
        out_shape=jax.ShapeDtypeStruct((n_blk * slab * EXPERT_ROWS, W), h2s.dtype),
        compiler_params=_params(("arbitrary",)),
        name="scatter",
    )(gstart, nblk, *dests, h2s)


def _experts_kernel(gstart_ref, nblk_ref, wg_ref, wu_ref, wd_ref, xs_ref, ys_ref, xbuf_ref, ybuf_ref,
                    wgb_ref, wub_ref, wdb_ref, xstage_ref, ystage_ref, zero_ref, xsem, ysem, zsem):
    e = pl.program_id(0)
    n_exp = pl.num_programs(0)
    blk = zero_ref.shape[0]
    slab = wgb_ref.shape[0] // LANES
    nused = gstart_ref[n_exp - 1] + nblk_ref[n_exp - 1]
    g0 = gstart_ref[e]
    n = nblk_ref[e]

    def x_copy(g, slot):
        return pltpu.make_async_copy(xs_ref.at[pl.ds(pl.multiple_of(g * blk, blk), blk)], xbuf_ref.at[slot],
                                     xsem.at[slot])

    def y_copy(g, slot):
        return pltpu.make_async_copy(ybuf_ref.at[slot], ys_ref.at[pl.ds(pl.multiple_of(g * blk, blk), blk)],
                                     ysem.at[slot])

    @pl.when(e == 0)
    def _():
        x_copy(0, 0).start()

    @pl.when(n > 0)
    def _():
        wgb_ref[...] = wg_ref[0].astype(BF16)
        wub_ref[...] = wu_ref[0].astype(BF16)
        wdb_ref[...] = wd_ref[0].astype(BF16)

    def block(j, carry):
        g = g0 + j
        slot = lax.rem(g, 2)
        x_copy(g, slot).wait()

        @pl.when(g + 1 < nused)
        def _():
            x_copy(g + 1, 1 - slot).start()

        x = _load_slabs(xstage_ref, xbuf_ref[slot], slab).astype(BF16)
        gate = _dot(x, wgb_ref[...])
        up = _dot(x, wub_ref[...])
        y = _dot((_silu(gate) * up).astype(BF16), wdb_ref[...])

        @pl.when(g >= 2)
        def _():
            y_copy(g - 2, slot).wait()

        _store_slabs(ybuf_ref.at[slot], ystage_ref, y)
        y_copy(g, slot).start()
        return carry

    lax.fori_loop(0, n, block, 0)

    @pl.when(e == n_exp - 1)
    def _():
        @pl.when(nused >= 2)
        def _():
            y_copy(nused - 2, lax.rem(nused, 2)).wait()

        y_copy(nused - 1, lax.rem(nused - 1, 2)).wait()
        _fill_zero_blocks(zero_ref, ys_ref, zsem, gstart_ref, nblk_ref, last_of_expert=False)


def _experts(gstart, nblk, xs, wg, wu, wd):
    rows, W = xs.shape
    E, D, H = wg.shape
    blk = (D // LANES) * EXPERT_ROWS
    grid_spec = pltpu.PrefetchScalarGridSpec(
        num_scalar_prefetch=2,
        grid=(E,),
        in_specs=[pl.BlockSpec((1, D, H), lambda e, gs, nb: (e, 0, 0)),
                  pl.BlockSpec((1, D, H), lambda e, gs, nb: (e, 0, 0)),
                  pl.BlockSpec((1, H, D), lambda e, gs, nb: (e, 0, 0)),
                  pl.BlockSpec(memory_space=pl.ANY)],
        out_specs=pl.BlockSpec(memory_space=pl.ANY),
        scratch_shapes=[pltpu.VMEM((2, blk, W), xs.dtype), pltpu.VMEM((2, blk, W), xs.dtype),
                        pltpu.VMEM((D, H), BF16), pltpu.VMEM((D, H), BF16), pltpu.VMEM((H, D), BF16),
                        _slab_scratch(EXPERT_ROWS, D), _slab_scratch(EXPERT_ROWS, D),
                        pltpu.VMEM((blk, W), xs.dtype),
                        pltpu.SemaphoreType.DMA((2,)), pltpu.SemaphoreType.DMA((2,)), pltpu.SemaphoreType.DMA(())],
    )
    return pl.pallas_call(
        _experts_kernel,
        grid_spec=grid_spec,
        out_shape=jax.ShapeDtypeStruct((rows, W), xs.dtype),
        compiler_params=_params(("arbitrary",)),
        name="experts",
    )(gstart, nblk, wg, wu, wd, xs)


def _combine_kernel(*refs, alpha):
    dest_refs = refs[:TOP_K]
    (wt_ref, h2_ref, x1_ref, mod_ref, wsg_ref, wsu_ref, wsd_ref, g_ref, b_ref, ys_ref,
     o_ref, buf_ref, stage0_ref, stage1_ref, sem) = refs[TOP_K:]
    tc, d = h2_ref.shape
    slab = d // LANES

    def body(t, carry):
        for k in range(TOP_K):
            src = _slab_rows(ys_ref, dest_refs[k][t], slab)
            pltpu.make_async_copy(src, _slab_rows(buf_ref.at[k], t, slab), sem).start(priority=k % 2)
        return carry

    lax.fori_loop(0, tc, body, 0)
    h = h2_ref[...]
    mid = (_silu(_dot(h, wsg_ref[...])) * _dot(h, wsu_ref[...])).astype(BF16)
    y = _dot(mid, wsd_ref[...])
    for k in range(TOP_K):
        pltpu.make_async_copy(ys_ref.at[pl.ds(0, slab * tc)], buf_ref.at[k], sem).wait()
    for k in range(TOP_K):
        stage_ref = stage0_ref if k % 2 == 0 else stage1_ref
        y = y + wt_ref[:, k:k + 1] * _load_slabs(stage_ref, buf_ref[k], slab)
    z = alpha * x1_ref[...] + (1.0 + mod_ref[0, 5:6, :]) * y
    o_ref[...] = _ln_plain(z) * g_ref[...] + b_ref[...]


def _combine(dests, wt, h2, x1, mod3, wsg, wsu, wsd, ln_g, ln_b, ys, *, seq, tc, alpha):
    N, D = x1.shape
    slab = D // LANES
    per_b = seq // tc
    row = lambda i: (i, 0)
    return pl.pallas_call(
        functools.partial(_combine_kernel, alpha=alpha),
        grid=(N // tc,),
        in_specs=[pl.BlockSpec((tc,), lambda i: (i,), memory_space=pltpu.SMEM)] * TOP_K
        + [pl.BlockSpec((tc, LANES), row), pl.BlockSpec((tc, D), row), pl.BlockSpec((tc, D), row),
           pl.BlockSpec((1, 6, D), lambda i: (i // per_b, 0, 0)),
           _full(wsg.shape), _full(wsu.shape), _full(wsd.shape), _full(ln_g.shape), _full(ln_b.shape),
           pl.BlockSpec(memory_space=pl.ANY)],
        out_specs=pl.BlockSpec((tc, D), row),
        out_shape=jax.ShapeDtypeStruct((N, D), F32),
        scratch_shapes=[pltpu.VMEM((TOP_K, slab * tc, LANES), ys.dtype), _slab_scratch(tc, D), _slab_scratch(tc, D),
                        pltpu.SemaphoreType.DMA(())],
        compiler_params=_params(("arbitrary",)),
        name="combine",
    )(*dests, wt, h2, x1, mod3, wsg, wsu, wsd, ln_g, ln_b, ys)


def _tiles(seq):
    t = lambda want: min(want, seq)
    return dict(proj=t(512), gla=t(256), route=t(256), move=t(256))


def _layer(x2, mod3, posb, invf, p, *, seq, alpha):
    N, D = x2.shape
    tl = _tiles(seq)
    qa, ka, va, ql, kl, vl, gl, lg = _inproj(x2, mod3, posb, invf, p["w_main"], p["b_main"], p["w_lo"], p["b_lo"],
                                             p["w_gk"], p["b_gk"], seq=seq, tm=tl["proj"])
    oa = _swa(qa, ka, va, p["sinks"], seq=seq)
    og = _gla(ql, kl, vl, gl, lg, p["norm_g"], seq=seq, tc=tl["gla"])
    x1, h2, h2s, logits_t = _outproj(oa, og, x2, mod3, p["w_oa"], p["w_og"], p["b_o"], p["ln1_g"], p["ln1_b"],
                                     p["wr_hi"], p["wr_lo"], seq=seq, tm=tl["proj"], alpha=alpha)
    idx, _, wt, rank, cnt = _route(logits_t, p["bias_col"], p["upper"], tr=tl["route"])
    E = cnt.shape[0]
    n_blk = (N * TOP_K) // EXPERT_ROWS + E
    dest, gstart, nblk = _plan(cnt, p["lower"], idx, rank, tr=tl["route"])
    gstart, nblk = gstart[:, 0], nblk[:, 0]
    dests = [dest[k] for k in range(TOP_K)]
    xs = _scatter(gstart, nblk, dests, h2s, ts=tl["move"], slab=D // LANES, n_blk=n_blk)
    ys = _experts(gstart, nblk, xs, p["wg"], p["wu"], p["wd"])
    return _combine(dests, wt, h2, x1, mod3, p["wsg"], p["wsu"], p["wsd"], p["ln2_g"], p["ln2_b"], ys,
                    seq=seq, tc=tl["move"], alpha=alpha)


def kernel(x, c, positions, w_ada, b_ada, w_in, b_in, attn_sinks, w_gk2, b_gk2, gla_norm_g, w_o, b_o, ln1_g, ln1_b, w_router, router_bias, w_exp_gate, w_exp_up, w_exp_down, w_sh_gate, w_sh_up, w_sh_down, ln2_g, ln2_b):
    B, S, D = x.shape
    depth = w_ada.shape[0]
    E = w_router.shape[2]
    alpha = float((2 * depth) ** 0.25)
    tl = _tiles(S)
    n_main = int(_SEG[-1])

    posb = jnp.broadcast_to(positions.astype(F32).reshape(B * S, 1), (B * S, LANES))
    half = ATTN_HEAD_DIM // 2
    invf = (ROPE_THETA ** (-(jnp.arange(LANES) % half).astype(F32) / half)).reshape(1, LANES)
    tr = tl["route"]
    upper = (jnp.arange(tr)[:, None] < jnp.arange(tr)[None, :]).astype(BF16)
    lower = (jnp.arange(E)[:, None] >= jnp.arange(E)[None, :]).astype(BF16)
    row = lambda v: v.reshape(1, -1)

    x2 = x.reshape(B * S, D)
    for l in range(depth):
        wr_t = w_router[l].T
        wr_hi = wr_t.astype(BF16)
        p = dict(
            w_main=w_in[l][:, :n_main].astype(BF16), b_main=row(b_in[l][:n_main]),
            w_lo=jnp.pad(w_in[l][:, n_main:], ((0, 0), (0, LANES - GLA_GATE_RANK))).astype(BF16),
            b_lo=row(jnp.pad(b_in[l][n_main:], (0, LANES - GLA_GATE_RANK))),
            w_gk=jnp.pad(w_gk2[l], ((0, LANES - GLA_GATE_RANK), (0, 0))), b_gk=row(b_gk2[l]),
            sinks=attn_sinks[l], norm_g=row(gla_norm_g[l]),
            w_oa=w_o[l][:ATTN_WIDTH].astype(BF16), w_og=w_o[l][ATTN_WIDTH:].astype(BF16), b_o=row(b_o[l]),
            ln1_g=row(ln1_g[l]), ln1_b=row(ln1_b[l]),
            wr_hi=wr_hi, wr_lo=(wr_t - wr_hi.astype(F32)).astype(BF16),
            bias_col=jnp.broadcast_to(router_bias[l].reshape(E, 1), (E, LANES)),
            upper=upper, lower=lower,
            wg=w_exp_gate[l], wu=w_exp_up[l], wd=w_exp_down[l],
            wsg=w_sh_gate[l].astype(BF16), wsu=w_sh_up[l].astype(BF16), wsd=w_sh_down[l].astype(BF16),
            ln2_g=row(ln2_g[l]), ln2_b=row(ln2_b[l]),
        )
        mod = _mod(c, w_ada[l], b_ada[l])
        mod3 = mod.reshape(B, 6, D)
        x2 = _layer(x2, mod3, posb, invf, p, seq=S, alpha=alpha)
    return x2.reshape(B, S, D)
```

```python
import functools

import jax
import jax.numpy as jnp
import numpy as np
from jax import lax
from jax.experimental import pallas as pl
from jax.experimental.pallas import tpu as pltpu

F32 = jnp.float32
BF16 = jnp.bfloat16
I32 = jnp.int32

ATTN_Q_HEADS = 8
ATTN_KV_HEADS = 2
ATTN_HEAD_DIM = 64
ATTN_BLOCK = 128
ROPE_THETA = 10000.0
GLA_HEADS = 4
GLA_KEY_DIM = 64
GLA_VAL_DIM = 128
GLA_GATE_RANK = 16
GLA_GATE_NORM = 16.0
GLA_CHUNK = 64
N_GROUPS = 8
TOPK_GROUPS = 4
TOP_K = 8
ROUTED_SCALE = 2.5
LN_EPS = 1e-5
NEG_INF = -1e30
REMOVED = -3e38

ATTN_WIDTH = ATTN_Q_HEADS * ATTN_HEAD_DIM
KV_WIDTH = ATTN_KV_HEADS * ATTN_HEAD_DIM
GLA_KWIDTH = GLA_HEADS * GLA_KEY_DIM
GLA_WIDTH = GLA_HEADS * GLA_VAL_DIM

LANES = 128
VMEM_LIMIT = 56 * 1024 * 1024
EXPERT_ROWS = 256
EXPERT_RING = 8


def _params(sem):
    return pltpu.CompilerParams(dimension_semantics=sem, vmem_limit_bytes=VMEM_LIMIT)


def _full(shape):
    return pl.BlockSpec(shape, lambda *_: (0,) * len(shape))


def _split_bf16(a):
    hi = a.astype(BF16)
    lo = (a - hi.astype(F32)).astype(BF16)
    return hi, lo


def _dot(a, b, dims=(((1,), (0,)), ((), ()))):
    return lax.dot_general(a, b, dims, preferred_element_type=F32)


NT = (((1,), (1,)), ((), ()))
TN = (((0,), (0,)), ((), ()))


def _dot3(a, b, dims=(((1,), (0,)), ((), ()))):
    ah, al = _split_bf16(a)
    bh, bl = _split_bf16(b)
    return _dot(ah, bh, dims) + (_dot(ah, bl, dims) + _dot(al, bh, dims))


def _ln_plain(x):
    mu = jnp.mean(x, axis=-1, keepdims=True)
    xc = x - mu
    var = jnp.mean(xc * xc, axis=-1, keepdims=True)
    return xc * lax.rsqrt(var + LN_EPS)


def _silu(x):
    return x * (1.0 / (1.0 + jnp.exp(-x)))


def _sigmoid(x):
    return 1.0 / (1.0 + jnp.exp(-x))


def _mod_kernel(c_ref, w_ref, b_ref, o_ref):
    c = c_ref[...]
    o_ref[...] = _dot3(_silu(c), w_ref[...]) + b_ref[...]


def _mod(c, w_ada, b_ada):
    B, D = c.shape
    n = w_ada.shape[1] // D
    return pl.pallas_call(
        _mod_kernel,
        grid=(n,),
        in_specs=[_full((B, D)),
                  pl.BlockSpec((D, D), lambda j: (0, j)),
                  pl.BlockSpec((1, D), lambda j: (0, j))],
        out_specs=pl.BlockSpec((B, D), lambda j: (0, j)),
        out_shape=jax.ShapeDtypeStruct((B, n * D), F32),
        compiler_params=_params(("arbitrary",)),
        name="mod",
    )(c, w_ada, b_ada.reshape(1, -1))


_SEG = np.cumsum([0, ATTN_WIDTH, KV_WIDTH, KV_WIDTH, GLA_KWIDTH, GLA_KWIDTH, GLA_WIDTH, GLA_WIDTH])


def _rope_chunk(t, cos, sin_signed, first_half):
    up = pltpu.roll(t, LANES - 32, axis=1)
    dn = pltpu.roll(t, 32, axis=1)
    return t * cos + jnp.where(first_half, up, dn) * sin_signed


def _inproj_kernel(x_ref, mod_ref, pos_ref, invf_ref, w_ref, b_ref, wlo_ref, blo_ref, wgk_ref, bgk_ref,
                   qa_ref, ka_ref, va_ref, ql_ref, kl_ref, vl_ref, gl_ref, lg_ref):
    x = x_ref[...]
    h = _ln_plain(x) * (1.0 + mod_ref[0, 1:2, :]) + mod_ref[0, 0:1, :]
    hb = h.astype(BF16)

    def seg(i):
        lo, hi = int(_SEG[i]), int(_SEG[i + 1])
        return _dot(hb, w_ref[:, lo:hi]) + b_ref[:, lo:hi]

    ang = pos_ref[...] * invf_ref[...]
    cos = jnp.cos(ang)
    sin = jnp.sin(ang)
    lane = lax.broadcasted_iota(I32, ang.shape, 1)
    first_half = (lane % ATTN_HEAD_DIM) < (ATTN_HEAD_DIM // 2)
    sin_signed = jnp.where(first_half, -sin, sin)

    q = seg(0)
    scale = ATTN_HEAD_DIM ** -0.5
    for c in range(ATTN_WIDTH // LANES):
        t = q[:, c * LANES:(c + 1) * LANES]
        qa_ref[:, c * LANES:(c + 1) * LANES] = (_rope_chunk(t, cos, sin_signed, first_half) * scale).astype(BF16)
    ka_ref[...] = _rope_chunk(seg(1), cos, sin_signed, first_half).astype(BF16)
    va_ref[...] = seg(2).astype(BF16)
    ql_ref[...] = seg(3).astype(BF16)
    kl_ref[...] = seg(4).astype(BF16)
    vl_ref[...] = seg(5).astype(BF16)
    gl_ref[...] = seg(6).astype(BF16)
    gk_lo = _dot(hb, wlo_ref[...]) + blo_ref[...]
    gk = _dot3(gk_lo, wgk_ref[...]) + bgk_ref[...]
    log_sig = jnp.minimum(gk, 0.0) - jnp.log(1.0 + jnp.exp(-jnp.abs(gk)))
    lg_ref[...] = log_sig * (1.0 / GLA_GATE_NORM)


def _inproj(x2, mod3, posb, invf, w_main, b_main, w_lo, b_lo, w_gk, b_gk, *, seq, tm):
    N, D = x2.shape
    per_b = seq // tm
    widths = [ATTN_WIDTH, KV_WIDTH, KV_WIDTH, GLA_KWIDTH, GLA_KWIDTH, GLA_WIDTH, GLA_WIDTH, GLA_KWIDTH]
    dtypes = [BF16] * 7 + [F32]
    row = lambda i: (i, 0)
    return pl.pallas_call(
        _inproj_kernel,
        grid=(N // tm,),
        in_specs=[pl.BlockSpec((tm, D), row),
                  pl.BlockSpec((1, 6, D), lambda i: (i // per_b, 0, 0)),
                  pl.BlockSpec((tm, LANES), row),
                  _full(invf.shape), _full(w_main.shape), _full(b_main.shape),
                  _full(w_lo.shape), _full(b_lo.shape), _full(w_gk.shape), _full(b_gk.shape)],
        out_specs=[pl.BlockSpec((tm, w), row) for w in widths],
        out_shape=[jax.ShapeDtypeStruct((N, w), dt) for w, dt in zip(widths, dtypes)],
        compiler_params=_params(("parallel",)),
        name="inproj",
    )(x2, mod3, posb, invf, w_main, b_main, w_lo, b_lo, w_gk, b_gk)


def _swa_kernel(sink_ref, q_ref, kc_ref, kp_ref, vc_ref, vp_ref, o_ref):
    j = pl.program_id(1)
    blk = ATTN_BLOCK
    k2 = jnp.concatenate([kp_ref[...], kc_ref[...]], axis=0)
    v2 = jnp.concatenate([vp_ref[...], vc_ref[...]], axis=0)
    row = lax.broadcasted_iota(I32, (blk, 2 * blk), 0)
    col = lax.broadcasted_iota(I32, (blk, 2 * blk), 1)
    dist = row + blk - col
    valid = (dist >= 0) & (dist < blk) & ((col >= blk) | (j > 0))
    group = ATTN_Q_HEADS // ATTN_KV_HEADS
    for h in range(ATTN_Q_HEADS):
        kv = h // group
        qh = q_ref[:, h * ATTN_HEAD_DIM:(h + 1) * ATTN_HEAD_DIM]
        kh = k2[:, kv * ATTN_HEAD_DIM:(kv + 1) * ATTN_HEAD_DIM]
        vh = v2[:, kv * ATTN_HEAD_DIM:(kv + 1) * ATTN_HEAD_DIM]
        s = jnp.where(valid, _dot(qh, kh, NT), NEG_INF)
        sink = sink_ref[h]
        m = jnp.maximum(jnp.max(s, axis=-1, keepdims=True), sink)
        p = jnp.exp(s - m)
        denom = jnp.sum(p, axis=-1, keepdims=True) + jnp.exp(sink - m)
        o = _dot(p.astype(BF16), vh) / denom
        o_ref[:, h * ATTN_HEAD_DIM:(h + 1) * ATTN_HEAD_DIM] = o.astype(BF16)


def _swa(q, k, v, sinks, *, seq):
    N = q.shape[0]
    nb = seq // ATTN_BLOCK
    B = N // seq
    cur = lambda b, j: (b * nb + j, 0)
    prev = lambda b, j: (b * nb + jnp.maximum(j - 1, 0), 0)
    return pl.pallas_call(
        _swa_kernel,
        grid=(B, nb),
        in_specs=[pl.BlockSpec(memory_space=pltpu.SMEM),
                  pl.BlockSpec((ATTN_BLOCK, ATTN_WIDTH), cur),
                  pl.BlockSpec((ATTN_BLOCK, KV_WIDTH), cur),
                  pl.BlockSpec((ATTN_BLOCK, KV_WIDTH), prev),
                  pl.BlockSpec((ATTN_BLOCK, KV_WIDTH), cur),
                  pl.BlockSpec((ATTN_BLOCK, KV_WIDTH), prev)],
        out_specs=pl.BlockSpec((ATTN_BLOCK, ATTN_WIDTH), cur),
        out_shape=jax.ShapeDtypeStruct((N, ATTN_WIDTH), BF16),
        compiler_params=_params(("parallel", "parallel")),
        name="swa",
    )(sinks, q, k, k, v, v)


def _gla_kernel(q_ref, k_ref, v_ref, g_ref, lg_ref, ng_ref, o_ref, st_ref, *, chunks):
    @pl.when(pl.program_id(1) == 0)
    def _():
        st_ref[...] = jnp.zeros_like(st_ref)

    C = GLA_CHUNK
    dk, dv = GLA_KEY_DIM, GLA_VAL_DIM
    r = lax.broadcasted_iota(I32, (C, C), 0)
    c = lax.broadcasted_iota(I32, (C, C), 1)
    causal = c <= r
    tri = jnp.where(causal, 1.0, 0.0).astype(BF16)
    st = st_ref[...]
    for n in range(chunks):
        rows = slice(n * C, (n + 1) * C)
        lg_hi, lg_lo = _split_bf16(lg_ref[rows, :])
        b = _dot(tri, lg_hi) + _dot(tri, lg_lo)
        b_last = b[C - 1:C, :]
        q_in = (q_ref[rows, :].astype(F32) * (dk ** -0.5) * jnp.exp(b)).astype(BF16)
        kf = k_ref[rows, :].astype(F32)
        k_in = (kf * jnp.exp(-b)).astype(BF16)
        k_out = (kf * jnp.exp(b_last - b)).astype(BF16)
        decay = jnp.exp(b_last)
        stb = st.astype(BF16)
        ut = []
        for h in range(GLA_HEADS):
            ks = slice(h * dk, (h + 1) * dk)
            vs = slice(h * dv, (h + 1) * dv)
            vh = v_ref[rows, vs]
            a = jnp.where(causal, _dot(q_in[:, ks], k_in[:, ks], NT), 0.0).astype(BF16)
            o = _dot(a, vh) + _dot(q_in[:, ks], stb[:, ks], NT)
            ut.append(_dot(vh, k_out[:, ks], TN))
            o = o * lax.rsqrt(jnp.mean(o * o, axis=-1, keepdims=True) + LN_EPS) * ng_ref[...]
            o_ref[rows, vs] = (o * _silu(g_ref[rows, vs].astype(F32))).astype(BF16)
        st = st * decay + jnp.concatenate(ut, axis=1)
    st_ref[...] = st


def _gla(q, k, v, g, lg, norm_g, *, seq, tc):
    N = q.shape[0]
    B = N // seq
    per_b = seq // tc
    row = lambda b, j: (b * per_b + j, 0)
    return pl.pallas_call(
        functools.partial(_gla_kernel, chunks=tc // GLA_CHUNK),
        grid=(B, per_b),
        in_specs=[pl.BlockSpec((tc, GLA_KWIDTH), row), pl.BlockSpec((tc, GLA_KWIDTH), row),
                  pl.BlockSpec((tc, GLA_WIDTH), row), pl.BlockSpec((tc, GLA_WIDTH), row),
                  pl.BlockSpec((tc, GLA_KWIDTH), row), _full(norm_g.shape)],
        out_specs=pl.BlockSpec((tc, GLA_WIDTH), row),
        out_shape=jax.ShapeDtypeStruct((N, GLA_WIDTH), BF16),
        scratch_shapes=[pltpu.VMEM((GLA_VAL_DIM, GLA_KWIDTH), F32)],
        compiler_params=_params(("parallel", "arbitrary")),
        name="gla",
    )(q, k, v, g, lg, norm_g)


def _slab_scratch(rows, d):
    return pltpu.VMEM((rows * (d // LANES), LANES), F32)


def _store_slabs(slabs_ref, stage_ref, v):
    rows, d = v.shape
    n = d // LANES
    for c in range(n):
        stage_ref[pl.ds(c, rows, stride=n), :] = v[:, c * LANES:(c + 1) * LANES]
    slabs_ref[...] = stage_ref[...].astype(slabs_ref.dtype)


def _load_slabs(stage_ref, slabs, n):
    stage_ref[...] = slabs.astype(F32)
    rows = stage_ref.shape[0] // n
    return jnp.concatenate([stage_ref[pl.ds(c, rows, stride=n), :] for c in range(n)], axis=1)


def _outproj_kernel(oa_ref, og_ref, x_ref, mod_ref, woa_ref, wog_ref, bo_ref, g_ref, b_ref, wrh_ref, wrl_ref,
                    x1_ref, h2_ref, h2s_ref, lt_ref, stage_ref, *, alpha):
    y = _dot(oa_ref[...], woa_ref[...]) + _dot(og_ref[...], wog_ref[...]) + bo_ref[...]
    z = alpha * x_ref[...] + (1.0 + mod_ref[0, 2:3, :]) * y
    x1 = _ln_plain(z) * g_ref[...] + b_ref[...]
    x1_ref[...] = x1
    h2 = _ln_plain(x1) * (1.0 + mod_ref[0, 4:5, :]) + mod_ref[0, 3:4, :]
    h2_ref[...] = h2.astype(BF16)
    _store_slabs(h2s_ref, stage_ref, h2)
    hh, hl = _split_bf16(h2)
    wh = wrh_ref[...]
    lt_ref[...] = _dot(wh, hh, NT) + (_dot(wh, hl, NT) + _dot(wrl_ref[...], hh, NT))


def _outproj(oa, og, x2, mod3, w_oa, w_og, b_o, ln_g, ln_b, wr_hi, wr_lo, *, seq, tm, alpha):
    N, D = x2.shape
    E = wr_hi.shape[0]
    per_b = seq // tm
    row = lambda i: (i, 0)
    return pl.pallas_call(
        functools.partial(_outproj_kernel, alpha=alpha),
        grid=(N // tm,),
        in_specs=[pl.BlockSpec((tm, ATTN_WIDTH), row), pl.BlockSpec((tm, GLA_WIDTH), row),
                  pl.BlockSpec((tm, D), row),
                  pl.BlockSpec((1, 6, D), lambda i: (i // per_b, 0, 0)),
                  _full(w_oa.shape), _full(w_og.shape), _full(b_o.shape), _full(ln_g.shape), _full(ln_b.shape),
                  _full(wr_hi.shape), _full(wr_lo.shape)],
        out_specs=[pl.BlockSpec((tm, D), row), pl.BlockSpec((tm, D), row),
                   pl.BlockSpec((tm * (D // LANES), LANES), row), pl.BlockSpec((E, tm), lambda i: (0, i))],
        out_shape=[jax.ShapeDtypeStruct((N, D), F32), jax.ShapeDtypeStruct((N, D), BF16),
                   jax.ShapeDtypeStruct((N * (D // LANES), LANES), BF16), jax.ShapeDtypeStruct((E, N), F32)],
        scratch_shapes=[_slab_scratch(tm, D)],
        compiler_params=_params(("parallel",)),
        name="outproj",
    )(oa, og, x2, mod3, w_oa, w_og, b_o, ln_g, ln_b, wr_hi, wr_lo)


def _first_index(eq, idx, size):
    return jnp.min(jnp.where(eq, idx, float(size)), axis=0, keepdims=True)


def _route_kernel(lt_ref, bias_ref, upper_ref, idx_ref, w_ref, wt_ref, rank_ref, cnt_ref, base_ref):
    @pl.when(pl.program_id(0) == 0)
    def _():
        base_ref[...] = jnp.zeros_like(base_ref)

    E, t = lt_ref.shape
    gsz = E // N_GROUPS
    scores = _sigmoid(lt_ref[...])
    biased = scores + bias_ref[:, 0:1]
    gi = lax.broadcasted_iota(I32, (gsz, t), 0).astype(F32)
    gs_rows = []
    for g in range(N_GROUPS):
        grp = biased[g * gsz:(g + 1) * gsz, :]
        m1 = jnp.max(grp, axis=0, keepdims=True)
        first = _first_index(grp == m1, gi, gsz)
        m2 = jnp.max(jnp.where(gi == first, REMOVED, grp), axis=0, keepdims=True)
        gs_rows.append(m1 + m2)
    gs = jnp.concatenate(gs_rows, axis=0)
    ri = lax.broadcasted_iota(I32, (N_GROUPS, t), 0).astype(F32)
    gsel = jnp.zeros((N_GROUPS, t), F32)
    for _ in range(TOPK_GROUPS):
        m = jnp.max(gs, axis=0, keepdims=True)
        hit = ri == _first_index(gs == m, ri, N_GROUPS)
        gsel = jnp.where(hit, 1.0, gsel)
        gs = jnp.where(hit, REMOVED, gs)
    cand = jnp.concatenate(
        [jnp.where(gsel[g:g + 1, :] > 0.5, biased[g * gsz:(g + 1) * gsz, :], NEG_INF) for g in range(N_GROUPS)],
        axis=0)
    ei = lax.broadcasted_iota(I32, (E, t), 0).astype(F32)
    idx_rows, w_rows, hits = [], [], []
    chosen = jnp.zeros((E, t), F32)
    for _ in range(TOP_K):
        m = jnp.max(cand, axis=0, keepdims=True)
        fi = _first_index(cand == m, ei, E)
        hit = ei == fi
        idx_rows.append(fi)
        w_rows.append(jnp.sum(jnp.where(hit, scores, 0.0), axis=0, keepdims=True))
        hits.append(hit)
        chosen = jnp.where(hit, 1.0, chosen)
        cand = jnp.where(hit, REMOVED, cand)
    w = jnp.concatenate(w_rows, axis=0)
    w = w / jnp.sum(w, axis=0, keepdims=True) * ROUTED_SCALE
    idx_ref[...] = jnp.concatenate(idx_rows, axis=0).astype(I32)
    w_ref[...] = w
    wpad = jnp.concatenate([w, jnp.zeros((LANES - TOP_K, t), F32)], axis=0)
    wt_ref[...] = wpad.T
    prefix = _dot(chosen.astype(BF16), upper_ref[...])
    pos = base_ref[:, 0:1] + prefix
    rank_ref[...] = jnp.concatenate(
        [jnp.sum(jnp.where(hit, pos, 0.0), axis=0, keepdims=True) for hit in hits], axis=0).astype(I32)
    base_ref[...] = base_ref[...] + jnp.sum(chosen, axis=1, keepdims=True)
    cnt_ref[...] = base_ref[...]


def _route(logits_t, bias_col, upper, *, tr):
    E, N = logits_t.shape
    col = lambda i: (0, i)
    return pl.pallas_call(
        _route_kernel,
        grid=(N // tr,),
        in_specs=[pl.BlockSpec((E, tr), col), _full(bias_col.shape), _full(upper.shape)],
        out_specs=[pl.BlockSpec((TOP_K, tr), col), pl.BlockSpec((TOP_K, tr), col),
                   pl.BlockSpec((tr, LANES), lambda i: (i, 0)),
                   pl.BlockSpec((TOP_K, tr), col), _full((E, LANES))],
        out_shape=[jax.ShapeDtypeStruct((TOP_K, N), I32), jax.ShapeDtypeStruct((TOP_K, N), F32),
                   jax.ShapeDtypeStruct((N, LANES), F32),
                   jax.ShapeDtypeStruct((TOP_K, N), I32), jax.ShapeDtypeStruct((E, LANES), F32)],
        scratch_shapes=[pltpu.VMEM((E, LANES), F32)],
        compiler_params=_params(("arbitrary",)),
        name="route",
    )(logits_t, bias_col, upper)


def _plan_kernel(cnt_ref, lower_ref, idx_ref, rank_ref, dest_ref, gstart_ref, nblk_ref, pstart_ref):
    E = cnt_ref.shape[0]
    bm = float(EXPERT_ROWS)

    @pl.when(pl.program_id(0) == 0)
    def _():
        cnt = cnt_ref[...]
        nblk = jnp.floor((cnt + (bm - 1.0)) * (1.0 / bm))
        padded = nblk * bm
        hi = jnp.floor(padded * (1.0 / 256.0))
        lo = padded - hi * 256.0
        low = lower_ref[...]
        pend = 256.0 * _dot(low, hi.astype(BF16)) + _dot(low, lo.astype(BF16))
        pstart_ref[...] = pend - padded
        gstart_ref[...] = ((pend - padded) * (1.0 / bm)).astype(I32)
        nblk_ref[...] = nblk.astype(I32)

    t = idx_ref.shape[1]
    ei = lax.broadcasted_iota(I32, (E, t), 0)
    ps = pstart_ref[:, 0:1]
    rows = []
    for k in range(TOP_K):
        hit = ei == idx_ref[k:k + 1, :]
        rows.append(jnp.sum(jnp.where(hit, ps, 0.0), axis=0, keepdims=True))
    dest_ref[...] = rank_ref[...] + jnp.concatenate(rows, axis=0).astype(I32)


def _plan(cnt, lower, idx, rank, *, tr):
    E = cnt.shape[0]
    N = idx.shape[1]
    col = lambda i: (0, i)
    return pl.pallas_call(
        _plan_kernel,
        grid=(N // tr,),
        in_specs=[_full(cnt.shape), _full(lower.shape), pl.BlockSpec((TOP_K, tr), col),
                  pl.BlockSpec((TOP_K, tr), col)],
        out_specs=[pl.BlockSpec((TOP_K, tr), col), _full((E, LANES)), _full((E, LANES))],
        out_shape=[jax.ShapeDtypeStruct((TOP_K, N), I32), jax.ShapeDtypeStruct((E, LANES), I32),
                   jax.ShapeDtypeStruct((E, LANES), I32)],
        scratch_shapes=[pltpu.VMEM((E, LANES), F32)],
        compiler_params=_params(("arbitrary",)),
        name="plan",
    )(cnt, lower, idx, rank)


def _slab_rows(ref, row, slab):
    return ref.at[pl.ds(pl.multiple_of(row * slab, slab), slab)]


def _fill_zero_blocks(zero_ref, out_ref, zsem, gstart_ref, nblk_ref, *, last_of_expert):
    blk = zero_ref.shape[0]
    n_exp = gstart_ref.shape[0]
    n_blk = out_ref.shape[0] // blk
    nused = gstart_ref[n_exp - 1] + nblk_ref[n_exp - 1]
    zero_ref[...] = jnp.zeros_like(zero_ref)

    def blk_copy(b):
        return pltpu.make_async_copy(zero_ref, out_ref.at[pl.ds(pl.multiple_of(b * blk, blk), blk)], zsem)

    def tail(b, carry):
        blk_copy(b).start()
        return carry

    lax.fori_loop(nused, n_blk, tail, 0)
    n_started = n_blk - nused
    if last_of_expert:
        def last(e, n):
            has = nblk_ref[e] > 0

            @pl.when(has)
            def _():
                blk_copy(gstart_ref[e] + nblk_ref[e] - 1).start()

            return n + has.astype(I32)

        n_started = lax.fori_loop(0, n_exp, last, n_started)

    def wait(_, carry):
        blk_copy(0).wait()
        return carry

    lax.fori_loop(0, n_started, wait, 0)


def _scatter_kernel(gstart_ref, nblk_ref, *refs):
    dest_refs = refs[:TOP_K]
    h_ref, xs_ref, zero_ref, sem, zsem = refs[TOP_K:]
    ts = dest_refs[0].shape[0]
    slab = h_ref.shape[0] // ts

    @pl.when(pl.program_id(0) == 0)
    def _():
        _fill_zero_blocks(zero_ref, xs_ref, zsem, gstart_ref, nblk_ref, last_of_expert=True)

    def body(t, carry):
        src = _slab_rows(h_ref, t, slab)
        for k in range(TOP_K):
            pltpu.make_async_copy(src, _slab_rows(xs_ref, dest_refs[k][t], slab), sem).start(priority=k % 2)
        return carry

    lax.fori_loop(0, ts, body, 0)
    for _ in range(TOP_K):
        pltpu.make_async_copy(h_ref, xs_ref.at[pl.ds(0, slab * ts)], sem).wait()


def _scatter(gstart, nblk, dests, h2s, *, ts, slab, n_blk):
    rows, W = h2s.shape
    grid_spec = pltpu.PrefetchScalarGridSpec(
        num_scalar_prefetch=2,
        grid=(rows // (slab * ts),),
        in_specs=[pl.BlockSpec((ts,), lambda i, be, nu: (i,), memory_space=pltpu.SMEM)] * TOP_K
        + [pl.BlockSpec((slab * ts, W), lambda i, be, nu: (i, 0))],
        out_specs=pl.BlockSpec(memory_space=pl.ANY),
        scratch_shapes=[pltpu.VMEM((slab * EXPERT_ROWS, W), h2s.dtype), pltpu.SemaphoreType.DMA(()),
                        pltpu.SemaphoreType.DMA(())],
    )
    return pl.pallas_call(
        _scatter_kernel,
        grid_spec=grid_spec,
        out_shape=jax.ShapeDtypeStruct((n_blk * slab * EXPERT_ROWS, W), h2s.dtype),
        compiler_params=_params(("arbitrary",)),
        name="scatter",
    )(gstart, nblk, *dests, h2s)


def _experts_kernel(gstart_ref, nblk_ref, wg_ref, wu_ref, wd_ref, xs_ref, ys_ref, xbuf_ref, ybuf_ref,
                    wgb_ref, wub_ref, wdb_ref, xstage_ref, ystage_ref, zero_ref, front_ref, xsem, ysem, zsem):
    e = pl.program_id(0)
    n_exp = pl.num_programs(0)
    ring = xbuf_ref.shape[0]
    blk = zero_ref.shape[0]
    slab = wgb_ref.shape[0] // LANES
    rows = blk // slab
    nused = gstart_ref[n_exp - 1] + nblk_ref[n_exp - 1]
    g0 = gstart_ref[e]
    n = nblk_ref[e]

    def x_copy(b):
        slot = lax.rem(b, ring)
        return pltpu.make_async_copy(xs_ref.at[pl.ds(pl.multiple_of(b * blk, blk), blk)], xbuf_ref.at[slot],
                                     xsem.at[slot])

    def y_copy(b):
        slot = lax.rem(b, ring)
        return pltpu.make_async_copy(ybuf_ref.at[slot], ys_ref.at[pl.ds(pl.multiple_of(b * blk, blk), blk)],
                                     ysem.at[slot])

    @pl.when(e == 0)
    def _():
        front_ref[0] = 0

    @pl.when(n > 0)
    def _():
        wgb_ref[...] = wg_ref[0].astype(BF16)
        wub_ref[...] = wu_ref[0].astype(BF16)
        wdb_ref[...] = wd_ref[0].astype(BF16)

    def group(g, nb):
        limit = jnp.minimum(nused, g + ring)

        def fetch(b, carry):
            x_copy(b).start()
            return carry

        lax.fori_loop(front_ref[0], limit, fetch, 0)
        front_ref[0] = jnp.maximum(front_ref[0], limit)

        parts = []
        for i in range(nb):
            x_copy(g + i).wait()
            parts.append(_load_slabs(xstage_ref.at[i % 2], xbuf_ref[lax.rem(g + i, ring)], slab).astype(BF16))
        x = parts[0] if nb == 1 else jnp.concatenate(parts, axis=0)
        gate = _dot(x, wgb_ref[...])
        up = _dot(x, wub_ref[...])
        y = _dot((_silu(gate) * up).astype(BF16), wdb_ref[...])
        for i in range(nb):
            b = g + i

            @pl.when(b >= ring)
            def _():
                y_copy(b - ring).wait()

            _store_slabs(ybuf_ref.at[lax.rem(b, ring)], ystage_ref.at[i % 2], y[i * rows:(i + 1) * rows])
            y_copy(b).start()

    n4 = lax.shift_right_logical(n, 2)
    left = n - 4 * n4

    def quad(j, carry):
        group(g0 + 4 * j, 4)
        return carry

    lax.fori_loop(0, n4, quad, 0)

    @pl.when(left >= 2)
    def _():
        group(g0 + 4 * n4, 2)

    @pl.when(lax.rem(left, 2) == 1)
    def _():
        group(g0 + n - 1, 1)

    @pl.when(e == n_exp - 1)
    def _():
        def drain(b, carry):
            y_copy(b).wait()
            return carry

        lax.fori_loop(jnp.maximum(nused - ring, 0), nused, drain, 0)
        _fill_zero_blocks(zero_ref, ys_ref, zsem, gstart_ref, nblk_ref, last_of_expert=False)


def _experts(gstart, nblk, xs, wg, wu, wd):
    rows, W = xs.shape
    E, D, H = wg.shape
    blk = (D // LANES) * EXPERT_ROWS
    grid_spec = pltpu.PrefetchScalarGridSpec(
        num_scalar_prefetch=2,
        grid=(E,),
        in_specs=[pl.BlockSpec((1, D, H), lambda e, gs, nb: (e, 0, 0)),
                  pl.BlockSpec((1, D, H), lambda e, gs, nb: (e, 0, 0)),
                  pl.BlockSpec((1, H, D), lambda e, gs, nb: (e, 0, 0)),
                  pl.BlockSpec(memory_space=pl.ANY)],
        out_specs=pl.BlockSpec(memory_space=pl.ANY),
        scratch_shapes=[pltpu.VMEM((EXPERT_RING, blk, W), xs.dtype), pltpu.VMEM((EXPERT_RING, blk, W), xs.dtype),
                        pltpu.VMEM((D, H), BF16), pltpu.VMEM((D, H), BF16), pltpu.VMEM((H, D), BF16),
                        pltpu.VMEM((2, blk, LANES), F32), pltpu.VMEM((2, blk, LANES), F32),
                        pltpu.VMEM((blk, W), xs.dtype), pltpu.SMEM((1,), I32),
                        pltpu.SemaphoreType.DMA((EXPERT_RING,)), pltpu.SemaphoreType.DMA((EXPERT_RING,)),
                        pltpu.SemaphoreType.DMA(())],
    )
    return pl.pallas_call(
        _experts_kernel,
        grid_spec=grid_spec,
        out_shape=jax.ShapeDtypeStruct((rows, W), xs.dtype),
        compiler_params=_params(("arbitrary",)),
        name="experts",
    )(gstart, nblk, wg, wu, wd, xs)


def _combine_kernel(*refs, alpha):
    dest_refs = refs[:TOP_K]
    (wt_ref, h2_ref, x1_ref, mod_ref, wsg_ref, wsu_ref, wsd_ref, g_ref, b_ref, ys_ref,
     o_ref, buf_ref, stage0_ref, stage1_ref, sem) = refs[TOP_K:]
    tc, d = h2_ref.shape
    slab = d // LANES

    def body(t, carry):
        for k in range(TOP_K):
            src = _slab_rows(ys_ref, dest_refs[k][t], slab)
            pltpu.make_async_copy(src, _slab_rows(buf_ref.at[k], t, slab), sem).start(priority=k % 2)
        return carry

    lax.fori_loop(0, tc, body, 0)
    h = h2_ref[...]
    mid = (_silu(_dot(h, wsg_ref[...])) * _dot(h, wsu_ref[...])).astype(BF16)
    y = _dot(mid, wsd_ref[...])
    for k in range(TOP_K):
        pltpu.make_async_copy(ys_ref.at[pl.ds(0, slab * tc)], buf_ref.at[k], sem).wait()
    for k in range(TOP_K):
        stage_ref = stage0_ref if k % 2 == 0 else stage1_ref
        y = y + wt_ref[:, k:k + 1] * _load_slabs(stage_ref, buf_ref[k], slab)
    z = alpha * x1_ref[...] + (1.0 + mod_ref[0, 5:6, :]) * y
    o_ref[...] = _ln_plain(z) * g_ref[...] + b_ref[...]


def _combine(dests, wt, h2, x1, mod3, wsg, wsu, wsd, ln_g, ln_b, ys, *, seq, tc, alpha):
    N, D = x1.shape
    slab = D // LANES
    per_b = seq // tc
    row = lambda i: (i, 0)
    return pl.pallas_call(
        functools.partial(_combine_kernel, alpha=alpha),
        grid=(N // tc,),
        in_specs=[pl.BlockSpec((tc,), lambda i: (i,), memory_space=pltpu.SMEM)] * TOP_K
        + [pl.BlockSpec((tc, LANES), row), pl.BlockSpec((tc, D), row), pl.BlockSpec((tc, D), row),
           pl.BlockSpec((1, 6, D), lambda i: (i // per_b, 0, 0)),
           _full(wsg.shape), _full(wsu.shape), _full(wsd.shape), _full(ln_g.shape), _full(ln_b.shape),
           pl.BlockSpec(memory_space=pl.ANY)],
        out_specs=pl.BlockSpec((tc, D), row),
        out_shape=jax.ShapeDtypeStruct((N, D), F32),
        scratch_shapes=[pltpu.VMEM((TOP_K, slab * tc, LANES), ys.dtype), _slab_scratch(tc, D), _slab_scratch(tc, D),
                        pltpu.SemaphoreType.DMA(())],
        compiler_params=_params(("arbitrary",)),
        name="combine",
    )(*dests, wt, h2, x1, mod3, wsg, wsu, wsd, ln_g, ln_b, ys)


def _tiles(seq):
    t = lambda want: min(want, seq)
    return dict(proj=t(512), gla=t(256), route=t(256), move=t(256))


def _layer(x2, mod3, posb, invf, p, *, seq, alpha):
    N, D = x2.shape
    tl = _tiles(seq)
    qa, ka, va, ql, kl, vl, gl, lg = _inproj(x2, mod3, posb, invf, p["w_main"], p["b_main"], p["w_lo"], p["b_lo"],
                                             p["w_gk"], p["b_gk"], seq=seq, tm=tl["proj"])
    oa = _swa(qa, ka, va, p["sinks"], seq=seq)
    og = _gla(ql, kl, vl, gl, lg, p["norm_g"], seq=seq, tc=tl["gla"])
    x1, h2, h2s, logits_t = _outproj(oa, og, x2, mod3, p["w_oa"], p["w_og"], p["b_o"], p["ln1_g"], p["ln1_b"],
                                     p["wr_hi"], p["wr_lo"], seq=seq, tm=tl["proj"], alpha=alpha)
    idx, _, wt, rank, cnt = _route(logits_t, p["bias_col"], p["upper"], tr=tl["route"])
    E = cnt.shape[0]
    n_blk = (N * TOP_K) // EXPERT_ROWS + E
    dest, gstart, nblk = _plan(cnt, p["lower"], idx, rank, tr=tl["route"])
    gstart, nblk = gstart[:, 0], nblk[:, 0]
    dests = [dest[k] for k in range(TOP_K)]
    xs = _scatter(gstart, nblk, dests, h2s, ts=tl["move"], slab=D // LANES, n_blk=n_blk)
    ys = _experts(gstart, nblk, xs, p["wg"], p["wu"], p["wd"])
    return _combine(dests, wt, h2, x1, mod3, p["wsg"], p["wsu"], p["wsd"], p["ln2_g"], p["ln2_b"], ys,
                    seq=seq, tc=tl["move"], alpha=alpha)


def kernel(x, c, positions, w_ada, b_ada, w_in, b_in, attn_sinks, w_gk2, b_gk2, gla_norm_g, w_o, b_o, ln1_g, ln1_b, w_router, router_bias, w_exp_gate, w_exp_up, w_exp_down, w_sh_gate, w_sh_up, w_sh_down, ln2_g, ln2_b):
    B, S, D = x.shape
    depth = w_ada.shape[0]
    E = w_router.shape[2]
    alpha = float((2 * depth) ** 0.25)
    tl = _tiles(S)
    n_main = int(_SEG[-1])

    posb = jnp.broadcast_to(positions.astype(F32).reshape(B * S, 1), (B * S, LANES))
    half = ATTN_HEAD_DIM // 2
    invf = (ROPE_THETA ** (-(jnp.arange(LANES) % half).astype(F32) / half)).reshape(1, LANES)
    tr = tl["route"]
    upper = (jnp.arange(tr)[:, None] < jnp.arange(tr)[None, :]).astype(BF16)
    lower = (jnp.arange(E)[:, None] >= jnp.arange(E)[None, :]).astype(BF16)
    row = lambda v: v.reshape(1, -1)

    x2 = x.reshape(B * S, D)
    for l in range(depth):
        wr_t = w_router[l].T
        wr_hi = wr_t.astype(BF16)
        p = dict(
            w_main=w_in[l][:, :n_main].astype(BF16), b_main=row(b_in[l][:n_main]),
            w_lo=jnp.pad(w_in[l][:, n_main:], ((0, 0), (0, LANES - GLA_GATE_RANK))).astype(BF16),
            b_lo=row(jnp.pad(b_in[l][n_main:], (0, LANES - GLA_GATE_RANK))),
            w_gk=jnp.pad(w_gk2[l], ((0, LANES - GLA_GATE_RANK), (0, 0))), b_gk=row(b_gk2[l]),
            sinks=attn_sinks[l], norm_g=row(gla_norm_g[l]),
            w_oa=w_o[l][:ATTN_WIDTH].astype(BF16), w_og=w_o[l][ATTN_WIDTH:].astype(BF16), b_o=row(b_o[l]),
            ln1_g=row(ln1_g[l]), ln1_b=row(ln1_b[l]),
            wr_hi=wr_hi, wr_lo=(wr_t - wr_hi.astype(F32)).astype(BF16),
            bias_col=jnp.broadcast_to(router_bias[l].reshape(E, 1), (E, LANES)),
            upper=upper, lower=lower,
            wg=w_exp_gate[l], wu=w_exp_up[l], wd=w_exp_down[l],
            wsg=w_sh_gate[l].astype(BF16), wsu=w_sh_up[l].astype(BF16), wsd=w_sh_down[l].astype(BF16),
            ln2_g=row(ln2_g[l]), ln2_b=row(ln2_b[l]),
        )
        mod = _mod(c, w_ada[l], b_ada[l])
        mod3 = mod.reshape(B, 6, D)
        x2 = _layer(x2, mod3, posb, invf, p, seq=S, alpha=alpha)
    return x2.reshape(B, S, D)
```

```python
import functools

import jax
import jax.numpy as jnp
import numpy as np
from jax import lax
from jax.experimental import pallas as pl
from jax.experimental.pallas import tpu as pltpu

F32 = jnp.float32
BF16 = jnp.bfloat16
I32 = jnp.int32

ATTN_Q_HEADS = 8
ATTN_KV_HEADS = 2
ATTN_HEAD_DIM = 64
ATTN_BLOCK = 128
ROPE_THETA = 10000.0
GLA_HEADS = 4
GLA_KEY_DIM = 64
GLA_VAL_DIM = 128
GLA_GATE_RANK = 16
GLA_GATE_NORM = 16.0
GLA_CHUNK = 64
N_GROUPS = 8
TOPK_GROUPS = 4
TOP_K = 8
ROUTED_SCALE = 2.5
LN_EPS = 1e-5
NEG_INF = -1e30
REMOVED = -3e38

ATTN_WIDTH = ATTN_Q_HEADS * ATTN_HEAD_DIM
KV_WIDTH = ATTN_KV_HEADS * ATTN_HEAD_DIM
GLA_KWIDTH = GLA_HEADS * GLA_KEY_DIM
GLA_WIDTH = GLA_HEADS * GLA_VAL_DIM

LANES = 128
VMEM_LIMIT = 56 * 1024 * 1024
EXPERT_ROWS = 256
EXPERT_RING = 8


def _params(sem):
    return pltpu.CompilerParams(dimension_semantics=sem, vmem_limit_bytes=VMEM_LIMIT)


def _full(shape):
    return pl.BlockSpec(shape, lambda *_: (0,) * len(shape))


def _split_bf16(a):
    hi = a.astype(BF16)
    lo = (a - hi.astype(F32)).astype(BF16)
    return hi, lo


def _dot(a, b, dims=(((1,), (0,)), ((), ()))):
    return lax.dot_general(a, b, dims, preferred_element_type=F32)


NT = (((1,), (1,)), ((), ()))
TN = (((0,), (0,)), ((), ()))


def _dot3(a, b, dims=(((1,), (0,)), ((), ()))):
    ah, al = _split_bf16(a)
    bh, bl = _split_bf16(b)
    return _dot(ah, bh, dims) + (_dot(ah, bl, dims) + _dot(al, bh, dims))


def _ln_plain(x):
    mu = jnp.mean(x, axis=-1, keepdims=True)
    xc = x - mu
    var = jnp.mean(xc * xc, axis=-1, keepdims=True)
    return xc * lax.rsqrt(var + LN_EPS)


def _silu(x):
    return x * (1.0 / (1.0 + jnp.exp(-x)))


def _sigmoid(x):
    return 1.0 / (1.0 + jnp.exp(-x))


def _mod_kernel(c_ref, w_ref, b_ref, o_ref):
    c = c_ref[...]
    o_ref[...] = _dot3(_silu(c), w_ref[...]) + b_ref[...]


def _mod(c, w_ada, b_ada):
    B, D = c.shape
    n = w_ada.shape[1] // D
    return pl.pallas_call(
        _mod_kernel,
        grid=(n,),
        in_specs=[_full((B, D)),
                  pl.BlockSpec((D, D), lambda j: (0, j)),
                  pl.BlockSpec((1, D), lambda j: (0, j))],
        out_specs=pl.BlockSpec((B, D), lambda j: (0, j)),
        out_shape=jax.ShapeDtypeStruct((B, n * D), F32),
        compiler_params=_params(("arbitrary",)),
        name="mod",
    )(c, w_ada, b_ada.reshape(1, -1))


_SEG = np.cumsum([0, ATTN_WIDTH, KV_WIDTH, KV_WIDTH, GLA_KWIDTH, GLA_KWIDTH, GLA_WIDTH, GLA_WIDTH])


def _rope_chunk(t, cos, sin_signed, first_half):
    up = pltpu.roll(t, LANES - 32, axis=1)
    dn = pltpu.roll(t, 32, axis=1)
    return t * cos + jnp.where(first_half, up, dn) * sin_signed


def _inproj_kernel(x_ref, mod_ref, pos_ref, invf_ref, w_ref, b_ref, wlo_ref, blo_ref, wgk_ref, bgk_ref,
                   qa_ref, ka_ref, va_ref, ql_ref, kl_ref, vl_ref, gl_ref, lg_ref):
    x = x_ref[...]
    h = _ln_plain(x) * (1.0 + mod_ref[0, 1:2, :]) + mod_ref[0, 0:1, :]
    hb = h.astype(BF16)

    def seg(i):
        lo, hi = int(_SEG[i]), int(_SEG[i + 1])
        return _dot(hb, w_ref[:, lo:hi]) + b_ref[:, lo:hi]

    ang = pos_ref[...] * invf_ref[...]
    cos = jnp.cos(ang)
    sin = jnp.sin(ang)
    lane = lax.broadcasted_iota(I32, ang.shape, 1)
    first_half = (lane % ATTN_HEAD_DIM) < (ATTN_HEAD_DIM // 2)
    sin_signed = jnp.where(first_half, -sin, sin)

    q = seg(0)
    scale = ATTN_HEAD_DIM ** -0.5
    for c in range(ATTN_WIDTH // LANES):
        t = q[:, c * LANES:(c + 1) * LANES]
        qa_ref[:, c * LANES:(c + 1) * LANES] = (_rope_chunk(t, cos, sin_signed, first_half) * scale).astype(BF16)
    ka_ref[...] = _rope_chunk(seg(1), cos, sin_signed, first_half).astype(BF16)
    va_ref[...] = seg(2).astype(BF16)
    ql_ref[...] = seg(3).astype(BF16)
    kl_ref[...] = seg(4).astype(BF16)
    vl_ref[...] = seg(5).astype(BF16)
    gl_ref[...] = seg(6).astype(BF16)
    gk_lo = _dot(hb, wlo_ref[...]) + blo_ref[...]
    gk = _dot3(gk_lo, wgk_ref[...]) + bgk_ref[...]
    log_sig = jnp.minimum(gk, 0.0) - jnp.log(1.0 + jnp.exp(-jnp.abs(gk)))
    lg_ref[...] = log_sig * (1.0 / GLA_GATE_NORM)


def _inproj(x2, mod3, posb, invf, w_main, b_main, w_lo, b_lo, w_gk, b_gk, *, seq, tm):
    N, D = x2.shape
    per_b = seq // tm
    widths = [ATTN_WIDTH, KV_WIDTH, KV_WIDTH, GLA_KWIDTH, GLA_KWIDTH, GLA_WIDTH, GLA_WIDTH, GLA_KWIDTH]
    dtypes = [BF16] * 7 + [F32]
    row = lambda i: (i, 0)
    return pl.pallas_call(
        _inproj_kernel,
        grid=(N // tm,),
        in_specs=[pl.BlockSpec((tm, D), row),
                  pl.BlockSpec((1, 6, D), lambda i: (i // per_b, 0, 0)),
                  pl.BlockSpec((tm, LANES), row),
                  _full(invf.shape), _full(w_main.shape), _full(b_main.shape),
                  _full(w_lo.shape), _full(b_lo.shape), _full(w_gk.shape), _full(b_gk.shape)],
        out_specs=[pl.BlockSpec((tm, w), row) for w in widths],
        out_shape=[jax.ShapeDtypeStruct((N, w), dt) for w, dt in zip(widths, dtypes)],
        compiler_params=_params(("parallel",)),
        name="inproj",
    )(x2, mod3, posb, invf, w_main, b_main, w_lo, b_lo, w_gk, b_gk)


def _swa_kernel(sink_ref, q_ref, kc_ref, kp_ref, vc_ref, vp_ref, o_ref):
    j = pl.program_id(1)
    blk = ATTN_BLOCK
    k2 = jnp.concatenate([kp_ref[...], kc_ref[...]], axis=0)
    v2 = jnp.concatenate([vp_ref[...], vc_ref[...]], axis=0)
    row = lax.broadcasted_iota(I32, (blk, 2 * blk), 0)
    col = lax.broadcasted_iota(I32, (blk, 2 * blk), 1)
    dist = row + blk - col
    valid = (dist >= 0) & (dist < blk) & ((col >= blk) | (j > 0))
    group = ATTN_Q_HEADS // ATTN_KV_HEADS
    for h in range(ATTN_Q_HEADS):
        kv = h // group
        qh = q_ref[:, h * ATTN_HEAD_DIM:(h + 1) * ATTN_HEAD_DIM]
        kh = k2[:, kv * ATTN_HEAD_DIM:(kv + 1) * ATTN_HEAD_DIM]
        vh = v2[:, kv * ATTN_HEAD_DIM:(kv + 1) * ATTN_HEAD_DIM]
        s = jnp.where(valid, _dot(qh, kh, NT), NEG_INF)
        sink = sink_ref[h]
        m = jnp.maximum(jnp.max(s, axis=-1, keepdims=True), sink)
        p = jnp.exp(s - m)
        denom = jnp.sum(p, axis=-1, keepdims=True) + jnp.exp(sink - m)
        o = _dot(p.astype(BF16), vh) / denom
        o_ref[:, h * ATTN_HEAD_DIM:(h + 1) * ATTN_HEAD_DIM] = o.astype(BF16)


def _swa(q, k, v, sinks, *, seq):
    N = q.shape[0]
    nb = seq // ATTN_BLOCK
    B = N // seq
    cur = lambda b, j: (b * nb + j, 0)
    prev = lambda b, j: (b * nb + jnp.maximum(j - 1, 0), 0)
    return pl.pallas_call(
        _swa_kernel,
        grid=(B, nb),
        in_specs=[pl.BlockSpec(memory_space=pltpu.SMEM),
                  pl.BlockSpec((ATTN_BLOCK, ATTN_WIDTH), cur),
                  pl.BlockSpec((ATTN_BLOCK, KV_WIDTH), cur),
                  pl.BlockSpec((ATTN_BLOCK, KV_WIDTH), prev),
                  pl.BlockSpec((ATTN_BLOCK, KV_WIDTH), cur),
                  pl.BlockSpec((ATTN_BLOCK, KV_WIDTH), prev)],
        out_specs=pl.BlockSpec((ATTN_BLOCK, ATTN_WIDTH), cur),
        out_shape=jax.ShapeDtypeStruct((N, ATTN_WIDTH), BF16),
        compiler_params=_params(("parallel", "parallel")),
        name="swa",
    )(sinks, q, k, k, v, v)


def _gla_kernel(q_ref, k_ref, v_ref, g_ref, lg_ref, ng_ref, o_ref, st_ref, *, chunks):
    @pl.when(pl.program_id(1) == 0)
    def _():
        st_ref[...] = jnp.zeros_like(st_ref)

    C = GLA_CHUNK
    dk, dv = GLA_KEY_DIM, GLA_VAL_DIM
    r = lax.broadcasted_iota(I32, (C, C), 0)
    c = lax.broadcasted_iota(I32, (C, C), 1)
    causal = c <= r
    tri = jnp.where(causal, 1.0, 0.0).astype(BF16)
    st = st_ref[...]
    for n in range(chunks):
        rows = slice(n * C, (n + 1) * C)
        lg_hi, lg_lo = _split_bf16(lg_ref[rows, :])
        b = _dot(tri, lg_hi) + _dot(tri, lg_lo)
        b_last = b[C - 1:C, :]
        q_in = (q_ref[rows, :].astype(F32) * (dk ** -0.5) * jnp.exp(b)).astype(BF16)
        kf = k_ref[rows, :].astype(F32)
        k_in = (kf * jnp.exp(-b)).astype(BF16)
        k_out = (kf * jnp.exp(b_last - b)).astype(BF16)
        decay = jnp.exp(b_last)
        stb = st.astype(BF16)
        ut = []
        for h in range(GLA_HEADS):
            ks = slice(h * dk, (h + 1) * dk)
            vs = slice(h * dv, (h + 1) * dv)
            vh = v_ref[rows, vs]
            a = jnp.where(causal, _dot(q_in[:, ks], k_in[:, ks], NT), 0.0).astype(BF16)
            o = _dot(a, vh) + _dot(q_in[:, ks], stb[:, ks], NT)
            ut.append(_dot(vh, k_out[:, ks], TN))
            o = o * lax.rsqrt(jnp.mean(o * o, axis=-1, keepdims=True) + LN_EPS) * ng_ref[...]
            o_ref[rows, vs] = (o * _silu(g_ref[rows, vs].astype(F32))).astype(BF16)
        st = st * decay + jnp.concatenate(ut, axis=1)
    st_ref[...] = st


def _gla(q, k, v, g, lg, norm_g, *, seq, tc):
    N = q.shape[0]
    B = N // seq
    per_b = seq // tc
    row = lambda b, j: (b * per_b + j, 0)
    return pl.pallas_call(
        functools.partial(_gla_kernel, chunks=tc // GLA_CHUNK),
        grid=(B, per_b),
        in_specs=[pl.BlockSpec((tc, GLA_KWIDTH), row), pl.BlockSpec((tc, GLA_KWIDTH), row),
                  pl.BlockSpec((tc, GLA_WIDTH), row), pl.BlockSpec((tc, GLA_WIDTH), row),
                  pl.BlockSpec((tc, GLA_KWIDTH), row), _full(norm_g.shape)],
        out_specs=pl.BlockSpec((tc, GLA_WIDTH), row),
        out_shape=jax.ShapeDtypeStruct((N, GLA_WIDTH), BF16),
        scratch_shapes=[pltpu.VMEM((GLA_VAL_DIM, GLA_KWIDTH), F32)],
        compiler_params=_params(("parallel", "arbitrary")),
        name="gla",
    )(q, k, v, g, lg, norm_g)


def _slab_scratch(rows, d):
    return pltpu.VMEM((rows * (d // LANES), LANES), F32)


def _store_slabs(slabs_ref, stage_ref, v):
    rows, d = v.shape
    n = d // LANES
    for c in range(n):
        stage_ref[pl.ds(c, rows, stride=n), :] = v[:, c * LANES:(c + 1) * LANES]
    slabs_ref[...] = stage_ref[...].astype(slabs_ref.dtype)


def _load_slabs(stage_ref, slabs, n):
    stage_ref[...] = slabs.astype(F32)
    rows = stage_ref.shape[0] // n
    return jnp.concatenate([stage_ref[pl.ds(c, rows, stride=n), :] for c in range(n)], axis=1)


def _outproj_kernel(oa_ref, og_ref, x_ref, mod_ref, woa_ref, wog_ref, bo_ref, g_ref, b_ref, wrh_ref, wrl_ref,
                    x1_ref, h2_ref, h2s_ref, lt_ref, stage_ref, *, alpha):
    y = _dot(oa_ref[...], woa_ref[...]) + _dot(og_ref[...], wog_ref[...]) + bo_ref[...]
    z = alpha * x_ref[...] + (1.0 + mod_ref[0, 2:3, :]) * y
    x1 = _ln_plain(z) * g_ref[...] + b_ref[...]
    x1_ref[...] = x1
    h2 = _ln_plain(x1) * (1.0 + mod_ref[0, 4:5, :]) + mod_ref[0, 3:4, :]
    h2_ref[...] = h2.astype(BF16)
    _store_slabs(h2s_ref, stage_ref, h2)
    hh, hl = _split_bf16(h2)
    wh = wrh_ref[...]
    lt_ref[...] = _dot(wh, hh, NT) + (_dot(wh, hl, NT) + _dot(wrl_ref[...], hh, NT))


def _outproj(oa, og, x2, mod3, w_oa, w_og, b_o, ln_g, ln_b, wr_hi, wr_lo, *, seq, tm, alpha):
    N, D = x2.shape
    E = wr_hi.shape[0]
    per_b = seq // tm
    row = lambda i: (i, 0)
    return pl.pallas_call(
        functools.partial(_outproj_kernel, alpha=alpha),
        grid=(N // tm,),
        in_specs=[pl.BlockSpec((tm, ATTN_WIDTH), row), pl.BlockSpec((tm, GLA_WIDTH), row),
                  pl.BlockSpec((tm, D), row),
                  pl.BlockSpec((1, 6, D), lambda i: (i // per_b, 0, 0)),
                  _full(w_oa.shape), _full(w_og.shape), _full(b_o.shape), _full(ln_g.shape), _full(ln_b.shape),
                  _full(wr_hi.shape), _full(wr_lo.shape)],
        out_specs=[pl.BlockSpec((tm, D), row), pl.BlockSpec((tm, D), row),
                   pl.BlockSpec((tm * (D // LANES), LANES), row), pl.BlockSpec((E, tm), lambda i: (0, i))],
        out_shape=[jax.ShapeDtypeStruct((N, D), F32), jax.ShapeDtypeStruct((N, D), BF16),
                   jax.ShapeDtypeStruct((N * (D // LANES), LANES), BF16), jax.ShapeDtypeStruct((E, N), F32)],
        scratch_shapes=[_slab_scratch(tm, D)],
        compiler_params=_params(("parallel",)),
        name="outproj",
    )(oa, og, x2, mod3, w_oa, w_og, b_o, ln_g, ln_b, wr_hi, wr_lo)


def _first_index(eq, idx, size):
    return jnp.min(jnp.where(eq, idx, float(size)), axis=0, keepdims=True)


def _route_kernel(lt_ref, bias_ref, upper_ref, idx_ref, w_ref, wt_ref, rank_ref, cnt_ref, base_ref):
    @pl.when(pl.program_id(0) == 0)
    def _():
        base_ref[...] = jnp.zeros_like(base_ref)

    E, t = lt_ref.shape
    gsz = E // N_GROUPS
    scores = _sigmoid(lt_ref[...])
    biased = scores + bias_ref[:, 0:1]
    gi = lax.broadcasted_iota(I32, (gsz, t), 0).astype(F32)
    gs_rows = []
    for g in range(N_GROUPS):
        grp = biased[g * gsz:(g + 1) * gsz, :]
        m1 = jnp.max(grp, axis=0, keepdims=True)
        first = _first_index(grp == m1, gi, gsz)
        m2 = jnp.max(jnp.where(gi == first, REMOVED, grp), axis=0, keepdims=True)
        gs_rows.append(m1 + m2)
    gs = jnp.concatenate(gs_rows, axis=0)
    ri = lax.broadcasted_iota(I32, (N_GROUPS, t), 0).astype(F32)
    gsel = jnp.zeros((N_GROUPS, t), F32)
    for _ in range(TOPK_GROUPS):
        m = jnp.max(gs, axis=0, keepdims=True)
        hit = ri == _first_index(gs == m, ri, N_GROUPS)
        gsel = jnp.where(hit, 1.0, gsel)
        gs = jnp.where(hit, REMOVED, gs)
    cand = jnp.concatenate(
        [jnp.where(gsel[g:g + 1, :] > 0.5, biased[g * gsz:(g + 1) * gsz, :], NEG_INF) for g in range(N_GROUPS)],
        axis=0)
    ei = lax.broadcasted_iota(I32, (E, t), 0).astype(F32)
    idx_rows, w_rows, hits = [], [], []
    chosen = jnp.zeros((E, t), F32)
    for _ in range(TOP_K):
        m = jnp.max(cand, axis=0, keepdims=True)
        fi = _first_index(cand == m, ei, E)
        hit = ei == fi
        idx_rows.append(fi)
        w_rows.append(jnp.sum(jnp.where(hit, scores, 0.0), axis=0, keepdims=True))
        hits.append(hit)
        chosen = jnp.where(hit, 1.0, chosen)
        cand = jnp.where(hit, REMOVED, cand)
    w = jnp.concatenate(w_rows, axis=0)
    w = w / jnp.sum(w, axis=0, keepdims=True) * ROUTED_SCALE
    idx_ref[...] = jnp.concatenate(idx_rows, axis=0).astype(I32)
    w_ref[...] = w
    wpad = jnp.concatenate([w, jnp.zeros((LANES - TOP_K, t), F32)], axis=0)
    wt_ref[...] = wpad.T
    prefix = _dot(chosen.astype(BF16), upper_ref[...])
    pos = base_ref[:, 0:1] + prefix
    rank_ref[...] = jnp.concatenate(
        [jnp.sum(jnp.where(hit, pos, 0.0), axis=0, keepdims=True) for hit in hits], axis=0).astype(I32)
    base_ref[...] = base_ref[...] + jnp.sum(chosen, axis=1, keepdims=True)
    cnt_ref[...] = base_ref[...]


def _route(logits_t, bias_col, upper, *, tr):
    E, N = logits_t.shape
    col = lambda i: (0, i)
    return pl.pallas_call(
        _route_kernel,
        grid=(N // tr,),
        in_specs=[pl.BlockSpec((E, tr), col), _full(bias_col.shape), _full(upper.shape)],
        out_specs=[pl.BlockSpec((TOP_K, tr), col), pl.BlockSpec((TOP_K, tr), col),
                   pl.BlockSpec((tr, LANES), lambda i: (i, 0)),
                   pl.BlockSpec((TOP_K, tr), col), _full((E, LANES))],
        out_shape=[jax.ShapeDtypeStruct((TOP_K, N), I32), jax.ShapeDtypeStruct((TOP_K, N), F32),
                   jax.ShapeDtypeStruct((N, LANES), F32),
                   jax.ShapeDtypeStruct((TOP_K, N), I32), jax.ShapeDtypeStruct((E, LANES), F32)],
        scratch_shapes=[pltpu.VMEM((E, LANES), F32)],
        compiler_params=_params(("arbitrary",)),
        name="route",
    )(logits_t, bias_col, upper)


def _plan_kernel(cnt_ref, lower_ref, idx_ref, rank_ref, dest_ref, gstart_ref, nblk_ref, pstart_ref):
    E = cnt_ref.shape[0]
    bm = float(EXPERT_ROWS)

    @pl.when(pl.program_id(0) == 0)
    def _():
        cnt = cnt_ref[...]
        nblk = jnp.floor((cnt + (bm - 1.0)) * (1.0 / bm))
        padded = nblk * bm
        hi = jnp.floor(padded * (1.0 / 256.0))
        lo = padded - hi * 256.0
        low = lower_ref[...]
        pend = 256.0 * _dot(low, hi.astype(BF16)) + _dot(low, lo.astype(BF16))
        pstart_ref[...] = pend - padded
        gstart_ref[...] = ((pend - padded) * (1.0 / bm)).astype(I32)
        nblk_ref[...] = nblk.astype(I32)

    t = idx_ref.shape[1]
    ei = lax.broadcasted_iota(I32, (E, t), 0)
    ps = pstart_ref[:, 0:1]
    rows = []
    for k in range(TOP_K):
        hit = ei == idx_ref[k:k + 1, :]
        rows.append(jnp.sum(jnp.where(hit, ps, 0.0), axis=0, keepdims=True))
    dest_ref[...] = rank_ref[...] + jnp.concatenate(rows, axis=0).astype(I32)


def _plan(cnt, lower, idx, rank, *, tr):
    E = cnt.shape[0]
    N = idx.shape[1]
    col = lambda i: (0, i)
    return pl.pallas_call(
        _plan_kernel,
        grid=(N // tr,),
        in_specs=[_full(cnt.shape), _full(lower.shape), pl.BlockSpec((TOP_K, tr), col),
                  pl.BlockSpec((TOP_K, tr), col)],
        out_specs=[pl.BlockSpec((TOP_K, tr), col), _full((E, LANES)), _full((E, LANES))],
        out_shape=[jax.ShapeDtypeStruct((TOP_K, N), I32), jax.ShapeDtypeStruct((E, LANES), I32),
                   jax.ShapeDtypeStruct((E, LANES), I32)],
        scratch_shapes=[pltpu.VMEM((E, LANES), F32)],
        compiler_params=_params(("arbitrary",)),
        name="plan",
    )(cnt, lower, idx, rank)


def _slab_rows(ref, row, slab):
    return ref.at[pl.ds(pl.multiple_of(row * slab, slab), slab)]


def _fill_zero_blocks(zero_ref, out_ref, zsem, gstart_ref, nblk_ref, *, last_of_expert):
    blk = zero_ref.shape[0]
    n_exp = gstart_ref.shape[0]
    n_blk = out_ref.shape[0] // blk
    nused = gstart_ref[n_exp - 1] + nblk_ref[n_exp - 1]
    zero_ref[...] = jnp.zeros_like(zero_ref)

    def blk_copy(b):
        return pltpu.make_async_copy(zero_ref, out_ref.at[pl.ds(pl.multiple_of(b * blk, blk), blk)], zsem)

    def tail(b, carry):
        blk_copy(b).start()
        return carry

    lax.fori_loop(nused, n_blk, tail, 0)
    n_started = n_blk - nused
    if last_of_expert:
        def last(e, n):
            has = nblk_ref[e] > 0

            @pl.when(has)
            def _():
                blk_copy(gstart_ref[e] + nblk_ref[e] - 1).start()

            return n + has.astype(I32)

        n_started = lax.fori_loop(0, n_exp, last, n_started)

    def wait(_, carry):
        blk_copy(0).wait()
        return carry

    lax.fori_loop(0, n_started, wait, 0)


def _scatter_kernel(gstart_ref, nblk_ref, *refs):
    dest_refs = refs[:TOP_K]
    h_ref, xs_ref, zero_ref, sem, zsem = refs[TOP_K:]
    ts = dest_refs[0].shape[0]
    slab = h_ref.shape[0] // ts

    @pl.when(pl.program_id(0) == 0)
    def _():
        _fill_zero_blocks(zero_ref, xs_ref, zsem, gstart_ref, nblk_ref, last_of_expert=True)

    def body(t, carry):
        src = _slab_rows(h_ref, t, slab)
        for k in range(TOP_K):
            pltpu.make_async_copy(src, _slab_rows(xs_ref, dest_refs[k][t], slab), sem).start(priority=k % 2)
        return carry

    lax.fori_loop(0, ts, body, 0)
    for _ in range(TOP_K):
        pltpu.make_async_copy(h_ref, xs_ref.at[pl.ds(0, slab * ts)], sem).wait()


def _scatter(gstart, nblk, dests, h2s, *, ts, slab, n_blk):
    rows, W = h2s.shape
    grid_spec = pltpu.PrefetchScalarGridSpec(
        num_scalar_prefetch=2,
        grid=(rows // (slab * ts),),
        in_specs=[pl.BlockSpec((ts,), lambda i, be, nu: (i,), memory_space=pltpu.SMEM)] * TOP_K
        + [pl.BlockSpec((slab * ts, W), lambda i, be, nu: (i, 0))],
        out_specs=pl.BlockSpec(memory_space=pl.ANY),
        scratch_shapes=[pltpu.VMEM((slab * EXPERT_ROWS, W), h2s.dtype), pltpu.SemaphoreType.DMA(()),
                        pltpu.SemaphoreType.DMA(())],
    )
    return pl.pallas_call(
        _scatter_kernel,
        grid_spec=grid_spec,
        out_shape=jax.ShapeDtypeStruct((n_blk * slab * EXPERT_ROWS, W), h2s.dtype),
        compiler_params=_params(("arbitrary",)),
        name="scatter",
    )(gstart, nblk, *dests, h2s)


def _experts_kernel(gstart_ref, nblk_ref, wg_ref, wu_ref, wd_ref, xs_ref, ys_ref, xbuf_ref, ybuf_ref,
                    wgb_ref, wub_ref, wdb_ref, xstage_ref, ystage_ref, zero_ref, front_ref, xsem, ysem, zsem):
    e = pl.program_id(0)
    n_exp = pl.num_programs(0)
    ring = xbuf_ref.shape[0]
    blk = zero_ref.shape[0]
    slab = wgb_ref.shape[0] // LANES
    rows = blk // slab
    nused = gstart_ref[n_exp - 1] + nblk_ref[n_exp - 1]
    g0 = gstart_ref[e]
    n = nblk_ref[e]

    def x_copy(b):
        slot = lax.rem(b, ring)
        return pltpu.make_async_copy(xs_ref.at[pl.ds(pl.multiple_of(b * blk, blk), blk)], xbuf_ref.at[slot],
                                     xsem.at[slot])

    def y_copy(b):
        slot = lax.rem(b, ring)
        return pltpu.make_async_copy(ybuf_ref.at[slot], ys_ref.at[pl.ds(pl.multiple_of(b * blk, blk), blk)],
                                     ysem.at[slot])

    @pl.when(e == 0)
    def _():
        front_ref[0] = 0

    @pl.when(n > 0)
    def _():
        wgb_ref[...] = wg_ref[0].astype(BF16)
        wub_ref[...] = wu_ref[0].astype(BF16)
        wdb_ref[...] = wd_ref[0].astype(BF16)

    def group(g, nb):
        limit = jnp.minimum(nused, g + ring)

        def fetch(b, carry):
            x_copy(b).start()
            return carry

        lax.fori_loop(front_ref[0], limit, fetch, 0)
        front_ref[0] = jnp.maximum(front_ref[0], limit)

        parts = []
        for i in range(nb):
            x_copy(g + i).wait()
            parts.append(_load_slabs(xstage_ref.at[i % 2], xbuf_ref[lax.rem(g + i, ring)], slab).astype(BF16))
        x = parts[0] if nb == 1 else jnp.concatenate(parts, axis=0)
        gate = _dot(x, wgb_ref[...])
        up = _dot(x, wub_ref[...])
        y = _dot((_silu(gate) * up).astype(BF16), wdb_ref[...])
        for i in range(nb):
            b = g + i

            @pl.when(b >= ring)
            def _():
                y_copy(b - ring).wait()

            _store_slabs(ybuf_ref.at[lax.rem(b, ring)], ystage_ref.at[i % 2], y[i * rows:(i + 1) * rows])
            y_copy(b).start()

    n4 = lax.shift_right_logical(n, 2)
    left = n - 4 * n4

    def quad(j, carry):
        group(g0 + 4 * j, 4)
        return carry

    lax.fori_loop(0, n4, quad, 0)

    @pl.when(left >= 2)
    def _():
        group(g0 + 4 * n4, 2)

    @pl.when(lax.rem(left, 2) == 1)
    def _():
        group(g0 + n - 1, 1)

    @pl.when(e == n_exp - 1)
    def _():
        def drain(b, carry):
            y_copy(b).wait()
            return carry

        lax.fori_loop(jnp.maximum(nused - ring, 0), nused, drain, 0)
        _fill_zero_blocks(zero_ref, ys_ref, zsem, gstart_ref, nblk_ref, last_of_expert=False)


def _experts(gstart, nblk, xs, wg, wu, wd):
    rows, W = xs.shape
    E, D, H = wg.shape
    blk = (D // LANES) * EXPERT_ROWS
    grid_spec = pltpu.PrefetchScalarGridSpec(
        num_scalar_prefetch=2,
        grid=(E,),
        in_specs=[pl.BlockSpec((1, D, H), lambda e, gs, nb: (e, 0, 0)),
                  pl.BlockSpec((1, D, H), lambda e, gs, nb: (e, 0, 0)),
                  pl.BlockSpec((1, H, D), lambda e, gs, nb: (e, 0, 0)),
                  pl.BlockSpec(memory_space=pl.ANY)],
        out_specs=pl.BlockSpec(memory_space=pl.ANY),
        scratch_shapes=[pltpu.VMEM((EXPERT_RING, blk, W), xs.dtype), pltpu.VMEM((EXPERT_RING, blk, W), xs.dtype),
                        pltpu.VMEM((D, H), BF16), pltpu.VMEM((D, H), BF16), pltpu.VMEM((H, D), BF16),
                        pltpu.VMEM((2, blk, LANES), F32), pltpu.VMEM((2, blk, LANES), F32),
                        pltpu.VMEM((blk, W), xs.dtype), pltpu.SMEM((1,), I32),
                        pltpu.SemaphoreType.DMA((EXPERT_RING,)), pltpu.SemaphoreType.DMA((EXPERT_RING,)),
                        pltpu.SemaphoreType.DMA(())],
    )
    return pl.pallas_call(
        _experts_kernel,
        grid_spec=grid_spec,
        out_shape=jax.ShapeDtypeStruct((rows, W), xs.dtype),
        compiler_params=_params(("arbitrary",)),
        name="experts",
    )(gstart, nblk, wg, wu, wd, xs)


def _combine_kernel(*refs, alpha):
    cur_refs, nxt_refs = refs[:TOP_K], refs[TOP_K:2 * TOP_K]
    (wt_ref, h2_ref, x1_ref, mod_ref, wsg_ref, wsu_ref, wsd_ref, g_ref, b_ref, ys_ref,
     o_ref, buf_ref, stage0_ref, stage1_ref, sem) = refs[2 * TOP_K:]
    tc, d = h2_ref.shape
    slab = d // LANES
    i = pl.program_id(0)
    par = lax.rem(i, 2)

    def gather(dest_refs, p):
        def body(t, carry):
            for k in range(TOP_K):
                src = _slab_rows(ys_ref, dest_refs[k][t], slab)
                pltpu.make_async_copy(src, _slab_rows(buf_ref.at[p, k], t, slab), sem.at[p]).start(priority=k % 2)
            return carry

        lax.fori_loop(0, tc, body, 0)

    @pl.when(i == 0)
    def _():
        gather(cur_refs, 0)

    @pl.when(i + 1 < pl.num_programs(0))
    def _():
        gather(nxt_refs, 1 - par)

    h = h2_ref[...]
    mid = (_silu(_dot(h, wsg_ref[...])) * _dot(h, wsu_ref[...])).astype(BF16)
    y = _dot(mid, wsd_ref[...])
    for k in range(TOP_K):
        pltpu.make_async_copy(ys_ref.at[pl.ds(0, slab * tc)], buf_ref.at[par, k], sem.at[par]).wait()
    for k in range(TOP_K):
        stage_ref = stage0_ref if k % 2 == 0 else stage1_ref
        y = y + wt_ref[:, k:k + 1] * _load_slabs(stage_ref, buf_ref[par, k], slab)
    z = alpha * x1_ref[...] + (1.0 + mod_ref[0, 5:6, :]) * y
    o_ref[...] = _ln_plain(z) * g_ref[...] + b_ref[...]


def _combine(dests, wt, h2, x1, mod3, wsg, wsu, wsd, ln_g, ln_b, ys, *, seq, tc, alpha):
    N, D = x1.shape
    slab = D // LANES
    per_b = seq // tc
    n_tiles = N // tc
    row = lambda i: (i, 0)
    return pl.pallas_call(
        functools.partial(_combine_kernel, alpha=alpha),
        grid=(n_tiles,),
        in_specs=[pl.BlockSpec((tc,), lambda i: (i,), memory_space=pltpu.SMEM)] * TOP_K
        + [pl.BlockSpec((tc,), lambda i: (jnp.minimum(i + 1, n_tiles - 1),), memory_space=pltpu.SMEM)] * TOP_K
        + [pl.BlockSpec((tc, LANES), row), pl.BlockSpec((tc, D), row), pl.BlockSpec((tc, D), row),
           pl.BlockSpec((1, 6, D), lambda i: (i // per_b, 0, 0)),
           _full(wsg.shape), _full(wsu.shape), _full(wsd.shape), _full(ln_g.shape), _full(ln_b.shape),
           pl.BlockSpec(memory_space=pl.ANY)],
        out_specs=pl.BlockSpec((tc, D), row),
        out_shape=jax.ShapeDtypeStruct((N, D), F32),
        scratch_shapes=[pltpu.VMEM((2, TOP_K, slab * tc, LANES), ys.dtype), _slab_scratch(tc, D),
                        _slab_scratch(tc, D), pltpu.SemaphoreType.DMA((2,))],
        compiler_params=_params(("arbitrary",)),
        name="combine",
    )(*dests, *dests, wt, h2, x1, mod3, wsg, wsu, wsd, ln_g, ln_b, ys)


def _tiles(seq):
    t = lambda want: min(want, seq)
    return dict(proj=t(512), gla=t(256), route=t(256), scatter=t(512), combine=t(256))


def _layer(x2, mod3, posb, invf, p, *, seq, alpha):
    N, D = x2.shape
    tl = _tiles(seq)
    qa, ka, va, ql, kl, vl, gl, lg = _inproj(x2, mod3, posb, invf, p["w_main"], p["b_main"], p["w_lo"], p["b_lo"],
                                             p["w_gk"], p["b_gk"], seq=seq, tm=tl["proj"])
    oa = _swa(qa, ka, va, p["sinks"], seq=seq)
    og = _gla(ql, kl, vl, gl, lg, p["norm_g"], seq=seq, tc=tl["gla"])
    x1, h2, h2s, logits_t = _outproj(oa, og, x2, mod3, p["w_oa"], p["w_og"], p["b_o"], p["ln1_g"], p["ln1_b"],
                                     p["wr_hi"], p["wr_lo"], seq=seq, tm=tl["proj"], alpha=alpha)
    idx, _, wt, rank, cnt = _route(logits_t, p["bias_col"], p["upper"], tr=tl["route"])
    E = cnt.shape[0]
    n_blk = (N * TOP_K) // EXPERT_ROWS + E
    dest, gstart, nblk = _plan(cnt, p["lower"], idx, rank, tr=tl["route"])
    gstart, nblk = gstart[:, 0], nblk[:, 0]
    dests = [dest[k] for k in range(TOP_K)]
    xs = _scatter(gstart, nblk, dests, h2s, ts=tl["scatter"], slab=D // LANES, n_blk=n_blk)
    ys = _experts(gstart, nblk, xs, p["wg"], p["wu"], p["wd"])
    return _combine(dests, wt, h2, x1, mod3, p["wsg"], p["wsu"], p["wsd"], p["ln2_g"], p["ln2_b"], ys,
                    seq=seq, tc=tl["combine"], alpha=alpha)


def kernel(x, c, positions, w_ada, b_ada, w_in, b_in, attn_sinks, w_gk2, b_gk2, gla_norm_g, w_o, b_o, ln1_g, ln1_b, w_router, router_bias, w_exp_gate, w_exp_up, w_exp_down, w_sh_gate, w_sh_up, w_sh_down, ln2_g, ln2_b):
    B, S, D = x.shape
    depth = w_ada.shape[0]
    E = w_router.shape[2]
    alpha = float((2 * depth) ** 0.25)
    tl = _tiles(S)
    n_main = int(_SEG[-1])

    posb = jnp.broadcast_to(positions.astype(F32).reshape(B * S, 1), (B * S, LANES))
    half = ATTN_HEAD_DIM // 2
    invf = (ROPE_THETA ** (-(jnp.arange(LANES) % half).astype(F32) / half)).reshape(1, LANES)
    tr = tl["route"]
    upper = (jnp.arange(tr)[:, None] < jnp.arange(tr)[None, :]).astype(BF16)
    lower = (jnp.arange(E)[:, None] >= jnp.arange(E)[None, :]).astype(BF16)
    row = lambda v: v.reshape(1, -1)

    x2 = x.reshape(B * S, D)
    for l in range(depth):
        wr_t = w_router[l].T
        wr_hi = wr_t.astype(BF16)
        p = dict(
            w_main=w_in[l][:, :n_main].astype(BF16), b_main=row(b_in[l][:n_main]),
            w_lo=jnp.pad(w_in[l][:, n_main:], ((0, 0), (0, LANES - GLA_GATE_RANK))).astype(BF16),
            b_lo=row(jnp.pad(b_in[l][n_main:], (0, LANES - GLA_GATE_RANK))),
            w_gk=jnp.pad(w_gk2[l], ((0, LANES - GLA_GATE_RANK), (0, 0))), b_gk=row(b_gk2[l]),
            sinks=attn_sinks[l], norm_g=row(gla_norm_g[l]),
            w_oa=w_o[l][:ATTN_WIDTH].astype(BF16), w_og=w_o[l][ATTN_WIDTH:].astype(BF16), b_o=row(b_o[l]),
            ln1_g=row(ln1_g[l]), ln1_b=row(ln1_b[l]),
            wr_hi=wr_hi, wr_lo=(wr_t - wr_hi.astype(F32)).astype(BF16),
            bias_col=jnp.broadcast_to(router_bias[l].reshape(E, 1), (E, LANES)),
            upper=upper, lower=lower,
            wg=w_exp_gate[l], wu=w_exp_up[l], wd=w_exp_down[l],
            wsg=w_sh_gate[l].astype(BF16), wsu=w_sh_up[l].astype(BF16), wsd=w_sh_down[l].astype(BF16),
            ln2_g=row(ln2_g[l]), ln2_b=row(ln2_b[l]),
        )
        mod = _mod(c, w_ada[l], b_ada[l])
        mod3 = mod.reshape(B, 6, D)
        x2 = _layer(x2, mod3, posb, invf, p, seq=S, alpha=alpha)
    return x2.reshape(B, S, D)
```

```python
import functools

import jax
import jax.numpy as jnp
import numpy as np
from jax import lax
from jax.experimental import pallas as pl
from jax.experimental.pallas import tpu as pltpu

F32 = jnp.float32
BF16 = jnp.bfloat16
I32 = jnp.int32

ATTN_Q_HEADS = 8
ATTN_KV_HEADS = 2
ATTN_HEAD_DIM = 64
ATTN_BLOCK = 128
ROPE_THETA = 10000.0
GLA_HEADS = 4
GLA_KEY_DIM = 64
GLA_VAL_DIM = 128
GLA_GATE_RANK = 16
GLA_GATE_NORM = 16.0
GLA_CHUNK = 64
N_GROUPS = 8
TOPK_GROUPS = 4
TOP_K = 8
ROUTED_SCALE = 2.5
LN_EPS = 1e-5
NEG_INF = -1e30
REMOVED = -3e38

ATTN_WIDTH = ATTN_Q_HEADS * ATTN_HEAD_DIM
KV_WIDTH = ATTN_KV_HEADS * ATTN_HEAD_DIM
GLA_KWIDTH = GLA_HEADS * GLA_KEY_DIM
GLA_WIDTH = GLA_HEADS * GLA_VAL_DIM

LANES = 128
VMEM_LIMIT = 56 * 1024 * 1024
EXPERT_ROWS = 256
EXPERT_RING = 8


def _params(sem):
    return pltpu.CompilerParams(dimension_semantics=sem, vmem_limit_bytes=VMEM_LIMIT)


def _full(shape):
    return pl.BlockSpec(shape, lambda *_: (0,) * len(shape))


def _split_bf16(a):
    hi = a.astype(BF16)
    lo = (a - hi.astype(F32)).astype(BF16)
    return hi, lo


def _dot(a, b, dims=(((1,), (0,)), ((), ()))):
    return lax.dot_general(a, b, dims, preferred_element_type=F32)


NT = (((1,), (1,)), ((), ()))
TN = (((0,), (0,)), ((), ()))


def _dot3(a, b, dims=(((1,), (0,)), ((), ()))):
    ah, al = _split_bf16(a)
    bh, bl = _split_bf16(b)
    return _dot(ah, bh, dims) + (_dot(ah, bl, dims) + _dot(al, bh, dims))


def _ln_plain(x):
    mu = jnp.mean(x, axis=-1, keepdims=True)
    xc = x - mu
    var = jnp.mean(xc * xc, axis=-1, keepdims=True)
    return xc * lax.rsqrt(var + LN_EPS)


def _silu(x):
    return x * (1.0 / (1.0 + jnp.exp(-x)))


def _sigmoid(x):
    return 1.0 / (1.0 + jnp.exp(-x))


def _mod_kernel(c_ref, w_ref, b_ref, o_ref):
    c = c_ref[...]
    o_ref[...] = _dot3(_silu(c), w_ref[...]) + b_ref[...]


def _mod(c, w_ada, b_ada):
    B, D = c.shape
    n = w_ada.shape[1] // D
    return pl.pallas_call(
        _mod_kernel,
        grid=(n,),
        in_specs=[_full((B, D)),
                  pl.BlockSpec((D, D), lambda j: (0, j)),
                  pl.BlockSpec((1, D), lambda j: (0, j))],
        out_specs=pl.BlockSpec((B, D), lambda j: (0, j)),
        out_shape=jax.ShapeDtypeStruct((B, n * D), F32),
        compiler_params=_params(("arbitrary",)),
        name="mod",
    )(c, w_ada, b_ada.reshape(1, -1))


_SEG = np.cumsum([0, ATTN_WIDTH, KV_WIDTH, KV_WIDTH, GLA_KWIDTH, GLA_KWIDTH, GLA_WIDTH, GLA_WIDTH])


def _rope_chunk(t, cos, sin_signed, first_half):
    up = pltpu.roll(t, LANES - 32, axis=1)
    dn = pltpu.roll(t, 32, axis=1)
    return t * cos + jnp.where(first_half, up, dn) * sin_signed


def _inproj_kernel(x_ref, mod_ref, pos_ref, invf_ref, w_ref, b_ref, wlo_ref, blo_ref, wgk_ref, bgk_ref,
                   qa_ref, ka_ref, va_ref, ql_ref, kl_ref, vl_ref, gl_ref, lg_ref):
    x = x_ref[...]
    h = _ln_plain(x) * (1.0 + mod_ref[0, 1:2, :]) + mod_ref[0, 0:1, :]
    hb = h.astype(BF16)

    def seg(i):
        lo, hi = int(_SEG[i]), int(_SEG[i + 1])
        return _dot(hb, w_ref[:, lo:hi]) + b_ref[:, lo:hi]

    ang = pos_ref[...] * invf_ref[...]
    cos = jnp.cos(ang)
    sin = jnp.sin(ang)
    lane = lax.broadcasted_iota(I32, ang.shape, 1)
    first_half = (lane % ATTN_HEAD_DIM) < (ATTN_HEAD_DIM // 2)
    sin_signed = jnp.where(first_half, -sin, sin)

    q = seg(0)
    scale = ATTN_HEAD_DIM ** -0.5
    for c in range(ATTN_WIDTH // LANES):
        t = q[:, c * LANES:(c + 1) * LANES]
        qa_ref[:, c * LANES:(c + 1) * LANES] = (_rope_chunk(t, cos, sin_signed, first_half) * scale).astype(BF16)
    ka_ref[...] = _rope_chunk(seg(1), cos, sin_signed, first_half).astype(BF16)
    va_ref[...] = seg(2).astype(BF16)
    ql_ref[...] = seg(3).astype(BF16)
    kl_ref[...] = seg(4).astype(BF16)
    vl_ref[...] = seg(5).astype(BF16)
    gl_ref[...] = seg(6).astype(BF16)
    gk_lo = _dot(hb, wlo_ref[...]) + blo_ref[...]
    gk = _dot3(gk_lo, wgk_ref[...]) + bgk_ref[...]
    log_sig = jnp.minimum(gk, 0.0) - jnp.log(1.0 + jnp.exp(-jnp.abs(gk)))
    lg_ref[...] = log_sig * (1.0 / GLA_GATE_NORM)


def _inproj(x2, mod3, posb, invf, w_main, b_main, w_lo, b_lo, w_gk, b_gk, *, seq, tm):
    N, D = x2.shape
    per_b = seq // tm
    widths = [ATTN_WIDTH, KV_WIDTH, KV_WIDTH, GLA_KWIDTH, GLA_KWIDTH, GLA_WIDTH, GLA_WIDTH, GLA_KWIDTH]
    dtypes = [BF16] * 7 + [F32]
    row = lambda i: (i, 0)
    return pl.pallas_call(
        _inproj_kernel,
        grid=(N // tm,),
        in_specs=[pl.BlockSpec((tm, D), row),
                  pl.BlockSpec((1, 6, D), lambda i: (i // per_b, 0, 0)),
                  pl.BlockSpec((tm, LANES), row),
                  _full(invf.shape), _full(w_main.shape), _full(b_main.shape),
                  _full(w_lo.shape), _full(b_lo.shape), _full(w_gk.shape), _full(b_gk.shape)],
        out_specs=[pl.BlockSpec((tm, w), row) for w in widths],
        out_shape=[jax.ShapeDtypeStruct((N, w), dt) for w, dt in zip(widths, dtypes)],
        compiler_params=_params(("parallel",)),
        name="inproj",
    )(x2, mod3, posb, invf, w_main, b_main, w_lo, b_lo, w_gk, b_gk)


def _swa_kernel(sink_ref, q_ref, kc_ref, kp_ref, vc_ref, vp_ref, o_ref):
    j = pl.program_id(1)
    blk = ATTN_BLOCK
    k2 = jnp.concatenate([kp_ref[...], kc_ref[...]], axis=0)
    v2 = jnp.concatenate([vp_ref[...], vc_ref[...]], axis=0)
    row = lax.broadcasted_iota(I32, (blk, 2 * blk), 0)
    col = lax.broadcasted_iota(I32, (blk, 2 * blk), 1)
    dist = row + blk - col
    valid = (dist >= 0) & (dist < blk) & ((col >= blk) | (j > 0))
    group = ATTN_Q_HEADS // ATTN_KV_HEADS
    for h in range(ATTN_Q_HEADS):
        kv = h // group
        qh = q_ref[:, h * ATTN_HEAD_DIM:(h + 1) * ATTN_HEAD_DIM]
        kh = k2[:, kv * ATTN_HEAD_DIM:(kv + 1) * ATTN_HEAD_DIM]
        vh = v2[:, kv * ATTN_HEAD_DIM:(kv + 1) * ATTN_HEAD_DIM]
        s = jnp.where(valid, _dot(qh, kh, NT), NEG_INF)
        sink = sink_ref[h]
        m = jnp.maximum(jnp.max(s, axis=-1, keepdims=True), sink)
        p = jnp.exp(s - m)
        denom = jnp.sum(p, axis=-1, keepdims=True) + jnp.exp(sink - m)
        o = _dot(p.astype(BF16), vh) / denom
        o_ref[:, h * ATTN_HEAD_DIM:(h + 1) * ATTN_HEAD_DIM] = o.astype(BF16)


def _swa(q, k, v, sinks, *, seq):
    N = q.shape[0]
    nb = seq // ATTN_BLOCK
    B = N // seq
    cur = lambda b, j: (b * nb + j, 0)
    prev = lambda b, j: (b * nb + jnp.maximum(j - 1, 0), 0)
    return pl.pallas_call(
        _swa_kernel,
        grid=(B, nb),
        in_specs=[pl.BlockSpec(memory_space=pltpu.SMEM),
                  pl.BlockSpec((ATTN_BLOCK, ATTN_WIDTH), cur),
                  pl.BlockSpec((ATTN_BLOCK, KV_WIDTH), cur),
                  pl.BlockSpec((ATTN_BLOCK, KV_WIDTH), prev),
                  pl.BlockSpec((ATTN_BLOCK, KV_WIDTH), cur),
                  pl.BlockSpec((ATTN_BLOCK, KV_WIDTH), prev)],
        out_specs=pl.BlockSpec((ATTN_BLOCK, ATTN_WIDTH), cur),
        out_shape=jax.ShapeDtypeStruct((N, ATTN_WIDTH), BF16),
        compiler_params=_params(("parallel", "parallel")),
        name="swa",
    )(sinks, q, k, k, v, v)


def _gla_kernel(q_ref, k_ref, v_ref, g_ref, lg_ref, ng_ref, o_ref, st_ref, *, chunks):
    @pl.when(pl.program_id(1) == 0)
    def _():
        st_ref[...] = jnp.zeros_like(st_ref)

    C = GLA_CHUNK
    dk, dv = GLA_KEY_DIM, GLA_VAL_DIM
    r = lax.broadcasted_iota(I32, (C, C), 0)
    c = lax.broadcasted_iota(I32, (C, C), 1)
    causal = c <= r
    tri = jnp.where(causal, 1.0, 0.0).astype(BF16)
    st = st_ref[...]
    for n in range(chunks):
        rows = slice(n * C, (n + 1) * C)
        lg_hi, lg_lo = _split_bf16(lg_ref[rows, :])
        b = _dot(tri, lg_hi) + _dot(tri, lg_lo)
        b_last = b[C - 1:C, :]
        q_in = (q_ref[rows, :].astype(F32) * (dk ** -0.5) * jnp.exp(b)).astype(BF16)
        kf = k_ref[rows, :].astype(F32)
        k_in = (kf * jnp.exp(-b)).astype(BF16)
        k_out = (kf * jnp.exp(b_last - b)).astype(BF16)
        decay = jnp.exp(b_last)
        stb = st.astype(BF16)
        ut = []
        for h in range(GLA_HEADS):
            ks = slice(h * dk, (h + 1) * dk)
            vs = slice(h * dv, (h + 1) * dv)
            vh = v_ref[rows, vs]
            a = jnp.where(causal, _dot(q_in[:, ks], k_in[:, ks], NT), 0.0).astype(BF16)
            o = _dot(a, vh) + _dot(q_in[:, ks], stb[:, ks], NT)
            ut.append(_dot(vh, k_out[:, ks], TN))
            o = o * lax.rsqrt(jnp.mean(o * o, axis=-1, keepdims=True) + LN_EPS) * ng_ref[...]
            o_ref[rows, vs] = (o * _silu(g_ref[rows, vs].astype(F32))).astype(BF16)
        st = st * decay + jnp.concatenate(ut, axis=1)
    st_ref[...] = st


def _gla(q, k, v, g, lg, norm_g, *, seq, tc):
    N = q.shape[0]
    B = N // seq
    per_b = seq // tc
    row = lambda b, j: (b * per_b + j, 0)
    return pl.pallas_call(
        functools.partial(_gla_kernel, chunks=tc // GLA_CHUNK),
        grid=(B, per_b),
        in_specs=[pl.BlockSpec((tc, GLA_KWIDTH), row), pl.BlockSpec((tc, GLA_KWIDTH), row),
                  pl.BlockSpec((tc, GLA_WIDTH), row), pl.BlockSpec((tc, GLA_WIDTH), row),
                  pl.BlockSpec((tc, GLA_KWIDTH), row), _full(norm_g.shape)],
        out_specs=pl.BlockSpec((tc, GLA_WIDTH), row),
        out_shape=jax.ShapeDtypeStruct((N, GLA_WIDTH), BF16),
        scratch_shapes=[pltpu.VMEM((GLA_VAL_DIM, GLA_KWIDTH), F32)],
        compiler_params=_params(("parallel", "arbitrary")),
        name="gla",
    )(q, k, v, g, lg, norm_g)


def _slab_scratch(rows, d):
    return pltpu.VMEM((rows * (d // LANES), LANES), F32)


def _store_slabs(slabs_ref, stage_ref, v):
    rows, d = v.shape
    n = d // LANES
    for c in range(n):
        stage_ref[pl.ds(c, rows, stride=n), :] = v[:, c * LANES:(c + 1) * LANES]
    slabs_ref[...] = stage_ref[...].astype(slabs_ref.dtype)


def _load_slabs(stage_ref, slabs, n):
    stage_ref[...] = slabs.astype(F32)
    rows = stage_ref.shape[0] // n
    return jnp.concatenate([stage_ref[pl.ds(c, rows, stride=n), :] for c in range(n)], axis=1)


def _outproj_kernel(oa_ref, og_ref, x_ref, mod_ref, woa_ref, wog_ref, bo_ref, g_ref, b_ref, wrh_ref, wrl_ref,
                    x1_ref, h2_ref, h2s_ref, lt_ref, stage_ref, *, alpha):
    y = _dot(oa_ref[...], woa_ref[...]) + _dot(og_ref[...], wog_ref[...]) + bo_ref[...]
    z = alpha * x_ref[...] + (1.0 + mod_ref[0, 2:3, :]) * y
    x1 = _ln_plain(z) * g_ref[...] + b_ref[...]
    x1_ref[...] = x1
    h2 = _ln_plain(x1) * (1.0 + mod_ref[0, 4:5, :]) + mod_ref[0, 3:4, :]
    h2_ref[...] = h2.astype(BF16)
    _store_slabs(h2s_ref, stage_ref, h2)
    hh, hl = _split_bf16(h2)
    wh = wrh_ref[...]
    lt_ref[...] = _dot(wh, hh, NT) + (_dot(wh, hl, NT) + _dot(wrl_ref[...], hh, NT))


def _outproj(oa, og, x2, mod3, w_oa, w_og, b_o, ln_g, ln_b, wr_hi, wr_lo, *, seq, tm, alpha):
    N, D = x2.shape
    E = wr_hi.shape[0]
    per_b = seq // tm
    row = lambda i: (i, 0)
    return pl.pallas_call(
        functools.partial(_outproj_kernel, alpha=alpha),
        grid=(N // tm,),
        in_specs=[pl.BlockSpec((tm, ATTN_WIDTH), row), pl.BlockSpec((tm, GLA_WIDTH), row),
                  pl.BlockSpec((tm, D), row),
                  pl.BlockSpec((1, 6, D), lambda i: (i // per_b, 0, 0)),
                  _full(w_oa.shape), _full(w_og.shape), _full(b_o.shape), _full(ln_g.shape), _full(ln_b.shape),
                  _full(wr_hi.shape), _full(wr_lo.shape)],
        out_specs=[pl.BlockSpec((tm, D), row), pl.BlockSpec((tm, D), row),
                   pl.BlockSpec((tm * (D // LANES), LANES), row), pl.BlockSpec((E, tm), lambda i: (0, i))],
        out_shape=[jax.ShapeDtypeStruct((N, D), F32), jax.ShapeDtypeStruct((N, D), BF16),
                   jax.ShapeDtypeStruct((N * (D // LANES), LANES), BF16), jax.ShapeDtypeStruct((E, N), F32)],
        scratch_shapes=[_slab_scratch(tm, D)],
        compiler_params=_params(("parallel",)),
        name="outproj",
    )(oa, og, x2, mod3, w_oa, w_og, b_o, ln_g, ln_b, wr_hi, wr_lo)


def _first_index(eq, idx, size):
    return jnp.min(jnp.where(eq, idx, float(size)), axis=0, keepdims=True)


def _route_kernel(lt_ref, bias_ref, upper_ref, idx_ref, w_ref, wt_ref, rank_ref, cnt_ref, base_ref):
    @pl.when(pl.program_id(0) == 0)
    def _():
        base_ref[...] = jnp.zeros_like(base_ref)

    E, t = lt_ref.shape
    gsz = E // N_GROUPS
    scores = _sigmoid(lt_ref[...])
    biased = scores + bias_ref[:, 0:1]
    gi = lax.broadcasted_iota(I32, (gsz, t), 0).astype(F32)
    gs_rows = []
    for g in range(N_GROUPS):
        grp = biased[g * gsz:(g + 1) * gsz, :]
        m1 = jnp.max(grp, axis=0, keepdims=True)
        first = _first_index(grp == m1, gi, gsz)
        m2 = jnp.max(jnp.where(gi == first, REMOVED, grp), axis=0, keepdims=True)
        gs_rows.append(m1 + m2)
    gs = jnp.concatenate(gs_rows, axis=0)
    ri = lax.broadcasted_iota(I32, (N_GROUPS, t), 0).astype(F32)
    gsel = jnp.zeros((N_GROUPS, t), F32)
    for _ in range(TOPK_GROUPS):
        m = jnp.max(gs, axis=0, keepdims=True)
        hit = ri == _first_index(gs == m, ri, N_GROUPS)
        gsel = jnp.where(hit, 1.0, gsel)
        gs = jnp.where(hit, REMOVED, gs)
    cand = jnp.concatenate(
        [jnp.where(gsel[g:g + 1, :] > 0.5, biased[g * gsz:(g + 1) * gsz, :], NEG_INF) for g in range(N_GROUPS)],
        axis=0)
    ei = lax.broadcasted_iota(I32, (E, t), 0).astype(F32)
    idx_rows, w_rows, hits = [], [], []
    chosen = jnp.zeros((E, t), F32)
    for _ in range(TOP_K):
        m = jnp.max(cand, axis=0, keepdims=True)
        fi = _first_index(cand == m, ei, E)
        hit = ei == fi
        idx_rows.append(fi)
        w_rows.append(jnp.sum(jnp.where(hit, scores, 0.0), axis=0, keepdims=True))
        hits.append(hit)
        chosen = jnp.where(hit, 1.0, chosen)
        cand = jnp.where(hit, REMOVED, cand)
    w = jnp.concatenate(w_rows, axis=0)
    w = w / jnp.sum(w, axis=0, keepdims=True) * ROUTED_SCALE
    idx_ref[...] = jnp.concatenate(idx_rows, axis=0).astype(I32)
    w_ref[...] = w
    wpad = jnp.concatenate([w, jnp.zeros((LANES - TOP_K, t), F32)], axis=0)
    wt_ref[...] = wpad.T
    prefix = _dot(chosen.astype(BF16), upper_ref[...])
    pos = base_ref[:, 0:1] + prefix
    rank_ref[...] = jnp.concatenate(
        [jnp.sum(jnp.where(hit, pos, 0.0), axis=0, keepdims=True) for hit in hits], axis=0).astype(I32)
    base_ref[...] = base_ref[...] + jnp.sum(chosen, axis=1, keepdims=True)
    cnt_ref[...] = base_ref[...]


def _route(logits_t, bias_col, upper, *, tr):
    E, N = logits_t.shape
    col = lambda i: (0, i)
    return pl.pallas_call(
        _route_kernel,
        grid=(N // tr,),
        in_specs=[pl.BlockSpec((E, tr), col), _full(bias_col.shape), _full(upper.shape)],
        out_specs=[pl.BlockSpec((TOP_K, tr), col), pl.BlockSpec((TOP_K, tr), col),
                   pl.BlockSpec((tr, LANES), lambda i: (i, 0)),
                   pl.BlockSpec((TOP_K, tr), col), _full((E, LANES))],
        out_shape=[jax.ShapeDtypeStruct((TOP_K, N), I32), jax.ShapeDtypeStruct((TOP_K, N), F32),
                   jax.ShapeDtypeStruct((N, LANES), F32),
                   jax.ShapeDtypeStruct((TOP_K, N), I32), jax.ShapeDtypeStruct((E, LANES), F32)],
        scratch_shapes=[pltpu.VMEM((E, LANES), F32)],
        compiler_params=_params(("arbitrary",)),
        name="route",
    )(logits_t, bias_col, upper)


def _plan_kernel(cnt_ref, lower_ref, idx_ref, rank_ref, dest_ref, gstart_ref, nblk_ref, pstart_ref):
    E = cnt_ref.shape[0]
    bm = float(EXPERT_ROWS)

    @pl.when(pl.program_id(0) == 0)
    def _():
        cnt = cnt_ref[...]
        nblk = jnp.floor((cnt + (bm - 1.0)) * (1.0 / bm))
        padded = nblk * bm
        hi = jnp.floor(padded * (1.0 / 256.0))
        lo = padded - hi * 256.0
        low = lower_ref[...]
        pend = 256.0 * _dot(low, hi.astype(BF16)) + _dot(low, lo.astype(BF16))
        pstart_ref[...] = pend - padded
        gstart_ref[...] = ((pend - padded) * (1.0 / bm)).astype(I32)
        nblk_ref[...] = nblk.astype(I32)

    t = idx_ref.shape[1]
    ei = lax.broadcasted_iota(I32, (E, t), 0)
    ps = pstart_ref[:, 0:1]
    rows = []
    for k in range(TOP_K):
        hit = ei == idx_ref[k:k + 1, :]
        rows.append(jnp.sum(jnp.where(hit, ps, 0.0), axis=0, keepdims=True))
    dest_ref[...] = rank_ref[...] + jnp.concatenate(rows, axis=0).astype(I32)


def _plan(cnt, lower, idx, rank, *, tr):
    E = cnt.shape[0]
    N = idx.shape[1]
    col = lambda i: (0, i)
    return pl.pallas_call(
        _plan_kernel,
        grid=(N // tr,),
        in_specs=[_full(cnt.shape), _full(lower.shape), pl.BlockSpec((TOP_K, tr), col),
                  pl.BlockSpec((TOP_K, tr), col)],
        out_specs=[pl.BlockSpec((TOP_K, tr), col), _full((E, LANES)), _full((E, LANES))],
        out_shape=[jax.ShapeDtypeStruct((TOP_K, N), I32), jax.ShapeDtypeStruct((E, LANES), I32),
                   jax.ShapeDtypeStruct((E, LANES), I32)],
        scratch_shapes=[pltpu.VMEM((E, LANES), F32)],
        compiler_params=_params(("arbitrary",)),
        name="plan",
    )(cnt, lower, idx, rank)


def _slab_rows(ref, row, slab):
    return ref.at[pl.ds(pl.multiple_of(row * slab, slab), slab)]


def _fill_zero_blocks(zero_ref, out_ref, zsem, gstart_ref, nblk_ref, *, last_of_expert):
    blk = zero_ref.shape[0]
    n_exp = gstart_ref.shape[0]
    n_blk = out_ref.shape[0] // blk
    nused = gstart_ref[n_exp - 1] + nblk_ref[n_exp - 1]
    zero_ref[...] = jnp.zeros_like(zero_ref)

    def blk_copy(b):
        return pltpu.make_async_copy(zero_ref, out_ref.at[pl.ds(pl.multiple_of(b * blk, blk), blk)], zsem)

    def tail(b, carry):
        blk_copy(b).start()
        return carry

    lax.fori_loop(nused, n_blk, tail, 0)
    n_started = n_blk - nused
    if last_of_expert:
        def last(e, n):
            has = nblk_ref[e] > 0

            @pl.when(has)
            def _():
                blk_copy(gstart_ref[e] + nblk_ref[e] - 1).start()

            return n + has.astype(I32)

        n_started = lax.fori_loop(0, n_exp, last, n_started)

    def wait(_, carry):
        blk_copy(0).wait()
        return carry

    lax.fori_loop(0, n_started, wait, 0)


def _scatter_kernel(gstart_ref, nblk_ref, *refs):
    dest_refs = refs[:TOP_K]
    h_ref, xs_ref, zero_ref, sem, zsem = refs[TOP_K:]
    ts = dest_refs[0].shape[0]
    slab = h_ref.shape[0] // ts

    @pl.when(pl.program_id(0) == 0)
    def _():
        _fill_zero_blocks(zero_ref, xs_ref, zsem, gstart_ref, nblk_ref, last_of_expert=True)

    def body(t, carry):
        src = _slab_rows(h_ref, t, slab)
        for k in range(TOP_K):
            pltpu.make_async_copy(src, _slab_rows(xs_ref, dest_refs[k][t], slab), sem).start(priority=k % 2)
        return carry

    lax.fori_loop(0, ts, body, 0)
    for _ in range(TOP_K):
        pltpu.make_async_copy(h_ref, xs_ref.at[pl.ds(0, slab * ts)], sem).wait()


def _scatter(gstart, nblk, dests, h2s, *, ts, slab, n_blk):
    rows, W = h2s.shape
    grid_spec = pltpu.PrefetchScalarGridSpec(
        num_scalar_prefetch=2,
        grid=(rows // (slab * ts),),
        in_specs=[pl.BlockSpec((ts,), lambda i, be, nu: (i,), memory_space=pltpu.SMEM)] * TOP_K
        + [pl.BlockSpec((slab * ts, W), lambda i, be, nu: (i, 0))],
        out_specs=pl.BlockSpec(memory_space=pl.ANY),
        scratch_shapes=[pltpu.VMEM((slab * EXPERT_ROWS, W), h2s.dtype), pltpu.SemaphoreType.DMA(()),
                        pltpu.SemaphoreType.DMA(())],
    )
    return pl.pallas_call(
        _scatter_kernel,
        grid_spec=grid_spec,
        out_shape=jax.ShapeDtypeStruct((n_blk * slab * EXPERT_ROWS, W), h2s.dtype),
        compiler_params=_params(("arbitrary",)),
        name="scatter",
    )(gstart, nblk, *dests, h2s)


def _experts_kernel(gstart_ref, nblk_ref, wg_ref, wu_ref, wd_ref, xs_ref, ys_ref, xbuf_ref, ybuf_ref,
                    wgb_ref, wub_ref, wdb_ref, xstage_ref, ystage_ref, zero_ref, front_ref, xsem, ysem, zsem):
    e = pl.program_id(0)
    n_exp = pl.num_programs(0)
    ring = xbuf_ref.shape[0]
    blk = zero_ref.shape[0]
    slab = wgb_ref.shape[0] // LANES
    rows = blk // slab
    nused = gstart_ref[n_exp - 1] + nblk_ref[n_exp - 1]
    g0 = gstart_ref[e]
    n = nblk_ref[e]

    def x_copy(b):
        slot = lax.rem(b, ring)
        return pltpu.make_async_copy(xs_ref.at[pl.ds(pl.multiple_of(b * blk, blk), blk)], xbuf_ref.at[slot],
                                     xsem.at[slot])

    def y_copy(b):
        slot = lax.rem(b, ring)
        return pltpu.make_async_copy(ybuf_ref.at[slot], ys_ref.at[pl.ds(pl.multiple_of(b * blk, blk), blk)],
                                     ysem.at[slot])

    @pl.when(e == 0)
    def _():
        front_ref[0] = 0

    @pl.when(n > 0)
    def _():
        wgb_ref[...] = wg_ref[0].astype(BF16)
        wub_ref[...] = wu_ref[0].astype(BF16)
        wdb_ref[...] = wd_ref[0].astype(BF16)

    def group(g, nb):
        limit = jnp.minimum(nused, g + ring)

        def fetch(b, carry):
            x_copy(b).start()
            return carry

        lax.fori_loop(front_ref[0], limit, fetch, 0)
        front_ref[0] = jnp.maximum(front_ref[0], limit)

        parts = []
        for i in range(nb):
            x_copy(g + i).wait()
            parts.append(_load_slabs(xstage_ref.at[i % 2], xbuf_ref[lax.rem(g + i, ring)], slab).astype(BF16))
        x = parts[0] if nb == 1 else jnp.concatenate(parts, axis=0)
        gate = _dot(x, wgb_ref[...])
        up = _dot(x, wub_ref[...])
        y = _dot((_silu(gate) * up).astype(BF16), wdb_ref[...])
        for i in range(nb):
            b = g + i

            @pl.when(b >= ring)
            def _():
                y_copy(b - ring).wait()

            _store_slabs(ybuf_ref.at[lax.rem(b, ring)], ystage_ref.at[i % 2], y[i * rows:(i + 1) * rows])
            y_copy(b).start()

    n4 = lax.shift_right_logical(n, 2)
    left = n - 4 * n4

    def quad(j, carry):
        group(g0 + 4 * j, 4)
        return carry

    lax.fori_loop(0, n4, quad, 0)

    @pl.when(left >= 2)
    def _():
        group(g0 + 4 * n4, 2)

    @pl.when(lax.rem(left, 2) == 1)
    def _():
        group(g0 + n - 1, 1)

    @pl.when(e == n_exp - 1)
    def _():
        def drain(b, carry):
            y_copy(b).wait()
            return carry

        lax.fori_loop(jnp.maximum(nused - ring, 0), nused, drain, 0)
        _fill_zero_blocks(zero_ref, ys_ref, zsem, gstart_ref, nblk_ref, last_of_expert=False)


def _experts(gstart, nblk, xs, wg, wu, wd):
    rows, W = xs.shape
    E, D, H = wg.shape
    blk = (D // LANES) * EXPERT_ROWS
    grid_spec = pltpu.PrefetchScalarGridSpec(
        num_scalar_prefetch=2,
        grid=(E,),
        in_specs=[pl.BlockSpec((1, D, H), lambda e, gs, nb: (e, 0, 0)),
                  pl.BlockSpec((1, D, H), lambda e, gs, nb: (e, 0, 0)),
                  pl.BlockSpec((1, H, D), lambda e, gs, nb: (e, 0, 0)),
                  pl.BlockSpec(memory_space=pl.ANY)],
        out_specs=pl.BlockSpec(memory_space=pl.ANY),
        scratch_shapes=[pltpu.VMEM((EXPERT_RING, blk, W), xs.dtype), pltpu.VMEM((EXPERT_RING, blk, W), xs.dtype),
                        pltpu.VMEM((D, H), BF16), pltpu.VMEM((D, H), BF16), pltpu.VMEM((H, D), BF16),
                        pltpu.VMEM((2, blk, LANES), F32), pltpu.VMEM((2, blk, LANES), F32),
                        pltpu.VMEM((blk, W), xs.dtype), pltpu.SMEM((1,), I32),
                        pltpu.SemaphoreType.DMA((EXPERT_RING,)), pltpu.SemaphoreType.DMA((EXPERT_RING,)),
                        pltpu.SemaphoreType.DMA(())],
    )
    return pl.pallas_call(
        _experts_kernel,
        grid_spec=grid_spec,
        out_shape=jax.ShapeDtypeStruct((rows, W), xs.dtype),
        compiler_params=_params(("arbitrary",)),
        name="experts",
    )(gstart, nblk, wg, wu, wd, xs)


def _combine_kernel(*refs, alpha):
    cur_refs, nxt_refs = refs[:TOP_K], refs[TOP_K:2 * TOP_K]
    (wt_ref, h2_ref, x1_ref, mod_ref, wsg_ref, wsu_ref, wsd_ref, g_ref, b_ref, ys_ref, o_ref) = refs[2 * TOP_K:-11]
    buf_refs = refs[-11:-3]
    stage0_ref, stage1_ref, sem = refs[-3:]
    tc, d = h2_ref.shape
    slab = d // LANES
    i = pl.program_id(0)
    last = pl.num_programs(0) - 1

    def row_copy(dest_refs, k, t):
        return pltpu.make_async_copy(_slab_rows(ys_ref, dest_refs[k][t], slab), _slab_rows(buf_refs[k], t, slab), sem)

    def wait_planes():
        for k in range(TOP_K):
            pltpu.make_async_copy(ys_ref.at[pl.ds(0, slab * tc)], buf_refs[k], sem).wait()

    @pl.when(i == 0)
    def _():
        def body(t, carry):
            for k in range(TOP_K):
                row_copy(cur_refs, k, t).start(priority=k % 2)
            return carry

        lax.fori_loop(0, tc, body, 0)

    h = h2_ref[...]
    mid = (_silu(_dot(h, wsg_ref[...])) * _dot(h, wsu_ref[...])).astype(BF16)
    y = _dot(mid, wsd_ref[...])
    wait_planes()
    for k in range(TOP_K):
        stage_ref = stage0_ref if k % 2 == 0 else stage1_ref
        y = y + wt_ref[:, k:k + 1] * _load_slabs(stage_ref, buf_refs[k][...], slab)
        for t in range(tc):
            row_copy(nxt_refs, k, t).start(priority=t % 2)
    z = alpha * x1_ref[...] + (1.0 + mod_ref[0, 5:6, :]) * y
    o_ref[...] = _ln_plain(z) * g_ref[...] + b_ref[...]

    @pl.when(i == last)
    def _():
        wait_planes()


def _combine(dests, wt, h2, x1, mod3, wsg, wsu, wsd, ln_g, ln_b, ys, *, seq, tc, alpha):
    N, D = x1.shape
    slab = D // LANES
    per_b = seq // tc
    n_tiles = N // tc
    row = lambda i: (i, 0)
    return pl.pallas_call(
        functools.partial(_combine_kernel, alpha=alpha),
        grid=(n_tiles,),
        in_specs=[pl.BlockSpec((tc,), lambda i: (i,), memory_space=pltpu.SMEM)] * TOP_K
        + [pl.BlockSpec((tc,), lambda i: (jnp.minimum(i + 1, n_tiles - 1),), memory_space=pltpu.SMEM)] * TOP_K
        + [pl.BlockSpec((tc, LANES), row), pl.BlockSpec((tc, D), row), pl.BlockSpec((tc, D), row),
           pl.BlockSpec((1, 6, D), lambda i: (i // per_b, 0, 0)),
           _full(wsg.shape), _full(wsu.shape), _full(wsd.shape), _full(ln_g.shape), _full(ln_b.shape),
           pl.BlockSpec(memory_space=pl.ANY)],
        out_specs=pl.BlockSpec((tc, D), row),
        out_shape=jax.ShapeDtypeStruct((N, D), F32),
        scratch_shapes=[pltpu.VMEM((slab * tc, LANES), ys.dtype)] * TOP_K
        + [_slab_scratch(tc, D), _slab_scratch(tc, D), pltpu.SemaphoreType.DMA(())],
        compiler_params=_params(("arbitrary",)),
        name="combine",
    )(*dests, *dests, wt, h2, x1, mod3, wsg, wsu, wsd, ln_g, ln_b, ys)


def _tiles(seq):
    t = lambda want: min(want, seq)
    return dict(proj=t(512), gla=t(256), route=t(256), scatter=t(512), combine=t(256))


def _layer(x2, mod3, posb, invf, p, *, seq, alpha):
    N, D = x2.shape
    tl = _tiles(seq)
    qa, ka, va, ql, kl, vl, gl, lg = _inproj(x2, mod3, posb, invf, p["w_main"], p["b_main"], p["w_lo"], p["b_lo"],
                                             p["w_gk"], p["b_gk"], seq=seq, tm=tl["proj"])
    oa = _swa(qa, ka, va, p["sinks"], seq=seq)
    og = _gla(ql, kl, vl, gl, lg, p["norm_g"], seq=seq, tc=tl["gla"])
    x1, h2, h2s, logits_t = _outproj(oa, og, x2, mod3, p["w_oa"], p["w_og"], p["b_o"], p["ln1_g"], p["ln1_b"],
                                     p["wr_hi"], p["wr_lo"], seq=seq, tm=tl["proj"], alpha=alpha)
    idx, _, wt, rank, cnt = _route(logits_t, p["bias_col"], p["upper"], tr=tl["route"])
    E = cnt.shape[0]
    n_blk = (N * TOP_K) // EXPERT_ROWS + E
    dest, gstart, nblk = _plan(cnt, p["lower"], idx, rank, tr=tl["route"])
    gstart, nblk = gstart[:, 0], nblk[:, 0]
    dests = [dest[k] for k in range(TOP_K)]
    xs = _scatter(gstart, nblk, dests, h2s, ts=tl["scatter"], slab=D // LANES, n_blk=n_blk)
    ys = _experts(gstart, nblk, xs, p["wg"], p["wu"], p["wd"])
    return _combine(dests, wt, h2, x1, mod3, p["wsg"], p["wsu"], p["wsd"], p["ln2_g"], p["ln2_b"], ys,
                    seq=seq, tc=tl["combine"], alpha=alpha)


def kernel(x, c, positions, w_ada, b_ada, w_in, b_in, attn_sinks, w_gk2, b_gk2, gla_norm_g, w_o, b_o, ln1_g, ln1_b, w_router, router_bias, w_exp_gate, w_exp_up, w_exp_down, w_sh_gate, w_sh_up, w_sh_down, ln2_g, ln2_b):
    B, S, D = x.shape
    depth = w_ada.shape[0]
    E = w_router.shape[2]
    alpha = float((2 * depth) ** 0.25)
    tl = _tiles(S)
    n_main = int(_SEG[-1])

    posb = jnp.broadcast_to(positions.astype(F32).reshape(B * S, 1), (B * S, LANES))
    half = ATTN_HEAD_DIM // 2
    invf = (ROPE_THETA ** (-(jnp.arange(LANES) % half).astype(F32) / half)).reshape(1, LANES)
    tr = tl["route"]
    upper = (jnp.arange(tr)[:, None] < jnp.arange(tr)[None, :]).astype(BF16)
    lower = (jnp.arange(E)[:, None] >= jnp.arange(E)[None, :]).astype(BF16)
    row = lambda v: v.reshape(1, -1)

    x2 = x.reshape(B * S, D)
    for l in range(depth):
        wr_t = w_router[l].T
        wr_hi = wr_t.astype(BF16)
        p = dict(
            w_main=w_in[l][:, :n_main].astype(BF16), b_main=row(b_in[l][:n_main]),
            w_lo=jnp.pad(w_in[l][:, n_main:], ((0, 0), (0, LANES - GLA_GATE_RANK))).astype(BF16),
            b_lo=row(jnp.pad(b_in[l][n_main:], (0, LANES - GLA_GATE_RANK))),
            w_gk=jnp.pad(w_gk2[l], ((0, LANES - GLA_GATE_RANK), (0, 0))), b_gk=row(b_gk2[l]),
            sinks=attn_sinks[l], norm_g=row(gla_norm_g[l]),
            w_oa=w_o[l][:ATTN_WIDTH].astype(BF16), w_og=w_o[l][ATTN_WIDTH:].astype(BF16), b_o=row(b_o[l]),
            ln1_g=row(ln1_g[l]), ln1_b=row(ln1_b[l]),
            wr_hi=wr_hi, wr_lo=(wr_t - wr_hi.astype(F32)).astype(BF16),
            bias_col=jnp.broadcast_to(router_bias[l].reshape(E, 1), (E, LANES)),
            upper=upper, lower=lower,
            wg=w_exp_gate[l], wu=w_exp_up[l], wd=w_exp_down[l],
            wsg=w_sh_gate[l].astype(BF16), wsu=w_sh_up[l].astype(BF16), wsd=w_sh_down[l].astype(BF16),
            ln2_g=row(ln2_g[l]), ln2_b=row(ln2_b[l]),
        )
        mod = _mod(c, w_ada[l], b_ada[l])
        mod3 = mod.reshape(B, 6, D)
        x2 = _layer(x2, mod3, posb, invf, p, seq=S, alpha=alpha)
    return x2.reshape(B, S, D)
```

```python
import functools

import jax
import jax.numpy as jnp
import numpy as np
from jax import lax
from jax.experimental import pallas as pl
from jax.experimental.pallas import tpu as pltpu

F32 = jnp.float32
BF16 = jnp.bfloat16
I32 = jnp.int32

ATTN_Q_HEADS = 8
ATTN_KV_HEADS = 2
ATTN_HEAD_DIM = 64
ATTN_BLOCK = 128
ROPE_THETA = 10000.0
GLA_HEADS = 4
GLA_KEY_DIM = 64
GLA_VAL_DIM = 128
GLA_GATE_RANK = 16
GLA_GATE_NORM = 16.0
GLA_CHUNK = 64
N_GROUPS = 8
TOPK_GROUPS = 4
TOP_K = 8
ROUTED_SCALE = 2.5
LN_EPS = 1e-5
NEG_INF = -1e30
REMOVED = -3e38

ATTN_WIDTH = ATTN_Q_HEADS * ATTN_HEAD_DIM
KV_WIDTH = ATTN_KV_HEADS * ATTN_HEAD_DIM
GLA_KWIDTH = GLA_HEADS * GLA_KEY_DIM
GLA_WIDTH = GLA_HEADS * GLA_VAL_DIM

LANES = 128
VMEM_LIMIT = 56 * 1024 * 1024
EXPERT_ROWS = 256
EXPERT_RING = 8


def _params(sem):
    return pltpu.CompilerParams(dimension_semantics=sem, vmem_limit_bytes=VMEM_LIMIT)


def _full(shape):
    return pl.BlockSpec(shape, lambda *_: (0,) * len(shape))


def _split_bf16(a):
    hi = a.astype(BF16)
    lo = (a - hi.astype(F32)).astype(BF16)
    return hi, lo


def _dot(a, b, dims=(((1,), (0,)), ((), ()))):
    return lax.dot_general(a, b, dims, preferred_element_type=F32)


NT = (((1,), (1,)), ((), ()))
TN = (((0,), (0,)), ((), ()))


def _dot3(a, b, dims=(((1,), (0,)), ((), ()))):
    ah, al = _split_bf16(a)
    bh, bl = _split_bf16(b)
    return _dot(ah, bh, dims) + (_dot(ah, bl, dims) + _dot(al, bh, dims))


def _ln_plain(x):
    mu = jnp.mean(x, axis=-1, keepdims=True)
    xc = x - mu
    var = jnp.mean(xc * xc, axis=-1, keepdims=True)
    return xc * lax.rsqrt(var + LN_EPS)


def _silu(x):
    return x * (1.0 / (1.0 + jnp.exp(-x)))


def _sigmoid(x):
    return 1.0 / (1.0 + jnp.exp(-x))


def _mod_kernel(c_ref, w_ref, b_ref, o_ref):
    c = c_ref[...]
    o_ref[...] = _dot3(_silu(c), w_ref[...]) + b_ref[...]


def _mod(c, w_ada, b_ada):
    B, D = c.shape
    n = w_ada.shape[1] // D
    return pl.pallas_call(
        _mod_kernel,
        grid=(n,),
        in_specs=[_full((B, D)),
                  pl.BlockSpec((D, D), lambda j: (0, j)),
                  pl.BlockSpec((1, D), lambda j: (0, j))],
        out_specs=pl.BlockSpec((B, D), lambda j: (0, j)),
        out_shape=jax.ShapeDtypeStruct((B, n * D), F32),
        compiler_params=_params(("arbitrary",)),
        name="mod",
    )(c, w_ada, b_ada.reshape(1, -1))


_SEG = np.cumsum([0, ATTN_WIDTH, KV_WIDTH, KV_WIDTH, GLA_KWIDTH, GLA_KWIDTH, GLA_WIDTH, GLA_WIDTH])


def _rope_chunk(t, cos, sin_signed, first_half):
    up = pltpu.roll(t, LANES - 32, axis=1)
    dn = pltpu.roll(t, 32, axis=1)
    return t * cos + jnp.where(first_half, up, dn) * sin_signed


def _inproj_kernel(x_ref, mod_ref, pos_ref, invf_ref, w_ref, b_ref, wlo_ref, blo_ref, wgk_ref, bgk_ref,
                   qa_ref, ka_ref, va_ref, ql_ref, kl_ref, vl_ref, gl_ref, lg_ref):
    x = x_ref[...]
    h = _ln_plain(x) * (1.0 + mod_ref[0, 1:2, :]) + mod_ref[0, 0:1, :]
    hb = h.astype(BF16)

    def seg(i):
        lo, hi = int(_SEG[i]), int(_SEG[i + 1])
        return _dot(hb, w_ref[:, lo:hi]) + b_ref[:, lo:hi]

    ang = pos_ref[...] * invf_ref[...]
    cos = jnp.cos(ang)
    sin = jnp.sin(ang)
    lane = lax.broadcasted_iota(I32, ang.shape, 1)
    first_half = (lane % ATTN_HEAD_DIM) < (ATTN_HEAD_DIM // 2)
    sin_signed = jnp.where(first_half, -sin, sin)

    q = seg(0)
    scale = ATTN_HEAD_DIM ** -0.5
    for c in range(ATTN_WIDTH // LANES):
        t = q[:, c * LANES:(c + 1) * LANES]
        qa_ref[:, c * LANES:(c + 1) * LANES] = (_rope_chunk(t, cos, sin_signed, first_half) * scale).astype(BF16)
    ka_ref[...] = _rope_chunk(seg(1), cos, sin_signed, first_half).astype(BF16)
    va_ref[...] = seg(2).astype(BF16)
    ql_ref[...] = seg(3).astype(BF16)
    kl_ref[...] = seg(4).astype(BF16)
    vl_ref[...] = seg(5).astype(BF16)
    gl_ref[...] = seg(6).astype(BF16)
    gk_lo = _dot(hb, wlo_ref[...]) + blo_ref[...]
    gk = _dot3(gk_lo, wgk_ref[...]) + bgk_ref[...]
    log_sig = jnp.minimum(gk, 0.0) - jnp.log(1.0 + jnp.exp(-jnp.abs(gk)))
    lg_ref[...] = log_sig * (1.0 / GLA_GATE_NORM)


def _inproj(x2, mod3, posb, invf, w_main, b_main, w_lo, b_lo, w_gk, b_gk, *, seq, tm):
    N, D = x2.shape
    per_b = seq // tm
    widths = [ATTN_WIDTH, KV_WIDTH, KV_WIDTH, GLA_KWIDTH, GLA_KWIDTH, GLA_WIDTH, GLA_WIDTH, GLA_KWIDTH]
    dtypes = [BF16] * 7 + [F32]
    row = lambda i: (i, 0)
    return pl.pallas_call(
        _inproj_kernel,
        grid=(N // tm,),
        in_specs=[pl.BlockSpec((tm, D), row),
                  pl.BlockSpec((1, 6, D), lambda i: (i // per_b, 0, 0)),
                  pl.BlockSpec((tm, LANES), row),
                  _full(invf.shape), _full(w_main.shape), _full(b_main.shape),
                  _full(w_lo.shape), _full(b_lo.shape), _full(w_gk.shape), _full(b_gk.shape)],
        out_specs=[pl.BlockSpec((tm, w), row) for w in widths],
        out_shape=[jax.ShapeDtypeStruct((N, w), dt) for w, dt in zip(widths, dtypes)],
        compiler_params=_params(("parallel",)),
        name="inproj",
    )(x2, mod3, posb, invf, w_main, b_main, w_lo, b_lo, w_gk, b_gk)


def _swa_kernel(sink_ref, q_ref, kc_ref, kp_ref, vc_ref, vp_ref, o_ref):
    j = pl.program_id(1)
    blk = ATTN_BLOCK
    k2 = jnp.concatenate([kp_ref[...], kc_ref[...]], axis=0)
    v2 = jnp.concatenate([vp_ref[...], vc_ref[...]], axis=0)
    row = lax.broadcasted_iota(I32, (blk, 2 * blk), 0)
    col = lax.broadcasted_iota(I32, (blk, 2 * blk), 1)
    dist = row + blk - col
    valid = (dist >= 0) & (dist < blk) & ((col >= blk) | (j > 0))
    group = ATTN_Q_HEADS // ATTN_KV_HEADS
    for h in range(ATTN_Q_HEADS):
        kv = h // group
        qh = q_ref[:, h * ATTN_HEAD_DIM:(h + 1) * ATTN_HEAD_DIM]
        kh = k2[:, kv * ATTN_HEAD_DIM:(kv + 1) * ATTN_HEAD_DIM]
        vh = v2[:, kv * ATTN_HEAD_DIM:(kv + 1) * ATTN_HEAD_DIM]
        s = jnp.where(valid, _dot(qh, kh, NT), NEG_INF)
        sink = sink_ref[h]
        m = jnp.maximum(jnp.max(s, axis=-1, keepdims=True), sink)
        p = jnp.exp(s - m)
        denom = jnp.sum(p, axis=-1, keepdims=True) + jnp.exp(sink - m)
        o = _dot(p.astype(BF16), vh) / denom
        o_ref[:, h * ATTN_HEAD_DIM:(h + 1) * ATTN_HEAD_DIM] = o.astype(BF16)


def _swa(q, k, v, sinks, *, seq):
    N = q.shape[0]
    nb = seq // ATTN_BLOCK
    B = N // seq
    cur = lambda b, j: (b * nb + j, 0)
    prev = lambda b, j: (b * nb + jnp.maximum(j - 1, 0), 0)
    return pl.pallas_call(
        _swa_kernel,
        grid=(B, nb),
        in_specs=[pl.BlockSpec(memory_space=pltpu.SMEM),
                  pl.BlockSpec((ATTN_BLOCK, ATTN_WIDTH), cur),
                  pl.BlockSpec((ATTN_BLOCK, KV_WIDTH), cur),
                  pl.BlockSpec((ATTN_BLOCK, KV_WIDTH), prev),
                  pl.BlockSpec((ATTN_BLOCK, KV_WIDTH), cur),
                  pl.BlockSpec((ATTN_BLOCK, KV_WIDTH), prev)],
        out_specs=pl.BlockSpec((ATTN_BLOCK, ATTN_WIDTH), cur),
        out_shape=jax.ShapeDtypeStruct((N, ATTN_WIDTH), BF16),
        compiler_params=_params(("parallel", "parallel")),
        name="swa",
    )(sinks, q, k, k, v, v)


def _gla_kernel(q_ref, k_ref, v_ref, g_ref, lg_ref, ng_ref, o_ref, st_ref, *, chunks):
    @pl.when(pl.program_id(1) == 0)
    def _():
        st_ref[...] = jnp.zeros_like(st_ref)

    C = GLA_CHUNK
    dk, dv = GLA_KEY_DIM, GLA_VAL_DIM
    r = lax.broadcasted_iota(I32, (C, C), 0)
    c = lax.broadcasted_iota(I32, (C, C), 1)
    causal = c <= r
    tri = jnp.where(causal, 1.0, 0.0).astype(BF16)
    st = st_ref[...]
    for n in range(chunks):
        rows = slice(n * C, (n + 1) * C)
        lg_hi, lg_lo = _split_bf16(lg_ref[rows, :])
        b = _dot(tri, lg_hi) + _dot(tri, lg_lo)
        b_last = b[C - 1:C, :]
        q_in = (q_ref[rows, :].astype(F32) * (dk ** -0.5) * jnp.exp(b)).astype(BF16)
        kf = k_ref[rows, :].astype(F32)
        k_in = (kf * jnp.exp(-b)).astype(BF16)
        k_out = (kf * jnp.exp(b_last - b)).astype(BF16)
        decay = jnp.exp(b_last)
        stb = st.astype(BF16)
        ut = []
        for h in range(GLA_HEADS):
            ks = slice(h * dk, (h + 1) * dk)
            vs = slice(h * dv, (h + 1) * dv)
            vh = v_ref[rows, vs]
            a = jnp.where(causal, _dot(q_in[:, ks], k_in[:, ks], NT), 0.0).astype(BF16)
            o = _dot(a, vh) + _dot(q_in[:, ks], stb[:, ks], NT)
            ut.append(_dot(vh, k_out[:, ks], TN))
            o = o * lax.rsqrt(jnp.mean(o * o, axis=-1, keepdims=True) + LN_EPS) * ng_ref[...]
            o_ref[rows, vs] = (o * _silu(g_ref[rows, vs].astype(F32))).astype(BF16)
        st = st * decay + jnp.concatenate(ut, axis=1)
    st_ref[...] = st


def _gla(q, k, v, g, lg, norm_g, *, seq, tc):
    N = q.shape[0]
    B = N // seq
    per_b = seq // tc
    row = lambda b, j: (b * per_b + j, 0)
    return pl.pallas_call(
        functools.partial(_gla_kernel, chunks=tc // GLA_CHUNK),
        grid=(B, per_b),
        in_specs=[pl.BlockSpec((tc, GLA_KWIDTH), row), pl.BlockSpec((tc, GLA_KWIDTH), row),
                  pl.BlockSpec((tc, GLA_WIDTH), row), pl.BlockSpec((tc, GLA_WIDTH), row),
                  pl.BlockSpec((tc, GLA_KWIDTH), row), _full(norm_g.shape)],
        out_specs=pl.BlockSpec((tc, GLA_WIDTH), row),
        out_shape=jax.ShapeDtypeStruct((N, GLA_WIDTH), BF16),
        scratch_shapes=[pltpu.VMEM((GLA_VAL_DIM, GLA_KWIDTH), F32)],
        compiler_params=_params(("parallel", "arbitrary")),
        name="gla",
    )(q, k, v, g, lg, norm_g)


def _slab_scratch(rows, d):
    return pltpu.VMEM((rows * (d // LANES), LANES), F32)


def _store_slabs(slabs_ref, stage_ref, v):
    rows, d = v.shape
    n = d // LANES
    for c in range(n):
        stage_ref[pl.ds(c, rows, stride=n), :] = v[:, c * LANES:(c + 1) * LANES]
    slabs_ref[...] = stage_ref[...].astype(slabs_ref.dtype)


def _load_slabs(stage_ref, slabs, n):
    stage_ref[...] = slabs.astype(F32)
    rows = stage_ref.shape[0] // n
    return jnp.concatenate([stage_ref[pl.ds(c, rows, stride=n), :] for c in range(n)], axis=1)


def _outproj_kernel(oa_ref, og_ref, x_ref, mod_ref, woa_ref, wog_ref, bo_ref, g_ref, b_ref, wrh_ref, wrl_ref,
                    x1_ref, h2_ref, h2s_ref, lt_ref, stage_ref, *, alpha):
    y = _dot(oa_ref[...], woa_ref[...]) + _dot(og_ref[...], wog_ref[...]) + bo_ref[...]
    z = alpha * x_ref[...] + (1.0 + mod_ref[0, 2:3, :]) * y
    x1 = _ln_plain(z) * g_ref[...] + b_ref[...]
    x1_ref[...] = x1
    h2 = _ln_plain(x1) * (1.0 + mod_ref[0, 4:5, :]) + mod_ref[0, 3:4, :]
    h2_ref[...] = h2.astype(BF16)
    _store_slabs(h2s_ref, stage_ref, h2)
    hh, hl = _split_bf16(h2)
    wh = wrh_ref[...]
    lt_ref[...] = _dot(wh, hh, NT) + (_dot(wh, hl, NT) + _dot(wrl_ref[...], hh, NT))


def _outproj(oa, og, x2, mod3, w_oa, w_og, b_o, ln_g, ln_b, wr_hi, wr_lo, *, seq, tm, alpha):
    N, D = x2.shape
    E = wr_hi.shape[0]
    per_b = seq // tm
    row = lambda i: (i, 0)
    return pl.pallas_call(
        functools.partial(_outproj_kernel, alpha=alpha),
        grid=(N // tm,),
        in_specs=[pl.BlockSpec((tm, ATTN_WIDTH), row), pl.BlockSpec((tm, GLA_WIDTH), row),
                  pl.BlockSpec((tm, D), row),
                  pl.BlockSpec((1, 6, D), lambda i: (i // per_b, 0, 0)),
                  _full(w_oa.shape), _full(w_og.shape), _full(b_o.shape), _full(ln_g.shape), _full(ln_b.shape),
                  _full(wr_hi.shape), _full(wr_lo.shape)],
        out_specs=[pl.BlockSpec((tm, D), row), pl.BlockSpec((tm, D), row),
                   pl.BlockSpec((tm * (D // LANES), LANES), row), pl.BlockSpec((E, tm), lambda i: (0, i))],
        out_shape=[jax.ShapeDtypeStruct((N, D), F32), jax.ShapeDtypeStruct((N, D), BF16),
                   jax.ShapeDtypeStruct((N * (D // LANES), LANES), BF16), jax.ShapeDtypeStruct((E, N), F32)],
        scratch_shapes=[_slab_scratch(tm, D)],
        compiler_params=_params(("parallel",)),
        name="outproj",
    )(oa, og, x2, mod3, w_oa, w_og, b_o, ln_g, ln_b, wr_hi, wr_lo)


def _first_index(eq, idx, size):
    return jnp.min(jnp.where(eq, idx, float(size)), axis=0, keepdims=True)


def _route_kernel(lt_ref, bias_ref, upper_ref, idx_ref, w_ref, wt_ref, rank_ref, cnt_ref, base_ref):
    @pl.when(pl.program_id(0) == 0)
    def _():
        base_ref[...] = jnp.zeros_like(base_ref)

    E, t = lt_ref.shape
    gsz = E // N_GROUPS
    scores = _sigmoid(lt_ref[...])
    biased = scores + bias_ref[:, 0:1]
    gi = lax.broadcasted_iota(I32, (gsz, t), 0).astype(F32)
    gs_rows = []
    for g in range(N_GROUPS):
        grp = biased[g * gsz:(g + 1) * gsz, :]
        m1 = jnp.max(grp, axis=0, keepdims=True)
        first = _first_index(grp == m1, gi, gsz)
        m2 = jnp.max(jnp.where(gi == first, REMOVED, grp), axis=0, keepdims=True)
        gs_rows.append(m1 + m2)
    gs = jnp.concatenate(gs_rows, axis=0)
    ri = lax.broadcasted_iota(I32, (N_GROUPS, t), 0).astype(F32)
    gsel = jnp.zeros((N_GROUPS, t), F32)
    for _ in range(TOPK_GROUPS):
        m = jnp.max(gs, axis=0, keepdims=True)
        hit = ri == _first_index(gs == m, ri, N_GROUPS)
        gsel = jnp.where(hit, 1.0, gsel)
        gs = jnp.where(hit, REMOVED, gs)
    cand = jnp.concatenate(
        [jnp.where(gsel[g:g + 1, :] > 0.5, biased[g * gsz:(g + 1) * gsz, :], NEG_INF) for g in range(N_GROUPS)],
        axis=0)
    ei = lax.broadcasted_iota(I32, (E, t), 0).astype(F32)
    idx_rows, w_rows, hits = [], [], []
    chosen = jnp.zeros((E, t), F32)
    for _ in range(TOP_K):
        m = jnp.max(cand, axis=0, keepdims=True)
        fi = _first_index(cand == m, ei, E)
        hit = ei == fi
        idx_rows.append(fi)
        w_rows.append(jnp.sum(jnp.where(hit, scores, 0.0), axis=0, keepdims=True))
        hits.append(hit)
        chosen = jnp.where(hit, 1.0, chosen)
        cand = jnp.where(hit, REMOVED, cand)
    w = jnp.concatenate(w_rows, axis=0)
    w = w / jnp.sum(w, axis=0, keepdims=True) * ROUTED_SCALE
    idx_ref[...] = jnp.concatenate(idx_rows, axis=0).astype(I32)
    w_ref[...] = w
    wpad = jnp.concatenate([w, jnp.zeros((LANES - TOP_K, t), F32)], axis=0)
    wt_ref[...] = wpad.T
    prefix = _dot(chosen.astype(BF16), upper_ref[...])
    pos = base_ref[:, 0:1] + prefix
    rank_ref[...] = jnp.concatenate(
        [jnp.sum(jnp.where(hit, pos, 0.0), axis=0, keepdims=True) for hit in hits], axis=0).astype(I32)
    base_ref[...] = base_ref[...] + jnp.sum(chosen, axis=1, keepdims=True)
    cnt_ref[...] = base_ref[...]


def _route(logits_t, bias_col, upper, *, tr):
    E, N = logits_t.shape
    col = lambda i: (0, i)
    return pl.pallas_call(
        _route_kernel,
        grid=(N // tr,),
        in_specs=[pl.BlockSpec((E, tr), col), _full(bias_col.shape), _full(upper.shape)],
        out_specs=[pl.BlockSpec((TOP_K, tr), col), pl.BlockSpec((TOP_K, tr), col),
                   pl.BlockSpec((tr, LANES), lambda i: (i, 0)),
                   pl.BlockSpec((TOP_K, tr), col), _full((E, LANES))],
        out_shape=[jax.ShapeDtypeStruct((TOP_K, N), I32), jax.ShapeDtypeStruct((TOP_K, N), F32),
                   jax.ShapeDtypeStruct((N, LANES), F32),
                   jax.ShapeDtypeStruct((TOP_K, N), I32), jax.ShapeDtypeStruct((E, LANES), F32)],
        scratch_shapes=[pltpu.VMEM((E, LANES), F32)],
        compiler_params=_params(("arbitrary",)),
        name="route",
    )(logits_t, bias_col, upper)


def _plan_kernel(cnt_ref, lower_ref, idx_ref, rank_ref, dest_ref, gstart_ref, nblk_ref, pstart_ref):
    E = cnt_ref.shape[0]
    bm = float(EXPERT_ROWS)

    @pl.when(pl.program_id(0) == 0)
    def _():
        cnt = cnt_ref[...]
        nblk = jnp.floor((cnt + (bm - 1.0)) * (1.0 / bm))
        padded = nblk * bm
        hi = jnp.floor(padded * (1.0 / 256.0))
        lo = padded - hi * 256.0
        low = lower_ref[...]
        pend = 256.0 * _dot(low, hi.astype(BF16)) + _dot(low, lo.astype(BF16))
        pstart_ref[...] = pend - padded
        gstart_ref[...] = ((pend - padded) * (1.0 / bm)).astype(I32)
        nblk_ref[...] = nblk.astype(I32)

    t = idx_ref.shape[1]
    ei = lax.broadcasted_iota(I32, (E, t), 0)
    ps = pstart_ref[:, 0:1]
    rows = []
    for k in range(TOP_K):
        hit = ei == idx_ref[k:k + 1, :]
        rows.append(jnp.sum(jnp.where(hit, ps, 0.0), axis=0, keepdims=True))
    dest_ref[...] = rank_ref[...] + jnp.concatenate(rows, axis=0).astype(I32)


def _plan(cnt, lower, idx, rank, *, tr):
    E = cnt.shape[0]
    N = idx.shape[1]
    col = lambda i: (0, i)
    return pl.pallas_call(
        _plan_kernel,
        grid=(N // tr,),
        in_specs=[_full(cnt.shape), _full(lower.shape), pl.BlockSpec((TOP_K, tr), col),
                  pl.BlockSpec((TOP_K, tr), col)],
        out_specs=[pl.BlockSpec((TOP_K, tr), col), _full((E, LANES)), _full((E, LANES))],
        out_shape=[jax.ShapeDtypeStruct((TOP_K, N), I32), jax.ShapeDtypeStruct((E, LANES), I32),
                   jax.ShapeDtypeStruct((E, LANES), I32)],
        scratch_shapes=[pltpu.VMEM((E, LANES), F32)],
        compiler_params=_params(("arbitrary",)),
        name="plan",
    )(cnt, lower, idx, rank)


def _slab_rows(ref, row, slab):
    return ref.at[pl.ds(pl.multiple_of(row * slab, slab), slab)]


def _fill_zero_blocks(zero_ref, out_ref, zsem, gstart_ref, nblk_ref, *, last_of_expert):
    blk = zero_ref.shape[0]
    n_exp = gstart_ref.shape[0]
    n_blk = out_ref.shape[0] // blk
    nused = gstart_ref[n_exp - 1] + nblk_ref[n_exp - 1]
    zero_ref[...] = jnp.zeros_like(zero_ref)

    def blk_copy(b):
        return pltpu.make_async_copy(zero_ref, out_ref.at[pl.ds(pl.multiple_of(b * blk, blk), blk)], zsem)

    def tail(b, carry):
        blk_copy(b).start()
        return carry

    lax.fori_loop(nused, n_blk, tail, 0)
    n_started = n_blk - nused
    if last_of_expert:
        def last(e, n):
            has = nblk_ref[e] > 0

            @pl.when(has)
            def _():
                blk_copy(gstart_ref[e] + nblk_ref[e] - 1).start()

            return n + has.astype(I32)

        n_started = lax.fori_loop(0, n_exp, last, n_started)

    def wait(_, carry):
        blk_copy(0).wait()
        return carry

    lax.fori_loop(0, n_started, wait, 0)


def _scatter_kernel(gstart_ref, nblk_ref, *refs):
    dest_refs = refs[:TOP_K]
    h_ref, xs_ref, zero_ref, sem, zsem = refs[TOP_K:]
    ts = dest_refs[0].shape[0]
    slab = h_ref.shape[0] // ts

    @pl.when(pl.program_id(0) == 0)
    def _():
        _fill_zero_blocks(zero_ref, xs_ref, zsem, gstart_ref, nblk_ref, last_of_expert=True)

    def body(t, carry):
        src = _slab_rows(h_ref, t, slab)
        for k in range(TOP_K):
            pltpu.make_async_copy(src, _slab_rows(xs_ref, dest_refs[k][t], slab), sem).start(priority=k % 2)
        return carry

    lax.fori_loop(0, ts, body, 0)
    for _ in range(TOP_K):
        pltpu.make_async_copy(h_ref, xs_ref.at[pl.ds(0, slab * ts)], sem).wait()


def _scatter(gstart, nblk, dests, h2s, *, ts, slab, n_blk):
    rows, W = h2s.shape
    grid_spec = pltpu.PrefetchScalarGridSpec(
        num_scalar_prefetch=2,
        grid=(rows // (slab * ts),),
        in_specs=[pl.BlockSpec((ts,), lambda i, be, nu: (i,), memory_space=pltpu.SMEM)] * TOP_K
        + [pl.BlockSpec((slab * ts, W), lambda i, be, nu: (i, 0))],
        out_specs=pl.BlockSpec(memory_space=pl.ANY),
        scratch_shapes=[pltpu.VMEM((slab * EXPERT_ROWS, W), h2s.dtype), pltpu.SemaphoreType.DMA(()),
                        pltpu.SemaphoreType.DMA(())],
    )
    return pl.pallas_call(
        _scatter_kernel,
        grid_spec=grid_spec,
        out_shape=jax.ShapeDtypeStruct((n_blk * slab * EXPERT_ROWS, W), h2s.dtype),
        compiler_params=_params(("arbitrary",)),
        name="scatter",
    )(gstart, nblk, *dests, h2s)


def _experts_kernel(gstart_ref, nblk_ref, wg_ref, wu_ref, wd_ref, xs_ref, ys_ref, xbuf_ref, ybuf_ref,
                    wgb_ref, wub_ref, wdb_ref, xstage_ref, ystage_ref, zero_ref, front_ref, xsem, ysem, zsem):
    e = pl.program_id(0)
    n_exp = pl.num_programs(0)
    ring = xbuf_ref.shape[0]
    blk = zero_ref.shape[0]
    slab = wgb_ref.shape[0] // LANES
    rows = blk // slab
    nused = gstart_ref[n_exp - 1] + nblk_ref[n_exp - 1]
    g0 = gstart_ref[e]
    n = nblk_ref[e]

    def x_copy(b):
        slot = lax.rem(b, ring)
        return pltpu.make_async_copy(xs_ref.at[pl.ds(pl.multiple_of(b * blk, blk), blk)], xbuf_ref.at[slot],
                                     xsem.at[slot])

    def y_copy(b):
        slot = lax.rem(b, ring)
        return pltpu.make_async_copy(ybuf_ref.at[slot], ys_ref.at[pl.ds(pl.multiple_of(b * blk, blk), blk)],
                                     ysem.at[slot])

    @pl.when(e == 0)
    def _():
        front_ref[0] = 0

    @pl.when(n > 0)
    def _():
        wgb_ref[...] = wg_ref[0].astype(BF16)
        wub_ref[...] = wu_ref[0].astype(BF16)
        wdb_ref[...] = wd_ref[0].astype(BF16)

    def group(g, nb):
        limit = jnp.minimum(nused, g + ring)

        def fetch(b, carry):
            x_copy(b).start()
            return carry

        lax.fori_loop(front_ref[0], limit, fetch, 0)
        front_ref[0] = jnp.maximum(front_ref[0], limit)

        for i in range(nb):
            x_copy(g + i).wait()
        for i in range(nb):
            @pl.when(g + i >= ring)
            def _():
                y_copy(g + i - ring).wait()

        for lo, hi in ([(0, nb)] if nb < 4 else [(0, nb // 2), (nb // 2, nb)]):
            parts = [_load_slabs(xstage_ref.at[i], xbuf_ref[lax.rem(g + i, ring)], slab).astype(BF16)
                     for i in range(lo, hi)]
            x = parts[0] if len(parts) == 1 else jnp.concatenate(parts, axis=0)
            gate = _dot(x, wgb_ref[...])
            up = _dot(x, wub_ref[...])
            y = _dot((_silu(gate) * up).astype(BF16), wdb_ref[...])
            for i in range(lo, hi):
                _store_slabs(ybuf_ref.at[lax.rem(g + i, ring)], ystage_ref.at[i], y[(i - lo) * rows:(i - lo + 1) * rows])
        for i in range(nb):
            y_copy(g + i).start()

    n4 = lax.shift_right_logical(n, 2)
    left = n - 4 * n4

    def quad(j, carry):
        group(g0 + 4 * j, 4)
        return carry

    lax.fori_loop(0, n4, quad, 0)

    @pl.when(left >= 2)
    def _():
        group(g0 + 4 * n4, 2)

    @pl.when(lax.rem(left, 2) == 1)
    def _():
        group(g0 + n - 1, 1)

    @pl.when(e == n_exp - 1)
    def _():
        def drain(b, carry):
            y_copy(b).wait()
            return carry

        lax.fori_loop(jnp.maximum(nused - ring, 0), nused, drain, 0)
        _fill_zero_blocks(zero_ref, ys_ref, zsem, gstart_ref, nblk_ref, last_of_expert=False)


def _experts(gstart, nblk, xs, wg, wu, wd):
    rows, W = xs.shape
    E, D, H = wg.shape
    blk = (D // LANES) * EXPERT_ROWS
    grid_spec = pltpu.PrefetchScalarGridSpec(
        num_scalar_prefetch=2,
        grid=(E,),
        in_specs=[pl.BlockSpec((1, D, H), lambda e, gs, nb: (e, 0, 0)),
                  pl.BlockSpec((1, D, H), lambda e, gs, nb: (e, 0, 0)),
                  pl.BlockSpec((1, H, D), lambda e, gs, nb: (e, 0, 0)),
                  pl.BlockSpec(memory_space=pl.ANY)],
        out_specs=pl.BlockSpec(memory_space=pl.ANY),
        scratch_shapes=[pltpu.VMEM((EXPERT_RING, blk, W), xs.dtype), pltpu.VMEM((EXPERT_RING, blk, W), xs.dtype),
                        pltpu.VMEM((D, H), BF16), pltpu.VMEM((D, H), BF16), pltpu.VMEM((H, D), BF16),
                        pltpu.VMEM((EXPERT_RING // 2, blk, LANES), F32), pltpu.VMEM((EXPERT_RING // 2, blk, LANES), F32),
                        pltpu.VMEM((blk, W), xs.dtype), pltpu.SMEM((1,), I32),
                        pltpu.SemaphoreType.DMA((EXPERT_RING,)), pltpu.SemaphoreType.DMA((EXPERT_RING,)),
                        pltpu.SemaphoreType.DMA(())],
    )
    return pl.pallas_call(
        _experts_kernel,
        grid_spec=grid_spec,
        out_shape=jax.ShapeDtypeStruct((rows, W), xs.dtype),
        compiler_params=_params(("arbitrary",)),
        name="experts",
    )(gstart, nblk, wg, wu, wd, xs)


def _combine_kernel(*refs, alpha):
    cur_refs, nxt_refs = refs[:TOP_K], refs[TOP_K:2 * TOP_K]
    (wt_ref, h2_ref, x1_ref, mod_ref, wsg_ref, wsu_ref, wsd_ref, g_ref, b_ref, ys_ref, o_ref) = refs[2 * TOP_K:-11]
    buf_refs = refs[-11:-3]
    stage0_ref, stage1_ref, sem = refs[-3:]
    tc, d = h2_ref.shape
    slab = d // LANES
    i = pl.program_id(0)
    last = pl.num_programs(0) - 1

    def row_copy(dest_refs, k, t):
        return pltpu.make_async_copy(_slab_rows(ys_ref, dest_refs[k][t], slab), _slab_rows(buf_refs[k], t, slab), sem)

    def wait_planes():
        for k in range(TOP_K):
            pltpu.make_async_copy(ys_ref.at[pl.ds(0, slab * tc)], buf_refs[k], sem).wait()

    @pl.when(i == 0)
    def _():
        def body(t, carry):
            for k in range(TOP_K):
                row_copy(cur_refs, k, t).start(priority=k % 2)
            return carry

        lax.fori_loop(0, tc, body, 0)

    h = h2_ref[...]
    mid = (_silu(_dot(h, wsg_ref[...])) * _dot(h, wsu_ref[...])).astype(BF16)
    y = _dot(mid, wsd_ref[...])
    wait_planes()
    for k in range(TOP_K):
        stage_ref = stage0_ref if k % 2 == 0 else stage1_ref
        y = y + wt_ref[:, k:k + 1] * _load_slabs(stage_ref, buf_refs[k][...], slab)
        for t in range(tc):
            row_copy(nxt_refs, k, t).start(priority=t % 2)
    z = alpha * x1_ref[...] + (1.0 + mod_ref[0, 5:6, :]) * y
    o_ref[...] = _ln_plain(z) * g_ref[...] + b_ref[...]

    @pl.when(i == last)
    def _():
        wait_planes()


def _combine(dests, wt, h2, x1, mod3, wsg, wsu, wsd, ln_g, ln_b, ys, *, seq, tc, alpha):
    N, D = x1.shape
    slab = D // LANES
    per_b = seq // tc
    n_tiles = N // tc
    row = lambda i: (i, 0)
    return pl.pallas_call(
        functools.partial(_combine_kernel, alpha=alpha),
        grid=(n_tiles,),
        in_specs=[pl.BlockSpec((tc,), lambda i: (i,), memory_space=pltpu.SMEM)] * TOP_K
        + [pl.BlockSpec((tc,), lambda i: (jnp.minimum(i + 1, n_tiles - 1),), memory_space=pltpu.SMEM)] * TOP_K
        + [pl.BlockSpec((tc, LANES), row), pl.BlockSpec((tc, D), row), pl.BlockSpec((tc, D), row),
           pl.BlockSpec((1, 6, D), lambda i: (i // per_b, 0, 0)),
           _full(wsg.shape), _full(wsu.shape), _full(wsd.shape), _full(ln_g.shape), _full(ln_b.shape),
           pl.BlockSpec(memory_space=pl.ANY)],
        out_specs=pl.BlockSpec((tc, D), row),
        out_shape=jax.ShapeDtypeStruct((N, D), F32),
        scratch_shapes=[pltpu.VMEM((slab * tc, LANES), ys.dtype)] * TOP_K
        + [_slab_scratch(tc, D), _slab_scratch(tc, D), pltpu.SemaphoreType.DMA(())],
        compiler_params=_params(("arbitrary",)),
        name="combine",
    )(*dests, *dests, wt, h2, x1, mod3, wsg, wsu, wsd, ln_g, ln_b, ys)


def _tiles(seq):
    t = lambda want: min(want, seq)
    return dict(proj=t(512), gla=t(256), route=t(256), scatter=t(512), combine=t(256))


def _layer(x2, mod3, posb, invf, p, *, seq, alpha):
    N, D = x2.shape
    tl = _tiles(seq)
    qa, ka, va, ql, kl, vl, gl, lg = _inproj(x2, mod3, posb, invf, p["w_main"], p["b_main"], p["w_lo"], p["b_lo"],
                                             p["w_gk"], p["b_gk"], seq=seq, tm=tl["proj"])
    oa = _swa(qa, ka, va, p["sinks"], seq=seq)
    og = _gla(ql, kl, vl, gl, lg, p["norm_g"], seq=seq, tc=tl["gla"])
    x1, h2, h2s, logits_t = _outproj(oa, og, x2, mod3, p["w_oa"], p["w_og"], p["b_o"], p["ln1_g"], p["ln1_b"],
                                     p["wr_hi"], p["wr_lo"], seq=seq, tm=tl["proj"], alpha=alpha)
    idx, _, wt, rank, cnt = _route(logits_t, p["bias_col"], p["upper"], tr=tl["route"])
    E = cnt.shape[0]
    n_blk = (N * TOP_K) // EXPERT_ROWS + E
    dest, gstart, nblk = _plan(cnt, p["lower"], idx, rank, tr=tl["route"])
    gstart, nblk = gstart[:, 0], nblk[:, 0]
    dests = [dest[k] for k in range(TOP_K)]
    xs = _scatter(gstart, nblk, dests, h2s, ts=tl["scatter"], slab=D // LANES, n_blk=n_blk)
    ys = _experts(gstart, nblk, xs, p["wg"], p["wu"], p["wd"])
    return _combine(dests, wt, h2, x1, mod3, p["wsg"], p["wsu"], p["wsd"], p["ln2_g"], p["ln2_b"], ys,
                    seq=seq, tc=tl["combine"], alpha=alpha)


def kernel(x, c, positions, w_ada, b_ada, w_in, b_in, attn_sinks, w_gk2, b_gk2, gla_norm_g, w_o, b_o, ln1_g, ln1_b, w_router, router_bias, w_exp_gate, w_exp_up, w_exp_down, w_sh_gate, w_sh_up, w_sh_down, ln2_g, ln2_b):
    B, S, D = x.shape
    depth = w_ada.shape[0]
    E = w_router.shape[2]
    alpha = float((2 * depth) ** 0.25)
    tl = _tiles(S)
    n_main = int(_SEG[-1])

    posb = jnp.broadcast_to(positions.astype(F32).reshape(B * S, 1), (B * S, LANES))
    half = ATTN_HEAD_DIM // 2
    invf = (ROPE_THETA ** (-(jnp.arange(LANES) % half).astype(F32) / half)).reshape(1, LANES)
    tr = tl["route"]
    upper = (jnp.arange(tr)[:, None] < jnp.arange(tr)[None, :]).astype(BF16)
    lower = (jnp.arange(E)[:, None] >= jnp.arange(E)[None, :]).astype(BF16)
    row = lambda v: v.reshape(1, -1)

    x2 = x.reshape(B * S, D)
    for l in range(depth):
        wr_t = w_router[l].T
        wr_hi = wr_t.astype(BF16)
        p = dict(
            w_main=w_in[l][:, :n_main].astype(BF16), b_main=row(b_in[l][:n_main]),
            w_lo=jnp.pad(w_in[l][:, n_main:], ((0, 0), (0, LANES - GLA_GATE_RANK))).astype(BF16),
            b_lo=row(jnp.pad(b_in[l][n_main:], (0, LANES - GLA_GATE_RANK))),
            w_gk=jnp.pad(w_gk2[l], ((0, LANES - GLA_GATE_RANK), (0, 0))), b_gk=row(b_gk2[l]),
            sinks=attn_sinks[l], norm_g=row(gla_norm_g[l]),
            w_oa=w_o[l][:ATTN_WIDTH].astype(BF16), w_og=w_o[l][ATTN_WIDTH:].astype(BF16), b_o=row(b_o[l]),
            ln1_g=row(ln1_g[l]), ln1_b=row(ln1_b[l]),
            wr_hi=wr_hi, wr_lo=(wr_t - wr_hi.astype(F32)).astype(BF16),
            bias_col=jnp.broadcast_to(router_bias[l].reshape(E, 1), (E, LANES)),
            upper=upper, lower=lower,
            wg=w_exp_gate[l], wu=w_exp_up[l], wd=w_exp_down[l],
            wsg=w_sh_gate[l].astype(BF16), wsu=w_sh_up[l].astype(BF16), wsd=w_sh_down[l].astype(BF16),
            ln2_g=row(ln2_g[l]), ln2_b=row(ln2_b[l]),
        )
        mod = _mod(c, w_ada[l], b_ada[l])
        mod3 = mod.reshape(B, 6, D)
        x2 = _layer(x2, mod3, posb, invf, p, seq=S, alpha=alpha)
    return x2.reshape(B, S, D)
```

```python
import functools

import jax
import jax.numpy as jnp
import numpy as np
from jax import lax
from jax.experimental import pallas as pl
from jax.experimental.pallas import tpu as pltpu

F32 = jnp.float32
BF16 = jnp.bfloat16
I32 = jnp.int32

ATTN_Q_HEADS = 8
ATTN_KV_HEADS = 2
ATTN_HEAD_DIM = 64
ATTN_BLOCK = 128
ROPE_THETA = 10000.0
GLA_HEADS = 4
GLA_KEY_DIM = 64
GLA_VAL_DIM = 128
GLA_GATE_RANK = 16
GLA_GATE_NORM = 16.0
GLA_CHUNK = 64
N_GROUPS = 8
TOPK_GROUPS = 4
TOP_K = 8
ROUTED_SCALE = 2.5
LN_EPS = 1e-5
NEG_INF = -1e30
REMOVED = -3e38

ATTN_WIDTH = ATTN_Q_HEADS * ATTN_HEAD_DIM
KV_WIDTH = ATTN_KV_HEADS * ATTN_HEAD_DIM
GLA_KWIDTH = GLA_HEADS * GLA_KEY_DIM
GLA_WIDTH = GLA_HEADS * GLA_VAL_DIM

LANES = 128
VMEM_LIMIT = 56 * 1024 * 1024
EXPERT_ROWS = 256
EXPERT_RING = 8


def _params(sem):
    return pltpu.CompilerParams(dimension_semantics=sem, vmem_limit_bytes=VMEM_LIMIT)


def _full(shape):
    return pl.BlockSpec(shape, lambda *_: (0,) * len(shape))


def _split_bf16(a):
    hi = a.astype(BF16)
    lo = (a - hi.astype(F32)).astype(BF16)
    return hi, lo


def _dot(a, b, dims=(((1,), (0,)), ((), ()))):
    return lax.dot_general(a, b, dims, preferred_element_type=F32)


NT = (((1,), (1,)), ((), ()))
TN = (((0,), (0,)), ((), ()))


def _dot3(a, b, dims=(((1,), (0,)), ((), ()))):
    ah, al = _split_bf16(a)
    bh, bl = _split_bf16(b)
    return _dot(ah, bh, dims) + (_dot(ah, bl, dims) + _dot(al, bh, dims))


def _ln_plain(x):
    mu = jnp.mean(x, axis=-1, keepdims=True)
    xc = x - mu
    var = jnp.mean(xc * xc, axis=-1, keepdims=True)
    return xc * lax.rsqrt(var + LN_EPS)


def _silu(x):
    return x * (1.0 / (1.0 + jnp.exp(-x)))


def _sigmoid(x):
    return 1.0 / (1.0 + jnp.exp(-x))


def _mod_kernel(c_ref, w_ref, b_ref, o_ref):
    c = c_ref[...]
    o_ref[...] = _dot3(_silu(c), w_ref[...]) + b_ref[...]


def _mod(c, w_ada, b_ada):
    B, D = c.shape
    n = w_ada.shape[1] // D
    return pl.pallas_call(
        _mod_kernel,
        grid=(n,),
        in_specs=[_full((B, D)),
                  pl.BlockSpec((D, D), lambda j: (0, j)),
                  pl.BlockSpec((1, D), lambda j: (0, j))],
        out_specs=pl.BlockSpec((B, D), lambda j: (0, j)),
        out_shape=jax.ShapeDtypeStruct((B, n * D), F32),
        compiler_params=_params(("arbitrary",)),
        name="mod",
    )(c, w_ada, b_ada.reshape(1, -1))


_SEG = np.cumsum([0, ATTN_WIDTH, KV_WIDTH, KV_WIDTH, GLA_KWIDTH, GLA_KWIDTH, GLA_WIDTH, GLA_WIDTH])


def _rope_chunk(t, cos, sin_signed, first_half):
    up = pltpu.roll(t, LANES - 32, axis=1)
    dn = pltpu.roll(t, 32, axis=1)
    return t * cos + jnp.where(first_half, up, dn) * sin_signed


def _inproj_kernel(x_ref, mod_ref, pos_ref, invf_ref, w_ref, b_ref, wlo_ref, blo_ref, wgk_ref, bgk_ref,
                   qa_ref, ka_ref, va_ref, ql_ref, kl_ref, vl_ref, gl_ref, lg_ref):
    x = x_ref[...]
    h = _ln_plain(x) * (1.0 + mod_ref[0, 1:2, :]) + mod_ref[0, 0:1, :]
    hb = h.astype(BF16)

    def seg(i):
        lo, hi = int(_SEG[i]), int(_SEG[i + 1])
        return _dot(hb, w_ref[:, lo:hi]) + b_ref[:, lo:hi]

    ang = pos_ref[...] * invf_ref[...]
    cos = jnp.cos(ang)
    sin = jnp.sin(ang)
    lane = lax.broadcasted_iota(I32, ang.shape, 1)
    first_half = (lane % ATTN_HEAD_DIM) < (ATTN_HEAD_DIM // 2)
    sin_signed = jnp.where(first_half, -sin, sin)

    q = seg(0)
    scale = ATTN_HEAD_DIM ** -0.5
    for c in range(ATTN_WIDTH // LANES):
        t = q[:, c * LANES:(c + 1) * LANES]
        qa_ref[:, c * LANES:(c + 1) * LANES] = (_rope_chunk(t, cos, sin_signed, first_half) * scale).astype(BF16)
    ka_ref[...] = _rope_chunk(seg(1), cos, sin_signed, first_half).astype(BF16)
    va_ref[...] = seg(2).astype(BF16)
    ql_ref[...] = seg(3).astype(BF16)
    kl_ref[...] = seg(4).astype(BF16)
    vl_ref[...] = seg(5).astype(BF16)
    gl_ref[...] = seg(6).astype(BF16)
    gk_lo = _dot(hb, wlo_ref[...]) + blo_ref[...]
    gk = _dot3(gk_lo, wgk_ref[...]) + bgk_ref[...]
    log_sig = jnp.minimum(gk, 0.0) - jnp.log(1.0 + jnp.exp(-jnp.abs(gk)))
    lg_ref[...] = log_sig * (1.0 / GLA_GATE_NORM)


def _inproj(x2, mod3, posb, invf, w_main, b_main, w_lo, b_lo, w_gk, b_gk, *, seq, tm):
    N, D = x2.shape
    per_b = seq // tm
    widths = [ATTN_WIDTH, KV_WIDTH, KV_WIDTH, GLA_KWIDTH, GLA_KWIDTH, GLA_WIDTH, GLA_WIDTH, GLA_KWIDTH]
    dtypes = [BF16] * 7 + [F32]
    row = lambda i: (i, 0)
    return pl.pallas_call(
        _inproj_kernel,
        grid=(N // tm,),
        in_specs=[pl.BlockSpec((tm, D), row),
                  pl.BlockSpec((1, 6, D), lambda i: (i // per_b, 0, 0)),
                  pl.BlockSpec((tm, LANES), row),
                  _full(invf.shape), _full(w_main.shape), _full(b_main.shape),
                  _full(w_lo.shape), _full(b_lo.shape), _full(w_gk.shape), _full(b_gk.shape)],
        out_specs=[pl.BlockSpec((tm, w), row) for w in widths],
        out_shape=[jax.ShapeDtypeStruct((N, w), dt) for w, dt in zip(widths, dtypes)],
        compiler_params=_params(("parallel",)),
        name="inproj",
    )(x2, mod3, posb, invf, w_main, b_main, w_lo, b_lo, w_gk, b_gk)


def _swa_kernel(sink_ref, q_ref, kc_ref, kp_ref, vc_ref, vp_ref, o_ref):
    j = pl.program_id(1)
    blk = ATTN_BLOCK
    row = lax.broadcasted_iota(I32, (blk, 2 * blk), 0)
    col = lax.broadcasted_iota(I32, (blk, 2 * blk), 1)
    dist = row + blk - col
    band = (dist >= 0) & (dist < blk)
    group = ATTN_Q_HEADS // ATTN_KV_HEADS
    hd = ATTN_HEAD_DIM
    gw = group * hd
    kw = 2 * blk
    bd = (lax.broadcasted_iota(I32, (group * kw, gw), 0) // kw) == (lax.broadcasted_iota(I32, (group * kw, gw), 1) // hd)
    lane_head = lax.broadcasted_iota(I32, (blk, gw), 1) // hd
    for s_blk in range(q_ref.shape[0] // blk):
        rows = slice(s_blk * blk, (s_blk + 1) * blk)
        if s_blk == 0:
            k2 = jnp.concatenate([kp_ref[...], kc_ref[0:blk, :]], axis=0)
            v2 = jnp.concatenate([vp_ref[...], vc_ref[0:blk, :]], axis=0)
            valid = band & ((col >= blk) | (j > 0))
        else:
            k2 = kc_ref[(s_blk - 1) * blk:(s_blk + 1) * blk, :]
            v2 = vc_ref[(s_blk - 1) * blk:(s_blk + 1) * blk, :]
            valid = band
        for g in range(ATTN_KV_HEADS):
            lanes = slice(g * gw, (g + 1) * gw)
            zero = jnp.zeros((), BF16)
            kd = jnp.where(bd, jnp.concatenate([k2[:, lanes]] * group, axis=0), zero)
            vd = jnp.where(bd, jnp.concatenate([v2[:, lanes]] * group, axis=0), zero)
            s_all = _dot(q_ref[rows, lanes], kd, NT)
            probs, denom = [], jnp.zeros((blk, gw), F32)
            for r in range(group):
                sink = sink_ref[g * group + r]
                s = jnp.where(valid, s_all[:, r * kw:(r + 1) * kw], NEG_INF)
                m = jnp.maximum(jnp.max(s, axis=-1, keepdims=True), sink)
                p = jnp.exp(s - m)
                d = jnp.sum(p, axis=-1, keepdims=True) + jnp.exp(sink - m)
                probs.append(p.astype(BF16))
                denom = jnp.where(lane_head == r, d, denom)
            o = _dot(jnp.concatenate(probs, axis=1), vd) / denom
            o_ref[rows, lanes] = o.astype(BF16)


def _repeat_kv_heads(t):
    n = t.shape[0]
    t = t.reshape(n, ATTN_KV_HEADS, 1, ATTN_HEAD_DIM)
    t = jnp.broadcast_to(t, (n, ATTN_KV_HEADS, ATTN_Q_HEADS // ATTN_KV_HEADS, ATTN_HEAD_DIM))
    return t.reshape(n, ATTN_WIDTH)


def _swa(q, k, v, sinks, *, seq, tq):
    N = q.shape[0]
    per_b = seq // tq
    sub = tq // ATTN_BLOCK
    B = N // seq
    cur = lambda b, j: (b * per_b + j, 0)
    prev = lambda b, j: ((b * per_b + j) * sub - jnp.minimum(j, 1), 0)
    return pl.pallas_call(
        _swa_kernel,
        grid=(B, per_b),
        in_specs=[pl.BlockSpec(memory_space=pltpu.SMEM),
                  pl.BlockSpec((tq, ATTN_WIDTH), cur),
                  pl.BlockSpec((tq, ATTN_WIDTH), cur),
                  pl.BlockSpec((ATTN_BLOCK, ATTN_WIDTH), prev),
                  pl.BlockSpec((tq, ATTN_WIDTH), cur),
                  pl.BlockSpec((ATTN_BLOCK, ATTN_WIDTH), prev)],
        out_specs=pl.BlockSpec((tq, ATTN_WIDTH), cur),
        out_shape=jax.ShapeDtypeStruct((N, ATTN_WIDTH), BF16),
        compiler_params=_params(("parallel", "parallel")),
        name="swa",
    )(sinks, q, k, k, v, v)


def _gla_kernel(q_ref, k_ref, v_ref, g_ref, lg_ref, ng_ref, o_ref, st_ref, *, chunks):
    @pl.when(pl.program_id(1) == 0)
    def _():
        st_ref[...] = jnp.zeros_like(st_ref)

    C = GLA_CHUNK
    dk, dv = GLA_KEY_DIM, GLA_VAL_DIM
    r = lax.broadcasted_iota(I32, (C, C), 0)
    c = lax.broadcasted_iota(I32, (C, C), 1)
    causal = c <= r
    tri = jnp.where(causal, 1.0, 0.0).astype(BF16)
    st = st_ref[...]
    for n in range(chunks):
        rows = slice(n * C, (n + 1) * C)
        lg_hi, lg_lo = _split_bf16(lg_ref[rows, :])
        b = _dot(tri, lg_hi) + _dot(tri, lg_lo)
        b_last = b[C - 1:C, :]
        q_in = (q_ref[rows, :].astype(F32) * (dk ** -0.5) * jnp.exp(b)).astype(BF16)
        kf = k_ref[rows, :].astype(F32)
        k_in = (kf * jnp.exp(-b)).astype(BF16)
        k_out = (kf * jnp.exp(b_last - b)).astype(BF16)
        decay = jnp.exp(b_last)
        stb = st.astype(BF16)
        ut = []
        for h in range(GLA_HEADS):
            ks = slice(h * dk, (h + 1) * dk)
            vs = slice(h * dv, (h + 1) * dv)
            vh = v_ref[rows, vs]
            a = jnp.where(causal, _dot(q_in[:, ks], k_in[:, ks], NT), 0.0).astype(BF16)
            o = _dot(a, vh) + _dot(q_in[:, ks], stb[:, ks], NT)
            ut.append(_dot(vh, k_out[:, ks], TN))
            o = o * lax.rsqrt(jnp.mean(o * o, axis=-1, keepdims=True) + LN_EPS) * ng_ref[...]
            o_ref[rows, vs] = (o * _silu(g_ref[rows, vs].astype(F32))).astype(BF16)
        st = st * decay + jnp.concatenate(ut, axis=1)
    st_ref[...] = st


def _gla(q, k, v, g, lg, norm_g, *, seq, tc):
    N = q.shape[0]
    B = N // seq
    per_b = seq // tc
    row = lambda b, j: (b * per_b + j, 0)
    return pl.pallas_call(
        functools.partial(_gla_kernel, chunks=tc // GLA_CHUNK),
        grid=(B, per_b),
        in_specs=[pl.BlockSpec((tc, GLA_KWIDTH), row), pl.BlockSpec((tc, GLA_KWIDTH), row),
                  pl.BlockSpec((tc, GLA_WIDTH), row), pl.BlockSpec((tc, GLA_WIDTH), row),
                  pl.BlockSpec((tc, GLA_KWIDTH), row), _full(norm_g.shape)],
        out_specs=pl.BlockSpec((tc, GLA_WIDTH), row),
        out_shape=jax.ShapeDtypeStruct((N, GLA_WIDTH), BF16),
        scratch_shapes=[pltpu.VMEM((GLA_VAL_DIM, GLA_KWIDTH), F32)],
        compiler_params=_params(("parallel", "arbitrary")),
        name="gla",
    )(q, k, v, g, lg, norm_g)


def _slab_scratch(rows, d):
    return pltpu.VMEM((rows * (d // LANES), LANES), F32)


def _store_slabs(slabs_ref, stage_ref, v):
    rows, d = v.shape
    n = d // LANES
    for c in range(n):
        stage_ref[pl.ds(c, rows, stride=n), :] = v[:, c * LANES:(c + 1) * LANES]
    slabs_ref[...] = stage_ref[...].astype(slabs_ref.dtype)


def _load_slabs(stage_ref, slabs, n):
    stage_ref[...] = slabs.astype(F32)
    rows = stage_ref.shape[0] // n
    return jnp.concatenate([stage_ref[pl.ds(c, rows, stride=n), :] for c in range(n)], axis=1)


def _outproj_kernel(oa_ref, og_ref, x_ref, mod_ref, woa_ref, wog_ref, bo_ref, g_ref, b_ref, wrh_ref, wrl_ref,
                    x1_ref, h2_ref, h2s_ref, lt_ref, stage_ref, *, alpha):
    y = _dot(oa_ref[...], woa_ref[...]) + _dot(og_ref[...], wog_ref[...]) + bo_ref[...]
    z = alpha * x_ref[...] + (1.0 + mod_ref[0, 2:3, :]) * y
    x1 = _ln_plain(z) * g_ref[...] + b_ref[...]
    x1_ref[...] = x1
    h2 = _ln_plain(x1) * (1.0 + mod_ref[0, 4:5, :]) + mod_ref[0, 3:4, :]
    h2_ref[...] = h2.astype(BF16)
    _store_slabs(h2s_ref, stage_ref, h2)
    hh, hl = _split_bf16(h2)
    wh = wrh_ref[...]
    lt_ref[...] = _dot(wh, hh, NT) + (_dot(wh, hl, NT) + _dot(wrl_ref[...], hh, NT))


def _outproj(oa, og, x2, mod3, w_oa, w_og, b_o, ln_g, ln_b, wr_hi, wr_lo, *, seq, tm, alpha):
    N, D = x2.shape
    E = wr_hi.shape[0]
    per_b = seq // tm
    row = lambda i: (i, 0)
    return pl.pallas_call(
        functools.partial(_outproj_kernel, alpha=alpha),
        grid=(N // tm,),
        in_specs=[pl.BlockSpec((tm, ATTN_WIDTH), row), pl.BlockSpec((tm, GLA_WIDTH), row),
                  pl.BlockSpec((tm, D), row),
                  pl.BlockSpec((1, 6, D), lambda i: (i // per_b, 0, 0)),
                  _full(w_oa.shape), _full(w_og.shape), _full(b_o.shape), _full(ln_g.shape), _full(ln_b.shape),
                  _full(wr_hi.shape), _full(wr_lo.shape)],
        out_specs=[pl.BlockSpec((tm, D), row), pl.BlockSpec((tm, D), row),
                   pl.BlockSpec((tm * (D // LANES), LANES), row), pl.BlockSpec((E, tm), lambda i: (0, i))],
        out_shape=[jax.ShapeDtypeStruct((N, D), F32), jax.ShapeDtypeStruct((N, D), BF16),
                   jax.ShapeDtypeStruct((N * (D // LANES), LANES), BF16), jax.ShapeDtypeStruct((E, N), F32)],
        scratch_shapes=[_slab_scratch(tm, D)],
        compiler_params=_params(("parallel",)),
        name="outproj",
    )(oa, og, x2, mod3, w_oa, w_og, b_o, ln_g, ln_b, wr_hi, wr_lo)


def _first_index(eq, idx, size):
    return jnp.min(jnp.where(eq, idx, float(size)), axis=0, keepdims=True)


def _route_kernel(lt_ref, bias_ref, upper_ref, idx_ref, w_ref, wt_ref, rank_ref, cnt_ref, base_ref):
    @pl.when(pl.program_id(0) == 0)
    def _():
        base_ref[...] = jnp.zeros_like(base_ref)

    E, t = lt_ref.shape
    gsz = E // N_GROUPS
    scores = _sigmoid(lt_ref[...])
    biased = scores + bias_ref[:, 0:1]
    gi = lax.broadcasted_iota(I32, (gsz, t), 0).astype(F32)
    gs_rows = []
    for g in range(N_GROUPS):
        grp = biased[g * gsz:(g + 1) * gsz, :]
        m1 = jnp.max(grp, axis=0, keepdims=True)
        first = _first_index(grp == m1, gi, gsz)
        m2 = jnp.max(jnp.where(gi == first, REMOVED, grp), axis=0, keepdims=True)
        gs_rows.append(m1 + m2)
    gs = jnp.concatenate(gs_rows, axis=0)
    ri = lax.broadcasted_iota(I32, (N_GROUPS, t), 0).astype(F32)
    gsel = jnp.zeros((N_GROUPS, t), F32)
    for _ in range(TOPK_GROUPS):
        m = jnp.max(gs, axis=0, keepdims=True)
        hit = ri == _first_index(gs == m, ri, N_GROUPS)
        gsel = jnp.where(hit, 1.0, gsel)
        gs = jnp.where(hit, REMOVED, gs)
    cand = jnp.concatenate(
        [jnp.where(gsel[g:g + 1, :] > 0.5, biased[g * gsz:(g + 1) * gsz, :], NEG_INF) for g in range(N_GROUPS)],
        axis=0)
    ei = lax.broadcasted_iota(I32, (E, t), 0).astype(F32)
    idx_rows, w_rows, hits = [], [], []
    chosen = jnp.zeros((E, t), F32)
    for _ in range(TOP_K):
        m = jnp.max(cand, axis=0, keepdims=True)
        fi = _first_index(cand == m, ei, E)
        hit = ei == fi
        idx_rows.append(fi)
        w_rows.append(jnp.sum(jnp.where(hit, scores, 0.0), axis=0, keepdims=True))
        hits.append(hit)
        chosen = jnp.where(hit, 1.0, chosen)
        cand = jnp.where(hit, REMOVED, cand)
    w = jnp.concatenate(w_rows, axis=0)
    w = w / jnp.sum(w, axis=0, keepdims=True) * ROUTED_SCALE
    idx_ref[...] = jnp.concatenate(idx_rows, axis=0).astype(I32)
    w_ref[...] = w
    wpad = jnp.concatenate([w, jnp.zeros((LANES - TOP_K, t), F32)], axis=0)
    wt_ref[...] = wpad.T
    prefix = _dot(chosen.astype(BF16), upper_ref[...])
    pos = base_ref[:, 0:1] + prefix
    rank_ref[...] = jnp.concatenate(
        [jnp.sum(jnp.where(hit, pos, 0.0), axis=0, keepdims=True) for hit in hits], axis=0).astype(I32)
    base_ref[...] = base_ref[...] + jnp.sum(chosen, axis=1, keepdims=True)
    cnt_ref[...] = base_ref[...]


def _route(logits_t, bias_col, upper, *, tr):
    E, N = logits_t.shape
    col = lambda i: (0, i)
    return pl.pallas_call(
        _route_kernel,
        grid=(N // tr,),
        in_specs=[pl.BlockSpec((E, tr), col), _full(bias_col.shape), _full(upper.shape)],
        out_specs=[pl.BlockSpec((TOP_K, tr), col), pl.BlockSpec((TOP_K, tr), col),
                   pl.BlockSpec((tr, LANES), lambda i: (i, 0)),
                   pl.BlockSpec((TOP_K, tr), col), _full((E, LANES))],
        out_shape=[jax.ShapeDtypeStruct((TOP_K, N), I32), jax.ShapeDtypeStruct((TOP_K, N), F32),
                   jax.ShapeDtypeStruct((N, LANES), F32),
                   jax.ShapeDtypeStruct((TOP_K, N), I32), jax.ShapeDtypeStruct((E, LANES), F32)],
        scratch_shapes=[pltpu.VMEM((E, LANES), F32)],
        compiler_params=_params(("arbitrary",)),
        name="route",
    )(logits_t, bias_col, upper)


def _plan_kernel(cnt_ref, lower_ref, idx_ref, rank_ref, dest_ref, gstart_ref, nblk_ref, pstart_ref):
    E = cnt_ref.shape[0]
    bm = float(EXPERT_ROWS)

    @pl.when(pl.program_id(0) == 0)
    def _():
        cnt = cnt_ref[...]
        nblk = jnp.floor((cnt + (bm - 1.0)) * (1.0 / bm))
        padded = nblk * bm
        hi = jnp.floor(padded * (1.0 / 256.0))
        lo = padded - hi * 256.0
        low = lower_ref[...]
        pend = 256.0 * _dot(low, hi.astype(BF16)) + _dot(low, lo.astype(BF16))
        pstart_ref[...] = pend - padded
        gstart_ref[...] = ((pend - padded) * (1.0 / bm)).astype(I32)
        nblk_ref[...] = nblk.astype(I32)

    t = idx_ref.shape[1]
    ei = lax.broadcasted_iota(I32, (E, t), 0)
    ps = pstart_ref[:, 0:1]
    rows = []
    for k in range(TOP_K):
        hit = ei == idx_ref[k:k + 1, :]
        rows.append(jnp.sum(jnp.where(hit, ps, 0.0), axis=0, keepdims=True))
    dest_ref[...] = rank_ref[...] + jnp.concatenate(rows, axis=0).astype(I32)


def _plan(cnt, lower, idx, rank, *, tr):
    E = cnt.shape[0]
    N = idx.shape[1]
    col = lambda i: (0, i)
    return pl.pallas_call(
        _plan_kernel,
        grid=(N // tr,),
        in_specs=[_full(cnt.shape), _full(lower.shape), pl.BlockSpec((TOP_K, tr), col),
                  pl.BlockSpec((TOP_K, tr), col)],
        out_specs=[pl.BlockSpec((TOP_K, tr), col), _full((E, LANES)), _full((E, LANES))],
        out_shape=[jax.ShapeDtypeStruct((TOP_K, N), I32), jax.ShapeDtypeStruct((E, LANES), I32),
                   jax.ShapeDtypeStruct((E, LANES), I32)],
        scratch_shapes=[pltpu.VMEM((E, LANES), F32)],
        compiler_params=_params(("arbitrary",)),
        name="plan",
    )(cnt, lower, idx, rank)


def _slab_rows(ref, row, slab):
    return ref.at[pl.ds(pl.multiple_of(row * slab, slab), slab)]


def _fill_zero_blocks(zero_ref, out_ref, zsem, gstart_ref, nblk_ref, *, last_of_expert):
    blk = zero_ref.shape[0]
    n_exp = gstart_ref.shape[0]
    n_blk = out_ref.shape[0] // blk
    nused = gstart_ref[n_exp - 1] + nblk_ref[n_exp - 1]
    zero_ref[...] = jnp.zeros_like(zero_ref)

    def blk_copy(b):
        return pltpu.make_async_copy(zero_ref, out_ref.at[pl.ds(pl.multiple_of(b * blk, blk), blk)], zsem)

    def tail(b, carry):
        blk_copy(b).start()
        return carry

    lax.fori_loop(nused, n_blk, tail, 0)
    n_started = n_blk - nused
    if last_of_expert:
        def last(e, n):
            has = nblk_ref[e] > 0

            @pl.when(has)
            def _():
                blk_copy(gstart_ref[e] + nblk_ref[e] - 1).start()

            return n + has.astype(I32)

        n_started = lax.fori_loop(0, n_exp, last, n_started)

    def wait(_, carry):
        blk_copy(0).wait()
        return carry

    lax.fori_loop(0, n_started, wait, 0)


def _scatter_kernel(gstart_ref, nblk_ref, *refs):
    dest_refs = refs[:TOP_K]
    h_ref, xs_ref, zero_ref, sem, zsem = refs[TOP_K:]
    ts = dest_refs[0].shape[0]
    slab = h_ref.shape[0] // ts

    @pl.when(pl.program_id(0) == 0)
    def _():
        _fill_zero_blocks(zero_ref, xs_ref, zsem, gstart_ref, nblk_ref, last_of_expert=True)

    def body(t, carry):
        src = _slab_rows(h_ref, t, slab)
        for k in range(TOP_K):
            pltpu.make_async_copy(src, _slab_rows(xs_ref, dest_refs[k][t], slab), sem).start(priority=k % 2)
        return carry

    lax.fori_loop(0, ts, body, 0)
    for _ in range(TOP_K):
        pltpu.make_async_copy(h_ref, xs_ref.at[pl.ds(0, slab * ts)], sem).wait()


def _scatter(gstart, nblk, dests, h2s, *, ts, slab, n_blk):
    rows, W = h2s.shape
    grid_spec = pltpu.PrefetchScalarGridSpec(
        num_scalar_prefetch=2,
        grid=(rows // (slab * ts),),
        in_specs=[pl.BlockSpec((ts,), lambda i, be, nu: (i,), memory_space=pltpu.SMEM)] * TOP_K
        + [pl.BlockSpec((slab * ts, W), lambda i, be, nu: (i, 0))],
        out_specs=pl.BlockSpec(memory_space=pl.ANY),
        scratch_shapes=[pltpu.VMEM((slab * EXPERT_ROWS, W), h2s.dtype), pltpu.SemaphoreType.DMA(()),
                        pltpu.SemaphoreType.DMA(())],
    )
    return pl.pallas_call(
        _scatter_kernel,
        grid_spec=grid_spec,
        out_shape=jax.ShapeDtypeStruct((n_blk * slab * EXPERT_ROWS, W), h2s.dtype),
        compiler_params=_params(("arbitrary",)),
        name="scatter",
    )(gstart, nblk, *dests, h2s)


def _experts_kernel(gstart_ref, nblk_ref, wg_ref, wu_ref, wd_ref, xs_ref, ys_ref, xbuf_ref, ybuf_ref,
                    wgb_ref, wub_ref, wdb_ref, xstage_ref, ystage_ref, zero_ref, front_ref, xsem, ysem, zsem):
    e = pl.program_id(0)
    n_exp = pl.num_programs(0)
    ring = xbuf_ref.shape[0]
    blk = zero_ref.shape[0]
    slab = wgb_ref.shape[0] // LANES
    rows = blk // slab
    nused = gstart_ref[n_exp - 1] + nblk_ref[n_exp - 1]
    g0 = gstart_ref[e]
    n = nblk_ref[e]

    def x_copy(b):
        slot = lax.rem(b, ring)
        return pltpu.make_async_copy(xs_ref.at[pl.ds(pl.multiple_of(b * blk, blk), blk)], xbuf_ref.at[slot],
                                     xsem.at[slot])

    def y_copy(b):
        slot = lax.rem(b, ring)
        return pltpu.make_async_copy(ybuf_ref.at[slot], ys_ref.at[pl.ds(pl.multiple_of(b * blk, blk), blk)],
                                     ysem.at[slot])

    @pl.when(e == 0)
    def _():
        front_ref[0] = 0

    @pl.when(n > 0)
    def _():
        wgb_ref[...] = wg_ref[0].astype(BF16)
        wub_ref[...] = wu_ref[0].astype(BF16)
        wdb_ref[...] = wd_ref[0].astype(BF16)

    def group(g, nb):
        limit = jnp.minimum(nused, g + ring)

        def fetch(b, carry):
            x_copy(b).start()
            return carry

        lax.fori_loop(front_ref[0], limit, fetch, 0)
        front_ref[0] = jnp.maximum(front_ref[0], limit)

        for i in range(nb):
            x_copy(g + i).wait()
        for i in range(nb):
            @pl.when(g + i >= ring)
            def _():
                y_copy(g + i - ring).wait()

        for lo, hi in ([(0, nb)] if nb < 4 else [(0, nb // 2), (nb // 2, nb)]):
            parts = [_load_slabs(xstage_ref.at[i], xbuf_ref[lax.rem(g + i, ring)], slab).astype(BF16)
                     for i in range(lo, hi)]
            x = parts[0] if len(parts) == 1 else jnp.concatenate(parts, axis=0)
            gate = _dot(x, wgb_ref[...])
            up = _dot(x, wub_ref[...])
            y = _dot((_silu(gate) * up).astype(BF16), wdb_ref[...])
            for i in range(lo, hi):
                _store_slabs(ybuf_ref.at[lax.rem(g + i, ring)], ystage_ref.at[i], y[(i - lo) * rows:(i - lo + 1) * rows])
        for i in range(nb):
            y_copy(g + i).start()

    n4 = lax.shift_right_logical(n, 2)
    left = n - 4 * n4

    def quad(j, carry):
        group(g0 + 4 * j, 4)
        return carry

    lax.fori_loop(0, n4, quad, 0)

    @pl.when(left >= 2)
    def _():
        group(g0 + 4 * n4, 2)

    @pl.when(lax.rem(left, 2) == 1)
    def _():
        group(g0 + n - 1, 1)

    @pl.when(e == n_exp - 1)
    def _():
        def drain(b, carry):
            y_copy(b).wait()
            return carry

        lax.fori_loop(jnp.maximum(nused - ring, 0), nused, drain, 0)
        _fill_zero_blocks(zero_ref, ys_ref, zsem, gstart_ref, nblk_ref, last_of_expert=False)


def _experts(gstart, nblk, xs, wg, wu, wd):
    rows, W = xs.shape
    E, D, H = wg.shape
    blk = (D // LANES) * EXPERT_ROWS
    grid_spec = pltpu.PrefetchScalarGridSpec(
        num_scalar_prefetch=2,
        grid=(E,),
        in_specs=[pl.BlockSpec((1, D, H), lambda e, gs, nb: (e, 0, 0)),
                  pl.BlockSpec((1, D, H), lambda e, gs, nb: (e, 0, 0)),
                  pl.BlockSpec((1, H, D), lambda e, gs, nb: (e, 0, 0)),
                  pl.BlockSpec(memory_space=pl.ANY)],
        out_specs=pl.BlockSpec(memory_space=pl.ANY),
        scratch_shapes=[pltpu.VMEM((EXPERT_RING, blk, W), xs.dtype), pltpu.VMEM((EXPERT_RING, blk, W), xs.dtype),
                        pltpu.VMEM((D, H), BF16), pltpu.VMEM((D, H), BF16), pltpu.VMEM((H, D), BF16),
                        pltpu.VMEM((EXPERT_RING // 2, blk, LANES), F32), pltpu.VMEM((EXPERT_RING // 2, blk, LANES), F32),
                        pltpu.VMEM((blk, W), xs.dtype), pltpu.SMEM((1,), I32),
                        pltpu.SemaphoreType.DMA((EXPERT_RING,)), pltpu.SemaphoreType.DMA((EXPERT_RING,)),
                        pltpu.SemaphoreType.DMA(())],
    )
    return pl.pallas_call(
        _experts_kernel,
        grid_spec=grid_spec,
        out_shape=jax.ShapeDtypeStruct((rows, W), xs.dtype),
        compiler_params=_params(("arbitrary",)),
        name="experts",
    )(gstart, nblk, wg, wu, wd, xs)


def _combine_kernel(*refs, alpha):
    cur_refs, nxt_refs = refs[:TOP_K], refs[TOP_K:2 * TOP_K]
    (wt_ref, h2_ref, x1_ref, mod_ref, wsg_ref, wsu_ref, wsd_ref, g_ref, b_ref, ys_ref, o_ref) = refs[2 * TOP_K:-11]
    buf_refs = refs[-11:-3]
    stage0_ref, stage1_ref, sem = refs[-3:]
    tc, d = h2_ref.shape
    slab = d // LANES
    i = pl.program_id(0)
    last = pl.num_programs(0) - 1

    def row_copy(dest_refs, k, t):
        return pltpu.make_async_copy(_slab_rows(ys_ref, dest_refs[k][t], slab), _slab_rows(buf_refs[k], t, slab), sem)

    def wait_planes():
        for k in range(TOP_K):
            pltpu.make_async_copy(ys_ref.at[pl.ds(0, slab * tc)], buf_refs[k], sem).wait()

    @pl.when(i == 0)
    def _():
        def body(t, carry):
            for k in range(TOP_K):
                row_copy(cur_refs, k, t).start(priority=k % 2)
            return carry

        lax.fori_loop(0, tc, body, 0)

    h = h2_ref[...]
    mid = (_silu(_dot(h, wsg_ref[...])) * _dot(h, wsu_ref[...])).astype(BF16)
    y = _dot(mid, wsd_ref[...])
    wait_planes()
    for k in range(TOP_K):
        stage_ref = stage0_ref if k % 2 == 0 else stage1_ref
        y = y + wt_ref[:, k:k + 1] * _load_slabs(stage_ref, buf_refs[k][...], slab)
        for t in range(tc):
            row_copy(nxt_refs, k, t).start(priority=t % 2)
    z = alpha * x1_ref[...] + (1.0 + mod_ref[0, 5:6, :]) * y
    o_ref[...] = _ln_plain(z) * g_ref[...] + b_ref[...]

    @pl.when(i == last)
    def _():
        wait_planes()


def _combine(dests, wt, h2, x1, mod3, wsg, wsu, wsd, ln_g, ln_b, ys, *, seq, tc, alpha):
    N, D = x1.shape
    slab = D // LANES
    per_b = seq // tc
    n_tiles = N // tc
    row = lambda i: (i, 0)
    return pl.pallas_call(
        functools.partial(_combine_kernel, alpha=alpha),
        grid=(n_tiles,),
        in_specs=[pl.BlockSpec((tc,), lambda i: (i,), memory_space=pltpu.SMEM)] * TOP_K
        + [pl.BlockSpec((tc,), lambda i: (jnp.minimum(i + 1, n_tiles - 1),), memory_space=pltpu.SMEM)] * TOP_K
        + [pl.BlockSpec((tc, LANES), row), pl.BlockSpec((tc, D), row), pl.BlockSpec((tc, D), row),
           pl.BlockSpec((1, 6, D), lambda i: (i // per_b, 0, 0)),
           _full(wsg.shape), _full(wsu.shape), _full(wsd.shape), _full(ln_g.shape), _full(ln_b.shape),
           pl.BlockSpec(memory_space=pl.ANY)],
        out_specs=pl.BlockSpec((tc, D), row),
        out_shape=jax.ShapeDtypeStruct((N, D), F32),
        scratch_shapes=[pltpu.VMEM((slab * tc, LANES), ys.dtype)] * TOP_K
        + [_slab_scratch(tc, D), _slab_scratch(tc, D), pltpu.SemaphoreType.DMA(())],
        compiler_params=_params(("arbitrary",)),
        name="combine",
    )(*dests, *dests, wt, h2, x1, mod3, wsg, wsu, wsd, ln_g, ln_b, ys)


def _tiles(seq):
    t = lambda want: min(want, seq)
    return dict(proj=t(512), swa=t(512), gla=t(512), route=t(256), scatter=t(512), combine=t(256))


def _layer(x2, mod3, posb, invf, p, *, seq, alpha):
    N, D = x2.shape
    tl = _tiles(seq)
    qa, ka, va, ql, kl, vl, gl, lg = _inproj(x2, mod3, posb, invf, p["w_main"], p["b_main"], p["w_lo"], p["b_lo"],
                                             p["w_gk"], p["b_gk"], seq=seq, tm=tl["proj"])
    oa = _swa(qa, _repeat_kv_heads(ka), _repeat_kv_heads(va), p["sinks"], seq=seq, tq=tl["swa"])
    og = _gla(ql, kl, vl, gl, lg, p["norm_g"], seq=seq, tc=tl["gla"])
    x1, h2, h2s, logits_t = _outproj(oa, og, x2, mod3, p["w_oa"], p["w_og"], p["b_o"], p["ln1_g"], p["ln1_b"],
                                     p["wr_hi"], p["wr_lo"], seq=seq, tm=tl["proj"], alpha=alpha)
    idx, _, wt, rank, cnt = _route(logits_t, p["bias_col"], p["upper"], tr=tl["route"])
    E = cnt.shape[0]
    n_blk = (N * TOP_K) // EXPERT_ROWS + E
    dest, gstart, nblk = _plan(cnt, p["lower"], idx, rank, tr=tl["route"])
    gstart, nblk = gstart[:, 0], nblk[:, 0]
    dests = [dest[k] for k in range(TOP_K)]
    xs = _scatter(gstart, nblk, dests, h2s, ts=tl["scatter"], slab=D // LANES, n_blk=n_blk)
    ys = _experts(gstart, nblk, xs, p["wg"], p["wu"], p["wd"])
    return _combine(dests, wt, h2, x1, mod3, p["wsg"], p["wsu"], p["wsd"], p["ln2_g"], p["ln2_b"], ys,
                    seq=seq, tc=tl["combine"], alpha=alpha)


def kernel(x, c, positions, w_ada, b_ada, w_in, b_in, attn_sinks, w_gk2, b_gk2, gla_norm_g, w_o, b_o, ln1_g, ln1_b, w_router, router_bias, w_exp_gate, w_exp_up, w_exp_down, w_sh_gate, w_sh_up, w_sh_down, ln2_g, ln2_b):
    B, S, D = x.shape
    depth = w_ada.shape[0]
    E = w_router.shape[2]
    alpha = float((2 * depth) ** 0.25)
    tl = _tiles(S)
    n_main = int(_SEG[-1])

    posb = jnp.broadcast_to(positions.astype(F32).reshape(B * S, 1), (B * S, LANES))
    half = ATTN_HEAD_DIM // 2
    invf = (ROPE_THETA ** (-(jnp.arange(LANES) % half).astype(F32) / half)).reshape(1, LANES)
    tr = tl["route"]
    upper = (jnp.arange(tr)[:, None] < jnp.arange(tr)[None, :]).astype(BF16)
    lower = (jnp.arange(E)[:, None] >= jnp.arange(E)[None, :]).astype(BF16)
    row = lambda v: v.reshape(1, -1)

    x2 = x.reshape(B * S, D)
    for l in range(depth):
        wr_t = w_router[l].T
        wr_hi = wr_t.astype(BF16)
        p = dict(
            w_main=w_in[l][:, :n_main].astype(BF16), b_main=row(b_in[l][:n_main]),
            w_lo=jnp.pad(w_in[l][:, n_main:], ((0, 0), (0, LANES - GLA_GATE_RANK))).astype(BF16),
            b_lo=row(jnp.pad(b_in[l][n_main:], (0, LANES - GLA_GATE_RANK))),
            w_gk=jnp.pad(w_gk2[l], ((0, LANES - GLA_GATE_RANK), (0, 0))), b_gk=row(b_gk2[l]),
            sinks=attn_sinks[l], norm_g=row(gla_norm_g[l]),
            w_oa=w_o[l][:ATTN_WIDTH].astype(BF16), w_og=w_o[l][ATTN_WIDTH:].astype(BF16), b_o=row(b_o[l]),
            ln1_g=row(ln1_g[l]), ln1_b=row(ln1_b[l]),
            wr_hi=wr_hi, wr_lo=(wr_t - wr_hi.astype(F32)).astype(BF16),
            bias_col=jnp.broadcast_to(router_bias[l].reshape(E, 1), (E, LANES)),
            upper=upper, lower=lower,
            wg=w_exp_gate[l], wu=w_exp_up[l], wd=w_exp_down[l],
            wsg=w_sh_gate[l].astype(BF16), wsu=w_sh_up[l].astype(BF16), wsd=w_sh_down[l].astype(BF16),
            ln2_g=row(ln2_g[l]), ln2_b=row(ln2_b[l]),
        )
        mod = _mod(c, w_ada[l], b_ada[l])
        mod3 = mod.reshape(B, 6, D)
        x2 = _layer(x2, mod3, posb, invf, p, seq=S, alpha=alpha)
    return x2.reshape(B, S, D)
```

```python
import functools

import jax
import jax.numpy as jnp
import numpy as np
from jax import lax
from jax.experimental import pallas as pl
from jax.experimental.pallas import tpu as pltpu

F32 = jnp.float32
BF16 = jnp.bfloat16
I32 = jnp.int32

ATTN_Q_HEADS = 8
ATTN_KV_HEADS = 2
ATTN_HEAD_DIM = 64
ATTN_BLOCK = 128
ROPE_THETA = 10000.0
GLA_HEADS = 4
GLA_KEY_DIM = 64
GLA_VAL_DIM = 128
GLA_GATE_RANK = 16
GLA_GATE_NORM = 16.0
GLA_CHUNK = 64
N_GROUPS = 8
TOPK_GROUPS = 4
TOP_K = 8
ROUTED_SCALE = 2.5
LN_EPS = 1e-5
NEG_INF = -1e30
REMOVED = -3e38

ATTN_WIDTH = ATTN_Q_HEADS * ATTN_HEAD_DIM
KV_WIDTH = ATTN_KV_HEADS * ATTN_HEAD_DIM
GLA_KWIDTH = GLA_HEADS * GLA_KEY_DIM
GLA_WIDTH = GLA_HEADS * GLA_VAL_DIM

LANES = 128
VMEM_LIMIT = 56 * 1024 * 1024
EXPERT_ROWS = 256
EXPERT_RING = 8


def _params(sem):
    return pltpu.CompilerParams(dimension_semantics=sem, vmem_limit_bytes=VMEM_LIMIT)


def _full(shape):
    return pl.BlockSpec(shape, lambda *_: (0,) * len(shape))


def _split_bf16(a):
    hi = a.astype(BF16)
    lo = (a - hi.astype(F32)).astype(BF16)
    return hi, lo


def _dot(a, b, dims=(((1,), (0,)), ((), ()))):
    return lax.dot_general(a, b, dims, preferred_element_type=F32)


NT = (((1,), (1,)), ((), ()))
TN = (((0,), (0,)), ((), ()))


def _dot3(a, b, dims=(((1,), (0,)), ((), ()))):
    ah, al = _split_bf16(a)
    bh, bl = _split_bf16(b)
    return _dot(ah, bh, dims) + (_dot(ah, bl, dims) + _dot(al, bh, dims))


def _ln_plain(x):
    mu = jnp.mean(x, axis=-1, keepdims=True)
    xc = x - mu
    var = jnp.mean(xc * xc, axis=-1, keepdims=True)
    return xc * lax.rsqrt(var + LN_EPS)


def _silu(x):
    return x * (1.0 / (1.0 + jnp.exp(-x)))


def _sigmoid(x):
    return 1.0 / (1.0 + jnp.exp(-x))


def _mod_kernel(c_ref, w_ref, b_ref, o_ref):
    c = c_ref[...]
    o_ref[...] = _dot3(_silu(c), w_ref[...]) + b_ref[...]


def _mod(c, w_ada, b_ada):
    B, D = c.shape
    n = w_ada.shape[1] // D
    return pl.pallas_call(
        _mod_kernel,
        grid=(n,),
        in_specs=[_full((B, D)),
                  pl.BlockSpec((D, D), lambda j: (0, j)),
                  pl.BlockSpec((1, D), lambda j: (0, j))],
        out_specs=pl.BlockSpec((B, D), lambda j: (0, j)),
        out_shape=jax.ShapeDtypeStruct((B, n * D), F32),
        compiler_params=_params(("arbitrary",)),
        name="mod",
    )(c, w_ada, b_ada.reshape(1, -1))


_SEG = np.cumsum([0, ATTN_WIDTH, KV_WIDTH, KV_WIDTH, GLA_KWIDTH, GLA_KWIDTH, GLA_WIDTH, GLA_WIDTH])


def _rope_chunk(t, cos, sin_signed, first_half):
    up = pltpu.roll(t, LANES - 32, axis=1)
    dn = pltpu.roll(t, 32, axis=1)
    return t * cos + jnp.where(first_half, up, dn) * sin_signed


def _inproj_kernel(x_ref, mod_ref, pos_ref, invf_ref, w_ref, b_ref, wlo_ref, blo_ref, wgk_ref, bgk_ref,
                   qa_ref, ka_ref, va_ref, ql_ref, kl_ref, vl_ref, gl_ref, lg_ref):
    x = x_ref[...]
    h = _ln_plain(x) * (1.0 + mod_ref[0, 1:2, :]) + mod_ref[0, 0:1, :]
    hb = h.astype(BF16)

    def seg(i):
        lo, hi = int(_SEG[i]), int(_SEG[i + 1])
        return _dot(hb, w_ref[:, lo:hi]) + b_ref[:, lo:hi]

    ang = pos_ref[...] * invf_ref[...]
    cos = jnp.cos(ang)
    sin = jnp.sin(ang)
    lane = lax.broadcasted_iota(I32, ang.shape, 1)
    first_half = (lane % ATTN_HEAD_DIM) < (ATTN_HEAD_DIM // 2)
    sin_signed = jnp.where(first_half, -sin, sin)

    q = seg(0)
    scale = ATTN_HEAD_DIM ** -0.5
    for c in range(ATTN_WIDTH // LANES):
        t = q[:, c * LANES:(c + 1) * LANES]
        qa_ref[:, c * LANES:(c + 1) * LANES] = (_rope_chunk(t, cos, sin_signed, first_half) * scale).astype(BF16)
    ka_ref[...] = _rope_chunk(seg(1), cos, sin_signed, first_half).astype(BF16)
    va_ref[...] = seg(2).astype(BF16)
    ql_ref[...] = seg(3).astype(BF16)
    kl_ref[...] = seg(4).astype(BF16)
    vl_ref[...] = seg(5).astype(BF16)
    gl_ref[...] = seg(6).astype(BF16)
    gk_lo = _dot(hb, wlo_ref[...]) + blo_ref[...]
    gk = _dot3(gk_lo, wgk_ref[...]) + bgk_ref[...]
    log_sig = jnp.minimum(gk, 0.0) - jnp.log(1.0 + jnp.exp(-jnp.abs(gk)))
    lg_ref[...] = log_sig * (1.0 / GLA_GATE_NORM)


def _inproj(x2, mod3, posb, invf, w_main, b_main, w_lo, b_lo, w_gk, b_gk, *, seq, tm):
    N, D = x2.shape
    per_b = seq // tm
    widths = [ATTN_WIDTH, KV_WIDTH, KV_WIDTH, GLA_KWIDTH, GLA_KWIDTH, GLA_WIDTH, GLA_WIDTH, GLA_KWIDTH]
    dtypes = [BF16] * 7 + [F32]
    row = lambda i: (i, 0)
    return pl.pallas_call(
        _inproj_kernel,
        grid=(N // tm,),
        in_specs=[pl.BlockSpec((tm, D), row),
                  pl.BlockSpec((1, 6, D), lambda i: (i // per_b, 0, 0)),
                  pl.BlockSpec((tm, LANES), row),
                  _full(invf.shape), _full(w_main.shape), _full(b_main.shape),
                  _full(w_lo.shape), _full(b_lo.shape), _full(w_gk.shape), _full(b_gk.shape)],
        out_specs=[pl.BlockSpec((tm, w), row) for w in widths],
        out_shape=[jax.ShapeDtypeStruct((N, w), dt) for w, dt in zip(widths, dtypes)],
        compiler_params=_params(("parallel",)),
        name="inproj",
    )(x2, mod3, posb, invf, w_main, b_main, w_lo, b_lo, w_gk, b_gk)


def _swa_kernel(sink_ref, rep_ref, q_ref, kc_ref, kp_ref, vc_ref, vp_ref, o_ref):
    j = pl.program_id(1)
    blk = ATTN_BLOCK
    row = lax.broadcasted_iota(I32, (blk, 2 * blk), 0)
    col = lax.broadcasted_iota(I32, (blk, 2 * blk), 1)
    dist = row + blk - col
    band = (dist >= 0) & (dist < blk)
    group = ATTN_Q_HEADS // ATTN_KV_HEADS
    hd = ATTN_HEAD_DIM
    gw = group * hd
    kw = 2 * blk
    bd = (lax.broadcasted_iota(I32, (group * kw, gw), 0) // kw) == (lax.broadcasted_iota(I32, (group * kw, gw), 1) // hd)
    lane_head = lax.broadcasted_iota(I32, (blk, gw), 1) // hd
    rep = rep_ref[...]
    k_all = _dot(jnp.concatenate([kp_ref[...], kc_ref[...]], axis=0), rep).astype(BF16)
    v_all = _dot(jnp.concatenate([vp_ref[...], vc_ref[...]], axis=0), rep).astype(BF16)
    for s_blk in range(q_ref.shape[0] // blk):
        rows = slice(s_blk * blk, (s_blk + 1) * blk)
        k2 = k_all[s_blk * blk:(s_blk + 2) * blk]
        v2 = v_all[s_blk * blk:(s_blk + 2) * blk]
        valid = band & ((col >= blk) | (j > 0)) if s_blk == 0 else band
        for g in range(ATTN_KV_HEADS):
            lanes = slice(g * gw, (g + 1) * gw)
            zero = jnp.zeros((), BF16)
            kd = jnp.where(bd, jnp.concatenate([k2[:, lanes]] * group, axis=0), zero)
            vd = jnp.where(bd, jnp.concatenate([v2[:, lanes]] * group, axis=0), zero)
            s_all = _dot(q_ref[rows, lanes], kd, NT)
            probs, denom = [], jnp.zeros((blk, gw), F32)
            for r in range(group):
                sink = sink_ref[g * group + r]
                s = jnp.where(valid, s_all[:, r * kw:(r + 1) * kw], NEG_INF)
                m = jnp.maximum(jnp.max(s, axis=-1, keepdims=True), sink)
                p = jnp.exp(s - m)
                d = jnp.sum(p, axis=-1, keepdims=True) + jnp.exp(sink - m)
                probs.append(p.astype(BF16))
                denom = jnp.where(lane_head == r, d, denom)
            o = _dot(jnp.concatenate(probs, axis=1), vd) / denom
            o_ref[rows, lanes] = o.astype(BF16)


def _swa(q, k, v, sinks, *, seq, tq):
    N = q.shape[0]
    src = jnp.arange(ATTN_WIDTH)
    src = (src // (ATTN_WIDTH // ATTN_KV_HEADS)) * ATTN_HEAD_DIM + src % ATTN_HEAD_DIM
    rep = (jnp.arange(KV_WIDTH)[:, None] == src[None, :]).astype(BF16)
    per_b = seq // tq
    sub = tq // ATTN_BLOCK
    B = N // seq
    cur = lambda b, j: (b * per_b + j, 0)
    prev = lambda b, j: ((b * per_b + j) * sub - jnp.minimum(j, 1), 0)
    return pl.pallas_call(
        _swa_kernel,
        grid=(B, per_b),
        in_specs=[pl.BlockSpec(memory_space=pltpu.SMEM), _full(rep.shape),
                  pl.BlockSpec((tq, ATTN_WIDTH), cur),
                  pl.BlockSpec((tq, KV_WIDTH), cur),
                  pl.BlockSpec((ATTN_BLOCK, KV_WIDTH), prev),
                  pl.BlockSpec((tq, KV_WIDTH), cur),
                  pl.BlockSpec((ATTN_BLOCK, KV_WIDTH), prev)],
        out_specs=pl.BlockSpec((tq, ATTN_WIDTH), cur),
        out_shape=jax.ShapeDtypeStruct((N, ATTN_WIDTH), BF16),
        compiler_params=_params(("parallel", "parallel")),
        name="swa",
    )(sinks, rep, q, k, k, v, v)


def _gla_kernel(q_ref, k_ref, v_ref, g_ref, lg_ref, ng_ref, o_ref, st_ref, *, chunks):
    @pl.when(pl.program_id(1) == 0)
    def _():
        st_ref[...] = jnp.zeros_like(st_ref)

    C = GLA_CHUNK
    dk, dv = GLA_KEY_DIM, GLA_VAL_DIM
    r = lax.broadcasted_iota(I32, (C, C), 0)
    c = lax.broadcasted_iota(I32, (C, C), 1)
    causal = c <= r
    tri = jnp.where(causal, 1.0, 0.0).astype(BF16)
    st = st_ref[...]
    for n in range(chunks):
        rows = slice(n * C, (n + 1) * C)
        lg_hi, lg_lo = _split_bf16(lg_ref[rows, :])
        b = _dot(tri, lg_hi) + _dot(tri, lg_lo)
        b_last = b[C - 1:C, :]
        q_in = (q_ref[rows, :].astype(F32) * (dk ** -0.5) * jnp.exp(b)).astype(BF16)
        kf = k_ref[rows, :].astype(F32)
        k_in = (kf * jnp.exp(-b)).astype(BF16)
        k_out = (kf * jnp.exp(b_last - b)).astype(BF16)
        decay = jnp.exp(b_last)
        stb = st.astype(BF16)
        ut = []
        for h in range(GLA_HEADS):
            ks = slice(h * dk, (h + 1) * dk)
            vs = slice(h * dv, (h + 1) * dv)
            vh = v_ref[rows, vs]
            a = jnp.where(causal, _dot(q_in[:, ks], k_in[:, ks], NT), 0.0).astype(BF16)
            o = _dot(a, vh) + _dot(q_in[:, ks], stb[:, ks], NT)
            ut.append(_dot(vh, k_out[:, ks], TN))
            o = o * lax.rsqrt(jnp.mean(o * o, axis=-1, keepdims=True) + LN_EPS) * ng_ref[...]
            o_ref[rows, vs] = (o * _silu(g_ref[rows, vs].astype(F32))).astype(BF16)
        st = st * decay + jnp.concatenate(ut, axis=1)
    st_ref[...] = st


def _gla(q, k, v, g, lg, norm_g, *, seq, tc):
    N = q.shape[0]
    B = N // seq
    per_b = seq // tc
    row = lambda b, j: (b * per_b + j, 0)
    return pl.pallas_call(
        functools.partial(_gla_kernel, chunks=tc // GLA_CHUNK),
        grid=(B, per_b),
        in_specs=[pl.BlockSpec((tc, GLA_KWIDTH), row), pl.BlockSpec((tc, GLA_KWIDTH), row),
                  pl.BlockSpec((tc, GLA_WIDTH), row), pl.BlockSpec((tc, GLA_WIDTH), row),
                  pl.BlockSpec((tc, GLA_KWIDTH), row), _full(norm_g.shape)],
        out_specs=pl.BlockSpec((tc, GLA_WIDTH), row),
        out_shape=jax.ShapeDtypeStruct((N, GLA_WIDTH), BF16),
        scratch_shapes=[pltpu.VMEM((GLA_VAL_DIM, GLA_KWIDTH), F32)],
        compiler_params=_params(("parallel", "arbitrary")),
        name="gla",
    )(q, k, v, g, lg, norm_g)


def _slab_scratch(rows, d):
    return pltpu.VMEM((rows * (d // LANES), LANES), F32)


def _store_slabs(slabs_ref, stage_ref, v):
    rows, d = v.shape
    n = d // LANES
    for c in range(n):
        stage_ref[pl.ds(c, rows, stride=n), :] = v[:, c * LANES:(c + 1) * LANES]
    slabs_ref[...] = stage_ref[...].astype(slabs_ref.dtype)


def _load_slabs(stage_ref, slabs, n):
    stage_ref[...] = slabs.astype(F32)
    rows = stage_ref.shape[0] // n
    return jnp.concatenate([stage_ref[pl.ds(c, rows, stride=n), :] for c in range(n)], axis=1)


def _outproj_kernel(oa_ref, og_ref, x_ref, mod_ref, woa_ref, wog_ref, bo_ref, g_ref, b_ref, wrh_ref, wrl_ref,
                    x1_ref, h2_ref, h2s_ref, lt_ref, stage_ref, *, alpha):
    y = _dot(oa_ref[...], woa_ref[...]) + _dot(og_ref[...], wog_ref[...]) + bo_ref[...]
    z = alpha * x_ref[...] + (1.0 + mod_ref[0, 2:3, :]) * y
    x1 = _ln_plain(z) * g_ref[...] + b_ref[...]
    x1_ref[...] = x1
    h2 = _ln_plain(x1) * (1.0 + mod_ref[0, 4:5, :]) + mod_ref[0, 3:4, :]
    h2_ref[...] = h2.astype(BF16)
    _store_slabs(h2s_ref, stage_ref, h2)
    hh, hl = _split_bf16(h2)
    wh = wrh_ref[...]
    lt_ref[...] = _dot(wh, hh, NT) + (_dot(wh, hl, NT) + _dot(wrl_ref[...], hh, NT))


def _outproj(oa, og, x2, mod3, w_oa, w_og, b_o, ln_g, ln_b, wr_hi, wr_lo, *, seq, tm, alpha):
    N, D = x2.shape
    E = wr_hi.shape[0]
    per_b = seq // tm
    row = lambda i: (i, 0)
    return pl.pallas_call(
        functools.partial(_outproj_kernel, alpha=alpha),
        grid=(N // tm,),
        in_specs=[pl.BlockSpec((tm, ATTN_WIDTH), row), pl.BlockSpec((tm, GLA_WIDTH), row),
                  pl.BlockSpec((tm, D), row),
                  pl.BlockSpec((1, 6, D), lambda i: (i // per_b, 0, 0)),
                  _full(w_oa.shape), _full(w_og.shape), _full(b_o.shape), _full(ln_g.shape), _full(ln_b.shape),
                  _full(wr_hi.shape), _full(wr_lo.shape)],
        out_specs=[pl.BlockSpec((tm, D), row), pl.BlockSpec((tm, D), row),
                   pl.BlockSpec((tm * (D // LANES), LANES), row), pl.BlockSpec((E, tm), lambda i: (0, i))],
        out_shape=[jax.ShapeDtypeStruct((N, D), F32), jax.ShapeDtypeStruct((N, D), BF16),
                   jax.ShapeDtypeStruct((N * (D // LANES), LANES), BF16), jax.ShapeDtypeStruct((E, N), F32)],
        scratch_shapes=[_slab_scratch(tm, D)],
        compiler_params=_params(("parallel",)),
        name="outproj",
    )(oa, og, x2, mod3, w_oa, w_og, b_o, ln_g, ln_b, wr_hi, wr_lo)


def _first_index(eq, idx, size):
    return jnp.min(jnp.where(eq, idx, float(size)), axis=0, keepdims=True)


def _route_kernel(lt_ref, bias_ref, upper_ref, idx_ref, w_ref, wt_ref, rank_ref, cnt_ref, base_ref):
    @pl.when(pl.program_id(0) == 0)
    def _():
        base_ref[...] = jnp.zeros_like(base_ref)

    E, t = lt_ref.shape
    gsz = E // N_GROUPS
    scores = _sigmoid(lt_ref[...])
    biased = scores + bias_ref[:, 0:1]
    gi = lax.broadcasted_iota(I32, (gsz, t), 0).astype(F32)
    gs_rows = []
    for g in range(N_GROUPS):
        grp = biased[g * gsz:(g + 1) * gsz, :]
        m1 = jnp.max(grp, axis=0, keepdims=True)
        first = _first_index(grp == m1, gi, gsz)
        m2 = jnp.max(jnp.where(gi == first, REMOVED, grp), axis=0, keepdims=True)
        gs_rows.append(m1 + m2)
    gs = jnp.concatenate(gs_rows, axis=0)
    ri = lax.broadcasted_iota(I32, (N_GROUPS, t), 0).astype(F32)
    gsel = jnp.zeros((N_GROUPS, t), F32)
    for _ in range(TOPK_GROUPS):
        m = jnp.max(gs, axis=0, keepdims=True)
        hit = ri == _first_index(gs == m, ri, N_GROUPS)
        gsel = jnp.where(hit, 1.0, gsel)
        gs = jnp.where(hit, REMOVED, gs)
    cand = jnp.concatenate(
        [jnp.where(gsel[g:g + 1, :] > 0.5, biased[g * gsz:(g + 1) * gsz, :], NEG_INF) for g in range(N_GROUPS)],
        axis=0)
    ei = lax.broadcasted_iota(I32, (E, t), 0).astype(F32)
    idx_rows, w_rows, hits = [], [], []
    chosen = jnp.zeros((E, t), F32)
    for _ in range(TOP_K):
        m = jnp.max(cand, axis=0, keepdims=True)
        fi = _first_index(cand == m, ei, E)
        hit = ei == fi
        idx_rows.append(fi)
        w_rows.append(jnp.sum(jnp.where(hit, scores, 0.0), axis=0, keepdims=True))
        hits.append(hit)
        chosen = jnp.where(hit, 1.0, chosen)
        cand = jnp.where(hit, REMOVED, cand)
    w = jnp.concatenate(w_rows, axis=0)
    w = w / jnp.sum(w, axis=0, keepdims=True) * ROUTED_SCALE
    idx_ref[...] = jnp.concatenate(idx_rows, axis=0).astype(I32)
    w_ref[...] = w
    wpad = jnp.concatenate([w, jnp.zeros((LANES - TOP_K, t), F32)], axis=0)
    wt_ref[...] = wpad.T
    prefix = _dot(chosen.astype(BF16), upper_ref[...])
    pos = base_ref[:, 0:1] + prefix
    rank_ref[...] = jnp.concatenate(
        [jnp.sum(jnp.where(hit, pos, 0.0), axis=0, keepdims=True) for hit in hits], axis=0).astype(I32)
    base_ref[...] = base_ref[...] + jnp.sum(chosen, axis=1, keepdims=True)
    cnt_ref[...] = base_ref[...]


def _route(logits_t, bias_col, upper, *, tr):
    E, N = logits_t.shape
    col = lambda i: (0, i)
    return pl.pallas_call(
        _route_kernel,
        grid=(N // tr,),
        in_specs=[pl.BlockSpec((E, tr), col), _full(bias_col.shape), _full(upper.shape)],
        out_specs=[pl.BlockSpec((TOP_K, tr), col), pl.BlockSpec((TOP_K, tr), col),
                   pl.BlockSpec((tr, LANES), lambda i: (i, 0)),
                   pl.BlockSpec((TOP_K, tr), col), _full((E, LANES))],
        out_shape=[jax.ShapeDtypeStruct((TOP_K, N), I32), jax.ShapeDtypeStruct((TOP_K, N), F32),
                   jax.ShapeDtypeStruct((N, LANES), F32),
                   jax.ShapeDtypeStruct((TOP_K, N), I32), jax.ShapeDtypeStruct((E, LANES), F32)],
        scratch_shapes=[pltpu.VMEM((E, LANES), F32)],
        compiler_params=_params(("arbitrary",)),
        name="route",
    )(logits_t, bias_col, upper)


def _plan_kernel(cnt_ref, lower_ref, idx_ref, rank_ref, dest_ref, gstart_ref, nblk_ref, pstart_ref):
    E = cnt_ref.shape[0]
    bm = float(EXPERT_ROWS)

    @pl.when(pl.program_id(0) == 0)
    def _():
        cnt = cnt_ref[...]
        nblk = jnp.floor((cnt + (bm - 1.0)) * (1.0 / bm))
        padded = nblk * bm
        hi = jnp.floor(padded * (1.0 / 256.0))
        lo = padded - hi * 256.0
        low = lower_ref[...]
        pend = 256.0 * _dot(low, hi.astype(BF16)) + _dot(low, lo.astype(BF16))
        pstart_ref[...] = pend - padded
        gstart_ref[...] = ((pend - padded) * (1.0 / bm)).astype(I32)
        nblk_ref[...] = nblk.astype(I32)

    t = idx_ref.shape[1]
    ei = lax.broadcasted_iota(I32, (E, t), 0)
    ps = pstart_ref[:, 0:1]
    rows = []
    for k in range(TOP_K):
        hit = ei == idx_ref[k:k + 1, :]
        rows.append(jnp.sum(jnp.where(hit, ps, 0.0), axis=0, keepdims=True))
    dest_ref[...] = rank_ref[...] + jnp.concatenate(rows, axis=0).astype(I32)


def _plan(cnt, lower, idx, rank, *, tr):
    E = cnt.shape[0]
    N = idx.shape[1]
    col = lambda i: (0, i)
    return pl.pallas_call(
        _plan_kernel,
        grid=(N // tr,),
        in_specs=[_full(cnt.shape), _full(lower.shape), pl.BlockSpec((TOP_K, tr), col),
                  pl.BlockSpec((TOP_K, tr), col)],
        out_specs=[pl.BlockSpec((TOP_K, tr), col), _full((E, LANES)), _full((E, LANES))],
        out_shape=[jax.ShapeDtypeStruct((TOP_K, N), I32), jax.ShapeDtypeStruct((E, LANES), I32),
                   jax.ShapeDtypeStruct((E, LANES), I32)],
        scratch_shapes=[pltpu.VMEM((E, LANES), F32)],
        compiler_params=_params(("arbitrary",)),
        name="plan",
    )(cnt, lower, idx, rank)


def _slab_rows(ref, row, slab):
    return ref.at[pl.ds(pl.multiple_of(row * slab, slab), slab)]


def _fill_zero_blocks(zero_ref, out_ref, zsem, gstart_ref, nblk_ref, *, last_of_expert):
    blk = zero_ref.shape[0]
    n_exp = gstart_ref.shape[0]
    n_blk = out_ref.shape[0] // blk
    nused = gstart_ref[n_exp - 1] + nblk_ref[n_exp - 1]
    zero_ref[...] = jnp.zeros_like(zero_ref)

    def blk_copy(b):
        return pltpu.make_async_copy(zero_ref, out_ref.at[pl.ds(pl.multiple_of(b * blk, blk), blk)], zsem)

    def tail(b, carry):
        blk_copy(b).start()
        return carry

    lax.fori_loop(nused, n_blk, tail, 0)
    n_started = n_blk - nused
    if last_of_expert:
        def last(e, n):
            has = nblk_ref[e] > 0

            @pl.when(has)
            def _():
                blk_copy(gstart_ref[e] + nblk_ref[e] - 1).start()

            return n + has.astype(I32)

        n_started = lax.fori_loop(0, n_exp, last, n_started)

    def wait(_, carry):
        blk_copy(0).wait()
        return carry

    lax.fori_loop(0, n_started, wait, 0)


def _scatter_kernel(gstart_ref, nblk_ref, *refs):
    dest_refs = refs[:TOP_K]
    h_ref, xs_ref, zero_ref, sem, zsem = refs[TOP_K:]
    ts = dest_refs[0].shape[0]
    slab = h_ref.shape[0] // ts

    @pl.when(pl.program_id(0) == 0)
    def _():
        _fill_zero_blocks(zero_ref, xs_ref, zsem, gstart_ref, nblk_ref, last_of_expert=True)

    def body(t, carry):
        src = _slab_rows(h_ref, t, slab)
        for k in range(TOP_K):
            pltpu.make_async_copy(src, _slab_rows(xs_ref, dest_refs[k][t], slab), sem).start(priority=k % 2)
        return carry

    lax.fori_loop(0, ts, body, 0)
    for _ in range(TOP_K):
        pltpu.make_async_copy(h_ref, xs_ref.at[pl.ds(0, slab * ts)], sem).wait()


def _scatter(gstart, nblk, dests, h2s, *, ts, slab, n_blk):
    rows, W = h2s.shape
    grid_spec = pltpu.PrefetchScalarGridSpec(
        num_scalar_prefetch=2,
        grid=(rows // (slab * ts),),
        in_specs=[pl.BlockSpec((ts,), lambda i, be, nu: (i,), memory_space=pltpu.SMEM)] * TOP_K
        + [pl.BlockSpec((slab * ts, W), lambda i, be, nu: (i, 0))],
        out_specs=pl.BlockSpec(memory_space=pl.ANY),
        scratch_shapes=[pltpu.VMEM((slab * EXPERT_ROWS, W), h2s.dtype), pltpu.SemaphoreType.DMA(()),
                        pltpu.SemaphoreType.DMA(())],
    )
    return pl.pallas_call(
        _scatter_kernel,
        grid_spec=grid_spec,
        out_shape=jax.ShapeDtypeStruct((n_blk * slab * EXPERT_ROWS, W), h2s.dtype),
        compiler_params=_params(("arbitrary",)),
        name="scatter",
    )(gstart, nblk, *dests, h2s)


def _experts_kernel(gstart_ref, nblk_ref, wg_ref, wu_ref, wd_ref, xs_ref, ys_ref, xbuf_ref, ybuf_ref,
                    wgb_ref, wub_ref, wdb_ref, xstage_ref, ystage_ref, zero_ref, front_ref, xsem, ysem, zsem):
    e = pl.program_id(0)
    n_exp = pl.num_programs(0)
    ring = xbuf_ref.shape[0]
    blk = zero_ref.shape[0]
    slab = wgb_ref.shape[0] // LANES
    rows = blk // slab
    nused = gstart_ref[n_exp - 1] + nblk_ref[n_exp - 1]
    g0 = gstart_ref[e]
    n = nblk_ref[e]

    def x_copy(b):
        slot = lax.rem(b, ring)
        return pltpu.make_async_copy(xs_ref.at[pl.ds(pl.multiple_of(b * blk, blk), blk)], xbuf_ref.at[slot],
                                     xsem.at[slot])

    def y_copy(b):
        slot = lax.rem(b, ring)
        return pltpu.make_async_copy(ybuf_ref.at[slot], ys_ref.at[pl.ds(pl.multiple_of(b * blk, blk), blk)],
                                     ysem.at[slot])

    @pl.when(e == 0)
    def _():
        front_ref[0] = 0

    @pl.when(n > 0)
    def _():
        wgb_ref[...] = wg_ref[0].astype(BF16)
        wub_ref[...] = wu_ref[0].astype(BF16)
        wdb_ref[...] = wd_ref[0].astype(BF16)

    def group(g, nb):
        limit = jnp.minimum(nused, g + ring)

        def fetch(b, carry):
            x_copy(b).start()
            return carry

        lax.fori_loop(front_ref[0], limit, fetch, 0)
        front_ref[0] = jnp.maximum(front_ref[0], limit)

        for i in range(nb):
            x_copy(g + i).wait()
        for i in range(nb):
            @pl.when(g + i >= ring)
            def _():
                y_copy(g + i - ring).wait()

        for lo, hi in ([(0, nb)] if nb < 4 else [(0, nb // 2), (nb // 2, nb)]):
            parts = [_load_slabs(xstage_ref.at[i], xbuf_ref[lax.rem(g + i, ring)], slab).astype(BF16)
                     for i in range(lo, hi)]
            x = parts[0] if len(parts) == 1 else jnp.concatenate(parts, axis=0)
            gate = _dot(x, wgb_ref[...])
            up = _dot(x, wub_ref[...])
            y = _dot((_silu(gate) * up).astype(BF16), wdb_ref[...])
            for i in range(lo, hi):
                _store_slabs(ybuf_ref.at[lax.rem(g + i, ring)], ystage_ref.at[i], y[(i - lo) * rows:(i - lo + 1) * rows])
        for i in range(nb):
            y_copy(g + i).start()

    n4 = lax.shift_right_logical(n, 2)
    left = n - 4 * n4

    def quad(j, carry):
        group(g0 + 4 * j, 4)
        return carry

    lax.fori_loop(0, n4, quad, 0)

    @pl.when(left >= 2)
    def _():
        group(g0 + 4 * n4, 2)

    @pl.when(lax.rem(left, 2) == 1)
    def _():
        group(g0 + n - 1, 1)

    @pl.when(e == n_exp - 1)
    def _():
        def drain(b, carry):
            y_copy(b).wait()
            return carry

        lax.fori_loop(jnp.maximum(nused - ring, 0), nused, drain, 0)
        _fill_zero_blocks(zero_ref, ys_ref, zsem, gstart_ref, nblk_ref, last_of_expert=False)


def _experts(gstart, nblk, xs, wg, wu, wd):
    rows, W = xs.shape
    E, D, H = wg.shape
    blk = (D // LANES) * EXPERT_ROWS
    grid_spec = pltpu.PrefetchScalarGridSpec(
        num_scalar_prefetch=2,
        grid=(E,),
        in_specs=[pl.BlockSpec((1, D, H), lambda e, gs, nb: (e, 0, 0)),
                  pl.BlockSpec((1, D, H), lambda e, gs, nb: (e, 0, 0)),
                  pl.BlockSpec((1, H, D), lambda e, gs, nb: (e, 0, 0)),
                  pl.BlockSpec(memory_space=pl.ANY)],
        out_specs=pl.BlockSpec(memory_space=pl.ANY),
        scratch_shapes=[pltpu.VMEM((EXPERT_RING, blk, W), xs.dtype), pltpu.VMEM((EXPERT_RING, blk, W), xs.dtype),
                        pltpu.VMEM((D, H), BF16), pltpu.VMEM((D, H), BF16), pltpu.VMEM((H, D), BF16),
                        pltpu.VMEM((EXPERT_RING // 2, blk, LANES), F32), pltpu.VMEM((EXPERT_RING // 2, blk, LANES), F32),
                        pltpu.VMEM((blk, W), xs.dtype), pltpu.SMEM((1,), I32),
                        pltpu.SemaphoreType.DMA((EXPERT_RING,)), pltpu.SemaphoreType.DMA((EXPERT_RING,)),
                        pltpu.SemaphoreType.DMA(())],
    )
    return pl.pallas_call(
        _experts_kernel,
        grid_spec=grid_spec,
        out_shape=jax.ShapeDtypeStruct((rows, W), xs.dtype),
        compiler_params=_params(("arbitrary",)),
        name="experts",
    )(gstart, nblk, wg, wu, wd, xs)


def _combine_kernel(*refs, alpha):
    cur_refs, nxt_refs = refs[:TOP_K], refs[TOP_K:2 * TOP_K]
    (wt_ref, h2_ref, x1_ref, mod_ref, wsg_ref, wsu_ref, wsd_ref, g_ref, b_ref, ys_ref, o_ref) = refs[2 * TOP_K:-11]
    buf_refs = refs[-11:-3]
    stage0_ref, stage1_ref, sem = refs[-3:]
    tc, d = h2_ref.shape
    slab = d // LANES
    i = pl.program_id(0)
    last = pl.num_programs(0) - 1

    def row_copy(dest_refs, k, t):
        return pltpu.make_async_copy(_slab_rows(ys_ref, dest_refs[k][t], slab), _slab_rows(buf_refs[k], t, slab), sem)

    def wait_planes():
        for k in range(TOP_K):
            pltpu.make_async_copy(ys_ref.at[pl.ds(0, slab * tc)], buf_refs[k], sem).wait()

    @pl.when(i == 0)
    def _():
        def body(t, carry):
            for k in range(TOP_K):
                row_copy(cur_refs, k, t).start(priority=k % 2)
            return carry

        lax.fori_loop(0, tc, body, 0)

    h = h2_ref[...]
    mid = (_silu(_dot(h, wsg_ref[...])) * _dot(h, wsu_ref[...])).astype(BF16)
    y = _dot(mid, wsd_ref[...])
    wait_planes()
    for k in range(TOP_K):
        stage_ref = stage0_ref if k % 2 == 0 else stage1_ref
        y = y + wt_ref[:, k:k + 1] * _load_slabs(stage_ref, buf_refs[k][...], slab)
        for t in range(tc):
            row_copy(nxt_refs, k, t).start(priority=t % 2)
    z = alpha * x1_ref[...] + (1.0 + mod_ref[0, 5:6, :]) * y
    o_ref[...] = _ln_plain(z) * g_ref[...] + b_ref[...]

    @pl.when(i == last)
    def _():
        wait_planes()


def _combine(dests, wt, h2, x1, mod3, wsg, wsu, wsd, ln_g, ln_b, ys, *, seq, tc, alpha):
    N, D = x1.shape
    slab = D // LANES
    per_b = seq // tc
    n_tiles = N // tc
    row = lambda i: (i, 0)
    return pl.pallas_call(
        functools.partial(_combine_kernel, alpha=alpha),
        grid=(n_tiles,),
        in_specs=[pl.BlockSpec((tc,), lambda i: (i,), memory_space=pltpu.SMEM)] * TOP_K
        + [pl.BlockSpec((tc,), lambda i: (jnp.minimum(i + 1, n_tiles - 1),), memory_space=pltpu.SMEM)] * TOP_K
        + [pl.BlockSpec((tc, LANES), row), pl.BlockSpec((tc, D), row), pl.BlockSpec((tc, D), row),
           pl.BlockSpec((1, 6, D), lambda i: (i // per_b, 0, 0)),
           _full(wsg.shape), _full(wsu.shape), _full(wsd.shape), _full(ln_g.shape), _full(ln_b.shape),
           pl.BlockSpec(memory_space=pl.ANY)],
        out_specs=pl.BlockSpec((tc, D), row),
        out_shape=jax.ShapeDtypeStruct((N, D), F32),
        scratch_shapes=[pltpu.VMEM((slab * tc, LANES), ys.dtype)] * TOP_K
        + [_slab_scratch(tc, D), _slab_scratch(tc, D), pltpu.SemaphoreType.DMA(())],
        compiler_params=_params(("arbitrary",)),
        name="combine",
    )(*dests, *dests, wt, h2, x1, mod3, wsg, wsu, wsd, ln_g, ln_b, ys)


def _tiles(seq):
    t = lambda want: min(want, seq)
    return dict(proj=t(512), swa=t(512), gla=t(512), route=t(256), scatter=t(512), combine=t(256))


def _layer(x2, mod3, posb, invf, p, *, seq, alpha):
    N, D = x2.shape
    tl = _tiles(seq)
    qa, ka, va, ql, kl, vl, gl, lg = _inproj(x2, mod3, posb, invf, p["w_main"], p["b_main"], p["w_lo"], p["b_lo"],
                                             p["w_gk"], p["b_gk"], seq=seq, tm=tl["proj"])
    oa = _swa(qa, ka, va, p["sinks"], seq=seq, tq=tl["swa"])
    og = _gla(ql, kl, vl, gl, lg, p["norm_g"], seq=seq, tc=tl["gla"])
    x1, h2, h2s, logits_t = _outproj(oa, og, x2, mod3, p["w_oa"], p["w_og"], p["b_o"], p["ln1_g"], p["ln1_b"],
                                     p["wr_hi"], p["wr_lo"], seq=seq, tm=tl["proj"], alpha=alpha)
    idx, _, wt, rank, cnt = _route(logits_t, p["bias_col"], p["upper"], tr=tl["route"])
    E = cnt.shape[0]
    n_blk = (N * TOP_K) // EXPERT_ROWS + E
    dest, gstart, nblk = _plan(cnt, p["lower"], idx, rank, tr=tl["route"])
    gstart, nblk = gstart[:, 0], nblk[:, 0]
    dests = [dest[k] for k in range(TOP_K)]
    xs = _scatter(gstart, nblk, dests, h2s, ts=tl["scatter"], slab=D // LANES, n_blk=n_blk)
    ys = _experts(gstart, nblk, xs, p["wg"], p["wu"], p["wd"])
    return _combine(dests, wt, h2, x1, mod3, p["wsg"], p["wsu"], p["wsd"], p["ln2_g"], p["ln2_b"], ys,
                    seq=seq, tc=tl["combine"], alpha=alpha)


def kernel(x, c, positions, w_ada, b_ada, w_in, b_in, attn_sinks, w_gk2, b_gk2, gla_norm_g, w_o, b_o, ln1_g, ln1_b, w_router, router_bias, w_exp_gate, w_exp_up, w_exp_down, w_sh_gate, w_sh_up, w_sh_down, ln2_g, ln2_b):
    B, S, D = x.shape
    depth = w_ada.shape[0]
    E = w_router.shape[2]
    alpha = float((2 * depth) ** 0.25)
    tl = _tiles(S)
    n_main = int(_SEG[-1])

    posb = jnp.broadcast_to(positions.astype(F32).reshape(B * S, 1), (B * S, LANES))
    half = ATTN_HEAD_DIM // 2
    invf = (ROPE_THETA ** (-(jnp.arange(LANES) % half).astype(F32) / half)).reshape(1, LANES)
    tr = tl["route"]
    upper = (jnp.arange(tr)[:, None] < jnp.arange(tr)[None, :]).astype(BF16)
    lower = (jnp.arange(E)[:, None] >= jnp.arange(E)[None, :]).astype(BF16)
    row = lambda v: v.reshape(1, -1)

    x2 = x.reshape(B * S, D)
    for l in range(depth):
        wr_t = w_router[l].T
        wr_hi = wr_t.astype(BF16)
        p = dict(
            w_main=w_in[l][:, :n_main].astype(BF16), b_main=row(b_in[l][:n_main]),
            w_lo=jnp.pad(w_in[l][:, n_main:], ((0, 0), (0, LANES - GLA_GATE_RANK))).astype(BF16),
            b_lo=row(jnp.pad(b_in[l][n_main:], (0, LANES - GLA_GATE_RANK))),
            w_gk=jnp.pad(w_gk2[l], ((0, LANES - GLA_GATE_RANK), (0, 0))), b_gk=row(b_gk2[l]),
            sinks=attn_sinks[l], norm_g=row(gla_norm_g[l]),
            w_oa=w_o[l][:ATTN_WIDTH].astype(BF16), w_og=w_o[l][ATTN_WIDTH:].astype(BF16), b_o=row(b_o[l]),
            ln1_g=row(ln1_g[l]), ln1_b=row(ln1_b[l]),
            wr_hi=wr_hi, wr_lo=(wr_t - wr_hi.astype(F32)).astype(BF16),
            bias_col=jnp.broadcast_to(router_bias[l].reshape(E, 1), (E, LANES)),
            upper=upper, lower=lower,
            wg=w_exp_gate[l], wu=w_exp_up[l], wd=w_exp_down[l],
            wsg=w_sh_gate[l].astype(BF16), wsu=w_sh_up[l].astype(BF16), wsd=w_sh_down[l].astype(BF16),
            ln2_g=row(ln2_g[l]), ln2_b=row(ln2_b[l]),
        )
        mod = _mod(c, w_ada[l], b_ada[l])
        mod3 = mod.reshape(B, 6, D)
        x2 = _layer(x2, mod3, posb, invf, p, seq=S, alpha=alpha)
    return x2.reshape(B, S, D)
```

```python
import functools

import jax
import jax.numpy as jnp
import numpy as np
from jax import lax
from jax.experimental import pallas as pl
from jax.experimental.pallas import tpu as pltpu

F32 = jnp.float32
BF16 = jnp.bfloat16
I32 = jnp.int32

ATTN_Q_HEADS = 8
ATTN_KV_HEADS = 2
ATTN_HEAD_DIM = 64
ATTN_BLOCK = 128
ROPE_THETA = 10000.0
GLA_HEADS = 4
GLA_KEY_DIM = 64
GLA_VAL_DIM = 128
GLA_GATE_RANK = 16
GLA_GATE_NORM = 16.0
GLA_CHUNK = 64
N_GROUPS = 8
TOPK_GROUPS = 4
TOP_K = 8
ROUTED_SCALE = 2.5
LN_EPS = 1e-5
NEG_INF = -1e30
REMOVED = -3e38

ATTN_WIDTH = ATTN_Q_HEADS * ATTN_HEAD_DIM
KV_WIDTH = ATTN_KV_HEADS * ATTN_HEAD_DIM
GLA_KWIDTH = GLA_HEADS * GLA_KEY_DIM
GLA_WIDTH = GLA_HEADS * GLA_VAL_DIM

LANES = 128
VMEM_LIMIT = 56 * 1024 * 1024
PROJ_ROWS = 256
EXPERT_ROWS = 256
EXPERT_RING = 8


def _params(sem):
    return pltpu.CompilerParams(dimension_semantics=sem, vmem_limit_bytes=VMEM_LIMIT)


def _full(shape):
    return pl.BlockSpec(shape, lambda *_: (0,) * len(shape))


def _split_bf16(a):
    hi = a.astype(BF16)
    lo = (a - hi.astype(F32)).astype(BF16)
    return hi, lo


def _dot(a, b, dims=(((1,), (0,)), ((), ()))):
    return lax.dot_general(a, b, dims, preferred_element_type=F32)


NT = (((1,), (1,)), ((), ()))
TN = (((0,), (0,)), ((), ()))


def _dot3(a, b, dims=(((1,), (0,)), ((), ()))):
    ah, al = _split_bf16(a)
    bh, bl = _split_bf16(b)
    return _dot(ah, bh, dims) + (_dot(ah, bl, dims) + _dot(al, bh, dims))


def _ln_plain(x):
    mu = jnp.mean(x, axis=-1, keepdims=True)
    xc = x - mu
    var = jnp.mean(xc * xc, axis=-1, keepdims=True)
    return xc * lax.rsqrt(var + LN_EPS)


def _silu(x):
    return x * (1.0 / (1.0 + jnp.exp(-x)))


def _sigmoid(x):
    return 1.0 / (1.0 + jnp.exp(-x))


def _mod_kernel(c_ref, w_ref, b_ref, o_ref):
    c = c_ref[...]
    o_ref[...] = _dot3(_silu(c), w_ref[...]) + b_ref[...]


def _mod(c, w_ada, b_ada):
    B, D = c.shape
    n = w_ada.shape[1] // D
    return pl.pallas_call(
        _mod_kernel,
        grid=(n,),
        in_specs=[_full((B, D)),
                  pl.BlockSpec((D, D), lambda j: (0, j)),
                  pl.BlockSpec((1, D), lambda j: (0, j))],
        out_specs=pl.BlockSpec((B, D), lambda j: (0, j)),
        out_shape=jax.ShapeDtypeStruct((B, n * D), F32),
        compiler_params=_params(("arbitrary",)),
        name="mod",
    )(c, w_ada, b_ada.reshape(1, -1))


_SEG = np.cumsum([0, ATTN_WIDTH, KV_WIDTH, KV_WIDTH, GLA_KWIDTH, GLA_KWIDTH, GLA_WIDTH, GLA_WIDTH])


def _rope_chunk(t, cos, sin_signed, first_half):
    up = pltpu.roll(t, LANES - 32, axis=1)
    dn = pltpu.roll(t, 32, axis=1)
    return t * cos + jnp.where(first_half, up, dn) * sin_signed


def _inproj_kernel(x_ref, mod_ref, pos_ref, invf_ref, w_ref, b_ref, wlo_ref, blo_ref, wgk_ref, bgk_ref,
                   qa_ref, ka_ref, va_ref, ql_ref, kl_ref, vl_ref, gl_ref, lg_ref):
    for r0 in range(0, x_ref.shape[0], PROJ_ROWS):
        rows = slice(r0, r0 + PROJ_ROWS)
        h = _ln_plain(x_ref[rows, :]) * (1.0 + mod_ref[0, 1:2, :]) + mod_ref[0, 0:1, :]
        hb = h.astype(BF16)

        def seg(i):
            lo, hi = int(_SEG[i]), int(_SEG[i + 1])
            return _dot(hb, w_ref[:, lo:hi]) + b_ref[:, lo:hi]

        ang = pos_ref[rows, :] * invf_ref[...]
        cos = jnp.cos(ang)
        sin = jnp.sin(ang)
        lane = lax.broadcasted_iota(I32, ang.shape, 1)
        first_half = (lane % ATTN_HEAD_DIM) < (ATTN_HEAD_DIM // 2)
        sin_signed = jnp.where(first_half, -sin, sin)

        q = seg(0)
        scale = ATTN_HEAD_DIM ** -0.5
        for c in range(ATTN_WIDTH // LANES):
            t = q[:, c * LANES:(c + 1) * LANES]
            qa_ref[rows, c * LANES:(c + 1) * LANES] = (
                _rope_chunk(t, cos, sin_signed, first_half) * scale).astype(BF16)
        ka_ref[rows, :] = _rope_chunk(seg(1), cos, sin_signed, first_half).astype(BF16)
        va_ref[rows, :] = seg(2).astype(BF16)
        ql_ref[rows, :] = seg(3).astype(BF16)
        kl_ref[rows, :] = seg(4).astype(BF16)
        vl_ref[rows, :] = seg(5).astype(BF16)
        gl_ref[rows, :] = seg(6).astype(BF16)
        gk_lo = _dot(hb, wlo_ref[...]) + blo_ref[...]
        gk = _dot3(gk_lo, wgk_ref[...]) + bgk_ref[...]
        log_sig = jnp.minimum(gk, 0.0) - jnp.log(1.0 + jnp.exp(-jnp.abs(gk)))
        lg_ref[rows, :] = log_sig * (1.0 / GLA_GATE_NORM)


def _inproj(x2, mod3, posb, invf, w_main, b_main, w_lo, b_lo, w_gk, b_gk, *, seq, tm):
    N, D = x2.shape
    per_b = seq // tm
    widths = [ATTN_WIDTH, KV_WIDTH, KV_WIDTH, GLA_KWIDTH, GLA_KWIDTH, GLA_WIDTH, GLA_WIDTH, GLA_KWIDTH]
    dtypes = [BF16] * 7 + [F32]
    row = lambda i: (i, 0)
    return pl.pallas_call(
        _inproj_kernel,
        grid=(N // tm,),
        in_specs=[pl.BlockSpec((tm, D), row),
                  pl.BlockSpec((1, 6, D), lambda i: (i // per_b, 0, 0)),
                  pl.BlockSpec((tm, LANES), row),
                  _full(invf.shape), _full(w_main.shape), _full(b_main.shape),
                  _full(w_lo.shape), _full(b_lo.shape), _full(w_gk.shape), _full(b_gk.shape)],
        out_specs=[pl.BlockSpec((tm, w), row) for w in widths],
        out_shape=[jax.ShapeDtypeStruct((N, w), dt) for w, dt in zip(widths, dtypes)],
        compiler_params=_params(("parallel",)),
        name="inproj",
    )(x2, mod3, posb, invf, w_main, b_main, w_lo, b_lo, w_gk, b_gk)


def _swa_kernel(sink_ref, rep_ref, q_ref, kc_ref, kp_ref, vc_ref, vp_ref, o_ref):
    j = pl.program_id(1)
    blk = ATTN_BLOCK
    row = lax.broadcasted_iota(I32, (blk, 2 * blk), 0)
    col = lax.broadcasted_iota(I32, (blk, 2 * blk), 1)
    dist = row + blk - col
    band = (dist >= 0) & (dist < blk)
    group = ATTN_Q_HEADS // ATTN_KV_HEADS
    hd = ATTN_HEAD_DIM
    gw = group * hd
    kw = 2 * blk
    bd = (lax.broadcasted_iota(I32, (group * kw, gw), 0) // kw) == (lax.broadcasted_iota(I32, (group * kw, gw), 1) // hd)
    lane_head = lax.broadcasted_iota(I32, (blk, gw), 1) // hd
    rep = rep_ref[...]
    k_all = _dot(jnp.concatenate([kp_ref[...], kc_ref[...]], axis=0), rep).astype(BF16)
    v_all = _dot(jnp.concatenate([vp_ref[...], vc_ref[...]], axis=0), rep).astype(BF16)
    for s_blk in range(q_ref.shape[0] // blk):
        rows = slice(s_blk * blk, (s_blk + 1) * blk)
        k2 = k_all[s_blk * blk:(s_blk + 2) * blk]
        v2 = v_all[s_blk * blk:(s_blk + 2) * blk]
        valid = band & ((col >= blk) | (j > 0)) if s_blk == 0 else band
        for g in range(ATTN_KV_HEADS):
            lanes = slice(g * gw, (g + 1) * gw)
            zero = jnp.zeros((), BF16)
            kd = jnp.where(bd, jnp.concatenate([k2[:, lanes]] * group, axis=0), zero)
            vd = jnp.where(bd, jnp.concatenate([v2[:, lanes]] * group, axis=0), zero)
            s_all = _dot(q_ref[rows, lanes], kd, NT)
            probs, denom = [], jnp.zeros((blk, gw), F32)
            for r in range(group):
                sink = sink_ref[g * group + r]
                s = jnp.where(valid, s_all[:, r * kw:(r + 1) * kw], NEG_INF)
                m = jnp.maximum(jnp.max(s, axis=-1, keepdims=True), sink)
                p = jnp.exp(s - m)
                d = jnp.sum(p, axis=-1, keepdims=True) + jnp.exp(sink - m)
                probs.append(p.astype(BF16))
                denom = jnp.where(lane_head == r, d, denom)
            o = _dot(jnp.concatenate(probs, axis=1), vd) / denom
            o_ref[rows, lanes] = o.astype(BF16)


def _swa(q, k, v, sinks, *, seq, tq):
    N = q.shape[0]
    src = jnp.arange(ATTN_WIDTH)
    src = (src // (ATTN_WIDTH // ATTN_KV_HEADS)) * ATTN_HEAD_DIM + src % ATTN_HEAD_DIM
    rep = (jnp.arange(KV_WIDTH)[:, None] == src[None, :]).astype(BF16)
    per_b = seq // tq
    sub = tq // ATTN_BLOCK
    B = N // seq
    cur = lambda b, j: (b * per_b + j, 0)
    prev = lambda b, j: ((b * per_b + j) * sub - jnp.minimum(j, 1), 0)
    return pl.pallas_call(
        _swa_kernel,
        grid=(B, per_b),
        in_specs=[pl.BlockSpec(memory_space=pltpu.SMEM), _full(rep.shape),
                  pl.BlockSpec((tq, ATTN_WIDTH), cur),
                  pl.BlockSpec((tq, KV_WIDTH), cur),
                  pl.BlockSpec((ATTN_BLOCK, KV_WIDTH), prev),
                  pl.BlockSpec((tq, KV_WIDTH), cur),
                  pl.BlockSpec((ATTN_BLOCK, KV_WIDTH), prev)],
        out_specs=pl.BlockSpec((tq, ATTN_WIDTH), cur),
        out_shape=jax.ShapeDtypeStruct((N, ATTN_WIDTH), BF16),
        compiler_params=_params(("parallel", "parallel")),
        name="swa",
    )(sinks, rep, q, k, k, v, v)


def _gla_kernel(q_ref, k_ref, v_ref, g_ref, lg_ref, ng_ref, o_ref, st_ref, *, chunks):
    @pl.when(pl.program_id(1) == 0)
    def _():
        st_ref[...] = jnp.zeros_like(st_ref)

    C = GLA_CHUNK
    dk, dv = GLA_KEY_DIM, GLA_VAL_DIM
    r = lax.broadcasted_iota(I32, (C, C), 0)
    c = lax.broadcasted_iota(I32, (C, C), 1)
    causal = c <= r
    tri = jnp.where(causal, 1.0, 0.0).astype(BF16)
    st = st_ref[...]
    for n in range(chunks):
        rows = slice(n * C, (n + 1) * C)
        lg_hi, lg_lo = _split_bf16(lg_ref[rows, :])
        b = _dot(tri, lg_hi) + _dot(tri, lg_lo)
        b_last = b[C - 1:C, :]
        q_in = (q_ref[rows, :].astype(F32) * (dk ** -0.5) * jnp.exp(b)).astype(BF16)
        kf = k_ref[rows, :].astype(F32)
        k_in = (kf * jnp.exp(-b)).astype(BF16)
        k_out = (kf * jnp.exp(b_last - b)).astype(BF16)
        decay = jnp.exp(b_last)
        stb = st.astype(BF16)
        ut = []
        for h in range(GLA_HEADS):
            ks = slice(h * dk, (h + 1) * dk)
            vs = slice(h * dv, (h + 1) * dv)
            vh = v_ref[rows, vs]
            a = jnp.where(causal, _dot(q_in[:, ks], k_in[:, ks], NT), 0.0).astype(BF16)
            o = _dot(a, vh) + _dot(q_in[:, ks], stb[:, ks], NT)
            ut.append(_dot(vh, k_out[:, ks], TN))
            o = o * lax.rsqrt(jnp.mean(o * o, axis=-1, keepdims=True) + LN_EPS) * ng_ref[...]
            o_ref[rows, vs] = (o * _silu(g_ref[rows, vs].astype(F32))).astype(BF16)
        st = st * decay + jnp.concatenate(ut, axis=1)
    st_ref[...] = st


def _gla(q, k, v, g, lg, norm_g, *, seq, tc):
    N = q.shape[0]
    B = N // seq
    per_b = seq // tc
    row = lambda b, j: (b * per_b + j, 0)
    return pl.pallas_call(
        functools.partial(_gla_kernel, chunks=tc // GLA_CHUNK),
        grid=(B, per_b),
        in_specs=[pl.BlockSpec((tc, GLA_KWIDTH), row), pl.BlockSpec((tc, GLA_KWIDTH), row),
                  pl.BlockSpec((tc, GLA_WIDTH), row), pl.BlockSpec((tc, GLA_WIDTH), row),
                  pl.BlockSpec((tc, GLA_KWIDTH), row), _full(norm_g.shape)],
        out_specs=pl.BlockSpec((tc, GLA_WIDTH), row),
        out_shape=jax.ShapeDtypeStruct((N, GLA_WIDTH), BF16),
        scratch_shapes=[pltpu.VMEM((GLA_VAL_DIM, GLA_KWIDTH), F32)],
        compiler_params=_params(("parallel", "arbitrary")),
        name="gla",
    )(q, k, v, g, lg, norm_g)


def _slab_scratch(rows, d):
    return pltpu.VMEM((rows * (d // LANES), LANES), F32)


def _store_slabs(slabs_ref, stage_ref, v):
    rows, d = v.shape
    n = d // LANES
    for c in range(n):
        stage_ref[pl.ds(c, rows, stride=n), :] = v[:, c * LANES:(c + 1) * LANES]
    slabs_ref[...] = stage_ref[...].astype(slabs_ref.dtype)


def _load_slabs(stage_ref, slabs, n):
    stage_ref[...] = slabs.astype(F32)
    rows = stage_ref.shape[0] // n
    return jnp.concatenate([stage_ref[pl.ds(c, rows, stride=n), :] for c in range(n)], axis=1)


def _outproj_kernel(oa_ref, og_ref, x_ref, mod_ref, woa_ref, wog_ref, bo_ref, g_ref, b_ref, wrh_ref, wrl_ref,
                    x1_ref, h2_ref, h2s_ref, lt_ref, stage_ref, *, alpha):
    y = _dot(oa_ref[...], woa_ref[...]) + _dot(og_ref[...], wog_ref[...]) + bo_ref[...]
    z = alpha * x_ref[...] + (1.0 + mod_ref[0, 2:3, :]) * y
    x1 = _ln_plain(z) * g_ref[...] + b_ref[...]
    x1_ref[...] = x1
    h2 = _ln_plain(x1) * (1.0 + mod_ref[0, 4:5, :]) + mod_ref[0, 3:4, :]
    h2_ref[...] = h2.astype(BF16)
    _store_slabs(h2s_ref, stage_ref, h2)
    hh, hl = _split_bf16(h2)
    wh = wrh_ref[...]
    lt_ref[...] = _dot(wh, hh, NT) + (_dot(wh, hl, NT) + _dot(wrl_ref[...], hh, NT))


def _outproj(oa, og, x2, mod3, w_oa, w_og, b_o, ln_g, ln_b, wr_hi, wr_lo, *, seq, tm, alpha):
    N, D = x2.shape
    E = wr_hi.shape[0]
    per_b = seq // tm
    row = lambda i: (i, 0)
    return pl.pallas_call(
        functools.partial(_outproj_kernel, alpha=alpha),
        grid=(N // tm,),
        in_specs=[pl.BlockSpec((tm, ATTN_WIDTH), row), pl.BlockSpec((tm, GLA_WIDTH), row),
                  pl.BlockSpec((tm, D), row),
                  pl.BlockSpec((1, 6, D), lambda i: (i // per_b, 0, 0)),
                  _full(w_oa.shape), _full(w_og.shape), _full(b_o.shape), _full(ln_g.shape), _full(ln_b.shape),
                  _full(wr_hi.shape), _full(wr_lo.shape)],
        out_specs=[pl.BlockSpec((tm, D), row), pl.BlockSpec((tm, D), row),
                   pl.BlockSpec((tm * (D // LANES), LANES), row), pl.BlockSpec((E, tm), lambda i: (0, i))],
        out_shape=[jax.ShapeDtypeStruct((N, D), F32), jax.ShapeDtypeStruct((N, D), BF16),
                   jax.ShapeDtypeStruct((N * (D // LANES), LANES), BF16), jax.ShapeDtypeStruct((E, N), F32)],
        scratch_shapes=[_slab_scratch(tm, D)],
        compiler_params=_params(("parallel",)),
        name="outproj",
    )(oa, og, x2, mod3, w_oa, w_og, b_o, ln_g, ln_b, wr_hi, wr_lo)


def _first_index(eq, idx, size):
    return jnp.min(jnp.where(eq, idx, float(size)), axis=0, keepdims=True)


def _route_kernel(lt_ref, bias_ref, upper_ref, idx_ref, w_ref, wt_ref, rank_ref, cnt_ref, base_ref):
    @pl.when(pl.program_id(0) == 0)
    def _():
        base_ref[...] = jnp.zeros_like(base_ref)

    E, t = lt_ref.shape
    gsz = E // N_GROUPS
    scores = _sigmoid(lt_ref[...])
    biased = scores + bias_ref[:, 0:1]
    gi = lax.broadcasted_iota(I32, (gsz, t), 0).astype(F32)
    gs_rows = []
    for g in range(N_GROUPS):
        grp = biased[g * gsz:(g + 1) * gsz, :]
        m1 = jnp.max(grp, axis=0, keepdims=True)
        first = _first_index(grp == m1, gi, gsz)
        m2 = jnp.max(jnp.where(gi == first, REMOVED, grp), axis=0, keepdims=True)
        gs_rows.append(m1 + m2)
    gs = jnp.concatenate(gs_rows, axis=0)
    ri = lax.broadcasted_iota(I32, (N_GROUPS, t), 0).astype(F32)
    gsel = jnp.zeros((N_GROUPS, t), F32)
    for _ in range(TOPK_GROUPS):
        m = jnp.max(gs, axis=0, keepdims=True)
        hit = ri == _first_index(gs == m, ri, N_GROUPS)
        gsel = jnp.where(hit, 1.0, gsel)
        gs = jnp.where(hit, REMOVED, gs)
    cand = jnp.concatenate(
        [jnp.where(gsel[g:g + 1, :] > 0.5, biased[g * gsz:(g + 1) * gsz, :], NEG_INF) for g in range(N_GROUPS)],
        axis=0)
    ei = lax.broadcasted_iota(I32, (E, t), 0).astype(F32)
    idx_rows, w_rows, hits = [], [], []
    chosen = jnp.zeros((E, t), F32)
    for _ in range(TOP_K):
        m = jnp.max(cand, axis=0, keepdims=True)
        fi = _first_index(cand == m, ei, E)
        hit = ei == fi
        idx_rows.append(fi)
        w_rows.append(jnp.sum(jnp.where(hit, scores, 0.0), axis=0, keepdims=True))
        hits.append(hit)
        chosen = jnp.where(hit, 1.0, chosen)
        cand = jnp.where(hit, REMOVED, cand)
    w = jnp.concatenate(w_rows, axis=0)
    w = w / jnp.sum(w, axis=0, keepdims=True) * ROUTED_SCALE
    idx_ref[...] = jnp.concatenate(idx_rows, axis=0).astype(I32)
    w_ref[...] = w
    wpad = jnp.concatenate([w, jnp.zeros((LANES - TOP_K, t), F32)], axis=0)
    wt_ref[...] = wpad.T
    prefix = _dot(chosen.astype(BF16), upper_ref[...])
    pos = base_ref[:, 0:1] + prefix
    rank_ref[...] = jnp.concatenate(
        [jnp.sum(jnp.where(hit, pos, 0.0), axis=0, keepdims=True) for hit in hits], axis=0).astype(I32)
    base_ref[...] = base_ref[...] + jnp.sum(chosen, axis=1, keepdims=True)
    cnt_ref[...] = base_ref[...]


def _route(logits_t, bias_col, upper, *, tr):
    E, N = logits_t.shape
    col = lambda i: (0, i)
    return pl.pallas_call(
        _route_kernel,
        grid=(N // tr,),
        in_specs=[pl.BlockSpec((E, tr), col), _full(bias_col.shape), _full(upper.shape)],
        out_specs=[pl.BlockSpec((TOP_K, tr), col), pl.BlockSpec((TOP_K, tr), col),
                   pl.BlockSpec((tr, LANES), lambda i: (i, 0)),
                   pl.BlockSpec((TOP_K, tr), col), _full((E, LANES))],
        out_shape=[jax.ShapeDtypeStruct((TOP_K, N), I32), jax.ShapeDtypeStruct((TOP_K, N), F32),
                   jax.ShapeDtypeStruct((N, LANES), F32),
                   jax.ShapeDtypeStruct((TOP_K, N), I32), jax.ShapeDtypeStruct((E, LANES), F32)],
        scratch_shapes=[pltpu.VMEM((E, LANES), F32)],
        compiler_params=_params(("arbitrary",)),
        name="route",
    )(logits_t, bias_col, upper)


def _plan_kernel(cnt_ref, lower_ref, idx_ref, rank_ref, dest_ref, gstart_ref, nblk_ref, pstart_ref, *, slab):
    E = cnt_ref.shape[0]
    bm = float(EXPERT_ROWS)

    @pl.when(pl.program_id(0) == 0)
    def _():
        cnt = cnt_ref[...]
        nblk = jnp.floor((cnt + (bm - 1.0)) * (1.0 / bm))
        padded = nblk * bm
        hi = jnp.floor(padded * (1.0 / 256.0))
        lo = padded - hi * 256.0
        low = lower_ref[...]
        pend = 256.0 * _dot(low, hi.astype(BF16)) + _dot(low, lo.astype(BF16))
        pstart_ref[...] = pend - padded
        gstart_ref[...] = ((pend - padded) * (1.0 / bm)).astype(I32)
        nblk_ref[...] = nblk.astype(I32)

    t = idx_ref.shape[1]
    ei = lax.broadcasted_iota(I32, (E, t), 0)
    ps = pstart_ref[:, 0:1]
    rows = []
    for k in range(TOP_K):
        hit = ei == idx_ref[k:k + 1, :]
        rows.append(jnp.sum(jnp.where(hit, ps, 0.0), axis=0, keepdims=True))
    dest_ref[...] = (rank_ref[...] + jnp.concatenate(rows, axis=0).astype(I32)) * slab


def _plan(cnt, lower, idx, rank, *, tr, slab):
    E = cnt.shape[0]
    N = idx.shape[1]
    col = lambda i: (0, i)
    return pl.pallas_call(
        functools.partial(_plan_kernel, slab=slab),
        grid=(N // tr,),
        in_specs=[_full(cnt.shape), _full(lower.shape), pl.BlockSpec((TOP_K, tr), col),
                  pl.BlockSpec((TOP_K, tr), col)],
        out_specs=[pl.BlockSpec((TOP_K, tr), col), _full((E, LANES)), _full((E, LANES))],
        out_shape=[jax.ShapeDtypeStruct((TOP_K, N), I32), jax.ShapeDtypeStruct((E, LANES), I32),
                   jax.ShapeDtypeStruct((E, LANES), I32)],
        scratch_shapes=[pltpu.VMEM((E, LANES), F32)],
        compiler_params=_params(("arbitrary",)),
        name="plan",
    )(cnt, lower, idx, rank)


def _slab_rows(ref, row, slab):
    return ref.at[pl.ds(pl.multiple_of(row * slab, slab), slab)]


def _slab_at(ref, first_row, slab):
    return ref.at[pl.ds(pl.multiple_of(first_row, slab), slab)]


def _fill_zero_blocks(zero_ref, out_ref, zsem, gstart_ref, nblk_ref, *, last_of_expert):
    blk = zero_ref.shape[0]
    n_exp = gstart_ref.shape[0]
    n_blk = out_ref.shape[0] // blk
    nused = gstart_ref[n_exp - 1] + nblk_ref[n_exp - 1]
    zero_ref[...] = jnp.zeros_like(zero_ref)

    def blk_copy(b):
        return pltpu.make_async_copy(zero_ref, out_ref.at[pl.ds(pl.multiple_of(b * blk, blk), blk)], zsem)

    def tail(b, carry):
        blk_copy(b).start()
        return carry

    lax.fori_loop(nused, n_blk, tail, 0)
    n_started = n_blk - nused
    if last_of_expert:
        def last(e, n):
            has = nblk_ref[e] > 0

            @pl.when(has)
            def _():
                blk_copy(gstart_ref[e] + nblk_ref[e] - 1).start()

            return n + has.astype(I32)

        n_started = lax.fori_loop(0, n_exp, last, n_started)

    def wait(_, carry):
        blk_copy(0).wait()
        return carry

    lax.fori_loop(0, n_started, wait, 0)


def _scatter_kernel(gstart_ref, nblk_ref, *refs):
    dest_refs = refs[:TOP_K]
    h_ref, xs_ref, zero_ref, sem, zsem = refs[TOP_K:]
    ts = dest_refs[0].shape[0]
    slab = h_ref.shape[0] // ts

    @pl.when(pl.program_id(0) == 0)
    def _():
        _fill_zero_blocks(zero_ref, xs_ref, zsem, gstart_ref, nblk_ref, last_of_expert=True)

    def body(t, carry):
        src = _slab_rows(h_ref, t, slab)
        for k in range(TOP_K):
            pltpu.make_async_copy(src, _slab_at(xs_ref, dest_refs[k][t], slab), sem).start(priority=k % 2)
        return carry

    lax.fori_loop(0, ts, body, 0)
    for _ in range(TOP_K):
        pltpu.make_async_copy(h_ref, xs_ref.at[pl.ds(0, slab * ts)], sem).wait()


def _scatter(gstart, nblk, dests, h2s, *, ts, slab, n_blk):
    rows, W = h2s.shape
    grid_spec = pltpu.PrefetchScalarGridSpec(
        num_scalar_prefetch=2,
        grid=(rows // (slab * ts),),
        in_specs=[pl.BlockSpec((ts,), lambda i, be, nu: (i,), memory_space=pltpu.SMEM)] * TOP_K
        + [pl.BlockSpec((slab * ts, W), lambda i, be, nu: (i, 0))],
        out_specs=pl.BlockSpec(memory_space=pl.ANY),
        scratch_shapes=[pltpu.VMEM((slab * EXPERT_ROWS, W), h2s.dtype), pltpu.SemaphoreType.DMA(()),
                        pltpu.SemaphoreType.DMA(())],
    )
    return pl.pallas_call(
        _scatter_kernel,
        grid_spec=grid_spec,
        out_shape=jax.ShapeDtypeStruct((n_blk * slab * EXPERT_ROWS, W), h2s.dtype),
        compiler_params=_params(("arbitrary",)),
        name="scatter",
    )(gstart, nblk, *dests, h2s)


def _experts_kernel(gstart_ref, nblk_ref, wg_ref, wu_ref, wd_ref, xs_ref, ys_ref, xbuf_ref, ybuf_ref,
                    wgb_ref, wub_ref, wdb_ref, xstage_ref, ystage_ref, zero_ref, front_ref, xsem, ysem, zsem):
    e = pl.program_id(0)
    n_exp = pl.num_programs(0)
    ring = xbuf_ref.shape[0]
    blk = zero_ref.shape[0]
    slab = wgb_ref.shape[0] // LANES
    rows = blk // slab
    nused = gstart_ref[n_exp - 1] + nblk_ref[n_exp - 1]
    g0 = gstart_ref[e]
    n = nblk_ref[e]

    def x_copy(b):
        slot = lax.rem(b, ring)
        return pltpu.make_async_copy(xs_ref.at[pl.ds(pl.multiple_of(b * blk, blk), blk)], xbuf_ref.at[slot],
                                     xsem.at[slot])

    def y_copy(b):
        slot = lax.rem(b, ring)
        return pltpu.make_async_copy(ybuf_ref.at[slot], ys_ref.at[pl.ds(pl.multiple_of(b * blk, blk), blk)],
                                     ysem.at[slot])

    @pl.when(e == 0)
    def _():
        front_ref[0] = 0

    @pl.when(n > 0)
    def _():
        wgb_ref[...] = wg_ref[0].astype(BF16)
        wub_ref[...] = wu_ref[0].astype(BF16)
        wdb_ref[...] = wd_ref[0].astype(BF16)

    def group(g, nb):
        limit = jnp.minimum(nused, g + ring)

        def fetch(b, carry):
            x_copy(b).start()
            return carry

        lax.fori_loop(front_ref[0], limit, fetch, 0)
        front_ref[0] = jnp.maximum(front_ref[0], limit)

        for i in range(nb):
            x_copy(g + i).wait()
        for i in range(nb):
            @pl.when(g + i >= ring)
            def _():
                y_copy(g + i - ring).wait()

        for lo, hi in ([(0, nb)] if nb < 4 else [(0, nb // 2), (nb // 2, nb)]):
            parts = [_load_slabs(xstage_ref.at[i], xbuf_ref[lax.rem(g + i, ring)], slab).astype(BF16)
                     for i in range(lo, hi)]
            x = parts[0] if len(parts) == 1 else jnp.concatenate(parts, axis=0)
            gate = _dot(x, wgb_ref[...])
            up = _dot(x, wub_ref[...])
            y = _dot((_silu(gate) * up).astype(BF16), wdb_ref[...])
            for i in range(lo, hi):
                _store_slabs(ybuf_ref.at[lax.rem(g + i, ring)], ystage_ref.at[i], y[(i - lo) * rows:(i - lo + 1) * rows])
        for i in range(nb):
            y_copy(g + i).start()

    n4 = lax.shift_right_logical(n, 2)
    left = n - 4 * n4

    def quad(j, carry):
        group(g0 + 4 * j, 4)
        return carry

    lax.fori_loop(0, n4, quad, 0)

    @pl.when(left >= 2)
    def _():
        group(g0 + 4 * n4, 2)

    @pl.when(lax.rem(left, 2) == 1)
    def _():
        group(g0 + n - 1, 1)

    @pl.when(e == n_exp - 1)
    def _():
        def drain(b, carry):
            y_copy(b).wait()
            return carry

        lax.fori_loop(jnp.maximum(nused - ring, 0), nused, drain, 0)
        _fill_zero_blocks(zero_ref, ys_ref, zsem, gstart_ref, nblk_ref, last_of_expert=False)


def _experts(gstart, nblk, xs, wg, wu, wd):
    rows, W = xs.shape
    E, D, H = wg.shape
    blk = (D // LANES) * EXPERT_ROWS
    grid_spec = pltpu.PrefetchScalarGridSpec(
        num_scalar_prefetch=2,
        grid=(E,),
        in_specs=[pl.BlockSpec((1, D, H), lambda e, gs, nb: (e, 0, 0)),
                  pl.BlockSpec((1, D, H), lambda e, gs, nb: (e, 0, 0)),
                  pl.BlockSpec((1, H, D), lambda e, gs, nb: (e, 0, 0)),
                  pl.BlockSpec(memory_space=pl.ANY)],
        out_specs=pl.BlockSpec(memory_space=pl.ANY),
        scratch_shapes=[pltpu.VMEM((EXPERT_RING, blk, W), xs.dtype), pltpu.VMEM((EXPERT_RING, blk, W), xs.dtype),
                        pltpu.VMEM((D, H), BF16), pltpu.VMEM((D, H), BF16), pltpu.VMEM((H, D), BF16),
                        pltpu.VMEM((EXPERT_RING // 2, blk, LANES), F32), pltpu.VMEM((EXPERT_RING // 2, blk, LANES), F32),
                        pltpu.VMEM((blk, W), xs.dtype), pltpu.SMEM((1,), I32),
                        pltpu.SemaphoreType.DMA((EXPERT_RING,)), pltpu.SemaphoreType.DMA((EXPERT_RING,)),
                        pltpu.SemaphoreType.DMA(())],
    )
    return pl.pallas_call(
        _experts_kernel,
        grid_spec=grid_spec,
        out_shape=jax.ShapeDtypeStruct((rows, W), xs.dtype),
        compiler_params=_params(("arbitrary",)),
        name="experts",
    )(gstart, nblk, wg, wu, wd, xs)


def _combine_kernel(*refs, alpha):
    cur_refs, nxt_refs = refs[:TOP_K], refs[TOP_K:2 * TOP_K]
    (wt_ref, h2_ref, x1_ref, mod_ref, wsg_ref, wsu_ref, wsd_ref, g_ref, b_ref, ys_ref, o_ref) = refs[2 * TOP_K:-11]
    buf_refs = refs[-11:-3]
    stage0_ref, stage1_ref, sem = refs[-3:]
    tc, d = h2_ref.shape
    slab = d // LANES
    i = pl.program_id(0)
    last = pl.num_programs(0) - 1

    def row_copy(dest_refs, k, t):
        return pltpu.make_async_copy(_slab_at(ys_ref, dest_refs[k][t], slab), _slab_rows(buf_refs[k], t, slab), sem)

    def wait_planes():
        for k in range(TOP_K):
            pltpu.make_async_copy(ys_ref.at[pl.ds(0, slab * tc)], buf_refs[k], sem).wait()

    @pl.when(i == 0)
    def _():
        def body(t, carry):
            for k in range(TOP_K):
                row_copy(cur_refs, k, t).start(priority=k % 2)
            return carry

        lax.fori_loop(0, tc, body, 0)

    h = h2_ref[...]
    mid = (_silu(_dot(h, wsg_ref[...])) * _dot(h, wsu_ref[...])).astype(BF16)
    y = _dot(mid, wsd_ref[...])
    wait_planes()
    for k in range(TOP_K):
        stage_ref = stage0_ref if k % 2 == 0 else stage1_ref
        y = y + wt_ref[:, k:k + 1] * _load_slabs(stage_ref, buf_refs[k][...], slab)
        for t in range(tc):
            row_copy(nxt_refs, k, t).start(priority=t % 2)
    z = alpha * x1_ref[...] + (1.0 + mod_ref[0, 5:6, :]) * y
    o_ref[...] = _ln_plain(z) * g_ref[...] + b_ref[...]

    @pl.when(i == last)
    def _():
        wait_planes()


def _combine(dests, wt, h2, x1, mod3, wsg, wsu, wsd, ln_g, ln_b, ys, *, seq, tc, alpha):
    N, D = x1.shape
    slab = D // LANES
    per_b = seq // tc
    n_tiles = N // tc
    row = lambda i: (i, 0)
    return pl.pallas_call(
        functools.partial(_combine_kernel, alpha=alpha),
        grid=(n_tiles,),
        in_specs=[pl.BlockSpec((tc,), lambda i: (i,), memory_space=pltpu.SMEM)] * TOP_K
        + [pl.BlockSpec((tc,), lambda i: (jnp.minimum(i + 1, n_tiles - 1),), memory_space=pltpu.SMEM)] * TOP_K
        + [pl.BlockSpec((tc, LANES), row), pl.BlockSpec((tc, D), row), pl.BlockSpec((tc, D), row),
           pl.BlockSpec((1, 6, D), lambda i: (i // per_b, 0, 0)),
           _full(wsg.shape), _full(wsu.shape), _full(wsd.shape), _full(ln_g.shape), _full(ln_b.shape),
           pl.BlockSpec(memory_space=pl.ANY)],
        out_specs=pl.BlockSpec((tc, D), row),
        out_shape=jax.ShapeDtypeStruct((N, D), F32),
        scratch_shapes=[pltpu.VMEM((slab * tc, LANES), ys.dtype)] * TOP_K
        + [_slab_scratch(tc, D), _slab_scratch(tc, D), pltpu.SemaphoreType.DMA(())],
        compiler_params=_params(("arbitrary",)),
        name="combine",
    )(*dests, *dests, wt, h2, x1, mod3, wsg, wsu, wsd, ln_g, ln_b, ys)


def _tiles(seq):
    t = lambda want: min(want, seq)
    return dict(proj=t(512), swa=t(512), gla=t(512), route=t(256), scatter=t(1024), combine=t(256))


def _layer(x2, mod3, posb, invf, p, *, seq, alpha):
    N, D = x2.shape
    tl = _tiles(seq)
    qa, ka, va, ql, kl, vl, gl, lg = _inproj(x2, mod3, posb, invf, p["w_main"], p["b_main"], p["w_lo"], p["b_lo"],
                                             p["w_gk"], p["b_gk"], seq=seq, tm=tl["proj"])
    oa = _swa(qa, ka, va, p["sinks"], seq=seq, tq=tl["swa"])
    og = _gla(ql, kl, vl, gl, lg, p["norm_g"], seq=seq, tc=tl["gla"])
    x1, h2, h2s, logits_t = _outproj(oa, og, x2, mod3, p["w_oa"], p["w_og"], p["b_o"], p["ln1_g"], p["ln1_b"],
                                     p["wr_hi"], p["wr_lo"], seq=seq, tm=tl["proj"], alpha=alpha)
    idx, _, wt, rank, cnt = _route(logits_t, p["bias_col"], p["upper"], tr=tl["route"])
    E = cnt.shape[0]
    n_blk = (N * TOP_K) // EXPERT_ROWS + E
    dest, gstart, nblk = _plan(cnt, p["lower"], idx, rank, tr=tl["route"], slab=D // LANES)
    gstart, nblk = gstart[:, 0], nblk[:, 0]
    dests = [dest[k] for k in range(TOP_K)]
    xs = _scatter(gstart, nblk, dests, h2s, ts=tl["scatter"], slab=D // LANES, n_blk=n_blk)
    ys = _experts(gstart, nblk, xs, p["wg"], p["wu"], p["wd"])
    return _combine(dests, wt, h2, x1, mod3, p["wsg"], p["wsu"], p["wsd"], p["ln2_g"], p["ln2_b"], ys,
                    seq=seq, tc=tl["combine"], alpha=alpha)


def kernel(x, c, positions, w_ada, b_ada, w_in, b_in, attn_sinks, w_gk2, b_gk2, gla_norm_g, w_o, b_o, ln1_g, ln1_b, w_router, router_bias, w_exp_gate, w_exp_up, w_exp_down, w_sh_gate, w_sh_up, w_sh_down, ln2_g, ln2_b):
    B, S, D = x.shape
    depth = w_ada.shape[0]
    E = w_router.shape[2]
    alpha = float((2 * depth) ** 0.25)
    tl = _tiles(S)
    n_main = int(_SEG[-1])

    posb = jnp.broadcast_to(positions.astype(F32).reshape(B * S, 1), (B * S, LANES))
    half = ATTN_HEAD_DIM // 2
    invf = (ROPE_THETA ** (-(jnp.arange(LANES) % half).astype(F32) / half)).reshape(1, LANES)
    tr = tl["route"]
    upper = (jnp.arange(tr)[:, None] < jnp.arange(tr)[None, :]).astype(BF16)
    lower = (jnp.arange(E)[:, None] >= jnp.arange(E)[None, :]).astype(BF16)
    row = lambda v: v.reshape(1, -1)

    x2 = x.reshape(B * S, D)
    for l in range(depth):
        wr_t = w_router[l].T
        wr_hi = wr_t.astype(BF16)
        p = dict(
            w_main=w_in[l][:, :n_main].astype(BF16), b_main=row(b_in[l][:n_main]),
            w_lo=jnp.pad(w_in[l][:, n_main:], ((0, 0), (0, LANES - GLA_GATE_RANK))).astype(BF16),
            b_lo=row(jnp.pad(b_in[l][n_main:], (0, LANES - GLA_GATE_RANK))),
            w_gk=jnp.pad(w_gk2[l], ((0, LANES - GLA_GATE_RANK), (0, 0))), b_gk=row(b_gk2[l]),
            sinks=attn_sinks[l], norm_g=row(gla_norm_g[l]),
            w_oa=w_o[l][:ATTN_WIDTH].astype(BF16), w_og=w_o[l][ATTN_WIDTH:].astype(BF16), b_o=row(b_o[l]),
            ln1_g=row(ln1_g[l]), ln1_b=row(ln1_b[l]),
            wr_hi=wr_hi, wr_lo=(wr_t - wr_hi.astype(F32)).astype(BF16),
            bias_col=jnp.broadcast_to(router_bias[l].reshape(E, 1), (E, LANES)),
            upper=upper, lower=lower,
            wg=w_exp_gate[l], wu=w_exp_up[l], wd=w_exp_down[l],
            wsg=w_sh_gate[l].astype(BF16), wsu=w_sh_up[l].astype(BF16), wsd=w_sh_down[l].astype(BF16),
            ln2_g=row(ln2_g[l]), ln2_b=row(ln2_b[l]),
        )
        mod = _mod(c, w_ada[l], b_ada[l])
        mod3 = mod.reshape(B, 6, D)
        x2 = _layer(x2, mod3, posb, invf, p, seq=S, alpha=alpha)
    return x2.reshape(B, S, D)
```

```python
import functools

import jax
import jax.numpy as jnp
import numpy as np
from jax import lax
from jax.experimental import pallas as pl
from jax.experimental.pallas import tpu as pltpu

F32 = jnp.float32
BF16 = jnp.bfloat16
I32 = jnp.int32

ATTN_Q_HEADS = 8
ATTN_KV_HEADS = 2
ATTN_HEAD_DIM = 64
ATTN_BLOCK = 128
ROPE_THETA = 10000.0
GLA_HEADS = 4
GLA_KEY_DIM = 64
GLA_VAL_DIM = 128
GLA_GATE_RANK = 16
GLA_GATE_NORM = 16.0
GLA_CHUNK = 64
N_GROUPS = 8
TOPK_GROUPS = 4
TOP_K = 8
ROUTED_SCALE = 2.5
LN_EPS = 1e-5
NEG_INF = -1e30
REMOVED = -3e38

ATTN_WIDTH = ATTN_Q_HEADS * ATTN_HEAD_DIM
KV_WIDTH = ATTN_KV_HEADS * ATTN_HEAD_DIM
GLA_KWIDTH = GLA_HEADS * GLA_KEY_DIM
GLA_WIDTH = GLA_HEADS * GLA_VAL_DIM

LANES = 128
VMEM_LIMIT = 56 * 1024 * 1024
PROJ_ROWS = 256
EXPERT_ROWS = 256
COMBINE_CHUNK = 8
EXPERT_RING = 8


def _params(sem):
    return pltpu.CompilerParams(dimension_semantics=sem, vmem_limit_bytes=VMEM_LIMIT)


def _full(shape):
    return pl.BlockSpec(shape, lambda *_: (0,) * len(shape))


def _split_bf16(a):
    hi = a.astype(BF16)
    lo = (a - hi.astype(F32)).astype(BF16)
    return hi, lo


def _dot(a, b, dims=(((1,), (0,)), ((), ()))):
    return lax.dot_general(a, b, dims, preferred_element_type=F32)


NT = (((1,), (1,)), ((), ()))
TN = (((0,), (0,)), ((), ()))


def _dot3(a, b, dims=(((1,), (0,)), ((), ()))):
    ah, al = _split_bf16(a)
    bh, bl = _split_bf16(b)
    return _dot(ah, bh, dims) + (_dot(ah, bl, dims) + _dot(al, bh, dims))


def _ln_plain(x):
    mu = jnp.mean(x, axis=-1, keepdims=True)
    xc = x - mu
    var = jnp.mean(xc * xc, axis=-1, keepdims=True)
    return xc * lax.rsqrt(var + LN_EPS)


def _silu(x):
    return x * (1.0 / (1.0 + jnp.exp(-x)))


def _sigmoid(x):
    return 1.0 / (1.0 + jnp.exp(-x))


def _mod_kernel(c_ref, w_ref, b_ref, o_ref):
    c = c_ref[...]
    o_ref[...] = _dot3(_silu(c), w_ref[...]) + b_ref[...]


def _mod(c, w_ada, b_ada):
    B, D = c.shape
    n = w_ada.shape[1] // D
    return pl.pallas_call(
        _mod_kernel,
        grid=(n,),
        in_specs=[_full((B, D)),
                  pl.BlockSpec((D, D), lambda j: (0, j)),
                  pl.BlockSpec((1, D), lambda j: (0, j))],
        out_specs=pl.BlockSpec((B, D), lambda j: (0, j)),
        out_shape=jax.ShapeDtypeStruct((B, n * D), F32),
        compiler_params=_params(("arbitrary",)),
        name="mod",
    )(c, w_ada, b_ada.reshape(1, -1))


_SEG = np.cumsum([0, ATTN_WIDTH, KV_WIDTH, KV_WIDTH, GLA_KWIDTH, GLA_KWIDTH, GLA_WIDTH, GLA_WIDTH])


def _rope_chunk(t, cos, sin_signed, first_half):
    up = pltpu.roll(t, LANES - 32, axis=1)
    dn = pltpu.roll(t, 32, axis=1)
    return t * cos + jnp.where(first_half, up, dn) * sin_signed


def _inproj_kernel(x_ref, mod_ref, pos_ref, invf_ref, w_ref, b_ref, wlo_ref, blo_ref, wgk_ref, bgk_ref,
                   qa_ref, ka_ref, va_ref, ql_ref, kl_ref, vl_ref, gl_ref, lg_ref):
    for r0 in range(0, x_ref.shape[0], PROJ_ROWS):
        rows = slice(r0, r0 + PROJ_ROWS)
        h = _ln_plain(x_ref[rows, :]) * (1.0 + mod_ref[0, 1:2, :]) + mod_ref[0, 0:1, :]
        hb = h.astype(BF16)

        def seg(i):
            lo, hi = int(_SEG[i]), int(_SEG[i + 1])
            return _dot(hb, w_ref[:, lo:hi]) + b_ref[:, lo:hi]

        ang = pos_ref[rows, :] * invf_ref[...]
        cos = jnp.cos(ang)
        sin = jnp.sin(ang)
        lane = lax.broadcasted_iota(I32, ang.shape, 1)
        first_half = (lane % ATTN_HEAD_DIM) < (ATTN_HEAD_DIM // 2)
        sin_signed = jnp.where(first_half, -sin, sin)

        q = seg(0)
        scale = ATTN_HEAD_DIM ** -0.5
        for c in range(ATTN_WIDTH // LANES):
            t = q[:, c * LANES:(c + 1) * LANES]
            qa_ref[rows, c * LANES:(c + 1) * LANES] = (
                _rope_chunk(t, cos, sin_signed, first_half) * scale).astype(BF16)
        ka_ref[rows, :] = _rope_chunk(seg(1), cos, sin_signed, first_half).astype(BF16)
        va_ref[rows, :] = seg(2).astype(BF16)
        ql_ref[rows, :] = seg(3).astype(BF16)
        kl_ref[rows, :] = seg(4).astype(BF16)
        vl_ref[rows, :] = seg(5).astype(BF16)
        gl_ref[rows, :] = seg(6).astype(BF16)
        gk_lo = _dot(hb, wlo_ref[...]) + blo_ref[...]
        gk = _dot3(gk_lo, wgk_ref[...]) + bgk_ref[...]
        log_sig = jnp.minimum(gk, 0.0) - jnp.log(1.0 + jnp.exp(-jnp.abs(gk)))
        lg_ref[rows, :] = log_sig * (1.0 / GLA_GATE_NORM)


def _inproj(x2, mod3, posb, invf, w_main, b_main, w_lo, b_lo, w_gk, b_gk, *, seq, tm):
    N, D = x2.shape
    per_b = seq // tm
    widths = [ATTN_WIDTH, KV_WIDTH, KV_WIDTH, GLA_KWIDTH, GLA_KWIDTH, GLA_WIDTH, GLA_WIDTH, GLA_KWIDTH]
    dtypes = [BF16] * 7 + [F32]
    row = lambda i: (i, 0)
    return pl.pallas_call(
        _inproj_kernel,
        grid=(N // tm,),
        in_specs=[pl.BlockSpec((tm, D), row),
                  pl.BlockSpec((1, 6, D), lambda i: (i // per_b, 0, 0)),
                  pl.BlockSpec((tm, LANES), row),
                  _full(invf.shape), _full(w_main.shape), _full(b_main.shape),
                  _full(w_lo.shape), _full(b_lo.shape), _full(w_gk.shape), _full(b_gk.shape)],
        out_specs=[pl.BlockSpec((tm, w), row) for w in widths],
        out_shape=[jax.ShapeDtypeStruct((N, w), dt) for w, dt in zip(widths, dtypes)],
        compiler_params=_params(("parallel",)),
        name="inproj",
    )(x2, mod3, posb, invf, w_main, b_main, w_lo, b_lo, w_gk, b_gk)


def _swa_kernel(sink_ref, rep_ref, q_ref, kc_ref, kp_ref, vc_ref, vp_ref, o_ref):
    j = pl.program_id(1)
    blk = ATTN_BLOCK
    row = lax.broadcasted_iota(I32, (blk, 2 * blk), 0)
    col = lax.broadcasted_iota(I32, (blk, 2 * blk), 1)
    dist = row + blk - col
    band = (dist >= 0) & (dist < blk)
    group = ATTN_Q_HEADS // ATTN_KV_HEADS
    hd = ATTN_HEAD_DIM
    gw = group * hd
    kw = 2 * blk
    bd = (lax.broadcasted_iota(I32, (group * kw, gw), 0) // kw) == (lax.broadcasted_iota(I32, (group * kw, gw), 1) // hd)
    lane_head = lax.broadcasted_iota(I32, (blk, gw), 1) // hd
    rep = rep_ref[...]
    k_all = _dot(jnp.concatenate([kp_ref[...], kc_ref[...]], axis=0), rep).astype(BF16)
    v_all = _dot(jnp.concatenate([vp_ref[...], vc_ref[...]], axis=0), rep).astype(BF16)
    for s_blk in range(q_ref.shape[0] // blk):
        rows = slice(s_blk * blk, (s_blk + 1) * blk)
        k2 = k_all[s_blk * blk:(s_blk + 2) * blk]
        v2 = v_all[s_blk * blk:(s_blk + 2) * blk]
        valid = band & ((col >= blk) | (j > 0)) if s_blk == 0 else band
        for g in range(ATTN_KV_HEADS):
            lanes = slice(g * gw, (g + 1) * gw)
            zero = jnp.zeros((), BF16)
            kd = jnp.where(bd, jnp.concatenate([k2[:, lanes]] * group, axis=0), zero)
            vd = jnp.where(bd, jnp.concatenate([v2[:, lanes]] * group, axis=0), zero)
            s_all = _dot(q_ref[rows, lanes], kd, NT)
            probs, denom = [], jnp.zeros((blk, gw), F32)
            for r in range(group):
                sink = sink_ref[g * group + r]
                s = jnp.where(valid, s_all[:, r * kw:(r + 1) * kw], NEG_INF)
                m = jnp.maximum(jnp.max(s, axis=-1, keepdims=True), sink)
                p = jnp.exp(s - m)
                d = jnp.sum(p, axis=-1, keepdims=True) + jnp.exp(sink - m)
                probs.append(p.astype(BF16))
                denom = jnp.where(lane_head == r, d, denom)
            o = _dot(jnp.concatenate(probs, axis=1), vd) / denom
            o_ref[rows, lanes] = o.astype(BF16)


def _swa(q, k, v, sinks, *, seq, tq):
    N = q.shape[0]
    src = jnp.arange(ATTN_WIDTH)
    src = (src // (ATTN_WIDTH // ATTN_KV_HEADS)) * ATTN_HEAD_DIM + src % ATTN_HEAD_DIM
    rep = (jnp.arange(KV_WIDTH)[:, None] == src[None, :]).astype(BF16)
    per_b = seq // tq
    sub = tq // ATTN_BLOCK
    B = N // seq
    cur = lambda b, j: (b * per_b + j, 0)
    prev = lambda b, j: ((b * per_b + j) * sub - jnp.minimum(j, 1), 0)
    return pl.pallas_call(
        _swa_kernel,
        grid=(B, per_b),
        in_specs=[pl.BlockSpec(memory_space=pltpu.SMEM), _full(rep.shape),
                  pl.BlockSpec((tq, ATTN_WIDTH), cur),
                  pl.BlockSpec((tq, KV_WIDTH), cur),
                  pl.BlockSpec((ATTN_BLOCK, KV_WIDTH), prev),
                  pl.BlockSpec((tq, KV_WIDTH), cur),
                  pl.BlockSpec((ATTN_BLOCK, KV_WIDTH), prev)],
        out_specs=pl.BlockSpec((tq, ATTN_WIDTH), cur),
        out_shape=jax.ShapeDtypeStruct((N, ATTN_WIDTH), BF16),
        compiler_params=_params(("parallel", "parallel")),
        name="swa",
    )(sinks, rep, q, k, k, v, v)


def _gla_kernel(q_ref, k_ref, v_ref, g_ref, lg_ref, ng_ref, o_ref, st_ref, *, chunks):
    @pl.when(pl.program_id(1) == 0)
    def _():
        st_ref[...] = jnp.zeros_like(st_ref)

    C = GLA_CHUNK
    dk, dv = GLA_KEY_DIM, GLA_VAL_DIM
    r = lax.broadcasted_iota(I32, (C, C), 0)
    c = lax.broadcasted_iota(I32, (C, C), 1)
    causal = c <= r
    tri = jnp.where(causal, 1.0, 0.0).astype(BF16)
    st = st_ref[...]
    for n in range(chunks):
        rows = slice(n * C, (n + 1) * C)
        lg_hi, lg_lo = _split_bf16(lg_ref[rows, :])
        b = _dot(tri, lg_hi) + _dot(tri, lg_lo)
        b_last = b[C - 1:C, :]
        q_in = (q_ref[rows, :].astype(F32) * (dk ** -0.5) * jnp.exp(b)).astype(BF16)
        kf = k_ref[rows, :].astype(F32)
        k_in = (kf * jnp.exp(-b)).astype(BF16)
        k_out = (kf * jnp.exp(b_last - b)).astype(BF16)
        decay = jnp.exp(b_last)
        stb = st.astype(BF16)
        ut = []
        for h in range(GLA_HEADS):
            ks = slice(h * dk, (h + 1) * dk)
            vs = slice(h * dv, (h + 1) * dv)
            vh = v_ref[rows, vs]
            a = jnp.where(causal, _dot(q_in[:, ks], k_in[:, ks], NT), 0.0).astype(BF16)
            o = _dot(a, vh) + _dot(q_in[:, ks], stb[:, ks], NT)
            ut.append(_dot(vh, k_out[:, ks], TN))
            o = o * lax.rsqrt(jnp.mean(o * o, axis=-1, keepdims=True) + LN_EPS) * ng_ref[...]
            o_ref[rows, vs] = (o * _silu(g_ref[rows, vs].astype(F32))).astype(BF16)
        st = st * decay + jnp.concatenate(ut, axis=1)
    st_ref[...] = st


def _gla(q, k, v, g, lg, norm_g, *, seq, tc):
    N = q.shape[0]
    B = N // seq
    per_b = seq // tc
    row = lambda b, j: (b * per_b + j, 0)
    return pl.pallas_call(
        functools.partial(_gla_kernel, chunks=tc // GLA_CHUNK),
        grid=(B, per_b),
        in_specs=[pl.BlockSpec((tc, GLA_KWIDTH), row), pl.BlockSpec((tc, GLA_KWIDTH), row),
                  pl.BlockSpec((tc, GLA_WIDTH), row), pl.BlockSpec((tc, GLA_WIDTH), row),
                  pl.BlockSpec((tc, GLA_KWIDTH), row), _full(norm_g.shape)],
        out_specs=pl.BlockSpec((tc, GLA_WIDTH), row),
        out_shape=jax.ShapeDtypeStruct((N, GLA_WIDTH), BF16),
        scratch_shapes=[pltpu.VMEM((GLA_VAL_DIM, GLA_KWIDTH), F32)],
        compiler_params=_params(("parallel", "arbitrary")),
        name="gla",
    )(q, k, v, g, lg, norm_g)


def _slab_scratch(rows, d):
    return pltpu.VMEM((rows * (d // LANES), LANES), F32)


def _store_slabs(slabs_ref, stage_ref, v):
    rows, d = v.shape
    n = d // LANES
    for c in range(n):
        stage_ref[pl.ds(c, rows, stride=n), :] = v[:, c * LANES:(c + 1) * LANES]
    slabs_ref[...] = stage_ref[...].astype(slabs_ref.dtype)


def _load_slabs(stage_ref, slabs, n):
    stage_ref[...] = slabs.astype(F32)
    rows = stage_ref.shape[0] // n
    return jnp.concatenate([stage_ref[pl.ds(c, rows, stride=n), :] for c in range(n)], axis=1)


def _outproj_kernel(oa_ref, og_ref, x_ref, mod_ref, woa_ref, wog_ref, bo_ref, g_ref, b_ref, wrh_ref, wrl_ref,
                    x1_ref, h2_ref, h2s_ref, lt_ref, stage_ref, *, alpha):
    y = _dot(oa_ref[...], woa_ref[...]) + _dot(og_ref[...], wog_ref[...]) + bo_ref[...]
    z = alpha * x_ref[...] + (1.0 + mod_ref[0, 2:3, :]) * y
    x1 = _ln_plain(z) * g_ref[...] + b_ref[...]
    x1_ref[...] = x1
    h2 = _ln_plain(x1) * (1.0 + mod_ref[0, 4:5, :]) + mod_ref[0, 3:4, :]
    h2_ref[...] = h2.astype(BF16)
    _store_slabs(h2s_ref, stage_ref, h2)
    hh, hl = _split_bf16(h2)
    wh = wrh_ref[...]
    lt_ref[...] = _dot(wh, hh, NT) + (_dot(wh, hl, NT) + _dot(wrl_ref[...], hh, NT))


def _outproj(oa, og, x2, mod3, w_oa, w_og, b_o, ln_g, ln_b, wr_hi, wr_lo, *, seq, tm, alpha):
    N, D = x2.shape
    E = wr_hi.shape[0]
    per_b = seq // tm
    row = lambda i: (i, 0)
    return pl.pallas_call(
        functools.partial(_outproj_kernel, alpha=alpha),
        grid=(N // tm,),
        in_specs=[pl.BlockSpec((tm, ATTN_WIDTH), row), pl.BlockSpec((tm, GLA_WIDTH), row),
                  pl.BlockSpec((tm, D), row),
                  pl.BlockSpec((1, 6, D), lambda i: (i // per_b, 0, 0)),
                  _full(w_oa.shape), _full(w_og.shape), _full(b_o.shape), _full(ln_g.shape), _full(ln_b.shape),
                  _full(wr_hi.shape), _full(wr_lo.shape)],
        out_specs=[pl.BlockSpec((tm, D), row), pl.BlockSpec((tm, D), row),
                   pl.BlockSpec((tm * (D // LANES), LANES), row), pl.BlockSpec((E, tm), lambda i: (0, i))],
        out_shape=[jax.ShapeDtypeStruct((N, D), F32), jax.ShapeDtypeStruct((N, D), BF16),
                   jax.ShapeDtypeStruct((N * (D // LANES), LANES), BF16), jax.ShapeDtypeStruct((E, N), F32)],
        scratch_shapes=[_slab_scratch(tm, D)],
        compiler_params=_params(("parallel",)),
        name="outproj",
    )(oa, og, x2, mod3, w_oa, w_og, b_o, ln_g, ln_b, wr_hi, wr_lo)


def _first_index(eq, idx, size):
    return jnp.min(jnp.where(eq, idx, float(size)), axis=0, keepdims=True)


def _route_kernel(lt_ref, bias_ref, upper_ref, idx_ref, w_ref, wt_ref, rank_ref, cnt_ref, base_ref):
    @pl.when(pl.program_id(0) == 0)
    def _():
        base_ref[...] = jnp.zeros_like(base_ref)

    E, t = lt_ref.shape
    gsz = E // N_GROUPS
    scores = _sigmoid(lt_ref[...])
    biased = scores + bias_ref[:, 0:1]
    gi = lax.broadcasted_iota(I32, (gsz, t), 0).astype(F32)
    gs_rows = []
    for g in range(N_GROUPS):
        grp = biased[g * gsz:(g + 1) * gsz, :]
        m1 = jnp.max(grp, axis=0, keepdims=True)
        first = _first_index(grp == m1, gi, gsz)
        m2 = jnp.max(jnp.where(gi == first, REMOVED, grp), axis=0, keepdims=True)
        gs_rows.append(m1 + m2)
    gs = jnp.concatenate(gs_rows, axis=0)
    ri = lax.broadcasted_iota(I32, (N_GROUPS, t), 0).astype(F32)
    gsel = jnp.zeros((N_GROUPS, t), F32)
    for _ in range(TOPK_GROUPS):
        m = jnp.max(gs, axis=0, keepdims=True)
        hit = ri == _first_index(gs == m, ri, N_GROUPS)
        gsel = jnp.where(hit, 1.0, gsel)
        gs = jnp.where(hit, REMOVED, gs)
    cand = jnp.concatenate(
        [jnp.where(gsel[g:g + 1, :] > 0.5, biased[g * gsz:(g + 1) * gsz, :], NEG_INF) for g in range(N_GROUPS)],
        axis=0)
    ei = lax.broadcasted_iota(I32, (E, t), 0).astype(F32)
    idx_rows, w_rows, hits = [], [], []
    chosen = jnp.zeros((E, t), F32)
    for _ in range(TOP_K):
        m = jnp.max(cand, axis=0, keepdims=True)
        fi = _first_index(cand == m, ei, E)
        hit = ei == fi
        idx_rows.append(fi)
        w_rows.append(jnp.sum(jnp.where(hit, scores, 0.0), axis=0, keepdims=True))
        hits.append(hit)
        chosen = jnp.where(hit, 1.0, chosen)
        cand = jnp.where(hit, REMOVED, cand)
    w = jnp.concatenate(w_rows, axis=0)
    w = w / jnp.sum(w, axis=0, keepdims=True) * ROUTED_SCALE
    idx_ref[...] = jnp.concatenate(idx_rows, axis=0).astype(I32)
    w_ref[...] = w
    wpad = jnp.concatenate([w, jnp.zeros((LANES - TOP_K, t), F32)], axis=0)
    wt_ref[...] = wpad.T
    prefix = _dot(chosen.astype(BF16), upper_ref[...])
    pos = base_ref[:, 0:1] + prefix
    rank_ref[...] = jnp.concatenate(
        [jnp.sum(jnp.where(hit, pos, 0.0), axis=0, keepdims=True) for hit in hits], axis=0).astype(I32)
    base_ref[...] = base_ref[...] + jnp.sum(chosen, axis=1, keepdims=True)
    cnt_ref[...] = base_ref[...]


def _route(logits_t, bias_col, upper, *, tr):
    E, N = logits_t.shape
    col = lambda i: (0, i)
    return pl.pallas_call(
        _route_kernel,
        grid=(N // tr,),
        in_specs=[pl.BlockSpec((E, tr), col), _full(bias_col.shape), _full(upper.shape)],
        out_specs=[pl.BlockSpec((TOP_K, tr), col), pl.BlockSpec((TOP_K, tr), col),
                   pl.BlockSpec((tr, LANES), lambda i: (i, 0)),
                   pl.BlockSpec((TOP_K, tr), col), _full((E, LANES))],
        out_shape=[jax.ShapeDtypeStruct((TOP_K, N), I32), jax.ShapeDtypeStruct((TOP_K, N), F32),
                   jax.ShapeDtypeStruct((N, LANES), F32),
                   jax.ShapeDtypeStruct((TOP_K, N), I32), jax.ShapeDtypeStruct((E, LANES), F32)],
        scratch_shapes=[pltpu.VMEM((E, LANES), F32)],
        compiler_params=_params(("arbitrary",)),
        name="route",
    )(logits_t, bias_col, upper)


def _plan_kernel(cnt_ref, lower_ref, idx_ref, rank_ref, dest_ref, gstart_ref, nblk_ref, pstart_ref, *, slab):
    E = cnt_ref.shape[0]
    bm = float(EXPERT_ROWS)

    @pl.when(pl.program_id(0) == 0)
    def _():
        cnt = cnt_ref[...]
        nblk = jnp.floor((cnt + (bm - 1.0)) * (1.0 / bm))
        padded = nblk * bm
        hi = jnp.floor(padded * (1.0 / 256.0))
        lo = padded - hi * 256.0
        low = lower_ref[...]
        pend = 256.0 * _dot(low, hi.astype(BF16)) + _dot(low, lo.astype(BF16))
        pstart_ref[...] = pend - padded
        gstart_ref[...] = ((pend - padded) * (1.0 / bm)).astype(I32)
        nblk_ref[...] = nblk.astype(I32)

    t = idx_ref.shape[1]
    ei = lax.broadcasted_iota(I32, (E, t), 0)
    ps = pstart_ref[:, 0:1]
    rows = []
    for k in range(TOP_K):
        hit = ei == idx_ref[k:k + 1, :]
        rows.append(jnp.sum(jnp.where(hit, ps, 0.0), axis=0, keepdims=True))
    dest_ref[...] = (rank_ref[...] + jnp.concatenate(rows, axis=0).astype(I32)) * slab


def _plan(cnt, lower, idx, rank, *, tr, slab):
    E = cnt.shape[0]
    N = idx.shape[1]
    col = lambda i: (0, i)
    return pl.pallas_call(
        functools.partial(_plan_kernel, slab=slab),
        grid=(N // tr,),
        in_specs=[_full(cnt.shape), _full(lower.shape), pl.BlockSpec((TOP_K, tr), col),
                  pl.BlockSpec((TOP_K, tr), col)],
        out_specs=[pl.BlockSpec((TOP_K, tr), col), _full((E, LANES)), _full((E, LANES))],
        out_shape=[jax.ShapeDtypeStruct((TOP_K, N), I32), jax.ShapeDtypeStruct((E, LANES), I32),
                   jax.ShapeDtypeStruct((E, LANES), I32)],
        scratch_shapes=[pltpu.VMEM((E, LANES), F32)],
        compiler_params=_params(("arbitrary",)),
        name="plan",
    )(cnt, lower, idx, rank)


def _slab_rows(ref, row, slab):
    return ref.at[pl.ds(pl.multiple_of(row * slab, slab), slab)]


def _slab_at(ref, first_row, slab):
    return ref.at[pl.ds(pl.multiple_of(first_row, slab), slab)]


def _fill_zero_blocks(zero_ref, out_ref, zsem, gstart_ref, nblk_ref, *, last_of_expert):
    blk = zero_ref.shape[0]
    n_exp = gstart_ref.shape[0]
    n_blk = out_ref.shape[0] // blk
    nused = gstart_ref[n_exp - 1] + nblk_ref[n_exp - 1]
    zero_ref[...] = jnp.zeros_like(zero_ref)

    def blk_copy(b):
        return pltpu.make_async_copy(zero_ref, out_ref.at[pl.ds(pl.multiple_of(b * blk, blk), blk)], zsem)

    def tail(b, carry):
        blk_copy(b).start()
        return carry

    lax.fori_loop(nused, n_blk, tail, 0)
    n_started = n_blk - nused
    if last_of_expert:
        def last(e, n):
            has = nblk_ref[e] > 0

            @pl.when(has)
            def _():
                blk_copy(gstart_ref[e] + nblk_ref[e] - 1).start()

            return n + has.astype(I32)

        n_started = lax.fori_loop(0, n_exp, last, n_started)

    def wait(_, carry):
        blk_copy(0).wait()
        return carry

    lax.fori_loop(0, n_started, wait, 0)


def _scatter_kernel(gstart_ref, nblk_ref, *refs):
    dest_refs = refs[:TOP_K]
    h_ref, xs_ref, zero_ref, sem, zsem = refs[TOP_K:]
    ts = dest_refs[0].shape[0]
    slab = h_ref.shape[0] // ts

    @pl.when(pl.program_id(0) == 0)
    def _():
        _fill_zero_blocks(zero_ref, xs_ref, zsem, gstart_ref, nblk_ref, last_of_expert=True)

    def body(t, carry):
        src = _slab_rows(h_ref, t, slab)
        for k in range(TOP_K):
            pltpu.make_async_copy(src, _slab_at(xs_ref, dest_refs[k][t], slab), sem).start(priority=k % 2)
        return carry

    lax.fori_loop(0, ts, body, 0)
    for _ in range(TOP_K):
        pltpu.make_async_copy(h_ref, xs_ref.at[pl.ds(0, slab * ts)], sem).wait()


def _scatter(gstart, nblk, dests, h2s, *, ts, slab, n_blk):
    rows, W = h2s.shape
    grid_spec = pltpu.PrefetchScalarGridSpec(
        num_scalar_prefetch=2,
        grid=(rows // (slab * ts),),
        in_specs=[pl.BlockSpec((ts,), lambda i, be, nu: (i,), memory_space=pltpu.SMEM)] * TOP_K
        + [pl.BlockSpec((slab * ts, W), lambda i, be, nu: (i, 0))],
        out_specs=pl.BlockSpec(memory_space=pl.ANY),
        scratch_shapes=[pltpu.VMEM((slab * EXPERT_ROWS, W), h2s.dtype), pltpu.SemaphoreType.DMA(()),
                        pltpu.SemaphoreType.DMA(())],
    )
    return pl.pallas_call(
        _scatter_kernel,
        grid_spec=grid_spec,
        out_shape=jax.ShapeDtypeStruct((n_blk * slab * EXPERT_ROWS, W), h2s.dtype),
        compiler_params=_params(("arbitrary",)),
        name="scatter",
    )(gstart, nblk, *dests, h2s)


def _experts_kernel(gstart_ref, nblk_ref, wg_ref, wu_ref, wd_ref, xs_ref, ys_ref, xbuf_ref, ybuf_ref,
                    wgb_ref, wub_ref, wdb_ref, xstage_ref, ystage_ref, zero_ref, front_ref, xsem, ysem, zsem):
    e = pl.program_id(0)
    n_exp = pl.num_programs(0)
    ring = xbuf_ref.shape[0]
    blk = zero_ref.shape[0]
    slab = wgb_ref.shape[0] // LANES
    rows = blk // slab
    nused = gstart_ref[n_exp - 1] + nblk_ref[n_exp - 1]
    g0 = gstart_ref[e]
    n = nblk_ref[e]

    def x_copy(b):
        slot = lax.rem(b, ring)
        return pltpu.make_async_copy(xs_ref.at[pl.ds(pl.multiple_of(b * blk, blk), blk)], xbuf_ref.at[slot],
                                     xsem.at[slot])

    def y_copy(b):
        slot = lax.rem(b, ring)
        return pltpu.make_async_copy(ybuf_ref.at[slot], ys_ref.at[pl.ds(pl.multiple_of(b * blk, blk), blk)],
                                     ysem.at[slot])

    @pl.when(e == 0)
    def _():
        front_ref[0] = 0

    @pl.when(n > 0)
    def _():
        wgb_ref[...] = wg_ref[0].astype(BF16)
        wub_ref[...] = wu_ref[0].astype(BF16)
        wdb_ref[...] = wd_ref[0].astype(BF16)

    def group(g, nb):
        limit = jnp.minimum(nused, g + ring)

        def fetch(b, carry):
            x_copy(b).start()
            return carry

        lax.fori_loop(front_ref[0], limit, fetch, 0)
        front_ref[0] = jnp.maximum(front_ref[0], limit)

        for i in range(nb):
            x_copy(g + i).wait()
        for i in range(nb):
            @pl.when(g + i >= ring)
            def _():
                y_copy(g + i - ring).wait()

        for lo, hi in ([(0, nb)] if nb < 4 else [(0, nb // 2), (nb // 2, nb)]):
            parts = [_load_slabs(xstage_ref.at[i], xbuf_ref[lax.rem(g + i, ring)], slab).astype(BF16)
                     for i in range(lo, hi)]
            x = parts[0] if len(parts) == 1 else jnp.concatenate(parts, axis=0)
            gate = _dot(x, wgb_ref[...])
            up = _dot(x, wub_ref[...])
            y = _dot((_silu(gate) * up).astype(BF16), wdb_ref[...])
            for i in range(lo, hi):
                _store_slabs(ybuf_ref.at[lax.rem(g + i, ring)], ystage_ref.at[i], y[(i - lo) * rows:(i - lo + 1) * rows])
        for i in range(nb):
            y_copy(g + i).start()

    n4 = lax.shift_right_logical(n, 2)
    left = n - 4 * n4

    def quad(j, carry):
        group(g0 + 4 * j, 4)
        return carry

    lax.fori_loop(0, n4, quad, 0)

    @pl.when(left >= 2)
    def _():
        group(g0 + 4 * n4, 2)

    @pl.when(lax.rem(left, 2) == 1)
    def _():
        group(g0 + n - 1, 1)

    @pl.when(e == n_exp - 1)
    def _():
        def drain(b, carry):
            y_copy(b).wait()
            return carry

        lax.fori_loop(jnp.maximum(nused - ring, 0), nused, drain, 0)
        _fill_zero_blocks(zero_ref, ys_ref, zsem, gstart_ref, nblk_ref, last_of_expert=False)


def _experts(gstart, nblk, xs, wg, wu, wd):
    rows, W = xs.shape
    E, D, H = wg.shape
    blk = (D // LANES) * EXPERT_ROWS
    grid_spec = pltpu.PrefetchScalarGridSpec(
        num_scalar_prefetch=2,
        grid=(E,),
        in_specs=[pl.BlockSpec((1, D, H), lambda e, gs, nb: (e, 0, 0)),
                  pl.BlockSpec((1, D, H), lambda e, gs, nb: (e, 0, 0)),
                  pl.BlockSpec((1, H, D), lambda e, gs, nb: (e, 0, 0)),
                  pl.BlockSpec(memory_space=pl.ANY)],
        out_specs=pl.BlockSpec(memory_space=pl.ANY),
        scratch_shapes=[pltpu.VMEM((EXPERT_RING, blk, W), xs.dtype), pltpu.VMEM((EXPERT_RING, blk, W), xs.dtype),
                        pltpu.VMEM((D, H), BF16), pltpu.VMEM((D, H), BF16), pltpu.VMEM((H, D), BF16),
                        pltpu.VMEM((EXPERT_RING // 2, blk, LANES), F32), pltpu.VMEM((EXPERT_RING // 2, blk, LANES), F32),
                        pltpu.VMEM((blk, W), xs.dtype), pltpu.SMEM((1,), I32),
                        pltpu.SemaphoreType.DMA((EXPERT_RING,)), pltpu.SemaphoreType.DMA((EXPERT_RING,)),
                        pltpu.SemaphoreType.DMA(())],
    )
    return pl.pallas_call(
        _experts_kernel,
        grid_spec=grid_spec,
        out_shape=jax.ShapeDtypeStruct((rows, W), xs.dtype),
        compiler_params=_params(("arbitrary",)),
        name="experts",
    )(gstart, nblk, wg, wu, wd, xs)


def _combine_kernel(*refs, alpha):
    cur_refs, nxt_refs = refs[:TOP_K], refs[TOP_K:2 * TOP_K]
    (wt_ref, h2_ref, x1_ref, mod_ref, wsg_ref, wsu_ref, wsd_ref, g_ref, b_ref, ys_ref, o_ref,
     buf0_ref, buf1_ref, acc_ref, stage_ref, sem) = refs[2 * TOP_K:]
    tc, d = h2_ref.shape
    slab = d // LANES
    i = pl.program_id(0)
    last = pl.num_programs(0) - 1
    par = lax.rem(i, 2)
    chunk = COMBINE_CHUNK

    def row_copy(dest_refs, buf_ref, p, k, t):
        return pltpu.make_async_copy(_slab_at(ys_ref, dest_refs[k][t], slab), _slab_rows(buf_ref.at[k], t, slab),
                                     sem.at[p])

    def wait_planes(buf_ref, p):
        for k in range(TOP_K):
            pltpu.make_async_copy(ys_ref.at[pl.ds(0, slab * tc)], buf_ref.at[k], sem.at[p]).wait()

    @pl.when(i == 0)
    def _():
        def body(t, carry):
            for k in range(TOP_K):
                row_copy(cur_refs, buf0_ref, 0, k, t).start(priority=k % 2)
            return carry

        lax.fori_loop(0, tc, body, 0)

    h = h2_ref[...]
    mid = (_silu(_dot(h, wsg_ref[...])) * _dot(h, wsu_ref[...])).astype(BF16)
    acc_ref[...] = _dot(mid, wsd_ref[...])

    def run(rd_ref, wr_ref, p):
        wait_planes(rd_ref, p)

        def it(j, carry):
            t0 = pl.multiple_of(j * chunk, chunk)
            r0 = pl.multiple_of(j * chunk * slab, chunk * slab)
            upd = acc_ref[pl.ds(t0, chunk), :]
            for k in range(TOP_K):
                stage_ref[k] = rd_ref[k, pl.ds(r0, chunk * slab), :].astype(F32)
                rows = jnp.concatenate([stage_ref[k, pl.ds(c, chunk, stride=slab), :] for c in range(slab)], axis=1)
                upd = upd + wt_ref[pl.ds(t0, chunk), k:k + 1] * rows
            acc_ref[pl.ds(t0, chunk), :] = upd
            for t in range(chunk):
                for k in range(TOP_K):
                    row_copy(nxt_refs, wr_ref, 1 - p, k, t0 + t).start(priority=k % 2)
            return carry

        lax.fori_loop(0, tc // chunk, it, 0)

    @pl.when(par == 0)
    def _():
        run(buf0_ref, buf1_ref, 0)

    @pl.when(par == 1)
    def _():
        run(buf1_ref, buf0_ref, 1)

    z = alpha * x1_ref[...] + (1.0 + mod_ref[0, 5:6, :]) * acc_ref[...]
    o_ref[...] = _ln_plain(z) * g_ref[...] + b_ref[...]

    @pl.when((i == last) & (par == 0))
    def _():
        wait_planes(buf1_ref, 1)

    @pl.when((i == last) & (par == 1))
    def _():
        wait_planes(buf0_ref, 0)


def _combine(dests, wt, h2, x1, mod3, wsg, wsu, wsd, ln_g, ln_b, ys, *, seq, tc, alpha):
    N, D = x1.shape
    slab = D // LANES
    per_b = seq // tc
    n_tiles = N // tc
    row = lambda i: (i, 0)
    return pl.pallas_call(
        functools.partial(_combine_kernel, alpha=alpha),
        grid=(n_tiles,),
        in_specs=[pl.BlockSpec((tc,), lambda i: (i,), memory_space=pltpu.SMEM)] * TOP_K
        + [pl.BlockSpec((tc,), lambda i: (jnp.minimum(i + 1, n_tiles - 1),), memory_space=pltpu.SMEM)] * TOP_K
        + [pl.BlockSpec((tc, LANES), row), pl.BlockSpec((tc, D), row), pl.BlockSpec((tc, D), row),
           pl.BlockSpec((1, 6, D), lambda i: (i // per_b, 0, 0)),
           _full(wsg.shape), _full(wsu.shape), _full(wsd.shape), _full(ln_g.shape), _full(ln_b.shape),
           pl.BlockSpec(memory_space=pl.ANY)],
        out_specs=pl.BlockSpec((tc, D), row),
        out_shape=jax.ShapeDtypeStruct((N, D), F32),
        scratch_shapes=[pltpu.VMEM((TOP_K, slab * tc, LANES), ys.dtype), pltpu.VMEM((TOP_K, slab * tc, LANES), ys.dtype),
                        pltpu.VMEM((tc, D), F32), pltpu.VMEM((TOP_K, COMBINE_CHUNK * slab, LANES), F32),
                        pltpu.SemaphoreType.DMA((2,))],
        compiler_params=_params(("arbitrary",)),
        name="combine",
    )(*dests, *dests, wt, h2, x1, mod3, wsg, wsu, wsd, ln_g, ln_b, ys)


def _tiles(seq):
    t = lambda want: min(want, seq)
    return dict(proj=t(512), swa=t(512), gla=t(512), route=t(256), scatter=t(1024), combine=t(256))


def _layer(x2, mod3, posb, invf, p, *, seq, alpha):
    N, D = x2.shape
    tl = _tiles(seq)
    qa, ka, va, ql, kl, vl, gl, lg = _inproj(x2, mod3, posb, invf, p["w_main"], p["b_main"], p["w_lo"], p["b_lo"],
                                             p["w_gk"], p["b_gk"], seq=seq, tm=tl["proj"])
    oa = _swa(qa, ka, va, p["sinks"], seq=seq, tq=tl["swa"])
    og = _gla(ql, kl, vl, gl, lg, p["norm_g"], seq=seq, tc=tl["gla"])
    x1, h2, h2s, logits_t = _outproj(oa, og, x2, mod3, p["w_oa"], p["w_og"], p["b_o"], p["ln1_g"], p["ln1_b"],
                                     p["wr_hi"], p["wr_lo"], seq=seq, tm=tl["proj"], alpha=alpha)
    idx, _, wt, rank, cnt = _route(logits_t, p["bias_col"], p["upper"], tr=tl["route"])
    E = cnt.shape[0]
    n_blk = (N * TOP_K) // EXPERT_ROWS + E
    dest, gstart, nblk = _plan(cnt, p["lower"], idx, rank, tr=tl["route"], slab=D // LANES)
    gstart, nblk = gstart[:, 0], nblk[:, 0]
    dests = [dest[k] for k in range(TOP_K)]
    xs = _scatter(gstart, nblk, dests, h2s, ts=tl["scatter"], slab=D // LANES, n_blk=n_blk)
    ys = _experts(gstart, nblk, xs, p["wg"], p["wu"], p["wd"])
    return _combine(dests, wt, h2, x1, mod3, p["wsg"], p["wsu"], p["wsd"], p["ln2_g"], p["ln2_b"], ys,
                    seq=seq, tc=tl["combine"], alpha=alpha)


def kernel(x, c, positions, w_ada, b_ada, w_in, b_in, attn_sinks, w_gk2, b_gk2, gla_norm_g, w_o, b_o, ln1_g, ln1_b, w_router, router_bias, w_exp_gate, w_exp_up, w_exp_down, w_sh_gate, w_sh_up, w_sh_down, ln2_g, ln2_b):
    B, S, D = x.shape
    depth = w_ada.shape[0]
    E = w_router.shape[2]
    alpha = float((2 * depth) ** 0.25)
    tl = _tiles(S)
    n_main = int(_SEG[-1])

    posb = jnp.broadcast_to(positions.astype(F32).reshape(B * S, 1), (B * S, LANES))
    half = ATTN_HEAD_DIM // 2
    invf = (ROPE_THETA ** (-(jnp.arange(LANES) % half).astype(F32) / half)).reshape(1, LANES)
    tr = tl["route"]
    upper = (jnp.arange(tr)[:, None] < jnp.arange(tr)[None, :]).astype(BF16)
    lower = (jnp.arange(E)[:, None] >= jnp.arange(E)[None, :]).astype(BF16)
    row = lambda v: v.reshape(1, -1)

    x2 = x.reshape(B * S, D)
    for l in range(depth):
        wr_t = w_router[l].T
        wr_hi = wr_t.astype(BF16)
        p = dict(
            w_main=w_in[l][:, :n_main].astype(BF16), b_main=row(b_in[l][:n_main]),
            w_lo=jnp.pad(w_in[l][:, n_main:], ((0, 0), (0, LANES - GLA_GATE_RANK))).astype(BF16),
            b_lo=row(jnp.pad(b_in[l][n_main:], (0, LANES - GLA_GATE_RANK))),
            w_gk=jnp.pad(w_gk2[l], ((0, LANES - GLA_GATE_RANK), (0, 0))), b_gk=row(b_gk2[l]),
            sinks=attn_sinks[l], norm_g=row(gla_norm_g[l]),
            w_oa=w_o[l][:ATTN_WIDTH].astype(BF16), w_og=w_o[l][ATTN_WIDTH:].astype(BF16), b_o=row(b_o[l]),
            ln1_g=row(ln1_g[l]), ln1_b=row(ln1_b[l]),
            wr_hi=wr_hi, wr_lo=(wr_t - wr_hi.astype(F32)).astype(BF16),
            bias_col=jnp.broadcast_to(router_bias[l].reshape(E, 1), (E, LANES)),
            upper=upper, lower=lower,
            wg=w_exp_gate[l], wu=w_exp_up[l], wd=w_exp_down[l],
            wsg=w_sh_gate[l].astype(BF16), wsu=w_sh_up[l].astype(BF16), wsd=w_sh_down[l].astype(BF16),
            ln2_g=row(ln2_g[l]), ln2_b=row(ln2_b[l]),
        )
        mod = _mod(c, w_ada[l], b_ada[l])
        mod3 = mod.reshape(B, 6, D)
        x2 = _layer(x2, mod3, posb, invf, p, seq=S, alpha=alpha)
    return x2.reshape(B, S, D)
```

```python
import functools

import jax
import jax.numpy as jnp
import numpy as np
from jax import lax
from jax.experimental import pallas as pl
from jax.experimental.pallas import tpu as pltpu

F32 = jnp.float32
BF16 = jnp.bfloat16
I32 = jnp.int32

ATTN_Q_HEADS = 8
ATTN_KV_HEADS = 2
ATTN_HEAD_DIM = 64
ATTN_BLOCK = 128
ROPE_THETA = 10000.0
GLA_HEADS = 4
GLA_KEY_DIM = 64
GLA_VAL_DIM = 128
GLA_GATE_RANK = 16
GLA_GATE_NORM = 16.0
GLA_CHUNK = 64
N_GROUPS = 8
TOPK_GROUPS = 4
TOP_K = 8
ROUTED_SCALE = 2.5
LN_EPS = 1e-5
NEG_INF = -1e30
REMOVED = -3e38

ATTN_WIDTH = ATTN_Q_HEADS * ATTN_HEAD_DIM
KV_WIDTH = ATTN_KV_HEADS * ATTN_HEAD_DIM
GLA_KWIDTH = GLA_HEADS * GLA_KEY_DIM
GLA_WIDTH = GLA_HEADS * GLA_VAL_DIM

LANES = 128
VMEM_LIMIT = 56 * 1024 * 1024
PROJ_ROWS = 256
EXPERT_ROWS = 256
COMBINE_CHUNK = 8
EXPERT_RING = 8


def _params(sem):
    return pltpu.CompilerParams(dimension_semantics=sem, vmem_limit_bytes=VMEM_LIMIT)


def _full(shape):
    return pl.BlockSpec(shape, lambda *_: (0,) * len(shape))


def _split_bf16(a):
    hi = a.astype(BF16)
    lo = (a - hi.astype(F32)).astype(BF16)
    return hi, lo


def _dot(a, b, dims=(((1,), (0,)), ((), ()))):
    return lax.dot_general(a, b, dims, preferred_element_type=F32)


NT = (((1,), (1,)), ((), ()))
TN = (((0,), (0,)), ((), ()))


def _dot3(a, b, dims=(((1,), (0,)), ((), ()))):
    ah, al = _split_bf16(a)
    bh, bl = _split_bf16(b)
    return _dot(ah, bh, dims) + (_dot(ah, bl, dims) + _dot(al, bh, dims))


def _ln_plain(x):
    mu = jnp.mean(x, axis=-1, keepdims=True)
    xc = x - mu
    var = jnp.mean(xc * xc, axis=-1, keepdims=True)
    return xc * lax.rsqrt(var + LN_EPS)


def _silu(x):
    return x * (1.0 / (1.0 + jnp.exp(-x)))


def _sigmoid(x):
    return 1.0 / (1.0 + jnp.exp(-x))


def _mod_kernel(c_ref, w_ref, b_ref, o_ref):
    c = c_ref[...]
    o_ref[...] = _dot3(_silu(c), w_ref[...]) + b_ref[...]


def _mod(c, w_ada, b_ada):
    B, D = c.shape
    n = w_ada.shape[1] // D
    return pl.pallas_call(
        _mod_kernel,
        grid=(n,),
        in_specs=[_full((B, D)),
                  pl.BlockSpec((D, D), lambda j: (0, j)),
                  pl.BlockSpec((1, D), lambda j: (0, j))],
        out_specs=pl.BlockSpec((B, D), lambda j: (0, j)),
        out_shape=jax.ShapeDtypeStruct((B, n * D), F32),
        compiler_params=_params(("arbitrary",)),
        name="mod",
    )(c, w_ada, b_ada.reshape(1, -1))


_SEG = np.cumsum([0, ATTN_WIDTH, KV_WIDTH, KV_WIDTH, GLA_KWIDTH, GLA_KWIDTH, GLA_WIDTH, GLA_WIDTH])


def _rope_chunk(t, cos, sin_signed, first_half):
    up = pltpu.roll(t, LANES - 32, axis=1)
    dn = pltpu.roll(t, 32, axis=1)
    return t * cos + jnp.where(first_half, up, dn) * sin_signed


def _inproj_kernel(x_ref, mod_ref, pos_ref, invf_ref, w_ref, b_ref, wlo_ref, blo_ref, wgk_ref, bgk_ref,
                   qa_ref, ka_ref, va_ref, ql_ref, kl_ref, vl_ref, gl_ref, lg_ref):
    for r0 in range(0, x_ref.shape[0], PROJ_ROWS):
        rows = slice(r0, r0 + PROJ_ROWS)
        h = _ln_plain(x_ref[rows, :]) * (1.0 + mod_ref[0, 1:2, :]) + mod_ref[0, 0:1, :]
        hb = h.astype(BF16)

        def seg(i):
            lo, hi = int(_SEG[i]), int(_SEG[i + 1])
            return _dot(hb, w_ref[:, lo:hi]) + b_ref[:, lo:hi]

        ang = pos_ref[rows, :] * invf_ref[...]
        cos = jnp.cos(ang)
        sin = jnp.sin(ang)
        lane = lax.broadcasted_iota(I32, ang.shape, 1)
        first_half = (lane % ATTN_HEAD_DIM) < (ATTN_HEAD_DIM // 2)
        sin_signed = jnp.where(first_half, -sin, sin)

        q = seg(0)
        scale = ATTN_HEAD_DIM ** -0.5
        for c in range(ATTN_WIDTH // LANES):
            t = q[:, c * LANES:(c + 1) * LANES]
            qa_ref[rows, c * LANES:(c + 1) * LANES] = (
                _rope_chunk(t, cos, sin_signed, first_half) * scale).astype(BF16)
        ka_ref[rows, :] = _rope_chunk(seg(1), cos, sin_signed, first_half).astype(BF16)
        va_ref[rows, :] = seg(2).astype(BF16)
        ql_ref[rows, :] = seg(3).astype(BF16)
        kl_ref[rows, :] = seg(4).astype(BF16)
        vl_ref[rows, :] = seg(5).astype(BF16)
        gl_ref[rows, :] = seg(6).astype(BF16)
        gk_lo = _dot(hb, wlo_ref[...]) + blo_ref[...]
        gk = _dot3(gk_lo, wgk_ref[...]) + bgk_ref[...]
        log_sig = jnp.minimum(gk, 0.0) - jnp.log(1.0 + jnp.exp(-jnp.abs(gk)))
        lg_ref[rows, :] = log_sig * (1.0 / GLA_GATE_NORM)


def _inproj(x2, mod3, posb, invf, w_main, b_main, w_lo, b_lo, w_gk, b_gk, *, seq, tm):
    N, D = x2.shape
    per_b = seq // tm
    widths = [ATTN_WIDTH, KV_WIDTH, KV_WIDTH, GLA_KWIDTH, GLA_KWIDTH, GLA_WIDTH, GLA_WIDTH, GLA_KWIDTH]
    dtypes = [BF16] * 7 + [F32]
    row = lambda i: (i, 0)
    return pl.pallas_call(
        _inproj_kernel,
        grid=(N // tm,),
        in_specs=[pl.BlockSpec((tm, D), row),
                  pl.BlockSpec((1, 6, D), lambda i: (i // per_b, 0, 0)),
                  pl.BlockSpec((tm, LANES), row),
                  _full(invf.shape), _full(w_main.shape), _full(b_main.shape),
                  _full(w_lo.shape), _full(b_lo.shape), _full(w_gk.shape), _full(b_gk.shape)],
        out_specs=[pl.BlockSpec((tm, w), row) for w in widths],
        out_shape=[jax.ShapeDtypeStruct((N, w), dt) for w, dt in zip(widths, dtypes)],
        compiler_params=_params(("parallel",)),
        name="inproj",
    )(x2, mod3, posb, invf, w_main, b_main, w_lo, b_lo, w_gk, b_gk)


def _swa_kernel(sink_ref, rep_ref, q_ref, kc_ref, kp_ref, vc_ref, vp_ref, o_ref):
    j = pl.program_id(1)
    blk = ATTN_BLOCK
    row = lax.broadcasted_iota(I32, (blk, 2 * blk), 0)
    col = lax.broadcasted_iota(I32, (blk, 2 * blk), 1)
    dist = row + blk - col
    band = (dist >= 0) & (dist < blk)
    group = ATTN_Q_HEADS // ATTN_KV_HEADS
    hd = ATTN_HEAD_DIM
    gw = group * hd
    kw = 2 * blk
    bd = (lax.broadcasted_iota(I32, (group * kw, gw), 0) // kw) == (lax.broadcasted_iota(I32, (group * kw, gw), 1) // hd)
    lane_head = lax.broadcasted_iota(I32, (blk, gw), 1) // hd
    rep = rep_ref[...]
    k_all = _dot(jnp.concatenate([kp_ref[...], kc_ref[...]], axis=0), rep).astype(BF16)
    v_all = _dot(jnp.concatenate([vp_ref[...], vc_ref[...]], axis=0), rep).astype(BF16)
    for s_blk in range(q_ref.shape[0] // blk):
        rows = slice(s_blk * blk, (s_blk + 1) * blk)
        k2 = k_all[s_blk * blk:(s_blk + 2) * blk]
        v2 = v_all[s_blk * blk:(s_blk + 2) * blk]
        valid = band & ((col >= blk) | (j > 0)) if s_blk == 0 else band
        for g in range(ATTN_KV_HEADS):
            lanes = slice(g * gw, (g + 1) * gw)
            zero = jnp.zeros((), BF16)
            kd = jnp.where(bd, jnp.concatenate([k2[:, lanes]] * group, axis=0), zero)
            vd = jnp.where(bd, jnp.concatenate([v2[:, lanes]] * group, axis=0), zero)
            s_all = _dot(q_ref[rows, lanes], kd, NT)
            probs, denom = [], jnp.zeros((blk, gw), F32)
            for r in range(group):
                sink = sink_ref[g * group + r]
                s = jnp.where(valid, s_all[:, r * kw:(r + 1) * kw], NEG_INF)
                m = jnp.maximum(jnp.max(s, axis=-1, keepdims=True), sink)
                p = jnp.exp(s - m)
                d = jnp.sum(p, axis=-1, keepdims=True) + jnp.exp(sink - m)
                probs.append(p.astype(BF16))
                denom = jnp.where(lane_head == r, d, denom)
            o = _dot(jnp.concatenate(probs, axis=1), vd) / denom
            o_ref[rows, lanes] = o.astype(BF16)


def _swa(q, k, v, sinks, *, seq, tq):
    N = q.shape[0]
    src = jnp.arange(ATTN_WIDTH)
    src = (src // (ATTN_WIDTH // ATTN_KV_HEADS)) * ATTN_HEAD_DIM + src % ATTN_HEAD_DIM
    rep = (jnp.arange(KV_WIDTH)[:, None] == src[None, :]).astype(BF16)
    per_b = seq // tq
    sub = tq // ATTN_BLOCK
    B = N // seq
    cur = lambda b, j: (b * per_b + j, 0)
    prev = lambda b, j: ((b * per_b + j) * sub - jnp.minimum(j, 1), 0)
    return pl.pallas_call(
        _swa_kernel,
        grid=(B, per_b),
        in_specs=[pl.BlockSpec(memory_space=pltpu.SMEM), _full(rep.shape),
                  pl.BlockSpec((tq, ATTN_WIDTH), cur),
                  pl.BlockSpec((tq, KV_WIDTH), cur),
                  pl.BlockSpec((ATTN_BLOCK, KV_WIDTH), prev),
                  pl.BlockSpec((tq, KV_WIDTH), cur),
                  pl.BlockSpec((ATTN_BLOCK, KV_WIDTH), prev)],
        out_specs=pl.BlockSpec((tq, ATTN_WIDTH), cur),
        out_shape=jax.ShapeDtypeStruct((N, ATTN_WIDTH), BF16),
        compiler_params=_params(("parallel", "parallel")),
        name="swa",
    )(sinks, rep, q, k, k, v, v)


def _gla_kernel(q_ref, k_ref, v_ref, g_ref, lg_ref, ng_ref, o_ref, st_ref, *, chunks):
    @pl.when(pl.program_id(1) == 0)
    def _():
        st_ref[...] = jnp.zeros_like(st_ref)

    C = GLA_CHUNK
    H, dk, dv = GLA_HEADS, GLA_KEY_DIM, GLA_VAL_DIM
    r = lax.broadcasted_iota(I32, (C, C), 0)
    c = lax.broadcasted_iota(I32, (C, C), 1)
    tri = jnp.where(c <= r, 1.0, 0.0).astype(BF16)

    def iota(shape, axis):
        return lax.broadcasted_iota(I32, shape, axis)

    causal_all = (iota((C, H * C), 1) % C) <= iota((C, H * C), 0)
    k_diag = (iota((H * C, H * dk), 0) // C) == (iota((H * C, H * dk), 1) // dk)
    v_diag = (iota((H * C, H * dv), 0) // C) == (iota((H * C, H * dv), 1) // dv)
    st_diag = (iota((H * dv, H * dk), 0) // dv) == (iota((H * dv, H * dk), 1) // dk)
    zero = jnp.zeros((), BF16)
    st = st_ref[...]
    for n in range(chunks):
        rows = slice(n * C, (n + 1) * C)
        lg_hi, lg_lo = _split_bf16(lg_ref[rows, :])
        b = _dot(tri, lg_hi) + _dot(tri, lg_lo)
        b_last = b[C - 1:C, :]
        q_in = (q_ref[rows, :].astype(F32) * (dk ** -0.5) * jnp.exp(b)).astype(BF16)
        kf = k_ref[rows, :].astype(F32)
        k_in = (kf * jnp.exp(-b)).astype(BF16)
        k_out = (kf * jnp.exp(b_last - b)).astype(BF16)
        decay = jnp.exp(b_last)
        v = v_ref[rows, :]
        kd = jnp.where(k_diag, jnp.concatenate([k_in] * H, axis=0), zero)
        vd = jnp.where(v_diag, jnp.concatenate([v] * H, axis=0), zero)
        a = jnp.where(causal_all, _dot(q_in, kd, NT), 0.0).astype(BF16)
        o = _dot(a, vd) + _dot(q_in, st.astype(BF16), NT)
        st = st * decay + jnp.where(st_diag, _dot(v, k_out, TN), 0.0)
        for h in range(H):
            vs = slice(h * dv, (h + 1) * dv)
            oh = o[:, vs]
            oh = oh * lax.rsqrt(jnp.mean(oh * oh, axis=-1, keepdims=True) + LN_EPS) * ng_ref[...]
            o_ref[rows, vs] = (oh * _silu(g_ref[rows, vs].astype(F32))).astype(BF16)
    st_ref[...] = st


def _gla(q, k, v, g, lg, norm_g, *, seq, tc):
    N = q.shape[0]
    B = N // seq
    per_b = seq // tc
    row = lambda b, j: (b * per_b + j, 0)
    return pl.pallas_call(
        functools.partial(_gla_kernel, chunks=tc // GLA_CHUNK),
        grid=(B, per_b),
        in_specs=[pl.BlockSpec((tc, GLA_KWIDTH), row), pl.BlockSpec((tc, GLA_KWIDTH), row),
                  pl.BlockSpec((tc, GLA_WIDTH), row), pl.BlockSpec((tc, GLA_WIDTH), row),
                  pl.BlockSpec((tc, GLA_KWIDTH), row), _full(norm_g.shape)],
        out_specs=pl.BlockSpec((tc, GLA_WIDTH), row),
        out_shape=jax.ShapeDtypeStruct((N, GLA_WIDTH), BF16),
        scratch_shapes=[pltpu.VMEM((GLA_WIDTH, GLA_KWIDTH), F32)],
        compiler_params=_params(("parallel", "arbitrary")),
        name="gla",
    )(q, k, v, g, lg, norm_g)


def _slab_scratch(rows, d):
    return pltpu.VMEM((rows * (d // LANES), LANES), F32)


def _store_slabs(slabs_ref, stage_ref, v):
    rows, d = v.shape
    n = d // LANES
    for c in range(n):
        stage_ref[pl.ds(c, rows, stride=n), :] = v[:, c * LANES:(c + 1) * LANES]
    slabs_ref[...] = stage_ref[...].astype(slabs_ref.dtype)


def _load_slabs(stage_ref, slabs, n):
    stage_ref[...] = slabs.astype(F32)
    rows = stage_ref.shape[0] // n
    return jnp.concatenate([stage_ref[pl.ds(c, rows, stride=n), :] for c in range(n)], axis=1)


def _outproj_kernel(oa_ref, og_ref, x_ref, mod_ref, woa_ref, wog_ref, bo_ref, g_ref, b_ref, wrh_ref, wrl_ref,
                    x1_ref, h2_ref, h2s_ref, lt_ref, stage_ref, *, alpha):
    y = _dot(oa_ref[...], woa_ref[...]) + _dot(og_ref[...], wog_ref[...]) + bo_ref[...]
    z = alpha * x_ref[...] + (1.0 + mod_ref[0, 2:3, :]) * y
    x1 = _ln_plain(z) * g_ref[...] + b_ref[...]
    x1_ref[...] = x1
    h2 = _ln_plain(x1) * (1.0 + mod_ref[0, 4:5, :]) + mod_ref[0, 3:4, :]
    h2_ref[...] = h2.astype(BF16)
    _store_slabs(h2s_ref, stage_ref, h2)
    hh, hl = _split_bf16(h2)
    wh = wrh_ref[...]
    lt_ref[...] = _dot(wh, hh, NT) + (_dot(wh, hl, NT) + _dot(wrl_ref[...], hh, NT))


def _outproj(oa, og, x2, mod3, w_oa, w_og, b_o, ln_g, ln_b, wr_hi, wr_lo, *, seq, tm, alpha):
    N, D = x2.shape
    E = wr_hi.shape[0]
    per_b = seq // tm
    row = lambda i: (i, 0)
    return pl.pallas_call(
        functools.partial(_outproj_kernel, alpha=alpha),
        grid=(N // tm,),
        in_specs=[pl.BlockSpec((tm, ATTN_WIDTH), row), pl.BlockSpec((tm, GLA_WIDTH), row),
                  pl.BlockSpec((tm, D), row),
                  pl.BlockSpec((1, 6, D), lambda i: (i // per_b, 0, 0)),
                  _full(w_oa.shape), _full(w_og.shape), _full(b_o.shape), _full(ln_g.shape), _full(ln_b.shape),
                  _full(wr_hi.shape), _full(wr_lo.shape)],
        out_specs=[pl.BlockSpec((tm, D), row), pl.BlockSpec((tm, D), row),
                   pl.BlockSpec((tm * (D // LANES), LANES), row), pl.BlockSpec((E, tm), lambda i: (0, i))],
        out_shape=[jax.ShapeDtypeStruct((N, D), F32), jax.ShapeDtypeStruct((N, D), BF16),
                   jax.ShapeDtypeStruct((N * (D // LANES), LANES), BF16), jax.ShapeDtypeStruct((E, N), F32)],
        scratch_shapes=[_slab_scratch(tm, D)],
        compiler_params=_params(("parallel",)),
        name="outproj",
    )(oa, og, x2, mod3, w_oa, w_og, b_o, ln_g, ln_b, wr_hi, wr_lo)


def _first_index(eq, idx, size):
    return jnp.min(jnp.where(eq, idx, float(size)), axis=0, keepdims=True)


def _route_kernel(lt_ref, bias_ref, upper_ref, idx_ref, w_ref, wt_ref, rank_ref, cnt_ref, base_ref):
    @pl.when(pl.program_id(0) == 0)
    def _():
        base_ref[...] = jnp.zeros_like(base_ref)

    E, t = lt_ref.shape
    gsz = E // N_GROUPS
    scores = _sigmoid(lt_ref[...])
    biased = scores + bias_ref[:, 0:1]
    gi = lax.broadcasted_iota(I32, (gsz, t), 0).astype(F32)
    gs_rows = []
    for g in range(N_GROUPS):
        grp = biased[g * gsz:(g + 1) * gsz, :]
        m1 = jnp.max(grp, axis=0, keepdims=True)
        first = _first_index(grp == m1, gi, gsz)
        m2 = jnp.max(jnp.where(gi == first, REMOVED, grp), axis=0, keepdims=True)
        gs_rows.append(m1 + m2)
    gs = jnp.concatenate(gs_rows, axis=0)
    ri = lax.broadcasted_iota(I32, (N_GROUPS, t), 0).astype(F32)
    gsel = jnp.zeros((N_GROUPS, t), F32)
    for _ in range(TOPK_GROUPS):
        m = jnp.max(gs, axis=0, keepdims=True)
        hit = ri == _first_index(gs == m, ri, N_GROUPS)
        gsel = jnp.where(hit, 1.0, gsel)
        gs = jnp.where(hit, REMOVED, gs)
    cand = jnp.concatenate(
        [jnp.where(gsel[g:g + 1, :] > 0.5, biased[g * gsz:(g + 1) * gsz, :], NEG_INF) for g in range(N_GROUPS)],
        axis=0)
    ei = lax.broadcasted_iota(I32, (E, t), 0).astype(F32)
    idx_rows, w_rows, hits = [], [], []
    chosen = jnp.zeros((E, t), F32)
    for _ in range(TOP_K):
        m = jnp.max(cand, axis=0, keepdims=True)
        fi = _first_index(cand == m, ei, E)
        hit = ei == fi
        idx_rows.append(fi)
        w_rows.append(jnp.sum(jnp.where(hit, scores, 0.0), axis=0, keepdims=True))
        hits.append(hit)
        chosen = jnp.where(hit, 1.0, chosen)
        cand = jnp.where(hit, REMOVED, cand)
    w = jnp.concatenate(w_rows, axis=0)
    w = w / jnp.sum(w, axis=0, keepdims=True) * ROUTED_SCALE
    idx_ref[...] = jnp.concatenate(idx_rows, axis=0).astype(I32)
    w_ref[...] = w
    wpad = jnp.concatenate([w, jnp.zeros((LANES - TOP_K, t), F32)], axis=0)
    wt_ref[...] = wpad.T
    prefix = _dot(chosen.astype(BF16), upper_ref[...])
    pos = base_ref[:, 0:1] + prefix
    rank_ref[...] = jnp.concatenate(
        [jnp.sum(jnp.where(hit, pos, 0.0), axis=0, keepdims=True) for hit in hits], axis=0).astype(I32)
    base_ref[...] = base_ref[...] + jnp.sum(chosen, axis=1, keepdims=True)
    cnt_ref[...] = base_ref[...]


def _route(logits_t, bias_col, upper, *, tr):
    E, N = logits_t.shape
    col = lambda i: (0, i)
    return pl.pallas_call(
        _route_kernel,
        grid=(N // tr,),
        in_specs=[pl.BlockSpec((E, tr), col), _full(bias_col.shape), _full(upper.shape)],
        out_specs=[pl.BlockSpec((TOP_K, tr), col), pl.BlockSpec((TOP_K, tr), col),
                   pl.BlockSpec((tr, LANES), lambda i: (i, 0)),
                   pl.BlockSpec((TOP_K, tr), col), _full((E, LANES))],
        out_shape=[jax.ShapeDtypeStruct((TOP_K, N), I32), jax.ShapeDtypeStruct((TOP_K, N), F32),
                   jax.ShapeDtypeStruct((N, LANES), F32),
                   jax.ShapeDtypeStruct((TOP_K, N), I32), jax.ShapeDtypeStruct((E, LANES), F32)],
        scratch_shapes=[pltpu.VMEM((E, LANES), F32)],
        compiler_params=_params(("arbitrary",)),
        name="route",
    )(logits_t, bias_col, upper)


def _plan_kernel(cnt_ref, lower_ref, idx_ref, rank_ref, dest_ref, gstart_ref, nblk_ref, pstart_ref, *, slab):
    E = cnt_ref.shape[0]
    bm = float(EXPERT_ROWS)

    @pl.when(pl.program_id(0) == 0)
    def _():
        cnt = cnt_ref[...]
        nblk = jnp.floor((cnt + (bm - 1.0)) * (1.0 / bm))
        padded = nblk * bm
        hi = jnp.floor(padded * (1.0 / 256.0))
        lo = padded - hi * 256.0
        low = lower_ref[...]
        pend = 256.0 * _dot(low, hi.astype(BF16)) + _dot(low, lo.astype(BF16))
        pstart_ref[...] = pend - padded
        gstart_ref[...] = ((pend - padded) * (1.0 / bm)).astype(I32)
        nblk_ref[...] = nblk.astype(I32)

    t = idx_ref.shape[1]
    ei = lax.broadcasted_iota(I32, (E, t), 0)
    ps = pstart_ref[:, 0:1]
    rows = []
    for k in range(TOP_K):
        hit = ei == idx_ref[k:k + 1, :]
        rows.append(jnp.sum(jnp.where(hit, ps, 0.0), axis=0, keepdims=True))
    dest_ref[...] = (rank_ref[...] + jnp.concatenate(rows, axis=0).astype(I32)) * slab


def _plan(cnt, lower, idx, rank, *, tr, slab):
    E = cnt.shape[0]
    N = idx.shape[1]
    col = lambda i: (0, i)
    return pl.pallas_call(
        functools.partial(_plan_kernel, slab=slab),
        grid=(N // tr,),
        in_specs=[_full(cnt.shape), _full(lower.shape), pl.BlockSpec((TOP_K, tr), col),
                  pl.BlockSpec((TOP_K, tr), col)],
        out_specs=[pl.BlockSpec((TOP_K, tr), col), _full((E, LANES)), _full((E, LANES))],
        out_shape=[jax.ShapeDtypeStruct((TOP_K, N), I32), jax.ShapeDtypeStruct((E, LANES), I32),
                   jax.ShapeDtypeStruct((E, LANES), I32)],
        scratch_shapes=[pltpu.VMEM((E, LANES), F32)],
        compiler_params=_params(("arbitrary",)),
        name="plan",
    )(cnt, lower, idx, rank)


def _slab_rows(ref, row, slab):
    return ref.at[pl.ds(pl.multiple_of(row * slab, slab), slab)]


def _slab_at(ref, first_row, slab):
    return ref.at[pl.ds(pl.multiple_of(first_row, slab), slab)]


def _fill_zero_blocks(zero_ref, out_ref, zsem, gstart_ref, nblk_ref, *, last_of_expert):
    blk = zero_ref.shape[0]
    n_exp = gstart_ref.shape[0]
    n_blk = out_ref.shape[0] // blk
    nused = gstart_ref[n_exp - 1] + nblk_ref[n_exp - 1]
    zero_ref[...] = jnp.zeros_like(zero_ref)

    def blk_copy(b):
        return pltpu.make_async_copy(zero_ref, out_ref.at[pl.ds(pl.multiple_of(b * blk, blk), blk)], zsem)

    def tail(b, carry):
        blk_copy(b).start()
        return carry

    lax.fori_loop(nused, n_blk, tail, 0)
    n_started = n_blk - nused
    if last_of_expert:
        def last(e, n):
            has = nblk_ref[e] > 0

            @pl.when(has)
            def _():
                blk_copy(gstart_ref[e] + nblk_ref[e] - 1).start()

            return n + has.astype(I32)

        n_started = lax.fori_loop(0, n_exp, last, n_started)

    def wait(_, carry):
        blk_copy(0).wait()
        return carry

    lax.fori_loop(0, n_started, wait, 0)


def _scatter_kernel(gstart_ref, nblk_ref, *refs):
    dest_refs = refs[:TOP_K]
    h_ref, xs_ref, zero_ref, sem, zsem = refs[TOP_K:]
    ts = dest_refs[0].shape[0]
    slab = h_ref.shape[0] // ts

    @pl.when(pl.program_id(0) == 0)
    def _():
        _fill_zero_blocks(zero_ref, xs_ref, zsem, gstart_ref, nblk_ref, last_of_expert=True)

    def body(t, carry):
        src = _slab_rows(h_ref, t, slab)
        for k in range(TOP_K):
            pltpu.make_async_copy(src, _slab_at(xs_ref, dest_refs[k][t], slab), sem).start(priority=k % 2)
        return carry

    lax.fori_loop(0, ts, body, 0)
    for _ in range(TOP_K):
        pltpu.make_async_copy(h_ref, xs_ref.at[pl.ds(0, slab * ts)], sem).wait()


def _scatter(gstart, nblk, dests, h2s, *, ts, slab, n_blk):
    rows, W = h2s.shape
    grid_spec = pltpu.PrefetchScalarGridSpec(
        num_scalar_prefetch=2,
        grid=(rows // (slab * ts),),
        in_specs=[pl.BlockSpec((ts,), lambda i, be, nu: (i,), memory_space=pltpu.SMEM)] * TOP_K
        + [pl.BlockSpec((slab * ts, W), lambda i, be, nu: (i, 0))],
        out_specs=pl.BlockSpec(memory_space=pl.ANY),
        scratch_shapes=[pltpu.VMEM((slab * EXPERT_ROWS, W), h2s.dtype), pltpu.SemaphoreType.DMA(()),
                        pltpu.SemaphoreType.DMA(())],
    )
    return pl.pallas_call(
        _scatter_kernel,
        grid_spec=grid_spec,
        out_shape=jax.ShapeDtypeStruct((n_blk * slab * EXPERT_ROWS, W), h2s.dtype),
        compiler_params=_params(("arbitrary",)),
        name="scatter",
    )(gstart, nblk, *dests, h2s)


def _experts_kernel(gstart_ref, nblk_ref, wg_ref, wu_ref, wd_ref, xs_ref, ys_ref, xbuf_ref, ybuf_ref,
                    wgb_ref, wub_ref, wdb_ref, xstage_ref, ystage_ref, zero_ref, front_ref, xsem, ysem, zsem):
    e = pl.program_id(0)
    n_exp = pl.num_programs(0)
    ring = xbuf_ref.shape[0]
    blk = zero_ref.shape[0]
    slab = wgb_ref.shape[0] // LANES
    rows = blk // slab
    nused = gstart_ref[n_exp - 1] + nblk_ref[n_exp - 1]
    g0 = gstart_ref[e]
    n = nblk_ref[e]

    def x_copy(b):
        slot = lax.rem(b, ring)
        return pltpu.make_async_copy(xs_ref.at[pl.ds(pl.multiple_of(b * blk, blk), blk)], xbuf_ref.at[slot],
                                     xsem.at[slot])

    def y_copy(b):
        slot = lax.rem(b, ring)
        return pltpu.make_async_copy(ybuf_ref.at[slot], ys_ref.at[pl.ds(pl.multiple_of(b * blk, blk), blk)],
                                     ysem.at[slot])

    @pl.when(e == 0)
    def _():
        front_ref[0] = 0

    @pl.when(n > 0)
    def _():
        wgb_ref[...] = wg_ref[0].astype(BF16)
        wub_ref[...] = wu_ref[0].astype(BF16)
        wdb_ref[...] = wd_ref[0].astype(BF16)

    def group(g, nb):
        limit = jnp.minimum(nused, g + ring)

        def fetch(b, carry):
            x_copy(b).start()
            return carry

        lax.fori_loop(front_ref[0], limit, fetch, 0)
        front_ref[0] = jnp.maximum(front_ref[0], limit)

        for i in range(nb):
            x_copy(g + i).wait()
        for i in range(nb):
            @pl.when(g + i >= ring)
            def _():
                y_copy(g + i - ring).wait()

        for lo, hi in ([(0, nb)] if nb < 4 else [(0, nb // 2), (nb // 2, nb)]):
            parts = [_load_slabs(xstage_ref.at[i], xbuf_ref[lax.rem(g + i, ring)], slab).astype(BF16)
                     for i in range(lo, hi)]
            x = parts[0] if len(parts) == 1 else jnp.concatenate(parts, axis=0)
            gate = _dot(x, wgb_ref[...])
            up = _dot(x, wub_ref[...])
            y = _dot((_silu(gate) * up).astype(BF16), wdb_ref[...])
            for i in range(lo, hi):
                _store_slabs(ybuf_ref.at[lax.rem(g + i, ring)], ystage_ref.at[i], y[(i - lo) * rows:(i - lo + 1) * rows])
        for i in range(nb):
            y_copy(g + i).start()

    n4 = lax.shift_right_logical(n, 2)
    left = n - 4 * n4

    def quad(j, carry):
        group(g0 + 4 * j, 4)
        return carry

    lax.fori_loop(0, n4, quad, 0)

    @pl.when(left >= 2)
    def _():
        group(g0 + 4 * n4, 2)

    @pl.when(lax.rem(left, 2) == 1)
    def _():
        group(g0 + n - 1, 1)

    @pl.when(e == n_exp - 1)
    def _():
        def drain(b, carry):
            y_copy(b).wait()
            return carry

        lax.fori_loop(jnp.maximum(nused - ring, 0), nused, drain, 0)
        _fill_zero_blocks(zero_ref, ys_ref, zsem, gstart_ref, nblk_ref, last_of_expert=False)


def _experts(gstart, nblk, xs, wg, wu, wd):
    rows, W = xs.shape
    E, D, H = wg.shape
    blk = (D // LANES) * EXPERT_ROWS
    grid_spec = pltpu.PrefetchScalarGridSpec(
        num_scalar_prefetch=2,
        grid=(E,),
        in_specs=[pl.BlockSpec((1, D, H), lambda e, gs, nb: (e, 0, 0)),
                  pl.BlockSpec((1, D, H), lambda e, gs, nb: (e, 0, 0)),
                  pl.BlockSpec((1, H, D), lambda e, gs, nb: (e, 0, 0)),
                  pl.BlockSpec(memory_space=pl.ANY)],
        out_specs=pl.BlockSpec(memory_space=pl.ANY),
        scratch_shapes=[pltpu.VMEM((EXPERT_RING, blk, W), xs.dtype), pltpu.VMEM((EXPERT_RING, blk, W), xs.dtype),
                        pltpu.VMEM((D, H), BF16), pltpu.VMEM((D, H), BF16), pltpu.VMEM((H, D), BF16),
                        pltpu.VMEM((EXPERT_RING // 2, blk, LANES), F32), pltpu.VMEM((EXPERT_RING // 2, blk, LANES), F32),
                        pltpu.VMEM((blk, W), xs.dtype), pltpu.SMEM((1,), I32),
                        pltpu.SemaphoreType.DMA((EXPERT_RING,)), pltpu.SemaphoreType.DMA((EXPERT_RING,)),
                        pltpu.SemaphoreType.DMA(())],
    )
    return pl.pallas_call(
        _experts_kernel,
        grid_spec=grid_spec,
        out_shape=jax.ShapeDtypeStruct((rows, W), xs.dtype),
        compiler_params=_params(("arbitrary",)),
        name="experts",
    )(gstart, nblk, wg, wu, wd, xs)


def _combine_kernel(*refs, alpha):
    cur_refs, nxt_refs = refs[:TOP_K], refs[TOP_K:2 * TOP_K]
    (wt_ref, h2_ref, x1_ref, mod_ref, wsg_ref, wsu_ref, wsd_ref, g_ref, b_ref, ys_ref, o_ref,
     buf0_ref, buf1_ref, acc_ref, stage_ref, sem) = refs[2 * TOP_K:]
    tc, d = h2_ref.shape
    slab = d // LANES
    i = pl.program_id(0)
    last = pl.num_programs(0) - 1
    par = lax.rem(i, 2)
    chunk = COMBINE_CHUNK

    def row_copy(dest_refs, buf_ref, p, k, t):
        return pltpu.make_async_copy(_slab_at(ys_ref, dest_refs[k][t], slab), _slab_rows(buf_ref.at[k], t, slab),
                                     sem.at[p])

    def wait_planes(buf_ref, p):
        for k in range(TOP_K):
            pltpu.make_async_copy(ys_ref.at[pl.ds(0, slab * tc)], buf_ref.at[k], sem.at[p]).wait()

    @pl.when(i == 0)
    def _():
        def body(t, carry):
            for k in range(TOP_K):
                row_copy(cur_refs, buf0_ref, 0, k, t).start(priority=k % 2)
            return carry

        lax.fori_loop(0, tc, body, 0)

    h = h2_ref[...]
    mid = (_silu(_dot(h, wsg_ref[...])) * _dot(h, wsu_ref[...])).astype(BF16)
    acc_ref[...] = _dot(mid, wsd_ref[...])

    def run(rd_ref, wr_ref, p):
        wait_planes(rd_ref, p)

        def it(j, carry):
            t0 = pl.multiple_of(j * chunk, chunk)
            r0 = pl.multiple_of(j * chunk * slab, chunk * slab)
            upd = acc_ref[pl.ds(t0, chunk), :]
            for k in range(TOP_K):
                stage_ref[k] = rd_ref[k, pl.ds(r0, chunk * slab), :].astype(F32)
                rows = jnp.concatenate([stage_ref[k, pl.ds(c, chunk, stride=slab), :] for c in range(slab)], axis=1)
                upd = upd + wt_ref[pl.ds(t0, chunk), k:k + 1] * rows
            acc_ref[pl.ds(t0, chunk), :] = upd
            for t in range(chunk):
                for k in range(TOP_K):
                    row_copy(nxt_refs, wr_ref, 1 - p, k, t0 + t).start(priority=k % 2)
            return carry

        lax.fori_loop(0, tc // chunk, it, 0)

    @pl.when(par == 0)
    def _():
        run(buf0_ref, buf1_ref, 0)

    @pl.when(par == 1)
    def _():
        run(buf1_ref, buf0_ref, 1)

    z = alpha * x1_ref[...] + (1.0 + mod_ref[0, 5:6, :]) * acc_ref[...]
    o_ref[...] = _ln_plain(z) * g_ref[...] + b_ref[...]

    @pl.when((i == last) & (par == 0))
    def _():
        wait_planes(buf1_ref, 1)

    @pl.when((i == last) & (par == 1))
    def _():
        wait_planes(buf0_ref, 0)


def _combine(dests, wt, h2, x1, mod3, wsg, wsu, wsd, ln_g, ln_b, ys, *, seq, tc, alpha):
    N, D = x1.shape
    slab = D // LANES
    per_b = seq // tc
    n_tiles = N // tc
    row = lambda i: (i, 0)
    return pl.pallas_call(
        functools.partial(_combine_kernel, alpha=alpha),
        grid=(n_tiles,),
        in_specs=[pl.BlockSpec((tc,), lambda i: (i,), memory_space=pltpu.SMEM)] * TOP_K
        + [pl.BlockSpec((tc,), lambda i: (jnp.minimum(i + 1, n_tiles - 1),), memory_space=pltpu.SMEM)] * TOP_K
        + [pl.BlockSpec((tc, LANES), row), pl.BlockSpec((tc, D), row), pl.BlockSpec((tc, D), row),
           pl.BlockSpec((1, 6, D), lambda i: (i // per_b, 0, 0)),
           _full(wsg.shape), _full(wsu.shape), _full(wsd.shape), _full(ln_g.shape), _full(ln_b.shape),
           pl.BlockSpec(memory_space=pl.ANY)],
        out_specs=pl.BlockSpec((tc, D), row),
        out_shape=jax.ShapeDtypeStruct((N, D), F32),
        scratch_shapes=[pltpu.VMEM((TOP_K, slab * tc, LANES), ys.dtype), pltpu.VMEM((TOP_K, slab * tc, LANES), ys.dtype),
                        pltpu.VMEM((tc, D), F32), pltpu.VMEM((TOP_K, COMBINE_CHUNK * slab, LANES), F32),
                        pltpu.SemaphoreType.DMA((2,))],
        compiler_params=_params(("arbitrary",)),
        name="combine",
    )(*dests, *dests, wt, h2, x1, mod3, wsg, wsu, wsd, ln_g, ln_b, ys)


def _tiles(seq):
    t = lambda want: min(want, seq)
    return dict(proj=t(512), swa=t(512), gla=t(512), route=t(256), scatter=t(1024), combine=t(256))


def _layer(x2, mod3, posb, invf, p, *, seq, alpha):
    N, D = x2.shape
    tl = _tiles(seq)
    qa, ka, va, ql, kl, vl, gl, lg = _inproj(x2, mod3, posb, invf, p["w_main"], p["b_main"], p["w_lo"], p["b_lo"],
                                             p["w_gk"], p["b_gk"], seq=seq, tm=tl["proj"])
    oa = _swa(qa, ka, va, p["sinks"], seq=seq, tq=tl["swa"])
    og = _gla(ql, kl, vl, gl, lg, p["norm_g"], seq=seq, tc=tl["gla"])
    x1, h2, h2s, logits_t = _outproj(oa, og, x2, mod3, p["w_oa"], p["w_og"], p["b_o"], p["ln1_g"], p["ln1_b"],
                                     p["wr_hi"], p["wr_lo"], seq=seq, tm=tl["proj"], alpha=alpha)
    idx, _, wt, rank, cnt = _route(logits_t, p["bias_col"], p["upper"], tr=tl["route"])
    E = cnt.shape[0]
    n_blk = (N * TOP_K) // EXPERT_ROWS + E
    dest, gstart, nblk = _plan(cnt, p["lower"], idx, rank, tr=tl["route"], slab=D // LANES)
    gstart, nblk = gstart[:, 0], nblk[:, 0]
    dests = [dest[k] for k in range(TOP_K)]
    xs = _scatter(gstart, nblk, dests, h2s, ts=tl["scatter"], slab=D // LANES, n_blk=n_blk)
    ys = _experts(gstart, nblk, xs, p["wg"], p["wu"], p["wd"])
    return _combine(dests, wt, h2, x1, mod3, p["wsg"], p["wsu"], p["wsd"], p["ln2_g"], p["ln2_b"], ys,
                    seq=seq, tc=tl["combine"], alpha=alpha)


def kernel(x, c, positions, w_ada, b_ada, w_in, b_in, attn_sinks, w_gk2, b_gk2, gla_norm_g, w_o, b_o, ln1_g, ln1_b, w_router, router_bias, w_exp_gate, w_exp_up, w_exp_down, w_sh_gate, w_sh_up, w_sh_down, ln2_g, ln2_b):
    B, S, D = x.shape
    depth = w_ada.shape[0]
    E = w_router.shape[2]
    alpha = float((2 * depth) ** 0.25)
    tl = _tiles(S)
    n_main = int(_SEG[-1])

    posb = jnp.broadcast_to(positions.astype(F32).reshape(B * S, 1), (B * S, LANES))
    half = ATTN_HEAD_DIM // 2
    invf = (ROPE_THETA ** (-(jnp.arange(LANES) % half).astype(F32) / half)).reshape(1, LANES)
    tr = tl["route"]
    upper = (jnp.arange(tr)[:, None] < jnp.arange(tr)[None, :]).astype(BF16)
    lower = (jnp.arange(E)[:, None] >= jnp.arange(E)[None, :]).astype(BF16)
    row = lambda v: v.reshape(1, -1)

    x2 = x.reshape(B * S, D)
    for l in range(depth):
        wr_t = w_router[l].T
        wr_hi = wr_t.astype(BF16)
        p = dict(
            w_main=w_in[l][:, :n_main].astype(BF16), b_main=row(b_in[l][:n_main]),
            w_lo=jnp.pad(w_in[l][:, n_main:], ((0, 0), (0, LANES - GLA_GATE_RANK))).astype(BF16),
            b_lo=row(jnp.pad(b_in[l][n_main:], (0, LANES - GLA_GATE_RANK))),
            w_gk=jnp.pad(w_gk2[l], ((0, LANES - GLA_GATE_RANK), (0, 0))), b_gk=row(b_gk2[l]),
            sinks=attn_sinks[l], norm_g=row(gla_norm_g[l]),
            w_oa=w_o[l][:ATTN_WIDTH].astype(BF16), w_og=w_o[l][ATTN_WIDTH:].astype(BF16), b_o=row(b_o[l]),
            ln1_g=row(ln1_g[l]), ln1_b=row(ln1_b[l]),
            wr_hi=wr_hi, wr_lo=(wr_t - wr_hi.astype(F32)).astype(BF16),
            bias_col=jnp.broadcast_to(router_bias[l].reshape(E, 1), (E, LANES)),
            upper=upper, lower=lower,
            wg=w_exp_gate[l], wu=w_exp_up[l], wd=w_exp_down[l],
            wsg=w_sh_gate[l].astype(BF16), wsu=w_sh_up[l].astype(BF16), wsd=w_sh_down[l].astype(BF16),
            ln2_g=row(ln2_g[l]), ln2_b=row(ln2_b[l]),
        )
        mod = _mod(c, w_ada[l], b_ada[l])
        mod3 = mod.reshape(B, 6, D)
        x2 = _layer(x2, mod3, posb, invf, p, seq=S, alpha=alpha)
    return x2.reshape(B, S, D)
```

```python
import functools

import jax
import jax.numpy as jnp
import numpy as np
from jax import lax
from jax.experimental import pallas as pl
from jax.experimental.pallas import tpu as pltpu

F32 = jnp.float32
BF16 = jnp.bfloat16
I32 = jnp.int32

ATTN_Q_HEADS = 8
ATTN_KV_HEADS = 2
ATTN_HEAD_DIM = 64
ATTN_BLOCK = 128
ROPE_THETA = 10000.0
GLA_HEADS = 4
GLA_KEY_DIM = 64
GLA_VAL_DIM = 128
GLA_GATE_RANK = 16
GLA_GATE_NORM = 16.0
GLA_CHUNK = 64
N_GROUPS = 8
TOPK_GROUPS = 4
TOP_K = 8
ROUTED_SCALE = 2.5
LN_EPS = 1e-5
NEG_INF = -1e30
REMOVED = -3e38

ATTN_WIDTH = ATTN_Q_HEADS * ATTN_HEAD_DIM
KV_WIDTH = ATTN_KV_HEADS * ATTN_HEAD_DIM
GLA_KWIDTH = GLA_HEADS * GLA_KEY_DIM
GLA_WIDTH = GLA_HEADS * GLA_VAL_DIM

LANES = 128
VMEM_LIMIT = 56 * 1024 * 1024
PROJ_ROWS = 256
EXPERT_ROWS = 256
COMBINE_CHUNK = 8
EXPERT_RING = 8


def _params(sem):
    return pltpu.CompilerParams(dimension_semantics=sem, vmem_limit_bytes=VMEM_LIMIT)


def _full(shape):
    return pl.BlockSpec(shape, lambda *_: (0,) * len(shape))


def _split_bf16(a):
    hi = a.astype(BF16)
    lo = (a - hi.astype(F32)).astype(BF16)
    return hi, lo


def _dot(a, b, dims=(((1,), (0,)), ((), ()))):
    return lax.dot_general(a, b, dims, preferred_element_type=F32)


NT = (((1,), (1,)), ((), ()))
TN = (((0,), (0,)), ((), ()))


def _dot3(a, b, dims=(((1,), (0,)), ((), ()))):
    ah, al = _split_bf16(a)
    bh, bl = _split_bf16(b)
    return _dot(ah, bh, dims) + (_dot(ah, bl, dims) + _dot(al, bh, dims))


def _ln_plain(x):
    mu = jnp.mean(x, axis=-1, keepdims=True)
    xc = x - mu
    var = jnp.mean(xc * xc, axis=-1, keepdims=True)
    return xc * lax.rsqrt(var + LN_EPS)


def _silu(x):
    return x * (1.0 / (1.0 + jnp.exp(-x)))


def _sigmoid(x):
    return 1.0 / (1.0 + jnp.exp(-x))


def _mod_kernel(c_ref, w_ref, b_ref, o_ref):
    c = c_ref[...]
    o_ref[...] = _dot3(_silu(c), w_ref[...]) + b_ref[...]


def _mod(c, w_ada, b_ada):
    B, D = c.shape
    n = w_ada.shape[1] // D
    return pl.pallas_call(
        _mod_kernel,
        grid=(n,),
        in_specs=[_full((B, D)),
                  pl.BlockSpec((D, D), lambda j: (0, j)),
                  pl.BlockSpec((1, D), lambda j: (0, j))],
        out_specs=pl.BlockSpec((B, D), lambda j: (0, j)),
        out_shape=jax.ShapeDtypeStruct((B, n * D), F32),
        compiler_params=_params(("arbitrary",)),
        name="mod",
    )(c, w_ada, b_ada.reshape(1, -1))


_SEG = np.cumsum([0, ATTN_WIDTH, KV_WIDTH, KV_WIDTH, GLA_KWIDTH, GLA_KWIDTH, GLA_WIDTH, GLA_WIDTH])


def _rope_chunk(t, cos, sin_signed, first_half):
    up = pltpu.roll(t, LANES - 32, axis=1)
    dn = pltpu.roll(t, 32, axis=1)
    return t * cos + jnp.where(first_half, up, dn) * sin_signed


def _inproj_kernel(x_ref, mod_ref, pos_ref, invf_ref, w_ref, b_ref, wlo_ref, blo_ref, wgk_ref, bgk_ref,
                   qa_ref, ka_ref, va_ref, ql_ref, kl_ref, vl_ref, gl_ref, lg_ref):
    for r0 in range(0, x_ref.shape[0], PROJ_ROWS):
        rows = slice(r0, r0 + PROJ_ROWS)
        h = _ln_plain(x_ref[rows, :]) * (1.0 + mod_ref[0, 1:2, :]) + mod_ref[0, 0:1, :]
        hb = h.astype(BF16)

        def seg(i):
            lo, hi = int(_SEG[i]), int(_SEG[i + 1])
            return _dot(hb, w_ref[:, lo:hi]) + b_ref[:, lo:hi]

        ang = pos_ref[rows, :] * invf_ref[...]
        cos = jnp.cos(ang)
        sin = jnp.sin(ang)
        lane = lax.broadcasted_iota(I32, ang.shape, 1)
        first_half = (lane % ATTN_HEAD_DIM) < (ATTN_HEAD_DIM // 2)
        sin_signed = jnp.where(first_half, -sin, sin)

        q = seg(0)
        scale = ATTN_HEAD_DIM ** -0.5
        for c in range(ATTN_WIDTH // LANES):
            t = q[:, c * LANES:(c + 1) * LANES]
            qa_ref[rows, c * LANES:(c + 1) * LANES] = (
                _rope_chunk(t, cos, sin_signed, first_half) * scale).astype(BF16)
        ka_ref[rows, :] = _rope_chunk(seg(1), cos, sin_signed, first_half).astype(BF16)
        va_ref[rows, :] = seg(2).astype(BF16)
        ql_ref[rows, :] = seg(3).astype(BF16)
        kl_ref[rows, :] = seg(4).astype(BF16)
        vl_ref[rows, :] = seg(5).astype(BF16)
        gl_ref[rows, :] = seg(6).astype(BF16)
        gk_lo = _dot(hb, wlo_ref[...]) + blo_ref[...]
        gk = _dot3(gk_lo, wgk_ref[...]) + bgk_ref[...]
        log_sig = jnp.minimum(gk, 0.0) - jnp.log(1.0 + jnp.exp(-jnp.abs(gk)))
        lg_ref[rows, :] = log_sig * (1.0 / GLA_GATE_NORM)


def _inproj(x2, mod3, posb, invf, w_main, b_main, w_lo, b_lo, w_gk, b_gk, *, seq, tm):
    N, D = x2.shape
    per_b = seq // tm
    widths = [ATTN_WIDTH, KV_WIDTH, KV_WIDTH, GLA_KWIDTH, GLA_KWIDTH, GLA_WIDTH, GLA_WIDTH, GLA_KWIDTH]
    dtypes = [BF16] * 7 + [F32]
    row = lambda i: (i, 0)
    return pl.pallas_call(
        _inproj_kernel,
        grid=(N // tm,),
        in_specs=[pl.BlockSpec((tm, D), row),
                  pl.BlockSpec((1, 6, D), lambda i: (i // per_b, 0, 0)),
                  pl.BlockSpec((tm, LANES), row),
                  _full(invf.shape), _full(w_main.shape), _full(b_main.shape),
                  _full(w_lo.shape), _full(b_lo.shape), _full(w_gk.shape), _full(b_gk.shape)],
        out_specs=[pl.BlockSpec((tm, w), row) for w in widths],
        out_shape=[jax.ShapeDtypeStruct((N, w), dt) for w, dt in zip(widths, dtypes)],
        compiler_params=_params(("parallel",)),
        name="inproj",
    )(x2, mod3, posb, invf, w_main, b_main, w_lo, b_lo, w_gk, b_gk)


def _swa_kernel(sink_ref, rep_ref, q_ref, kc_ref, kp_ref, vc_ref, vp_ref, o_ref):
    j = pl.program_id(1)
    blk = ATTN_BLOCK
    row = lax.broadcasted_iota(I32, (blk, 2 * blk), 0)
    col = lax.broadcasted_iota(I32, (blk, 2 * blk), 1)
    dist = row + blk - col
    band = (dist >= 0) & (dist < blk)
    group = ATTN_Q_HEADS // ATTN_KV_HEADS
    hd = ATTN_HEAD_DIM
    gw = group * hd
    kw = 2 * blk
    bd = (lax.broadcasted_iota(I32, (group * kw, gw), 0) // kw) == (lax.broadcasted_iota(I32, (group * kw, gw), 1) // hd)
    lane_head = lax.broadcasted_iota(I32, (blk, gw), 1) // hd
    rep = rep_ref[...]
    k_all = _dot(jnp.concatenate([kp_ref[...], kc_ref[...]], axis=0), rep).astype(BF16)
    v_all = _dot(jnp.concatenate([vp_ref[...], vc_ref[...]], axis=0), rep).astype(BF16)
    for s_blk in range(q_ref.shape[0] // blk):
        rows = slice(s_blk * blk, (s_blk + 1) * blk)
        k2 = k_all[s_blk * blk:(s_blk + 2) * blk]
        v2 = v_all[s_blk * blk:(s_blk + 2) * blk]
        valid = band & ((col >= blk) | (j > 0)) if s_blk == 0 else band
        for g in range(ATTN_KV_HEADS):
            lanes = slice(g * gw, (g + 1) * gw)
            zero = jnp.zeros((), BF16)
            kd = jnp.where(bd, jnp.concatenate([k2[:, lanes]] * group, axis=0), zero)
            vd = jnp.where(bd, jnp.concatenate([v2[:, lanes]] * group, axis=0), zero)
            s_all = _dot(q_ref[rows, lanes], kd, NT)
            probs, denom = [], jnp.zeros((blk, gw), F32)
            for r in range(group):
                sink = sink_ref[g * group + r]
                s = jnp.where(valid, s_all[:, r * kw:(r + 1) * kw], NEG_INF)
                m = jnp.maximum(jnp.max(s, axis=-1, keepdims=True), sink)
                p = jnp.exp(s - m)
                d = jnp.sum(p, axis=-1, keepdims=True) + jnp.exp(sink - m)
                probs.append(p.astype(BF16))
                denom = jnp.where(lane_head == r, d, denom)
            o = _dot(jnp.concatenate(probs, axis=1), vd) / denom
            o_ref[rows, lanes] = o.astype(BF16)


def _swa(q, k, v, sinks, *, seq, tq):
    N = q.shape[0]
    src = jnp.arange(ATTN_WIDTH)
    src = (src // (ATTN_WIDTH // ATTN_KV_HEADS)) * ATTN_HEAD_DIM + src % ATTN_HEAD_DIM
    rep = (jnp.arange(KV_WIDTH)[:, None] == src[None, :]).astype(BF16)
    per_b = seq // tq
    sub = tq // ATTN_BLOCK
    B = N // seq
    cur = lambda b, j: (b * per_b + j, 0)
    prev = lambda b, j: ((b * per_b + j) * sub - jnp.minimum(j, 1), 0)
    return pl.pallas_call(
        _swa_kernel,
        grid=(B, per_b),
        in_specs=[pl.BlockSpec(memory_space=pltpu.SMEM), _full(rep.shape),
                  pl.BlockSpec((tq, ATTN_WIDTH), cur),
                  pl.BlockSpec((tq, KV_WIDTH), cur),
                  pl.BlockSpec((ATTN_BLOCK, KV_WIDTH), prev),
                  pl.BlockSpec((tq, KV_WIDTH), cur),
                  pl.BlockSpec((ATTN_BLOCK, KV_WIDTH), prev)],
        out_specs=pl.BlockSpec((tq, ATTN_WIDTH), cur),
        out_shape=jax.ShapeDtypeStruct((N, ATTN_WIDTH), BF16),
        compiler_params=_params(("parallel", "parallel")),
        name="swa",
    )(sinks, rep, q, k, k, v, v)


def _gla_kernel(q_ref, k_ref, v_ref, g_ref, lg_ref, ng_ref, o_ref, st_ref, *, chunks):
    @pl.when(pl.program_id(1) == 0)
    def _():
        st_ref[...] = jnp.zeros_like(st_ref)

    C = GLA_CHUNK
    H, dk, dv = GLA_HEADS, GLA_KEY_DIM, GLA_VAL_DIM
    r = lax.broadcasted_iota(I32, (C, C), 0)
    c = lax.broadcasted_iota(I32, (C, C), 1)
    tri = jnp.where(c <= r, 1.0, 0.0).astype(BF16)

    def iota(shape, axis):
        return lax.broadcasted_iota(I32, shape, axis)

    causal_all = (iota((C, H * C), 1) % C) <= iota((C, H * C), 0)
    k_diag = (iota((H * C, H * dk), 0) // C) == (iota((H * C, H * dk), 1) // dk)
    v_diag = (iota((H * C, H * dv), 0) // C) == (iota((H * C, H * dv), 1) // dv)
    st_diag = (iota((H * dv, H * dk), 0) // dv) == (iota((H * dv, H * dk), 1) // dk)
    zero = jnp.zeros((), BF16)
    st = st_ref[...]
    for n in range(chunks):
        rows = slice(n * C, (n + 1) * C)
        lg_hi, lg_lo = _split_bf16(lg_ref[rows, :])
        b = _dot(tri, lg_hi) + _dot(tri, lg_lo)
        b_last = b[C - 1:C, :]
        q_in = (q_ref[rows, :].astype(F32) * (dk ** -0.5) * jnp.exp(b)).astype(BF16)
        kf = k_ref[rows, :].astype(F32)
        k_in = (kf * jnp.exp(-b)).astype(BF16)
        k_out = (kf * jnp.exp(b_last - b)).astype(BF16)
        decay = jnp.exp(b_last)
        v = v_ref[rows, :]
        kd = jnp.where(k_diag, jnp.concatenate([k_in] * H, axis=0), zero)
        vd = jnp.where(v_diag, jnp.concatenate([v] * H, axis=0), zero)
        a = jnp.where(causal_all, _dot(q_in, kd, NT), 0.0).astype(BF16)
        o = _dot(a, vd) + _dot(q_in, st.astype(BF16), NT)
        st = st * decay + jnp.where(st_diag, _dot(v, k_out, TN), 0.0)
        for h in range(H):
            vs = slice(h * dv, (h + 1) * dv)
            oh = o[:, vs]
            oh = oh * lax.rsqrt(jnp.mean(oh * oh, axis=-1, keepdims=True) + LN_EPS) * ng_ref[...]
            o_ref[rows, vs] = (oh * _silu(g_ref[rows, vs].astype(F32))).astype(BF16)
    st_ref[...] = st


def _gla(q, k, v, g, lg, norm_g, *, seq, tc):
    N = q.shape[0]
    B = N // seq
    per_b = seq // tc
    row = lambda b, j: (b * per_b + j, 0)
    return pl.pallas_call(
        functools.partial(_gla_kernel, chunks=tc // GLA_CHUNK),
        grid=(B, per_b),
        in_specs=[pl.BlockSpec((tc, GLA_KWIDTH), row), pl.BlockSpec((tc, GLA_KWIDTH), row),
                  pl.BlockSpec((tc, GLA_WIDTH), row), pl.BlockSpec((tc, GLA_WIDTH), row),
                  pl.BlockSpec((tc, GLA_KWIDTH), row), _full(norm_g.shape)],
        out_specs=pl.BlockSpec((tc, GLA_WIDTH), row),
        out_shape=jax.ShapeDtypeStruct((N, GLA_WIDTH), BF16),
        scratch_shapes=[pltpu.VMEM((GLA_WIDTH, GLA_KWIDTH), F32)],
        compiler_params=_params(("parallel", "arbitrary")),
        name="gla",
    )(q, k, v, g, lg, norm_g)


def _slab_scratch(rows, d):
    return pltpu.VMEM((rows * (d // LANES), LANES), F32)


def _store_slabs(slabs_ref, stage_ref, v):
    rows, d = v.shape
    n = d // LANES
    for c in range(n):
        stage_ref[pl.ds(c, rows, stride=n), :] = v[:, c * LANES:(c + 1) * LANES]
    slabs_ref[...] = stage_ref[...].astype(slabs_ref.dtype)


def _load_slabs(stage_ref, slabs, n):
    stage_ref[...] = slabs.astype(F32)
    rows = stage_ref.shape[0] // n
    return jnp.concatenate([stage_ref[pl.ds(c, rows, stride=n), :] for c in range(n)], axis=1)


def _outproj_kernel(oa_ref, og_ref, x_ref, mod_ref, woa_ref, wog_ref, bo_ref, g_ref, b_ref, wrh_ref, wrl_ref,
                    x1_ref, h2_ref, h2s_ref, lt_ref, stage_ref, *, alpha):
    y = _dot(oa_ref[...], woa_ref[...]) + _dot(og_ref[...], wog_ref[...]) + bo_ref[...]
    z = alpha * x_ref[...] + (1.0 + mod_ref[0, 2:3, :]) * y
    x1 = _ln_plain(z) * g_ref[...] + b_ref[...]
    x1_ref[...] = x1
    h2 = _ln_plain(x1) * (1.0 + mod_ref[0, 4:5, :]) + mod_ref[0, 3:4, :]
    h2_ref[...] = h2.astype(BF16)
    _store_slabs(h2s_ref, stage_ref, h2)
    hh, hl = _split_bf16(h2)
    wh = wrh_ref[...]
    lt_ref[...] = _dot(wh, hh, NT) + (_dot(wh, hl, NT) + _dot(wrl_ref[...], hh, NT))


def _outproj(oa, og, x2, mod3, w_oa, w_og, b_o, ln_g, ln_b, wr_hi, wr_lo, *, seq, tm, alpha):
    N, D = x2.shape
    E = wr_hi.shape[0]
    per_b = seq // tm
    row = lambda i: (i, 0)
    return pl.pallas_call(
        functools.partial(_outproj_kernel, alpha=alpha),
        grid=(N // tm,),
        in_specs=[pl.BlockSpec((tm, ATTN_WIDTH), row), pl.BlockSpec((tm, GLA_WIDTH), row),
                  pl.BlockSpec((tm, D), row),
                  pl.BlockSpec((1, 6, D), lambda i: (i // per_b, 0, 0)),
                  _full(w_oa.shape), _full(w_og.shape), _full(b_o.shape), _full(ln_g.shape), _full(ln_b.shape),
                  _full(wr_hi.shape), _full(wr_lo.shape)],
        out_specs=[pl.BlockSpec((tm, D), row), pl.BlockSpec((tm, D), row),
                   pl.BlockSpec((tm * (D // LANES), LANES), row), pl.BlockSpec((E, tm), lambda i: (0, i))],
        out_shape=[jax.ShapeDtypeStruct((N, D), F32), jax.ShapeDtypeStruct((N, D), BF16),
                   jax.ShapeDtypeStruct((N * (D // LANES), LANES), BF16), jax.ShapeDtypeStruct((E, N), F32)],
        scratch_shapes=[_slab_scratch(tm, D)],
        compiler_params=_params(("parallel",)),
        name="outproj",
    )(oa, og, x2, mod3, w_oa, w_og, b_o, ln_g, ln_b, wr_hi, wr_lo)


def _first_index(eq, idx, size):
    return jnp.min(jnp.where(eq, idx, float(size)), axis=0, keepdims=True)


def _route_kernel(lt_ref, bias_ref, upper_ref, idx_ref, w_ref, wt_ref, rank_ref, cnt_ref, base_ref):
    @pl.when(pl.program_id(0) == 0)
    def _():
        base_ref[...] = jnp.zeros_like(base_ref)

    E, t = lt_ref.shape
    gsz = E // N_GROUPS
    scores = _sigmoid(lt_ref[...])
    biased = scores + bias_ref[:, 0:1]
    gi = lax.broadcasted_iota(I32, (gsz, t), 0).astype(F32)
    gs_rows = []
    for g in range(N_GROUPS):
        grp = biased[g * gsz:(g + 1) * gsz, :]
        m1 = jnp.max(grp, axis=0, keepdims=True)
        first = _first_index(grp == m1, gi, gsz)
        m2 = jnp.max(jnp.where(gi == first, REMOVED, grp), axis=0, keepdims=True)
        gs_rows.append(m1 + m2)
    gs = jnp.concatenate(gs_rows, axis=0)
    ri = lax.broadcasted_iota(I32, (N_GROUPS, t), 0).astype(F32)
    gsel = jnp.zeros((N_GROUPS, t), F32)
    for _ in range(TOPK_GROUPS):
        m = jnp.max(gs, axis=0, keepdims=True)
        hit = ri == _first_index(gs == m, ri, N_GROUPS)
        gsel = jnp.where(hit, 1.0, gsel)
        gs = jnp.where(hit, REMOVED, gs)
    cand = jnp.concatenate(
        [jnp.where(gsel[g:g + 1, :] > 0.5, biased[g * gsz:(g + 1) * gsz, :], NEG_INF) for g in range(N_GROUPS)],
        axis=0)
    ei = lax.broadcasted_iota(I32, (E, t), 0).astype(F32)
    idx_rows, w_rows, hits = [], [], []
    chosen = jnp.zeros((E, t), F32)
    for _ in range(TOP_K):
        m = jnp.max(cand, axis=0, keepdims=True)
        fi = _first_index(cand == m, ei, E)
        hit = ei == fi
        idx_rows.append(fi)
        w_rows.append(jnp.sum(jnp.where(hit, scores, 0.0), axis=0, keepdims=True))
        hits.append(hit)
        chosen = jnp.where(hit, 1.0, chosen)
        cand = jnp.where(hit, REMOVED, cand)
    w = jnp.concatenate(w_rows, axis=0)
    w = w / jnp.sum(w, axis=0, keepdims=True) * ROUTED_SCALE
    idx_ref[...] = jnp.concatenate(idx_rows, axis=0).astype(I32)
    w_ref[...] = w
    wpad = jnp.concatenate([w, jnp.zeros((LANES - TOP_K, t), F32)], axis=0)
    wt_ref[...] = wpad.T
    prefix = _dot(chosen.astype(BF16), upper_ref[...])
    pos = base_ref[:, 0:1] + prefix
    rank_ref[...] = jnp.concatenate(
        [jnp.sum(jnp.where(hit, pos, 0.0), axis=0, keepdims=True) for hit in hits], axis=0).astype(I32)
    base_ref[...] = base_ref[...] + jnp.sum(chosen, axis=1, keepdims=True)
    cnt_ref[...] = base_ref[...]


def _route(logits_t, bias_col, upper, *, tr):
    E, N = logits_t.shape
    col = lambda i: (0, i)
    return pl.pallas_call(
        _route_kernel,
        grid=(N // tr,),
        in_specs=[pl.BlockSpec((E, tr), col), _full(bias_col.shape), _full(upper.shape)],
        out_specs=[pl.BlockSpec((TOP_K, tr), col), pl.BlockSpec((TOP_K, tr), col),
                   pl.BlockSpec((tr, LANES), lambda i: (i, 0)),
                   pl.BlockSpec((TOP_K, tr), col), _full((E, LANES))],
        out_shape=[jax.ShapeDtypeStruct((TOP_K, N), I32), jax.ShapeDtypeStruct((TOP_K, N), F32),
                   jax.ShapeDtypeStruct((N, LANES), F32),
                   jax.ShapeDtypeStruct((TOP_K, N), I32), jax.ShapeDtypeStruct((E, LANES), F32)],
        scratch_shapes=[pltpu.VMEM((E, LANES), F32)],
        compiler_params=_params(("arbitrary",)),
        name="route",
    )(logits_t, bias_col, upper)


def _plan_kernel(cnt_ref, lower_ref, idx_ref, rank_ref, dest_ref, gstart_ref, nblk_ref, pstart_ref, *, slab):
    E = cnt_ref.shape[0]
    bm = float(EXPERT_ROWS)

    @pl.when(pl.program_id(0) == 0)
    def _():
        cnt = cnt_ref[...]
        nblk = jnp.floor((cnt + (bm - 1.0)) * (1.0 / bm))
        padded = nblk * bm
        hi = jnp.floor(padded * (1.0 / 256.0))
        lo = padded - hi * 256.0
        low = lower_ref[...]
        pend = 256.0 * _dot(low, hi.astype(BF16)) + _dot(low, lo.astype(BF16))
        pstart_ref[...] = pend - padded
        gstart_ref[...] = ((pend - padded) * (1.0 / bm)).astype(I32)
        nblk_ref[...] = nblk.astype(I32)

    t = idx_ref.shape[1]
    ei = lax.broadcasted_iota(I32, (E, t), 0)
    ps = pstart_ref[:, 0:1]
    rows = []
    for k in range(TOP_K):
        hit = ei == idx_ref[k:k + 1, :]
        rows.append(jnp.sum(jnp.where(hit, ps, 0.0), axis=0, keepdims=True))
    dest_ref[...] = (rank_ref[...] + jnp.concatenate(rows, axis=0).astype(I32)) * slab


def _plan(cnt, lower, idx, rank, *, tr, slab):
    E = cnt.shape[0]
    N = idx.shape[1]
    col = lambda i: (0, i)
    return pl.pallas_call(
        functools.partial(_plan_kernel, slab=slab),
        grid=(N // tr,),
        in_specs=[_full(cnt.shape), _full(lower.shape), pl.BlockSpec((TOP_K, tr), col),
                  pl.BlockSpec((TOP_K, tr), col)],
        out_specs=[pl.BlockSpec((TOP_K, tr), col), _full((E, LANES)), _full((E, LANES))],
        out_shape=[jax.ShapeDtypeStruct((TOP_K, N), I32), jax.ShapeDtypeStruct((E, LANES), I32),
                   jax.ShapeDtypeStruct((E, LANES), I32)],
        scratch_shapes=[pltpu.VMEM((E, LANES), F32)],
        compiler_params=_params(("arbitrary",)),
        name="plan",
    )(cnt, lower, idx, rank)


def _slab_rows(ref, row, slab):
    return ref.at[pl.ds(pl.multiple_of(row * slab, slab), slab)]


def _slab_at(ref, first_row, slab):
    return ref.at[pl.ds(pl.multiple_of(first_row, slab), slab)]


def _fill_zero_blocks(zero_ref, out_ref, zsem, gstart_ref, nblk_ref, *, last_of_expert):
    blk = zero_ref.shape[0]
    n_exp = gstart_ref.shape[0]
    n_blk = out_ref.shape[0] // blk
    nused = gstart_ref[n_exp - 1] + nblk_ref[n_exp - 1]
    zero_ref[...] = jnp.zeros_like(zero_ref)

    def blk_copy(b):
        return pltpu.make_async_copy(zero_ref, out_ref.at[pl.ds(pl.multiple_of(b * blk, blk), blk)], zsem)

    def tail(b, carry):
        blk_copy(b).start()
        return carry

    lax.fori_loop(nused, n_blk, tail, 0)
    n_started = n_blk - nused
    if last_of_expert:
        def last(e, n):
            has = nblk_ref[e] > 0

            @pl.when(has)
            def _():
                blk_copy(gstart_ref[e] + nblk_ref[e] - 1).start()

            return n + has.astype(I32)

        n_started = lax.fori_loop(0, n_exp, last, n_started)

    def wait(_, carry):
        blk_copy(0).wait()
        return carry

    lax.fori_loop(0, n_started, wait, 0)


def _scatter_kernel(gstart_ref, nblk_ref, *refs):
    dest_refs = refs[:TOP_K]
    h_ref, xs_ref, zero_ref, sem, zsem = refs[TOP_K:]
    ts = dest_refs[0].shape[0]
    slab = h_ref.shape[0] // ts

    @pl.when(pl.program_id(0) == 0)
    def _():
        _fill_zero_blocks(zero_ref, xs_ref, zsem, gstart_ref, nblk_ref, last_of_expert=True)

    def body(t, carry):
        src = _slab_rows(h_ref, t, slab)
        for k in range(TOP_K):
            pltpu.make_async_copy(src, _slab_at(xs_ref, dest_refs[k][t], slab), sem).start(priority=k % 2)
        return carry

    lax.fori_loop(0, ts, body, 0)
    for _ in range(TOP_K):
        pltpu.make_async_copy(h_ref, xs_ref.at[pl.ds(0, slab * ts)], sem).wait()


def _scatter(gstart, nblk, dests, h2s, *, ts, slab, n_blk):
    rows, W = h2s.shape
    grid_spec = pltpu.PrefetchScalarGridSpec(
        num_scalar_prefetch=2,
        grid=(rows // (slab * ts),),
        in_specs=[pl.BlockSpec((ts,), lambda i, be, nu: (i,), memory_space=pltpu.SMEM)] * TOP_K
        + [pl.BlockSpec((slab * ts, W), lambda i, be, nu: (i, 0))],
        out_specs=pl.BlockSpec(memory_space=pl.ANY),
        scratch_shapes=[pltpu.VMEM((slab * EXPERT_ROWS, W), h2s.dtype), pltpu.SemaphoreType.DMA(()),
                        pltpu.SemaphoreType.DMA(())],
    )
    return pl.pallas_call(
        _scatter_kernel,
        grid_spec=grid_spec,
        out_shape=jax.ShapeDtypeStruct((n_blk * slab * EXPERT_ROWS, W), h2s.dtype),
        compiler_params=_params(("arbitrary",)),
        name="scatter",
    )(gstart, nblk, *dests, h2s)


def _experts_kernel(gstart_ref, nblk_ref, wg_ref, wu_ref, wd_ref, xs_ref, ys_ref, xbuf_ref, ybuf_ref,
                    wgb_ref, wub_ref, wdb_ref, xstage_ref, ystage_ref, zero_ref, front_ref, xsem, ysem, zsem):
    e = pl.program_id(0)
    n_exp = pl.num_programs(0)
    ring = xbuf_ref.shape[0]
    blk = zero_ref.shape[0]
    slab = wgb_ref.shape[0] // LANES
    rows = blk // slab
    nused = gstart_ref[n_exp - 1] + nblk_ref[n_exp - 1]
    g0 = gstart_ref[e]
    n = nblk_ref[e]

    def x_copy(b):
        slot = lax.rem(b, ring)
        return pltpu.make_async_copy(xs_ref.at[pl.ds(pl.multiple_of(b * blk, blk), blk)], xbuf_ref.at[slot],
                                     xsem.at[slot])

    def y_copy(b):
        slot = lax.rem(b, ring)
        return pltpu.make_async_copy(ybuf_ref.at[slot], ys_ref.at[pl.ds(pl.multiple_of(b * blk, blk), blk)],
                                     ysem.at[slot])

    @pl.when(e == 0)
    def _():
        front_ref[0] = 0

    @pl.when(n > 0)
    def _():
        wgb_ref[...] = wg_ref[0].astype(BF16)
        wub_ref[...] = wu_ref[0].astype(BF16)
        wdb_ref[...] = wd_ref[0].astype(BF16)

    def group(g, nb):
        limit = jnp.minimum(nused, g + ring)

        def fetch(b, carry):
            x_copy(b).start()
            return carry

        lax.fori_loop(front_ref[0], limit, fetch, 0)
        front_ref[0] = jnp.maximum(front_ref[0], limit)

        for i in range(nb):
            x_copy(g + i).wait()
        for i in range(nb):
            @pl.when(g + i >= ring)
            def _():
                y_copy(g + i - ring).wait()

        for lo, hi in ([(0, nb)] if nb < 4 else [(0, nb // 2), (nb // 2, nb)]):
            parts = [_load_slabs(xstage_ref.at[i], xbuf_ref[lax.rem(g + i, ring)], slab).astype(BF16)
                     for i in range(lo, hi)]
            x = parts[0] if len(parts) == 1 else jnp.concatenate(parts, axis=0)
            gate = _dot(x, wgb_ref[...])
            up = _dot(x, wub_ref[...])
            y = _dot((_silu(gate) * up).astype(BF16), wdb_ref[...])
            for i in range(lo, hi):
                _store_slabs(ybuf_ref.at[lax.rem(g + i, ring)], ystage_ref.at[i], y[(i - lo) * rows:(i - lo + 1) * rows])
        for i in range(nb):
            y_copy(g + i).start()

    n4 = lax.shift_right_logical(n, 2)
    left = n - 4 * n4

    def quad(j, carry):
        group(g0 + 4 * j, 4)
        return carry

    lax.fori_loop(0, n4, quad, 0)

    @pl.when(left >= 2)
    def _():
        group(g0 + 4 * n4, 2)

    @pl.when(lax.rem(left, 2) == 1)
    def _():
        group(g0 + n - 1, 1)

    @pl.when(e == n_exp - 1)
    def _():
        def drain(b, carry):
            y_copy(b).wait()
            return carry

        lax.fori_loop(jnp.maximum(nused - ring, 0), nused, drain, 0)
        _fill_zero_blocks(zero_ref, ys_ref, zsem, gstart_ref, nblk_ref, last_of_expert=False)


def _experts(gstart, nblk, xs, wg, wu, wd):
    rows, W = xs.shape
    E, D, H = wg.shape
    blk = (D // LANES) * EXPERT_ROWS
    grid_spec = pltpu.PrefetchScalarGridSpec(
        num_scalar_prefetch=2,
        grid=(E,),
        in_specs=[pl.BlockSpec((1, D, H), lambda e, gs, nb: (e, 0, 0)),
                  pl.BlockSpec((1, D, H), lambda e, gs, nb: (e, 0, 0)),
                  pl.BlockSpec((1, H, D), lambda e, gs, nb: (e, 0, 0)),
                  pl.BlockSpec(memory_space=pl.ANY)],
        out_specs=pl.BlockSpec(memory_space=pl.ANY),
        scratch_shapes=[pltpu.VMEM((EXPERT_RING, blk, W), xs.dtype), pltpu.VMEM((EXPERT_RING, blk, W), xs.dtype),
                        pltpu.VMEM((D, H), BF16), pltpu.VMEM((D, H), BF16), pltpu.VMEM((H, D), BF16),
                        pltpu.VMEM((EXPERT_RING // 2, blk, LANES), F32), pltpu.VMEM((EXPERT_RING // 2, blk, LANES), F32),
                        pltpu.VMEM((blk, W), xs.dtype), pltpu.SMEM((1,), I32),
                        pltpu.SemaphoreType.DMA((EXPERT_RING,)), pltpu.SemaphoreType.DMA((EXPERT_RING,)),
                        pltpu.SemaphoreType.DMA(())],
    )
    return pl.pallas_call(
        _experts_kernel,
        grid_spec=grid_spec,
        out_shape=jax.ShapeDtypeStruct((rows, W), xs.dtype),
        compiler_params=_params(("arbitrary",)),
        name="experts",
    )(gstart, nblk, wg, wu, wd, xs)


def _combine_kernel(*refs, alpha):
    cur_refs, nxt_refs = refs[:TOP_K], refs[TOP_K:2 * TOP_K]
    (wt_ref, h2_ref, x1_ref, mod_ref, wsg_ref, wsu_ref, wsd_ref, g_ref, b_ref, ys_ref, o_ref,
     buf0_ref, buf1_ref, acc_ref, stage_ref, sem) = refs[2 * TOP_K:]
    tc, d = h2_ref.shape
    slab = d // LANES
    i = pl.program_id(0)
    last = pl.num_programs(0) - 1
    par = lax.rem(i, 2)
    chunk = COMBINE_CHUNK

    def row_copy(dest_refs, buf_ref, p, k, t):
        return pltpu.make_async_copy(_slab_at(ys_ref, dest_refs[k][t], slab), _slab_rows(buf_ref.at[k], t, slab),
                                     sem.at[p])

    def wait_planes(buf_ref, p):
        for k in range(TOP_K):
            pltpu.make_async_copy(ys_ref.at[pl.ds(0, slab * tc)], buf_ref.at[k], sem.at[p]).wait()

    @pl.when(i == 0)
    def _():
        def body(t, carry):
            for k in range(TOP_K):
                row_copy(cur_refs, buf0_ref, 0, k, t).start(priority=k % 2)
            return carry

        lax.fori_loop(0, tc, body, 0)

    h = h2_ref[...]
    mid = (_silu(_dot(h, wsg_ref[...])) * _dot(h, wsu_ref[...])).astype(BF16)
    acc_ref[...] = _dot(mid, wsd_ref[...])

    def run(rd_ref, wr_ref, p):
        wait_planes(rd_ref, p)

        def it(j, carry):
            t0 = pl.multiple_of(j * chunk, chunk)
            r0 = pl.multiple_of(j * chunk * slab, chunk * slab)
            upd = acc_ref[pl.ds(t0, chunk), :]
            for k in range(TOP_K):
                stage_ref[k] = rd_ref[k, pl.ds(r0, chunk * slab), :].astype(F32)
                rows = jnp.concatenate([stage_ref[k, pl.ds(c, chunk, stride=slab), :] for c in range(slab)], axis=1)
                upd = upd + wt_ref[pl.ds(t0, chunk), k:k + 1] * rows
            acc_ref[pl.ds(t0, chunk), :] = upd
            for t in range(chunk):
                for k in range(TOP_K):
                    row_copy(nxt_refs, wr_ref, 1 - p, k, t0 + t).start(priority=k % 2)
            return carry

        lax.fori_loop(0, tc // chunk, it, 0)

    @pl.when(par == 0)
    def _():
        run(buf0_ref, buf1_ref, 0)

    @pl.when(par == 1)
    def _():
        run(buf1_ref, buf0_ref, 1)

    z = alpha * x1_ref[...] + (1.0 + mod_ref[0, 5:6, :]) * acc_ref[...]
    o_ref[...] = _ln_plain(z) * g_ref[...] + b_ref[...]

    @pl.when((i == last) & (par == 0))
    def _():
        wait_planes(buf1_ref, 1)

    @pl.when((i == last) & (par == 1))
    def _():
        wait_planes(buf0_ref, 0)


def _combine(dests, wt, h2, x1, mod3, wsg, wsu, wsd, ln_g, ln_b, ys, *, seq, tc, alpha):
    N, D = x1.shape
    slab = D // LANES
    per_b = seq // tc
    n_tiles = N // tc
    row = lambda i: (i, 0)
    return pl.pallas_call(
        functools.partial(_combine_kernel, alpha=alpha),
        grid=(n_tiles,),
        in_specs=[pl.BlockSpec((tc,), lambda i: (i,), memory_space=pltpu.SMEM)] * TOP_K
        + [pl.BlockSpec((tc,), lambda i: (jnp.minimum(i + 1, n_tiles - 1),), memory_space=pltpu.SMEM)] * TOP_K
        + [pl.BlockSpec((tc, LANES), row), pl.BlockSpec((tc, D), row), pl.BlockSpec((tc, D), row),
           pl.BlockSpec((1, 6, D), lambda i: (i // per_b, 0, 0)),
           _full(wsg.shape), _full(wsu.shape), _full(wsd.shape), _full(ln_g.shape), _full(ln_b.shape),
           pl.BlockSpec(memory_space=pl.ANY)],
        out_specs=pl.BlockSpec((tc, D), row),
        out_shape=jax.ShapeDtypeStruct((N, D), F32),
        scratch_shapes=[pltpu.VMEM((TOP_K, slab * tc, LANES), ys.dtype), pltpu.VMEM((TOP_K, slab * tc, LANES), ys.dtype),
                        pltpu.VMEM((tc, D), F32), pltpu.VMEM((TOP_K, COMBINE_CHUNK * slab, LANES), F32),
                        pltpu.SemaphoreType.DMA((2,))],
        compiler_params=_params(("arbitrary",)),
        name="combine",
    )(*dests, *dests, wt, h2, x1, mod3, wsg, wsu, wsd, ln_g, ln_b, ys)


def _tiles(seq):
    t = lambda want: min(want, seq)
    return dict(proj=t(512), swa=t(512), gla=t(512), route=t(512), scatter=t(1024), combine=t(512))


def _layer(x2, mod3, posb, invf, p, *, seq, alpha):
    N, D = x2.shape
    tl = _tiles(seq)
    qa, ka, va, ql, kl, vl, gl, lg = _inproj(x2, mod3, posb, invf, p["w_main"], p["b_main"], p["w_lo"], p["b_lo"],
                                             p["w_gk"], p["b_gk"], seq=seq, tm=tl["proj"])
    oa = _swa(qa, ka, va, p["sinks"], seq=seq, tq=tl["swa"])
    og = _gla(ql, kl, vl, gl, lg, p["norm_g"], seq=seq, tc=tl["gla"])
    x1, h2, h2s, logits_t = _outproj(oa, og, x2, mod3, p["w_oa"], p["w_og"], p["b_o"], p["ln1_g"], p["ln1_b"],
                                     p["wr_hi"], p["wr_lo"], seq=seq, tm=tl["proj"], alpha=alpha)
    idx, _, wt, rank, cnt = _route(logits_t, p["bias_col"], p["upper"], tr=tl["route"])
    E = cnt.shape[0]
    n_blk = (N * TOP_K) // EXPERT_ROWS + E
    dest, gstart, nblk = _plan(cnt, p["lower"], idx, rank, tr=tl["route"], slab=D // LANES)
    gstart, nblk = gstart[:, 0], nblk[:, 0]
    dests = [dest[k] for k in range(TOP_K)]
    xs = _scatter(gstart, nblk, dests, h2s, ts=tl["scatter"], slab=D // LANES, n_blk=n_blk)
    ys = _experts(gstart, nblk, xs, p["wg"], p["wu"], p["wd"])
    return _combine(dests, wt, h2, x1, mod3, p["wsg"], p["wsu"], p["wsd"], p["ln2_g"], p["ln2_b"], ys,
                    seq=seq, tc=tl["combine"], alpha=alpha)


def kernel(x, c, positions, w_ada, b_ada, w_in, b_in, attn_sinks, w_gk2, b_gk2, gla_norm_g, w_o, b_o, ln1_g, ln1_b, w_router, router_bias, w_exp_gate, w_exp_up, w_exp_down, w_sh_gate, w_sh_up, w_sh_down, ln2_g, ln2_b):
    B, S, D = x.shape
    depth = w_ada.shape[0]
    E = w_router.shape[2]
    alpha = float((2 * depth) ** 0.25)
    tl = _tiles(S)
    n_main = int(_SEG[-1])

    posb = jnp.broadcast_to(positions.astype(F32).reshape(B * S, 1), (B * S, LANES))
    half = ATTN_HEAD_DIM // 2
    invf = (ROPE_THETA ** (-(jnp.arange(LANES) % half).astype(F32) / half)).reshape(1, LANES)
    tr = tl["route"]
    upper = (jnp.arange(tr)[:, None] < jnp.arange(tr)[None, :]).astype(BF16)
    lower = (jnp.arange(E)[:, None] >= jnp.arange(E)[None, :]).astype(BF16)
    row = lambda v: v.reshape(1, -1)

    x2 = x.reshape(B * S, D)
    for l in range(depth):
        wr_t = w_router[l].T
        wr_hi = wr_t.astype(BF16)
        p = dict(
            w_main=w_in[l][:, :n_main].astype(BF16), b_main=row(b_in[l][:n_main]),
            w_lo=jnp.pad(w_in[l][:, n_main:], ((0, 0), (0, LANES - GLA_GATE_RANK))).astype(BF16),
            b_lo=row(jnp.pad(b_in[l][n_main:], (0, LANES - GLA_GATE_RANK))),
            w_gk=jnp.pad(w_gk2[l], ((0, LANES - GLA_GATE_RANK), (0, 0))), b_gk=row(b_gk2[l]),
            sinks=attn_sinks[l], norm_g=row(gla_norm_g[l]),
            w_oa=w_o[l][:ATTN_WIDTH].astype(BF16), w_og=w_o[l][ATTN_WIDTH:].astype(BF16), b_o=row(b_o[l]),
            ln1_g=row(ln1_g[l]), ln1_b=row(ln1_b[l]),
            wr_hi=wr_hi, wr_lo=(wr_t - wr_hi.astype(F32)).astype(BF16),
            bias_col=jnp.broadcast_to(router_bias[l].reshape(E, 1), (E, LANES)),
            upper=upper, lower=lower,
            wg=w_exp_gate[l], wu=w_exp_up[l], wd=w_exp_down[l],
            wsg=w_sh_gate[l].astype(BF16), wsu=w_sh_up[l].astype(BF16), wsd=w_sh_down[l].astype(BF16),
            ln2_g=row(ln2_g[l]), ln2_b=row(ln2_b[l]),
        )
        mod = _mod(c, w_ada[l], b_ada[l])
        mod3 = mod.reshape(B, 6, D)
        x2 = _layer(x2, mod3, posb, invf, p, seq=S, alpha=alpha)
    return x2.reshape(B, S, D)
```
